```python
import math
import jax, jax.numpy as jnp
from jax import lax
import numpy as np

D_MODEL = 1024
BATCH = 2
SEQ = 8192
DEPTH = 4

HEAD_DIM = 64
MIX_WIDTH = D_MODEL
A_WIDTH = MIX_WIDTH // 4
A_HEADS = A_WIDTH // HEAD_DIM
CHUNK = 128
B_WIDTH = MIX_WIDTH // 2
B_Q_HEADS = B_WIDTH // HEAD_DIM
B_KV_HEADS = 2
B_GROUP = B_Q_HEADS // B_KV_HEADS
WINDOW = 128
ROPE_THETA = 10000.0
C_WIDTH = MIX_WIDTH // 4
C_GROUP = 16
C_GROUPS = C_WIDTH // C_GROUP
C_STATE = 64
DT_MIN = 0.001
DT_MAX = 0.1
IN_A = 2 * A_WIDTH
IN_Q = B_WIDTH
IN_KV = B_KV_HEADS * HEAD_DIM
IN_C = C_WIDTH
IN_COLS = IN_A + IN_Q + 2 * IN_KV + IN_C
D_FF = 4 * D_MODEL
PLE_DIM = 256
EPS = 1e-6

kernel_name = "hybrid_gmlp_swa_s5_trunk"


def rmsnorm(x, g):
    xf = x.astype(jnp.float32)
    y = xf * lax.rsqrt(jnp.mean(xf * xf, axis=-1, keepdims=True) + EPS)
    return (y * g.astype(jnp.float32)).astype(x.dtype)


def layernorm(x, g, b):
    xf = x.astype(jnp.float32)
    mu = jnp.mean(xf, axis=-1, keepdims=True)
    xc = xf - mu
    y = xc * lax.rsqrt(jnp.mean(xc * xc, axis=-1, keepdims=True) + EPS)
    return (y * g.astype(jnp.float32) + b.astype(jnp.float32)).astype(x.dtype)


def rope_tables(positions):
    inv = 1.0 / (ROPE_THETA ** (jnp.arange(0, HEAD_DIM, 2, dtype=jnp.float32) / HEAD_DIM))
    ang = positions.astype(jnp.float32)[..., None] * inv
    return jnp.cos(ang), jnp.sin(ang)


def apply_rope(x, cos, sin):
    xf = x.astype(jnp.float32)
    x1, x2 = jnp.split(xf, 2, axis=-1)
    c = cos[:, :, None, :]
    s = sin[:, :, None, :]
    return jnp.concatenate([x1 * c - x2 * s, x2 * c + x1 * s], axis=-1).astype(x.dtype)


def chunk_gmlp(z, ln_g, ln_b, w_s, b_s):
    bsz, L, _ = z.shape
    z = jax.nn.gelu(z).reshape(bsz, L // CHUNK, CHUNK, A_HEADS, 2 * HEAD_DIM)
    u, v = jnp.split(z, 2, axis=-1)
    v = layernorm(v, ln_g, ln_b)
    causal = jnp.tril(jnp.ones((CHUNK, CHUNK), dtype=bool))
    w = jnp.where(causal, w_s, 0.0).astype(v.dtype)
    sv = jnp.einsum('hts,bnshd->bnthd', w, v) + b_s.T[None, None, :, :, None].astype(v.dtype)
    return (u * sv).reshape(bsz, L, A_WIDTH)


def swa_sink_attention(q, k, v, sinks):
    bsz, L = q.shape[:2]
    nb = L // WINDOW
    qb = q.reshape(bsz, nb, WINDOW, B_KV_HEADS, B_GROUP, HEAD_DIM)

    def band(t):
        t = t.reshape(bsz, nb, WINDOW, B_KV_HEADS, HEAD_DIM)
        prev = jnp.pad(t[:, :-1], ((0, 0), (1, 0), (0, 0), (0, 0), (0, 0)))
        return jnp.concatenate([prev, t], axis=2)

    kb, vb = band(k), band(v)
    s = jnp.einsum('bnqhgd,bnkhd->bnhgqk', qb, kb,
                   preferred_element_type=jnp.float32) * (HEAD_DIM ** -0.5)
    qi = jnp.arange(WINDOW)[:, None] + WINDOW
    kj = jnp.arange(2 * WINDOW)[None, :]
    diff = qi - kj
    band_ok = (diff >= 0) & (diff < WINDOW)
    not_first = jnp.arange(nb)[:, None, None] > 0
    mask = band_ok[None] & (not_first | (kj >= WINDOW)[None])
    s = jnp.where(mask[None, :, None, None], s, -jnp.inf)
    sink = sinks.astype(jnp.float32).reshape(B_KV_HEADS, B_GROUP)[None, None, :, :, None, None]
    m = jnp.maximum(jnp.max(s, axis=-1, keepdims=True), sink)
    pr = jnp.exp(s - m)
    denom = jnp.sum(pr, axis=-1, keepdims=True) + jnp.exp(sink - m)
    o = jnp.einsum('bnhgqk,bnkhd->bnqhgd', (pr / denom).astype(v.dtype), vb)
    return o.reshape(bsz, L, B_WIDTH)


def s5_ssm(u, a_re, a_im, log_dt, b_re, b_im, c_re, c_im, d_skip, glu_w1, glu_w2):
    bsz, L, _ = u.shape
    uf = u.astype(jnp.float32).reshape(bsz, L, C_GROUPS, C_GROUP)
    lam = lax.complex(a_re.astype(jnp.float32), a_im.astype(jnp.float32))
    dt = jnp.exp(log_dt.astype(jnp.float32))[:, None]
    lam_bar = jnp.exp(lam * dt)
    bmat = lax.complex(b_re.astype(jnp.float32), b_im.astype(jnp.float32))
    b_bar = ((lam_bar - 1.0) / lam)[..., None] * bmat
    bu = jnp.einsum('gph,blgh->blgp', b_bar, uf.astype(jnp.complex64))
    a_seq = jnp.broadcast_to(lam_bar, bu.shape)

    def combine(e1, e2):
        a1, x1 = e1
        a2, x2 = e2
        return a1 * a2, a2 * x1 + x2

    _, states = lax.associative_scan(combine, (a_seq, bu), axis=1)
    cmat = lax.complex(c_re.astype(jnp.float32), c_im.astype(jnp.float32))
    y = jnp.real(jnp.einsum('ghp,blgp->blgh', cmat, states)) + d_skip.astype(jnp.float32) * uf
    y = jax.nn.gelu(y.reshape(bsz, L, C_WIDTH)).astype(u.dtype)
    return (y @ glu_w1) * jax.nn.sigmoid(y @ glu_w2)


def hybrid_layer(h, p_i, cos, sin, attn_norm_g, w_in, gmlp_ln_g, gmlp_ln_b, gmlp_ws, gmlp_bs,
                 q_norm_g, k_norm_g, sinks, ssm_a_re, ssm_a_im, ssm_log_dt, ssm_b_re, ssm_b_im,
                 ssm_c_re, ssm_c_im, ssm_d, glu_w1, glu_w2, mix_out_g, w_out,
                 mlp_norm_g, w_ff1, w_ff2, ple_norm_g, w_ple_gate, w_ple_proj):
    bsz, L, _ = h.shape
    xn = rmsnorm(h, attn_norm_g)
    z = xn @ w_in
    za, zq, zk, zv, zc = jnp.split(
        z, [IN_A, IN_A + IN_Q, IN_A + IN_Q + IN_KV, IN_A + IN_Q + 2 * IN_KV], axis=-1)
    ya = chunk_gmlp(za, gmlp_ln_g, gmlp_ln_b, gmlp_ws, gmlp_bs)
    q = zq.reshape(bsz, L, B_Q_HEADS, HEAD_DIM)
    k = zk.reshape(bsz, L, B_KV_HEADS, HEAD_DIM)
    v = zv.reshape(bsz, L, B_KV_HEADS, HEAD_DIM)
    q = apply_rope(rmsnorm(q, q_norm_g), cos, sin)
    k = apply_rope(rmsnorm(k, k_norm_g), cos, sin)
    yb = swa_sink_attention(q, k, v, sinks)
    yc = s5_ssm(zc, ssm_a_re, ssm_a_im, ssm_log_dt, ssm_b_re, ssm_b_im,
                ssm_c_re, ssm_c_im, ssm_d, glu_w1, glu_w2)
    y = jnp.concatenate([
        rmsnorm(ya, mix_out_g[:A_WIDTH]),
        rmsnorm(yb, mix_out_g[A_WIDTH:A_WIDTH + B_WIDTH]),
        rmsnorm(yc, mix_out_g[A_WIDTH + B_WIDTH:]),
    ], axis=-1)
    h = h + y @ w_out
    hn = rmsnorm(h, mlp_norm_g)
    h = h + jnp.square(jax.nn.relu(hn @ w_ff1)) @ w_ff2
    gate = jax.nn.sigmoid(rmsnorm(h, ple_norm_g) @ w_ple_gate)
    return h + gate * (p_i @ w_ple_proj)


def setup_inputs(seed: int = 0) -> dict:
    key = jax.random.key(seed)
    ks = iter(jax.random.split(key, 40))
    f32 = jnp.float32

    def nrm(shape, scale):
        return jax.random.normal(next(ks), shape, f32) * scale

    def gain(shape):
        return 1.0 + nrm(shape, 0.02)

    x = nrm((BATCH, SEQ, D_MODEL), 1.0)
    p = nrm((DEPTH, BATCH, SEQ, PLE_DIM), 1.0)
    offsets = jax.random.randint(next(ks), (BATCH, 1), 0, 1024, dtype=jnp.int32)
    positions = offsets + jnp.arange(SEQ, dtype=jnp.int32)[None, :]

    n_idx = jnp.arange(C_STATE, dtype=f32)
    ssm_a_re = -0.5 + nrm((DEPTH, C_GROUPS, C_STATE), 0.01)
    ssm_a_im = math.pi * n_idx[None, None, :] + nrm((DEPTH, C_GROUPS, C_STATE), 0.01)
    ssm_log_dt = jax.random.uniform(next(ks), (DEPTH, C_GROUPS), f32,
                                    math.log(DT_MIN), math.log(DT_MAX))
    b_scale = (2.0 * C_GROUP) ** -0.5
    c_scale = (2.0 * C_STATE) ** -0.5

    return {
        "x": x,
        "p": p,
        "positions": positions,
        "attn_norm_g": gain((DEPTH, D_MODEL)),
        "w_in": nrm((DEPTH, D_MODEL, IN_COLS), D_MODEL ** -0.5),
        "gmlp_ln_g": gain((DEPTH, A_HEADS, HEAD_DIM)),
        "gmlp_ln_b": nrm((DEPTH, A_HEADS, HEAD_DIM), 0.02),
        "gmlp_ws": nrm((DEPTH, A_HEADS, CHUNK, CHUNK), 0.5 * CHUNK ** -0.5),
        "gmlp_bs": gain((DEPTH, A_HEADS, CHUNK)),
        "q_norm_g": gain((DEPTH, HEAD_DIM)),
        "k_norm_g": gain((DEPTH, HEAD_DIM)),
        "sinks": nrm((DEPTH, B_Q_HEADS), 0.5),
        "ssm_a_re": ssm_a_re,
        "ssm_a_im": ssm_a_im,
        "ssm_log_dt": ssm_log_dt,
        "ssm_b_re": nrm((DEPTH, C_GROUPS, C_STATE, C_GROUP), b_scale),
        "ssm_b_im": nrm((DEPTH, C_GROUPS, C_STATE, C_GROUP), b_scale),
        "ssm_c_re": nrm((DEPTH, C_GROUPS, C_GROUP, C_STATE), c_scale),
        "ssm_c_im": nrm((DEPTH, C_GROUPS, C_GROUP, C_STATE), c_scale),
        "ssm_d": nrm((DEPTH, C_GROUPS, C_GROUP), 0.5),
        "glu_w1": nrm((DEPTH, C_WIDTH, C_WIDTH), C_WIDTH ** -0.5),
        "glu_w2": nrm((DEPTH, C_WIDTH, C_WIDTH), C_WIDTH ** -0.5),
        "mix_out_g": gain((DEPTH, MIX_WIDTH)),
        "w_out": nrm((DEPTH, MIX_WIDTH, D_MODEL), MIX_WIDTH ** -0.5),
        "mlp_norm_g": gain((DEPTH, D_MODEL)),
        "w_ff1": nrm((DEPTH, D_MODEL, D_FF), D_MODEL ** -0.5),
        "w_ff2": nrm((DEPTH, D_FF, D_MODEL), D_FF ** -0.5),
        "ple_norm_g": gain((DEPTH, D_MODEL)),
        "w_ple_gate": nrm((DEPTH, D_MODEL, D_MODEL), D_MODEL ** -0.5),
        "w_ple_proj": nrm((DEPTH, PLE_DIM, D_MODEL), 0.5 * PLE_DIM ** -0.5),
    }


def reference(x, p, positions, attn_norm_g, w_in, gmlp_ln_g, gmlp_ln_b, gmlp_ws, gmlp_bs,
              q_norm_g, k_norm_g, sinks, ssm_a_re, ssm_a_im, ssm_log_dt, ssm_b_re, ssm_b_im,
              ssm_c_re, ssm_c_im, ssm_d, glu_w1, glu_w2, mix_out_g, w_out,
              mlp_norm_g, w_ff1, w_ff2, ple_norm_g, w_ple_gate, w_ple_proj):
    cos, sin = rope_tables(positions)
    h = x
    for i in range(DEPTH):
        h = hybrid_layer(h, p[i], cos, sin, attn_norm_g[i], w_in[i], gmlp_ln_g[i], gmlp_ln_b[i],
                         gmlp_ws[i], gmlp_bs[i], q_norm_g[i], k_norm_g[i], sinks[i],
                         ssm_a_re[i], ssm_a_im[i], ssm_log_dt[i], ssm_b_re[i], ssm_b_im[i],
                         ssm_c_re[i], ssm_c_im[i], ssm_d[i], glu_w1[i], glu_w2[i],
                         mix_out_g[i], w_out[i], mlp_norm_g[i], w_ff1[i], w_ff2[i],
                         ple_norm_g[i], w_ple_gate[i], w_ple_proj[i])
    return h
```

```python
import math

import numpy as np
import jax
import jax.numpy as jnp
from jax import lax
from jax.experimental import pallas as pl
from jax.experimental.pallas import tpu as pltpu

F32 = jnp.float32
BF16 = jnp.bfloat16

D_MODEL = 1024
HEAD_DIM = 64
A_WIDTH = 256
A_HEADS = 4
CHUNK = 128
B_WIDTH = 512
B_Q_HEADS = 8
B_KV_HEADS = 2
WINDOW = 128
ROPE_THETA = 10000.0
C_WIDTH = 256
C_GROUP = 16
C_GROUPS = 16
C_STATE = 64
N_STATE = C_GROUPS * C_STATE
IN_COLS = 1536
D_FF = 4096
PLE_DIM = 256
EPS = 1e-6
NEG_BIG = -1e30

COL_A = 0
COL_Q = 512
COL_KV = 1024
COL_C = 1280

ROWS_PROJ = 512
ROWS_MIX = 512
ROWS_SSM = 2048
SSM_T = 8
SSM_N = ROWS_SSM // SSM_T
FF_SPLIT = 4
VMEM_LIMIT = 56 * 1024 * 1024


def _const_spec(shape):
    nd = len(shape)
    return pl.BlockSpec(shape, lambda *_: (0,) * nd, pipeline_mode=pl.Buffered(1))


def _rms(x, g):
    ms = jnp.mean(x * x, axis=-1, keepdims=True)
    return x * lax.rsqrt(ms + EPS) * g


def _dot(a, b):
    return jnp.dot(a, b, preferred_element_type=F32)


def _seg_mean(x, pmat):
    hi = x.astype(BF16)
    lo = (x - hi.astype(F32)).astype(BF16)
    return _dot(hi, pmat) + _dot(lo, pmat)


def _inproj_kernel(h_ref, g_ref, w_ref, z_ref):
    xn = _rms(h_ref[...], g_ref[...]).astype(BF16)
    z_ref[...] = _dot(xn, w_ref[...])


def _inproj(h, g, w):
    n = h.shape[0]
    return pl.pallas_call(
        _inproj_kernel,
        grid=(n // ROWS_PROJ,),
        in_specs=[
            pl.BlockSpec((ROWS_PROJ, D_MODEL), lambda i: (i, 0)),
            _const_spec((1, D_MODEL)),
            _const_spec((D_MODEL, IN_COLS)),
        ],
        out_specs=pl.BlockSpec((ROWS_PROJ, IN_COLS), lambda i: (i, 0)),
        out_shape=jax.ShapeDtypeStruct((n, IN_COLS), F32),
        compiler_params=pltpu.CompilerParams(
            dimension_semantics=("arbitrary",), vmem_limit_bytes=VMEM_LIMIT),
        name="inproj",
    )(h, g, w)


def _gmlp_kernel(z_ref, w_ref, bs_ref, lng_ref, lnb_ref, mg_ref, pm_ref, o_ref):
    pm = pm_ref[...]
    row = lax.broadcasted_iota(jnp.int32, (CHUNK, CHUNK), 0)
    col = lax.broadcasted_iota(jnp.int32, (CHUNK, CHUNK), 1)
    lane = lax.broadcasted_iota(jnp.int32, (CHUNK, A_WIDTH), 1)
    ws = [jnp.where(row >= col, w_ref[h], jnp.zeros((), BF16)) for h in range(A_HEADS)]
    for c in range(ROWS_MIX // CHUNK):
        rows = pl.ds(c * CHUNK, CHUNK)
        u = jax.nn.gelu(z_ref[rows, 0:A_WIDTH])
        v = jax.nn.gelu(z_ref[rows, A_WIDTH:2 * A_WIDTH])
        vc = v - _seg_mean(v, pm)
        var = _seg_mean(vc * vc, pm)
        vn = (vc * lax.rsqrt(var + EPS) * lng_ref[...] + lnb_ref[...]).astype(BF16)
        sv = bs_ref[...]
        for h in range(A_HEADS):
            in_head = (lane >= h * HEAD_DIM) & (lane < (h + 1) * HEAD_DIM)
            sv = sv + _dot(ws[h], jnp.where(in_head, vn, jnp.zeros((), BF16)))
        o_ref[rows, :] = _rms(u * sv, mg_ref[...]).astype(BF16)


def _gmlp(z, w, bs, lng, lnb, mg, pm):
    n = z.shape[0]
    return pl.pallas_call(
        _gmlp_kernel,
        grid=(n // ROWS_MIX,),
        in_specs=[
            pl.BlockSpec((ROWS_MIX, 2 * A_WIDTH), lambda i: (i, COL_A // (2 * A_WIDTH))),
            _const_spec((A_HEADS, CHUNK, CHUNK)),
            _const_spec((CHUNK, A_WIDTH)),
            _const_spec((1, A_WIDTH)),
            _const_spec((1, A_WIDTH)),
            _const_spec((1, A_WIDTH)),
            _const_spec((A_WIDTH, A_WIDTH)),
        ],
        out_specs=pl.BlockSpec((ROWS_MIX, A_WIDTH), lambda i: (i, 0)),
        out_shape=jax.ShapeDtypeStruct((n, A_WIDTH), BF16),
        compiler_params=pltpu.CompilerParams(dimension_semantics=("arbitrary",)),
        name="gmlp",
    )(z, w, bs, lng, lnb, mg, pm)


def _rope(x, cos, sin_signed):
    width = x.shape[-1]
    lane = lax.broadcasted_iota(jnp.int32, x.shape, 1)
    first_half = (lane % HEAD_DIM) < (HEAD_DIM // 2)
    partner = jnp.where(first_half,
                        pltpu.roll(x, width - HEAD_DIM // 2, 1),
                        pltpu.roll(x, HEAD_DIM // 2, 1))
    return x * cos + partner * sin_signed


def _attn_kernel(blocks_per_seq, sinks_ref, q_ref, kv_ref, kvp_ref, cos_ref, sin_ref,
                 cosp_ref, sinp_ref, qg_ref, kg_ref, mg_ref, pm_ref, o_ref):
    nq = ROWS_MIX // WINDOW
    pm = pm_ref[...]
    pm_kv = pm[0:2 * HEAD_DIM, 0:2 * HEAD_DIM]
    first_block = (pl.program_id(0) % blocks_per_seq) == 0

    k_all = jnp.concatenate([kvp_ref[:, 0:128], kv_ref[:, 0:128]], axis=0)
    v_all = jnp.concatenate([kvp_ref[:, 128:256], kv_ref[:, 128:256]], axis=0)
    cos_all = jnp.concatenate([cosp_ref[...], cos_ref[...]], axis=0)
    sin_all = jnp.concatenate([sinp_ref[...], sin_ref[...]], axis=0)
    k_all = k_all * lax.rsqrt(_seg_mean(k_all * k_all, pm_kv) + EPS) * kg_ref[...]
    k_all = _rope(k_all, cos_all, sin_all)
    lane_kv = lax.broadcasted_iota(jnp.int32, k_all.shape, 1)
    low = lane_kv < HEAD_DIM
    k_sw = pltpu.roll(k_all, HEAD_DIM, 1)
    v_sw = pltpu.roll(v_all, HEAD_DIM, 1)
    zero = jnp.zeros((), F32)
    k_pl = [[jnp.where(low, k_all, zero).astype(BF16), jnp.where(low, zero, k_sw).astype(BF16)],
            [jnp.where(low, k_sw, zero).astype(BF16), jnp.where(low, zero, k_all).astype(BF16)]]
    v_pl = [[jnp.where(low, v_all, zero).astype(BF16), jnp.where(low, zero, v_sw).astype(BF16)],
            [jnp.where(low, v_sw, zero).astype(BF16), jnp.where(low, zero, v_all).astype(BF16)]]

    q = q_ref[...]
    qn = jnp.concatenate(
        [q[:, s:s + 256] * lax.rsqrt(_seg_mean(q[:, s:s + 256] * q[:, s:s + 256], pm) + EPS)
         for s in (0, 256)], axis=1) * qg_ref[...]
    cos_q = jnp.concatenate([cos_ref[...]] * 4, axis=1)
    sin_q = jnp.concatenate([sin_ref[...]] * 4, axis=1)
    qr = (_rope(qn, cos_q, sin_q) * (HEAD_DIM ** -0.5)).astype(BF16)

    qi = lax.broadcasted_iota(jnp.int32, (2 * WINDOW, 2 * WINDOW), 0) % WINDOW
    kj = lax.broadcasted_iota(jnp.int32, (2 * WINDOW, 2 * WINDOW), 1)
    band = ((kj > qi) & (kj < WINDOW)) | ((kj >= WINDOW) & (kj - WINDOW <= qi))
    top = lax.broadcasted_iota(jnp.int32, (2 * WINDOW, 1), 0) < WINDOW

    for b in range(nq):
        rows = slice(b * WINDOW, (b + 1) * WINDOW)
        win = slice(b * WINDOW, (b + 2) * WINDOW)
        mask = band
        if b == 0:
            mask = band & (kj >= jnp.where(first_block, WINDOW, 0))
        pairs = []
        for j in range(B_KV_HEADS):
            qa = qr[rows, (2 * j) * 128:(2 * j + 1) * 128]
            qb = qr[rows, (2 * j + 1) * 128:(2 * j + 2) * 128]
            qs = jnp.concatenate([qa, qb], axis=0)
            acc = None
            for par in range(2):
                s = lax.dot_general(qs, k_pl[j][par][win], (((1,), (1,)), ((), ())),
                                    preferred_element_type=F32)
                s = jnp.where(mask, s, NEG_BIG)
                sink = jnp.where(top, sinks_ref[4 * j + par], sinks_ref[4 * j + 2 + par])
                m = jnp.maximum(jnp.max(s, axis=-1, keepdims=True), sink)
                pr = jnp.exp(s - m)
                denom = jnp.sum(pr, axis=-1, keepdims=True) + jnp.exp(sink - m)
                pn = (pr * (1.0 / denom)).astype(BF16)
                o = _dot(pn, v_pl[j][par][win])
                acc = o if acc is None else acc + o
            pairs += [acc[0:WINDOW], acc[WINDOW:2 * WINDOW]]
        yb = jnp.concatenate(pairs, axis=1)
        o_ref[rows, :] = _rms(yb, mg_ref[...]).astype(BF16)


def _attn(z, cos2, sin2, sinks, qg, kg, mg, pm, seq_len):
    n = z.shape[0]
    per = ROWS_MIX // WINDOW
    kern = lambda *refs: _attn_kernel(seq_len // ROWS_MIX, *refs)
    return pl.pallas_call(
        kern,
        grid=(n // ROWS_MIX,),
        in_specs=[
            pl.BlockSpec(memory_space=pltpu.SMEM),
            pl.BlockSpec((ROWS_MIX, B_WIDTH), lambda i: (i, COL_Q // B_WIDTH)),
            pl.BlockSpec((ROWS_MIX, 256), lambda i: (i, COL_KV // 256)),
            pl.BlockSpec((WINDOW, 256), lambda i: (jnp.maximum(i * per - 1, 0), COL_KV // 256)),
            pl.BlockSpec((ROWS_MIX, 128), lambda i: (i, 0)),
            pl.BlockSpec((ROWS_MIX, 128), lambda i: (i, 0)),
            pl.BlockSpec((WINDOW, 128), lambda i: (jnp.maximum(i * per - 1, 0), 0)),
            pl.BlockSpec((WINDOW, 128), lambda i: (jnp.maximum(i * per - 1, 0), 0)),
            _const_spec((1, B_WIDTH)),
            _const_spec((1, 128)),
            _const_spec((1, B_WIDTH)),
            _const_spec((256, 256)),
        ],
        out_specs=pl.BlockSpec((ROWS_MIX, B_WIDTH), lambda i: (i, 0)),
        out_shape=jax.ShapeDtypeStruct((n, B_WIDTH), BF16),
        compiler_params=pltpu.CompilerParams(dimension_semantics=("arbitrary",)),
        name="attn",
    )(sinks, z, z, z, cos2, sin2, cos2, sin2, qg, kg, mg, pm)


def _ssm_kernel(blocks_per_seq, ua_ref, ub_ref, kd_ref, wb_ref, wc_ref, astep_ref, atab_ref,
                w1_ref, w2_ref, mg_ref, o_ref, inc_ref, sprev_ref, ya_ref, yb_ref, carry_ref):
    @pl.when((pl.program_id(0) % blocks_per_seq) == 0)
    def _():
        carry_ref[...] = jnp.zeros_like(carry_ref)

    xs = [jnp.concatenate([ua_ref[pl.ds(j, SSM_N, stride=SSM_T), :],
                           ub_ref[pl.ds(j, SSM_N, stride=SSM_T), :]], axis=1).astype(BF16)
          for j in range(SSM_T)]

    inc = _dot(xs[0], wb_ref[0])
    for j in range(1, SSM_T):
        inc = inc + _dot(xs[j], wb_ref[j])
    inc_ref[...] = inc

    row = lax.broadcasted_iota(jnp.int32, (SSM_N, 128), 0)
    n_steps = int(math.log2(SSM_N))
    for lt in range(N_STATE // 128):
        re_cols = pl.ds(lt * 128, 128)
        im_cols = pl.ds(N_STATE + lt * 128, 128)
        xr = inc_ref[:, re_cols]
        xi = inc_ref[:, im_cols]
        for k in range(n_steps):
            sh = 1 << k
            ar = astep_ref[pl.ds(k, 1), re_cols]
            ai = astep_ref[pl.ds(k, 1), im_cols]
            keep = row >= sh
            sr = jnp.where(keep, pltpu.roll(xr, sh, 0), 0.0)
            si = jnp.where(keep, pltpu.roll(xi, sh, 0), 0.0)
            xr, xi = xr + ar * sr - ai * si, xi + ar * si + ai * sr
        cr = carry_ref[:, re_cols]
        ci = carry_ref[:, im_cols]
        tr = atab_ref[:, re_cols]
        ti = atab_ref[:, im_cols]
        xr, xi = xr + tr * cr - ti * ci, xi + tr * ci + ti * cr
        first = row == 0
        sprev_ref[:, re_cols] = jnp.where(first, cr, pltpu.roll(xr, 1, 0)).astype(BF16)
        sprev_ref[:, im_cols] = jnp.where(first, ci, pltpu.roll(xi, 1, 0)).astype(BF16)
        carry_ref[:, re_cols] = xr[SSM_N - 1:SSM_N, :]
        carry_ref[:, im_cols] = xi[SSM_N - 1:SSM_N, :]

    sprev = sprev_ref[...]
    for t in range(SSM_T):
        y = _dot(sprev, wc_ref[t])
        for j in range(t + 1):
            y = y + _dot(xs[j], kd_ref[t - j])
        ya_ref[pl.ds(t, SSM_N, stride=SSM_T), :] = y[:, 0:128]
        yb_ref[pl.ds(t, SSM_N, stride=SSM_T), :] = y[:, 128:256]

    y = jax.nn.gelu(jnp.concatenate([ya_ref[...], yb_ref[...]], axis=1)).astype(BF16)
    yc = _dot(y, w1_ref[...]) * jax.nn.sigmoid(_dot(y, w2_ref[...]))
    o_ref[...] = _rms(yc, mg_ref[...]).astype(BF16)


def _ssm(z, kd, wb, wc, astep, atab, w1, w2, mg, seq_len):
    n = z.shape[0]
    kern = lambda *refs: _ssm_kernel(seq_len // ROWS_SSM, *refs)
    return pl.pallas_call(
        kern,
        grid=(n // ROWS_SSM,),
        in_specs=[
            pl.BlockSpec((ROWS_SSM, 128), lambda i: (i, COL_C // 128)),
            pl.BlockSpec((ROWS_SSM, 128), lambda i: (i, COL_C // 128 + 1)),
            _const_spec((SSM_T, C_WIDTH, C_WIDTH)),
            _const_spec((SSM_T, C_WIDTH, 2 * N_STATE)),
            _const_spec((SSM_T, 2 * N_STATE, C_WIDTH)),
            _const_spec(astep.shape),
            _const_spec((SSM_N, 2 * N_STATE)),
            _const_spec((C_WIDTH, C_WIDTH)),
            _const_spec((C_WIDTH, C_WIDTH)),
            _const_spec((1, C_WIDTH)),
        ],
        out_specs=pl.BlockSpec((ROWS_SSM, C_WIDTH), lambda i: (i, 0)),
        out_shape=jax.ShapeDtypeStruct((n, C_WIDTH), BF16),
        scratch_shapes=[
            pltpu.VMEM((SSM_N, 2 * N_STATE), F32),
            pltpu.VMEM((SSM_N, 2 * N_STATE), BF16),
            pltpu.VMEM((ROWS_SSM, 128), F32),
            pltpu.VMEM((ROWS_SSM, 128), F32),
            pltpu.VMEM((1, 2 * N_STATE), F32),
        ],
        compiler_params=pltpu.CompilerParams(
            dimension_semantics=("arbitrary",), vmem_limit_bytes=VMEM_LIMIT),
        name="ssm",
    )(z, z, kd, wb, wc, astep, atab, w1, w2, mg)


def _cmul(ar, ai, br, bi):
    return ar * br - ai * bi, ar * bi + ai * br


def _ssm_tables(a_re, a_im, log_dt, b_re, b_im, c_re, c_im, d_skip):
    hp = lax.Precision.HIGHEST
    dt = jnp.exp(log_dt)[:, None]
    mag = jnp.exp(a_re * dt)
    lr, li = mag * jnp.cos(a_im * dt), mag * jnp.sin(a_im * dt)
    den = a_re * a_re + a_im * a_im
    fr, fi = _cmul(lr - 1.0, li, a_re / den, -a_im / den)
    bbr, bbi = _cmul(fr[..., None], fi[..., None], b_re, b_im)
    pows = [(jnp.ones_like(lr), jnp.zeros_like(li))]
    for _ in range(SSM_T):
        pows.append(_cmul(pows[-1][0], pows[-1][1], lr, li))
    eye = jnp.eye(C_GROUPS, dtype=F32)

    def blockdiag(m):
        g, x, y = m.shape
        return (eye[:, None, :, None] * m[:, :, None, :]).reshape(g * x, g * y)

    kd, wb, wc = [], [], []
    for d in range(SSM_T):
        mr, mi = _cmul(c_re, c_im, pows[d][0][:, None, :], pows[d][1][:, None, :])
        k = (jnp.einsum('gop,gpi->gio', mr, bbr, precision=hp)
             - jnp.einsum('gop,gpi->gio', mi, bbi, precision=hp))
        if d == 0:
            k = k + d_skip[:, :, None] * jnp.eye(C_GROUP, dtype=F32)[None]
        kd.append(blockdiag(k))
        pr, pi = pows[SSM_T - 1 - d]
        wr, wi = _cmul(pr[..., None], pi[..., None], bbr, bbi)
        wb.append(jnp.concatenate([blockdiag(jnp.swapaxes(wr, 1, 2)),
                                   blockdiag(jnp.swapaxes(wi, 1, 2))], axis=1))
        mr, mi = _cmul(c_re, c_im, pows[d + 1][0][:, None, :], pows[d + 1][1][:, None, :])
        wc.append(jnp.concatenate([blockdiag(jnp.swapaxes(mr, 1, 2)),
                                   blockdiag(-jnp.swapaxes(mi, 1, 2))], axis=0))
    kd = jnp.stack(kd).astype(BF16)
    wb = jnp.stack(wb).astype(BF16)
    wc = jnp.stack(wc).astype(BF16)

    ar, ai = pows[SSM_T][0].reshape(1, N_STATE), pows[SSM_T][1].reshape(1, N_STATE)
    steps = [(ar, ai)]
    for _ in range(int(math.log2(SSM_N)) - 1):
        steps.append(_cmul(steps[-1][0], steps[-1][1], steps[-1][0], steps[-1][1]))
    tr, ti = ar, ai
    for sr, si in steps:
        nr, ni = _cmul(tr, ti, sr, si)
        tr, ti = jnp.concatenate([tr, nr], axis=0), jnp.concatenate([ti, ni], axis=0)
    astep = jnp.concatenate([jnp.concatenate([s[0] for s in steps], axis=0),
                             jnp.concatenate([s[1] for s in steps], axis=0)], axis=1)
    atab = jnp.concatenate([tr, ti], axis=1)
    return kd, wb, wc, astep, atab


def _post_kernel(h_ref, ya_ref, yb_ref, yc_ref, p_ref, wo_ref, g1_ref, w1_ref, w2_ref,
                 g2_ref, wg_ref, wp_ref, o_ref):
    h = h_ref[...]
    h = h + (_dot(ya_ref[...], wo_ref[0:A_WIDTH, :])
             + _dot(yb_ref[...], wo_ref[A_WIDTH:A_WIDTH + B_WIDTH, :])
             + _dot(yc_ref[...], wo_ref[A_WIDTH + B_WIDTH:, :]))
    hn = _rms(h, g1_ref[...]).astype(BF16)
    piece = D_FF // FF_SPLIT
    ff = None
    for c in range(FF_SPLIT):
        a = jnp.maximum(_dot(hn, w1_ref[:, c * piece:(c + 1) * piece]), 0.0)
        t = _dot((a * a).astype(BF16), w2_ref[c * piece:(c + 1) * piece, :])
        ff = t if ff is None else ff + t
    h = h + ff
    gate = jax.nn.sigmoid(_dot(_rms(h, g2_ref[...]).astype(BF16), wg_ref[...]))
    o_ref[...] = h + gate * _dot(p_ref[...].astype(BF16), wp_ref[...])


def _post(h, ya, yb, yc, p, wo, g1, w1, w2, g2, wg, wp):
    n = h.shape[0]
    rb = lambda w: pl.BlockSpec((ROWS_PROJ, w), lambda i: (i, 0))
    return pl.pallas_call(
        _post_kernel,
        grid=(n // ROWS_PROJ,),
        in_specs=[
            rb(D_MODEL), rb(A_WIDTH), rb(B_WIDTH), rb(C_WIDTH), rb(PLE_DIM),
            _const_spec((D_MODEL, D_MODEL)),
            _const_spec((1, D_MODEL)),
            _const_spec((D_MODEL, D_FF)),
            _const_spec((D_FF, D_MODEL)),
            _const_spec((1, D_MODEL)),
            _const_spec((D_MODEL, D_MODEL)),
            _const_spec((PLE_DIM, D_MODEL)),
        ],
        out_specs=rb(D_MODEL),
        out_shape=jax.ShapeDtypeStruct((n, D_MODEL), F32),
        compiler_params=pltpu.CompilerParams(
            dimension_semantics=("arbitrary",), vmem_limit_bytes=VMEM_LIMIT),
        name="post",
    )(h, ya, yb, yc, p, wo, g1, w1, w2, g2, wg, wp)


def _in_col_perm():
    a = np.arange(2 * A_WIDTH).reshape(A_HEADS, 2, HEAD_DIM)
    perm_a = np.concatenate([a[:, 0].reshape(-1), a[:, 1].reshape(-1)])
    return np.concatenate([perm_a, np.arange(2 * A_WIDTH, IN_COLS)])


def kernel(x, p, positions, attn_norm_g, w_in, gmlp_ln_g, gmlp_ln_b, gmlp_ws, gmlp_bs, q_norm_g, k_norm_g, sinks, ssm_a_re, ssm_a_im, ssm_log_dt, ssm_b_re, ssm_b_im, ssm_c_re, ssm_c_im, ssm_d, glu_w1, glu_w2, mix_out_g, w_out, mlp_norm_g, w_ff1, w_ff2, ple_norm_g, w_ple_gate, w_ple_proj):
    bsz, seq_len, _ = x.shape
    depth = w_in.shape[0]
    n = bsz * seq_len

    inv = 1.0 / (ROPE_THETA ** (jnp.arange(0, HEAD_DIM, 2, dtype=F32) / HEAD_DIM))
    ang = positions.astype(F32).reshape(n, 1) * inv
    cos2 = jnp.tile(jnp.cos(ang), (1, 4))
    sin2 = jnp.tile(jnp.concatenate([-jnp.sin(ang), jnp.sin(ang)], axis=1), (1, 2))

    seg = np.arange(256) // HEAD_DIM
    pm = jnp.asarray((seg[:, None] == seg[None, :]) / HEAD_DIM, dtype=BF16)
    perm = _in_col_perm()

    h = x.reshape(n, D_MODEL)
    for i in range(depth):
        row = lambda v: v.reshape(1, -1).astype(F32)
        z = _inproj(h, row(attn_norm_g[i]), w_in[i][:, perm].astype(BF16))
        mg = mix_out_g[i]
        ya = _gmlp(z, gmlp_ws[i].astype(BF16),
                   jnp.repeat(gmlp_bs[i].T, HEAD_DIM, axis=1),
                   row(gmlp_ln_g[i]), row(gmlp_ln_b[i]), row(mg[:A_WIDTH]), pm)
        yb = _attn(z, cos2, sin2, sinks[i],
                   row(jnp.tile(q_norm_g[i], B_Q_HEADS)), row(jnp.tile(k_norm_g[i], B_KV_HEADS)),
                   row(mg[A_WIDTH:A_WIDTH + B_WIDTH]), pm, seq_len)
        tables = _ssm_tables(ssm_a_re[i], ssm_a_im[i], ssm_log_dt[i], ssm_b_re[i], ssm_b_im[i],
                             ssm_c_re[i], ssm_c_im[i], ssm_d[i])
        yc = _ssm(z, *tables, glu_w1[i].astype(BF16), glu_w2[i].astype(BF16),
                  row(mg[A_WIDTH + B_WIDTH:]), seq_len)
        h = _post(h, ya, yb, yc, p[i].reshape(n, PLE_DIM), w_out[i].astype(BF16),
                  row(mlp_norm_g[i]), w_ff1[i].astype(BF16), w_ff2[i].astype(BF16),
                  row(ple_norm_g[i]), w_ple_gate[i].astype(BF16), w_ple_proj[i].astype(BF16))
    return h.reshape(bsz, seq_len, D_MODEL)
```

```python
import functools
import math

import numpy as np
import jax
import jax.numpy as jnp
from jax import lax
from jax.experimental import pallas as pl
from jax.experimental.pallas import tpu as pltpu

F32 = jnp.float32
BF16 = jnp.bfloat16

D_MODEL = 1024
HEAD_DIM = 64
A_WIDTH = 256
A_HEADS = 4
CHUNK = 128
B_WIDTH = 512
B_Q_HEADS = 8
B_KV_HEADS = 2
WINDOW = 128
ROPE_THETA = 10000.0
C_WIDTH = 256
C_GROUP = 16
C_GROUPS = 16
C_STATE = 64
N_STATE = C_GROUPS * C_STATE
IN_COLS = 1536
D_FF = 4096
PLE_DIM = 256
EPS = 1e-6
NEG_BIG = -1e30

COL_A = 0
COL_Q = 512
COL_KV = 1024
COL_C = 1280

ROWS_PROJ = 512
ROWS_MIX = 512
ROWS_SSM = 2048
SSM_T = 8
SSM_N = ROWS_SSM // SSM_T
SSM_LOG_N = int(math.log2(SSM_N))
FF_SPLIT = 4
VMEM_LIMIT = 56 * 1024 * 1024


def _const_spec(shape):
    nd = len(shape)
    return pl.BlockSpec(shape, lambda *_: (0,) * nd, pipeline_mode=pl.Buffered(1))


def _layer_spec(shape, layer):
    nd = len(shape)
    return pl.BlockSpec((None,) + tuple(shape), lambda *_: (layer,) + (0,) * nd,
                        pipeline_mode=pl.Buffered(1))


def _rms(x, g):
    ms = jnp.mean(x * x, axis=-1, keepdims=True)
    return x * lax.rsqrt(ms + EPS) * g


def _dot(a, b):
    return jnp.dot(a, b, preferred_element_type=F32)


def _dot_nt(a, b):
    return lax.dot_general(a, b, (((1,), (1,)), ((), ())), preferred_element_type=F32)


def _split(x):
    hi = x.astype(BF16)
    return hi, (x - hi.astype(F32)).astype(BF16)


def _seg_mean(x, pmat):
    hi, lo = _split(x)
    return _dot(hi, pmat) + _dot(lo, pmat)


def _cmul(ar, ai, br, bi):
    return ar * br - ai * bi, ar * bi + ai * br


def _inproj_kernel(h_ref, g_ref, w_ref, z_ref):
    xn = _rms(h_ref[...], g_ref[...]).astype(BF16)
    z_ref[...] = _dot(xn, w_ref[...])


def _inproj(layer, h, g, w):
    n = h.shape[0]
    return pl.pallas_call(
        _inproj_kernel,
        grid=(n // ROWS_PROJ,),
        in_specs=[
            pl.BlockSpec((ROWS_PROJ, D_MODEL), lambda i: (i, 0)),
            _layer_spec((1, D_MODEL), layer),
            _layer_spec((D_MODEL, IN_COLS), layer),
        ],
        out_specs=pl.BlockSpec((ROWS_PROJ, IN_COLS), lambda i: (i, 0)),
        out_shape=jax.ShapeDtypeStruct((n, IN_COLS), F32),
        compiler_params=pltpu.CompilerParams(
            dimension_semantics=("arbitrary",), vmem_limit_bytes=VMEM_LIMIT),
        name="inproj",
    )(h, g, w)


def _gmlp_kernel(z_ref, w_ref, bs_ref, lng_ref, lnb_ref, mg_ref, pm_ref, o_ref):
    pm = pm_ref[...]
    row = lax.broadcasted_iota(jnp.int32, (CHUNK, CHUNK), 0)
    col = lax.broadcasted_iota(jnp.int32, (CHUNK, CHUNK), 1)
    lane = lax.broadcasted_iota(jnp.int32, (CHUNK, A_WIDTH), 1)
    ws = [jnp.where(row >= col, w_ref[h], jnp.zeros((), BF16)) for h in range(A_HEADS)]
    for c in range(ROWS_MIX // CHUNK):
        rows = pl.ds(c * CHUNK, CHUNK)
        u = jax.nn.gelu(z_ref[rows, 0:A_WIDTH])
        v = jax.nn.gelu(z_ref[rows, A_WIDTH:2 * A_WIDTH])
        vc = v - _seg_mean(v, pm)
        var = _seg_mean(vc * vc, pm)
        vn = (vc * lax.rsqrt(var + EPS) * lng_ref[...] + lnb_ref[...]).astype(BF16)
        sv = bs_ref[...]
        for h in range(A_HEADS):
            in_head = (lane >= h * HEAD_DIM) & (lane < (h + 1) * HEAD_DIM)
            sv = sv + _dot(ws[h], jnp.where(in_head, vn, jnp.zeros((), BF16)))
        o_ref[rows, :] = _rms(u * sv, mg_ref[...]).astype(BF16)


def _gmlp(layer, z, w, bs, lng, lnb, mg, pm):
    n = z.shape[0]
    return pl.pallas_call(
        _gmlp_kernel,
        grid=(n // ROWS_MIX,),
        in_specs=[
            pl.BlockSpec((ROWS_MIX, 2 * A_WIDTH), lambda i: (i, COL_A // (2 * A_WIDTH))),
            _layer_spec((A_HEADS, CHUNK, CHUNK), layer),
            _layer_spec((CHUNK, A_WIDTH), layer),
            _layer_spec((1, A_WIDTH), layer),
            _layer_spec((1, A_WIDTH), layer),
            _layer_spec((1, A_WIDTH), layer),
            _const_spec((A_WIDTH, A_WIDTH)),
        ],
        out_specs=pl.BlockSpec((ROWS_MIX, A_WIDTH), lambda i: (i, 0)),
        out_shape=jax.ShapeDtypeStruct((n, A_WIDTH), BF16),
        compiler_params=pltpu.CompilerParams(dimension_semantics=("arbitrary",)),
        name="gmlp",
    )(z, w, bs, lng, lnb, mg, pm)


def _rope(x, cos, sin_signed):
    width = x.shape[-1]
    lane = lax.broadcasted_iota(jnp.int32, x.shape, 1)
    first_half = (lane % HEAD_DIM) < (HEAD_DIM // 2)
    partner = jnp.where(first_half,
                        pltpu.roll(x, width - HEAD_DIM // 2, 1),
                        pltpu.roll(x, HEAD_DIM // 2, 1))
    return x * cos + partner * sin_signed


def _attn_kernel(layer, blocks_per_seq, sinks_ref, q_ref, kv_ref, kvp_ref, cos_ref, sin_ref,
                 cosp_ref, sinp_ref, qg_ref, kg_ref, mg_ref, pm_ref, o_ref):
    nq = ROWS_MIX // WINDOW
    pm = pm_ref[...]
    pm_kv = pm[0:2 * HEAD_DIM, 0:2 * HEAD_DIM]
    first_block = (pl.program_id(0) % blocks_per_seq) == 0

    k_all = jnp.concatenate([kvp_ref[:, 0:128], kv_ref[:, 0:128]], axis=0)
    v_all = jnp.concatenate([kvp_ref[:, 128:256], kv_ref[:, 128:256]], axis=0)
    cos_all = jnp.concatenate([cosp_ref[...], cos_ref[...]], axis=0)
    sin_all = jnp.concatenate([sinp_ref[...], sin_ref[...]], axis=0)
    k_all = k_all * lax.rsqrt(_seg_mean(k_all * k_all, pm_kv) + EPS) * kg_ref[...]
    k_all = _rope(k_all, cos_all, sin_all)
    lane_kv = lax.broadcasted_iota(jnp.int32, k_all.shape, 1)
    low = lane_kv < HEAD_DIM
    k_sw = pltpu.roll(k_all, HEAD_DIM, 1)
    v_sw = pltpu.roll(v_all, HEAD_DIM, 1)
    zero = jnp.zeros((), F32)
    k_pl = [[jnp.where(low, k_all, zero).astype(BF16), jnp.where(low, zero, k_sw).astype(BF16)],
            [jnp.where(low, k_sw, zero).astype(BF16), jnp.where(low, zero, k_all).astype(BF16)]]
    v_pl = [[jnp.where(low, v_all, zero).astype(BF16), jnp.where(low, zero, v_sw).astype(BF16)],
            [jnp.where(low, v_sw, zero).astype(BF16), jnp.where(low, zero, v_all).astype(BF16)]]

    q = q_ref[...]
    qn = jnp.concatenate(
        [q[:, s:s + 256] * lax.rsqrt(_seg_mean(q[:, s:s + 256] * q[:, s:s + 256], pm) + EPS)
         for s in (0, 256)], axis=1) * qg_ref[...]
    cos_q = jnp.concatenate([cos_ref[...]] * 4, axis=1)
    sin_q = jnp.concatenate([sin_ref[...]] * 4, axis=1)
    qr = (_rope(qn, cos_q, sin_q) * (HEAD_DIM ** -0.5)).astype(BF16)

    qi = lax.broadcasted_iota(jnp.int32, (2 * WINDOW, 2 * WINDOW), 0) % WINDOW
    kj = lax.broadcasted_iota(jnp.int32, (2 * WINDOW, 2 * WINDOW), 1)
    band = ((kj > qi) & (kj < WINDOW)) | ((kj >= WINDOW) & (kj - WINDOW <= qi))
    top = lax.broadcasted_iota(jnp.int32, (2 * WINDOW, 1), 0) < WINDOW

    for b in range(nq):
        rows = slice(b * WINDOW, (b + 1) * WINDOW)
        win = slice(b * WINDOW, (b + 2) * WINDOW)
        mask = band
        if b == 0:
            mask = band & (kj >= jnp.where(first_block, WINDOW, 0))
        pairs = []
        for j in range(B_KV_HEADS):
            qa = qr[rows, (2 * j) * 128:(2 * j + 1) * 128]
            qb = qr[rows, (2 * j + 1) * 128:(2 * j + 2) * 128]
            qs = jnp.concatenate([qa, qb], axis=0)
            acc = None
            for par in range(2):
                s = _dot_nt(qs, k_pl[j][par][win])
                s = jnp.where(mask, s, NEG_BIG)
                sink = jnp.where(top, sinks_ref[layer, 4 * j + par],
                                 sinks_ref[layer, 4 * j + 2 + par])
                m = jnp.maximum(jnp.max(s, axis=-1, keepdims=True), sink)
                pr = jnp.exp(s - m)
                denom = jnp.sum(pr, axis=-1, keepdims=True) + jnp.exp(sink - m)
                pn = (pr * (1.0 / denom)).astype(BF16)
                o = _dot(pn, v_pl[j][par][win])
                acc = o if acc is None else acc + o
            pairs += [acc[0:WINDOW], acc[WINDOW:2 * WINDOW]]
        yb = jnp.concatenate(pairs, axis=1)
        o_ref[rows, :] = _rms(yb, mg_ref[...]).astype(BF16)


def _attn(layer, z, cos2, sin2, sinks, qg, kg, mg, pm, seq_len):
    n = z.shape[0]
    per = ROWS_MIX // WINDOW
    prev = lambda i: jnp.maximum(i * per - 1, 0)
    return pl.pallas_call(
        functools.partial(_attn_kernel, layer, seq_len // ROWS_MIX),
        grid=(n // ROWS_MIX,),
        in_specs=[
            pl.BlockSpec(memory_space=pltpu.SMEM),
            pl.BlockSpec((ROWS_MIX, B_WIDTH), lambda i: (i, COL_Q // B_WIDTH)),
            pl.BlockSpec((ROWS_MIX, 256), lambda i: (i, COL_KV // 256)),
            pl.BlockSpec((WINDOW, 256), lambda i: (prev(i), COL_KV // 256)),
            pl.BlockSpec((ROWS_MIX, 128), lambda i: (i, 0)),
            pl.BlockSpec((ROWS_MIX, 128), lambda i: (i, 0)),
            pl.BlockSpec((WINDOW, 128), lambda i: (prev(i), 0)),
            pl.BlockSpec((WINDOW, 128), lambda i: (prev(i), 0)),
            _layer_spec((1, B_WIDTH), layer),
            _layer_spec((1, 128), layer),
            _layer_spec((1, B_WIDTH), layer),
            _const_spec((256, 256)),
        ],
        out_specs=pl.BlockSpec((ROWS_MIX, B_WIDTH), lambda i: (i, 0)),
        out_shape=jax.ShapeDtypeStruct((n, B_WIDTH), BF16),
        compiler_params=pltpu.CompilerParams(dimension_semantics=("arbitrary",)),
        name="attn",
    )(sinks, z, z, z, cos2, sin2, cos2, sin2, qg, kg, mg, pm)


def _tables_kernel(are_ref, aim_ref, ldt_ref, bre_ref, bim_ref, cre_ref, cim_ref, d_ref,
                   kd_ref, wb_ref, wct_ref, astep_ref, atab_ref):
    a_re, a_im = are_ref[...], aim_ref[...]
    dt = jnp.exp(ldt_ref[...])
    mag = jnp.exp(a_re * dt)
    lr, li = mag * jnp.cos(a_im * dt), mag * jnp.sin(a_im * dt)
    den = a_re * a_re + a_im * a_im
    fr, fi = _cmul(lr - 1.0, li, a_re / den, -a_im / den)
    bbr, bbi = _cmul(fr, fi, bre_ref[...], bim_ref[...])
    c_re, c_im = cre_ref[...], cim_ref[...]

    row_g = lax.broadcasted_iota(jnp.int32, (C_WIDTH, N_STATE), 0) // C_GROUP
    col_g = lax.broadcasted_iota(jnp.int32, (C_WIDTH, N_STATE), 1) // C_STATE
    same_group = row_g == col_g

    def blockdiag(m):
        return jnp.where(same_group, jnp.concatenate([m] * C_GROUPS, axis=0), 0.0)

    def state_matrix(mr, mi):
        return jnp.concatenate([blockdiag(mr), blockdiag(mi)], axis=1)

    pows = [(jnp.ones_like(lr), jnp.zeros_like(li))]
    for _ in range(SSM_T):
        pows.append(_cmul(pows[-1][0], pows[-1][1], lr, li))

    c0_hi, c0_lo = _split(state_matrix(c_re, -c_im))
    rr = lax.broadcasted_iota(jnp.int32, (C_WIDTH, C_WIDTH), 0)
    cc = lax.broadcasted_iota(jnp.int32, (C_WIDTH, C_WIDTH), 1)
    for d in range(SSM_T):
        wr, wi = _cmul(pows[d][0], pows[d][1], bbr, bbi)
        w_full = state_matrix(wr, wi)
        wb_ref[SSM_T - 1 - d] = w_full.astype(BF16)
        w_hi, w_lo = _split(w_full)
        k = _dot_nt(w_hi, c0_hi) + _dot_nt(w_hi, c0_lo) + _dot_nt(w_lo, c0_hi)
        if d == 0:
            k = k + jnp.where(rr == cc, d_ref[...], 0.0)
        kd_ref[d] = k.astype(BF16)
        mr, mi = _cmul(pows[d + 1][0], pows[d + 1][1], c_re, c_im)
        wct_ref[d] = state_matrix(mr, -mi).astype(BF16)

    re_cols, im_cols = pl.ds(0, N_STATE), pl.ds(N_STATE, N_STATE)
    sr, si = pows[SSM_T]
    atab_ref[0:1, re_cols] = sr
    atab_ref[0:1, im_cols] = si
    for k in range(SSM_LOG_N):
        span = 1 << k
        astep_ref[pl.ds(k, 1), re_cols] = sr
        astep_ref[pl.ds(k, 1), im_cols] = si
        tr, ti = _cmul(atab_ref[0:span, re_cols], atab_ref[0:span, im_cols], sr, si)
        atab_ref[pl.ds(span, span), re_cols] = tr
        atab_ref[pl.ds(span, span), im_cols] = ti
        sr, si = _cmul(sr, si, sr, si)


def _tables(a_re, a_im, ldt, b_re, b_im, c_re, c_im, d_skip):
    depth = a_re.shape[0]
    vec = lambda w: pl.BlockSpec((None, 1, w), lambda l: (l, 0, 0))
    mat = lambda r, c: pl.BlockSpec((None, r, c), lambda l: (l, 0, 0))
    cube = lambda r, c: pl.BlockSpec((None, SSM_T, r, c), lambda l: (l, 0, 0, 0))
    return pl.pallas_call(
        _tables_kernel,
        grid=(depth,),
        in_specs=[vec(N_STATE), vec(N_STATE), vec(N_STATE),
                  mat(C_GROUP, N_STATE), mat(C_GROUP, N_STATE),
                  mat(C_GROUP, N_STATE), mat(C_GROUP, N_STATE), vec(C_WIDTH)],
        out_specs=[cube(C_WIDTH, C_WIDTH), cube(C_WIDTH, 2 * N_STATE), cube(C_WIDTH, 2 * N_STATE),
                   mat(SSM_LOG_N, 2 * N_STATE), mat(SSM_N, 2 * N_STATE)],
        out_shape=[jax.ShapeDtypeStruct((depth, SSM_T, C_WIDTH, C_WIDTH), BF16),
                   jax.ShapeDtypeStruct((depth, SSM_T, C_WIDTH, 2 * N_STATE), BF16),
                   jax.ShapeDtypeStruct((depth, SSM_T, C_WIDTH, 2 * N_STATE), BF16),
                   jax.ShapeDtypeStruct((depth, SSM_LOG_N, 2 * N_STATE), F32),
                   jax.ShapeDtypeStruct((depth, SSM_N, 2 * N_STATE), F32)],
        compiler_params=pltpu.CompilerParams(
            dimension_semantics=("arbitrary",), vmem_limit_bytes=VMEM_LIMIT),
        name="ssm_tables",
    )(a_re, a_im, ldt, b_re, b_im, c_re, c_im, d_skip)


def _ssm_kernel(blocks_per_seq, ua_ref, ub_ref, kd_ref, wb_ref, wct_ref, astep_ref, atab_ref,
                w1_ref, w2_ref, mg_ref, o_ref, inc_ref, sprev_ref, ya_ref, yb_ref, carry_ref):
    @pl.when((pl.program_id(0) % blocks_per_seq) == 0)
    def _():
        carry_ref[...] = jnp.zeros_like(carry_ref)

    xs = [jnp.concatenate([ua_ref[pl.ds(j, SSM_N, stride=SSM_T), :],
                           ub_ref[pl.ds(j, SSM_N, stride=SSM_T), :]], axis=1).astype(BF16)
          for j in range(SSM_T)]

    inc = _dot(xs[0], wb_ref[0])
    for j in range(1, SSM_T):
        inc = inc + _dot(xs[j], wb_ref[j])
    inc_ref[...] = inc

    row = lax.broadcasted_iota(jnp.int32, (SSM_N, 128), 0)
    for lt in range(N_STATE // 128):
        re_cols = pl.ds(lt * 128, 128)
        im_cols = pl.ds(N_STATE + lt * 128, 128)
        xr = inc_ref[:, re_cols]
        xi = inc_ref[:, im_cols]
        for k in range(SSM_LOG_N):
            sh = 1 << k
            ar = astep_ref[pl.ds(k, 1), re_cols]
            ai = astep_ref[pl.ds(k, 1), im_cols]
            keep = row >= sh
            sr = jnp.where(keep, pltpu.roll(xr, sh, 0), 0.0)
            si = jnp.where(keep, pltpu.roll(xi, sh, 0), 0.0)
            xr, xi = xr + ar * sr - ai * si, xi + ar * si + ai * sr
        cr = carry_ref[:, re_cols]
        ci = carry_ref[:, im_cols]
        tr = atab_ref[:, re_cols]
        ti = atab_ref[:, im_cols]
        xr, xi = xr + tr * cr - ti * ci, xi + tr * ci + ti * cr
        first = row == 0
        sprev_ref[:, re_cols] = jnp.where(first, cr, pltpu.roll(xr, 1, 0)).astype(BF16)
        sprev_ref[:, im_cols] = jnp.where(first, ci, pltpu.roll(xi, 1, 0)).astype(BF16)
        carry_ref[:, re_cols] = xr[SSM_N - 1:SSM_N, :]
        carry_ref[:, im_cols] = xi[SSM_N - 1:SSM_N, :]

    sprev = sprev_ref[...]
    for t in range(SSM_T):
        y = _dot_nt(sprev, wct_ref[t])
        for j in range(t + 1):
            y = y + _dot(xs[j], kd_ref[t - j])
        ya_ref[pl.ds(t, SSM_N, stride=SSM_T), :] = y[:, 0:128]
        yb_ref[pl.ds(t, SSM_N, stride=SSM_T), :] = y[:, 128:256]

    y = jax.nn.gelu(jnp.concatenate([ya_ref[...], yb_ref[...]], axis=1)).astype(BF16)
    yc = _dot(y, w1_ref[...]) * jax.nn.sigmoid(_dot(y, w2_ref[...]))
    o_ref[...] = _rms(yc, mg_ref[...]).astype(BF16)


def _ssm(layer, z, kd, wb, wct, astep, atab, w1, w2, mg, seq_len):
    n = z.shape[0]
    return pl.pallas_call(
        functools.partial(_ssm_kernel, seq_len // ROWS_SSM),
        grid=(n // ROWS_SSM,),
        in_specs=[
            pl.BlockSpec((ROWS_SSM, 128), lambda i: (i, COL_C // 128)),
            pl.BlockSpec((ROWS_SSM, 128), lambda i: (i, COL_C // 128 + 1)),
            _layer_spec((SSM_T, C_WIDTH, C_WIDTH), layer),
            _layer_spec((SSM_T, C_WIDTH, 2 * N_STATE), layer),
            _layer_spec((SSM_T, C_WIDTH, 2 * N_STATE), layer),
            _layer_spec((SSM_LOG_N, 2 * N_STATE), layer),
            _layer_spec((SSM_N, 2 * N_STATE), layer),
            _layer_spec((C_WIDTH, C_WIDTH), layer),
            _layer_spec((C_WIDTH, C_WIDTH), layer),
            _layer_spec((1, C_WIDTH), layer),
        ],
        out_specs=pl.BlockSpec((ROWS_SSM, C_WIDTH), lambda i: (i, 0)),
        out_shape=jax.ShapeDtypeStruct((n, C_WIDTH), BF16),
        scratch_shapes=[
            pltpu.VMEM((SSM_N, 2 * N_STATE), F32),
            pltpu.VMEM((SSM_N, 2 * N_STATE), BF16),
            pltpu.VMEM((ROWS_SSM, 128), F32),
            pltpu.VMEM((ROWS_SSM, 128), F32),
            pltpu.VMEM((1, 2 * N_STATE), F32),
        ],
        compiler_params=pltpu.CompilerParams(
            dimension_semantics=("arbitrary",), vmem_limit_bytes=VMEM_LIMIT),
        name="ssm",
    )(z, z, kd, wb, wct, astep, atab, w1, w2, mg)


def _post_kernel(h_ref, ya_ref, yb_ref, yc_ref, p_ref, wo_ref, g1_ref, w1_ref, w2_ref,
                 g2_ref, wg_ref, wp_ref, o_ref):
    h = h_ref[...]
    h = h + (_dot(ya_ref[...], wo_ref[0:A_WIDTH, :])
             + _dot(yb_ref[...], wo_ref[A_WIDTH:A_WIDTH + B_WIDTH, :])
             + _dot(yc_ref[...], wo_ref[A_WIDTH + B_WIDTH:, :]))
    hn = _rms(h, g1_ref[...]).astype(BF16)
    piece = D_FF // FF_SPLIT
    ff = None
    for c in range(FF_SPLIT):
        a = jnp.maximum(_dot(hn, w1_ref[:, c * piece:(c + 1) * piece]), 0.0)
        t = _dot((a * a).astype(BF16), w2_ref[c * piece:(c + 1) * piece, :])
        ff = t if ff is None else ff + t
    h = h + ff
    gate = jax.nn.sigmoid(_dot(_rms(h, g2_ref[...]).astype(BF16), wg_ref[...]))
    o_ref[...] = h + gate * _dot(p_ref[...].astype(BF16), wp_ref[...])


def _post(layer, h, ya, yb, yc, p, wo, g1, w1, w2, g2, wg, wp):
    n = h.shape[0]
    rb = lambda w: pl.BlockSpec((ROWS_PROJ, w), lambda i: (i, 0))
    return pl.pallas_call(
        _post_kernel,
        grid=(n // ROWS_PROJ,),
        in_specs=[
            rb(D_MODEL), rb(A_WIDTH), rb(B_WIDTH), rb(C_WIDTH),
            pl.BlockSpec((None, ROWS_PROJ, PLE_DIM), lambda i: (layer, i, 0)),
            _layer_spec((D_MODEL, D_MODEL), layer),
            _layer_spec((1, D_MODEL), layer),
            _layer_spec((D_MODEL, D_FF), layer),
            _layer_spec((D_FF, D_MODEL), layer),
            _layer_spec((1, D_MODEL), layer),
            _layer_spec((D_MODEL, D_MODEL), layer),
            _layer_spec((PLE_DIM, D_MODEL), layer),
        ],
        out_specs=rb(D_MODEL),
        out_shape=jax.ShapeDtypeStruct((n, D_MODEL), F32),
        compiler_params=pltpu.CompilerParams(
            dimension_semantics=("arbitrary",), vmem_limit_bytes=VMEM_LIMIT),
        name="post",
    )(h, ya, yb, yc, p, wo, g1, w1, w2, g2, wg, wp)


def _in_col_perm():
    a = np.arange(2 * A_WIDTH).reshape(A_HEADS, 2, HEAD_DIM)
    perm_a = np.concatenate([a[:, 0].reshape(-1), a[:, 1].reshape(-1)])
    return np.concatenate([perm_a, np.arange(2 * A_WIDTH, IN_COLS)])


def kernel(x, p, positions, attn_norm_g, w_in, gmlp_ln_g, gmlp_ln_b, gmlp_ws, gmlp_bs, q_norm_g, k_norm_g, sinks, ssm_a_re, ssm_a_im, ssm_log_dt, ssm_b_re, ssm_b_im, ssm_c_re, ssm_c_im, ssm_d, glu_w1, glu_w2, mix_out_g, w_out, mlp_norm_g, w_ff1, w_ff2, ple_norm_g, w_ple_gate, w_ple_proj):
    bsz, seq_len, _ = x.shape
    depth = w_in.shape[0]
    n = bsz * seq_len

    inv = 1.0 / (ROPE_THETA ** (jnp.arange(0, HEAD_DIM, 2, dtype=F32) / HEAD_DIM))
    ang = positions.astype(F32).reshape(n, 1) * inv
    cos2 = jnp.tile(jnp.cos(ang), (1, 4))
    sin2 = jnp.tile(jnp.concatenate([-jnp.sin(ang), jnp.sin(ang)], axis=1), (1, 2))

    seg = np.arange(256) // HEAD_DIM
    pm = jnp.asarray((seg[:, None] == seg[None, :]) / HEAD_DIM, dtype=BF16)

    rows = lambda v: v.reshape(depth, 1, -1).astype(F32)
    g_attn, g_mlp, g_ple = rows(attn_norm_g), rows(mlp_norm_g), rows(ple_norm_g)
    ln_g, ln_b = rows(gmlp_ln_g), rows(gmlp_ln_b)
    mg_a = rows(mix_out_g[:, :A_WIDTH])
    mg_b = rows(mix_out_g[:, A_WIDTH:A_WIDTH + B_WIDTH])
    mg_c = rows(mix_out_g[:, A_WIDTH + B_WIDTH:])
    qg = rows(jnp.tile(q_norm_g, (1, B_Q_HEADS)))
    kg = rows(jnp.tile(k_norm_g, (1, B_KV_HEADS)))
    bs = jnp.repeat(jnp.swapaxes(gmlp_bs, 1, 2), HEAD_DIM, axis=2)
    w_in_b = w_in[:, :, _in_col_perm()].astype(BF16)
    ws_b, w_out_b = gmlp_ws.astype(BF16), w_out.astype(BF16)
    w_ff1_b, w_ff2_b = w_ff1.astype(BF16), w_ff2.astype(BF16)
    w_gate_b, w_proj_b = w_ple_gate.astype(BF16), w_ple_proj.astype(BF16)
    glu1_b, glu2_b = glu_w1.astype(BF16), glu_w2.astype(BF16)
    p2 = p.reshape(depth, n, PLE_DIM)

    kd, wb, wct, astep, atab = _tables(
        rows(ssm_a_re), rows(ssm_a_im), rows(jnp.repeat(ssm_log_dt, C_STATE, axis=1)),
        jnp.transpose(ssm_b_re, (0, 3, 1, 2)).reshape(depth, C_GROUP, N_STATE),
        jnp.transpose(ssm_b_im, (0, 3, 1, 2)).reshape(depth, C_GROUP, N_STATE),
        jnp.transpose(ssm_c_re, (0, 2, 1, 3)).reshape(depth, C_GROUP, N_STATE),
        jnp.transpose(ssm_c_im, (0, 2, 1, 3)).reshape(depth, C_GROUP, N_STATE),
        rows(ssm_d))

    h = x.reshape(n, D_MODEL)
    for i in range(depth):
        z = _inproj(i, h, g_attn, w_in_b)
        ya = _gmlp(i, z, ws_b, bs, ln_g, ln_b, mg_a, pm)
        yb = _attn(i, z, cos2, sin2, sinks, qg, kg, mg_b, pm, seq_len)
        yc = _ssm(i, z, kd, wb, wct, astep, atab, glu1_b, glu2_b, mg_c, seq_len)
        h = _post(i, h, ya, yb, yc, p2, w_out_b, g_mlp, w_ff1_b, w_ff2_b, g_ple, w_gate_b, w_proj_b)
    return h.reshape(bsz, seq_len, D_MODEL)
```

```python
import functools
import math

import numpy as np
import jax
import jax.numpy as jnp
from jax import lax
from jax.experimental import pallas as pl
from jax.experimental.pallas import tpu as pltpu

F32 = jnp.float32
BF16 = jnp.bfloat16

D_MODEL = 1024
HEAD_DIM = 64
A_WIDTH = 256
A_HEADS = 4
CHUNK = 128
B_WIDTH = 512
B_Q_HEADS = 8
B_KV_HEADS = 2
WINDOW = 128
ROPE_THETA = 10000.0
C_WIDTH = 256
C_GROUP = 16
C_GROUPS = 16
C_STATE = 64
N_STATE = C_GROUPS * C_STATE
IN_COLS = 1536
D_FF = 4096
PLE_DIM = 256
EPS = 1e-6
NEG_BIG = -1e30
LANES = 128

COL_A = 0
COL_Q = 512
COL_K = 1024
COL_V = 1152
COL_C = 1280

ROWS_PROJ = 512
ROWS_MIX = 512
ROWS_SSM = 2048
SSM_T = 8
SSM_N = ROWS_SSM // SSM_T
SSM_LOG_N = int(math.log2(SSM_N))
FF_SPLIT = 4
VMEM_LIMIT = 56 * 1024 * 1024


def _const_spec(shape):
    nd = len(shape)
    return pl.BlockSpec(shape, lambda *_: (0,) * nd, pipeline_mode=pl.Buffered(1))


def _layer_spec(shape, layer):
    nd = len(shape)
    return pl.BlockSpec((None,) + tuple(shape), lambda *_: (layer,) + (0,) * nd,
                        pipeline_mode=pl.Buffered(1))


def _rms(x, g):
    ms = jnp.mean(x * x, axis=-1, keepdims=True)
    return x * lax.rsqrt(ms + EPS) * g


def _dot(a, b):
    return jnp.dot(a, b, preferred_element_type=F32)


def _dot_nt(a, b):
    return lax.dot_general(a, b, (((1,), (1,)), ((), ())), preferred_element_type=F32)


def _split(x):
    hi = x.astype(BF16)
    return hi, (x - hi.astype(F32)).astype(BF16)


def _seg_mean(x, pmat):
    hi, lo = _split(x)
    return _dot(hi, pmat) + _dot(lo, pmat)


def _cmul(ar, ai, br, bi):
    return ar * br - ai * bi, ar * bi + ai * br


def _rope(x, cos, sin_signed):
    width = x.shape[-1]
    lane = lax.broadcasted_iota(jnp.int32, x.shape, 1)
    first_half = (lane % HEAD_DIM) < (HEAD_DIM // 2)
    partner = jnp.where(first_half,
                        pltpu.roll(x, width - HEAD_DIM // 2, 1),
                        pltpu.roll(x, HEAD_DIM // 2, 1))
    return x * cos + partner * sin_signed


def _gmlp_chunk(z_ref, rows, ws, bs_ref, lng_ref, lnb_ref, mg_ref, pm):
    lane = lax.broadcasted_iota(jnp.int32, (CHUNK, A_WIDTH), 1)
    u = jax.nn.gelu(z_ref[rows, COL_A:COL_A + A_WIDTH])
    v = jax.nn.gelu(z_ref[rows, COL_A + A_WIDTH:COL_A + 2 * A_WIDTH])
    vc = v - _seg_mean(v, pm)
    var = _seg_mean(vc * vc, pm)
    vn = (vc * lax.rsqrt(var + EPS) * lng_ref[...] + lnb_ref[...]).astype(BF16)
    sv = bs_ref[...]
    for h in range(A_HEADS):
        in_head = (lane >= h * HEAD_DIM) & (lane < (h + 1) * HEAD_DIM)
        sv = sv + _dot(ws[h], jnp.where(in_head, vn, jnp.zeros((), BF16)))
    return _rms(u * sv, mg_ref[...]).astype(BF16)


def _place_heads(x):
    low = lax.broadcasted_iota(jnp.int32, x.shape, 1) < HEAD_DIM
    sw = pltpu.roll(x, HEAD_DIM, 1)
    zero = jnp.zeros((), F32)
    return [[jnp.where(low, x, zero).astype(BF16), jnp.where(low, zero, sw).astype(BF16)],
            [jnp.where(low, sw, zero).astype(BF16), jnp.where(low, zero, x).astype(BF16)]]


def _mix_kernel(layer, blocks_per_seq, sinks_ref, h_ref, cos_ref, sin_ref, g_ref, w_ref,
                ws_ref, bs_ref, lng_ref, lnb_ref, mga_ref, qg_ref, kg_ref, mgb_ref, pm_ref,
                ya_ref, yb_ref, zc_ref, z_ref, kprev_ref, vprev_ref):
    nq = ROWS_MIX // WINDOW
    first_block = (pl.program_id(0) % blocks_per_seq) == 0

    @pl.when(first_block)
    def _():
        kprev_ref[...] = jnp.zeros_like(kprev_ref)
        vprev_ref[...] = jnp.zeros_like(vprev_ref)

    xn = _rms(h_ref[...], g_ref[...]).astype(BF16)
    z_ref[...] = _dot(xn, w_ref[...])
    zc_ref[...] = z_ref[:, COL_C:COL_C + C_WIDTH]

    pm = pm_ref[...]
    pm_kv = pm[0:LANES, 0:LANES]

    row = lax.broadcasted_iota(jnp.int32, (CHUNK, CHUNK), 0)
    col = lax.broadcasted_iota(jnp.int32, (CHUNK, CHUNK), 1)
    ws = [jnp.where(row >= col, ws_ref[h], jnp.zeros((), BF16)) for h in range(A_HEADS)]
    for c in range(ROWS_MIX // CHUNK):
        rows = pl.ds(c * CHUNK, CHUNK)
        ya_ref[rows, :] = _gmlp_chunk(z_ref, rows, ws, bs_ref, lng_ref, lnb_ref, mga_ref, pm)

    k_cur = z_ref[:, COL_K:COL_K + LANES]
    k_cur = k_cur * lax.rsqrt(_seg_mean(k_cur * k_cur, pm_kv) + EPS) * kg_ref[...]
    k_cur = _rope(k_cur, cos_ref[...], sin_ref[...])
    v_cur = z_ref[:, COL_V:COL_V + LANES]
    k_all = jnp.concatenate([kprev_ref[...], k_cur], axis=0)
    v_all = jnp.concatenate([vprev_ref[...], v_cur], axis=0)
    kprev_ref[...] = k_cur[ROWS_MIX - WINDOW:, :]
    vprev_ref[...] = v_cur[ROWS_MIX - WINDOW:, :]
    k_pl = _place_heads(k_all)
    v_pl = _place_heads(v_all)

    q = z_ref[:, COL_Q:COL_Q + B_WIDTH]
    qn = jnp.concatenate(
        [q[:, s:s + 256] * lax.rsqrt(_seg_mean(q[:, s:s + 256] * q[:, s:s + 256], pm) + EPS)
         for s in (0, 256)], axis=1) * qg_ref[...]
    cos_q = jnp.concatenate([cos_ref[...]] * 4, axis=1)
    sin_q = jnp.concatenate([sin_ref[...]] * 4, axis=1)
    qr = (_rope(qn, cos_q, sin_q) * (HEAD_DIM ** -0.5)).astype(BF16)

    qi = lax.broadcasted_iota(jnp.int32, (2 * WINDOW, WINDOW), 0) % WINDOW
    kc = lax.broadcasted_iota(jnp.int32, (2 * WINDOW, WINDOW), 1)
    from_cur = kc <= qi
    top = lax.broadcasted_iota(jnp.int32, (2 * WINDOW, 1), 0) < WINDOW
    prev_ok = kc >= jnp.where(first_block, WINDOW, 0)
    zero = jnp.zeros((), F32)

    for b in range(nq):
        rows = slice(b * WINDOW, (b + 1) * WINDOW)
        win = slice(b * WINDOW, (b + 2) * WINDOW)
        pairs = []
        for j in range(B_KV_HEADS):
            qa = qr[rows, (2 * j) * LANES:(2 * j + 1) * LANES]
            qb = qr[rows, (2 * j + 1) * LANES:(2 * j + 2) * LANES]
            qs = jnp.concatenate([qa, qb], axis=0)
            acc = None
            for slot in range(2):
                s = _dot_nt(qs, k_pl[j][slot][win])
                s_prev = s[:, 0:WINDOW]
                if b == 0:
                    s_prev = jnp.where(prev_ok, s_prev, NEG_BIG)
                s = jnp.where(from_cur, s[:, WINDOW:], s_prev)
                sink = jnp.where(top, sinks_ref[layer, 4 * j + slot],
                                 sinks_ref[layer, 4 * j + 2 + slot])
                m = jnp.maximum(jnp.max(s, axis=-1, keepdims=True), sink)
                pr = jnp.exp(s - m)
                denom = jnp.sum(pr, axis=-1, keepdims=True) + jnp.exp(sink - m)
                pn = pr * (1.0 / denom)
                p2 = jnp.concatenate([jnp.where(from_cur, zero, pn),
                                      jnp.where(from_cur, pn, zero)], axis=1).astype(BF16)
                o = _dot(p2, v_pl[j][slot][win])
                acc = o if acc is None else acc + o
            pairs += [acc[0:WINDOW], acc[WINDOW:2 * WINDOW]]
        yb = jnp.concatenate(pairs, axis=1)
        yb_ref[rows, :] = _rms(yb, mgb_ref[...]).astype(BF16)


def _mix(layer, h, cos2, sin2, sinks, g, w, ws, bs, lng, lnb, mga, qg, kg, mgb, pm, seq_len):
    n = h.shape[0]
    rb = lambda w_: pl.BlockSpec((ROWS_MIX, w_), lambda i: (i, 0))
    return pl.pallas_call(
        functools.partial(_mix_kernel, layer, seq_len // ROWS_MIX),
        grid=(n // ROWS_MIX,),
        in_specs=[
            pl.BlockSpec(memory_space=pltpu.SMEM),
            rb(D_MODEL), rb(LANES), rb(LANES),
            _layer_spec((1, D_MODEL), layer),
            _layer_spec((D_MODEL, IN_COLS), layer),
            _layer_spec((A_HEADS, CHUNK, CHUNK), layer),
            _layer_spec((CHUNK, A_WIDTH), layer),
            _layer_spec((1, A_WIDTH), layer),
            _layer_spec((1, A_WIDTH), layer),
            _layer_spec((1, A_WIDTH), layer),
            _layer_spec((1, B_WIDTH), layer),
            _layer_spec((1, LANES), layer),
            _layer_spec((1, B_WIDTH), layer),
            _const_spec((256, 256)),
        ],
        out_specs=[rb(A_WIDTH), rb(B_WIDTH), rb(C_WIDTH)],
        out_shape=[jax.ShapeDtypeStruct((n, A_WIDTH), BF16),
                   jax.ShapeDtypeStruct((n, B_WIDTH), BF16),
                   jax.ShapeDtypeStruct((n, C_WIDTH), F32)],
        scratch_shapes=[
            pltpu.VMEM((ROWS_MIX, IN_COLS), F32),
            pltpu.VMEM((WINDOW, LANES), F32),
            pltpu.VMEM((WINDOW, LANES), F32),
        ],
        compiler_params=pltpu.CompilerParams(
            dimension_semantics=("arbitrary",), vmem_limit_bytes=VMEM_LIMIT),
        name="mix",
    )(sinks, h, cos2, sin2, g, w, ws, bs, lng, lnb, mga, qg, kg, mgb, pm)


def _tables_kernel(are_ref, aim_ref, ldt_ref, bre_ref, bim_ref, cre_ref, cim_ref, d_ref,
                   kd_ref, wb_ref, wct_ref, astep_ref, atab_ref):
    a_re, a_im = are_ref[...], aim_ref[...]
    dt = jnp.exp(ldt_ref[...])
    mag = jnp.exp(a_re * dt)
    lr, li = mag * jnp.cos(a_im * dt), mag * jnp.sin(a_im * dt)
    den = a_re * a_re + a_im * a_im
    fr, fi = _cmul(lr - 1.0, li, a_re / den, -a_im / den)
    bbr, bbi = _cmul(fr, fi, bre_ref[...], bim_ref[...])
    c_re, c_im = cre_ref[...], cim_ref[...]

    row_g = lax.broadcasted_iota(jnp.int32, (C_WIDTH, N_STATE), 0) // C_GROUP
    col_g = lax.broadcasted_iota(jnp.int32, (C_WIDTH, N_STATE), 1) // C_STATE
    same_group = row_g == col_g

    def blockdiag(m):
        return jnp.where(same_group, jnp.concatenate([m] * C_GROUPS, axis=0), 0.0)

    def state_matrix(mr, mi):
        return jnp.concatenate([blockdiag(mr), blockdiag(mi)], axis=1)

    pows = [(jnp.ones_like(lr), jnp.zeros_like(li))]
    for _ in range(SSM_T):
        pows.append(_cmul(pows[-1][0], pows[-1][1], lr, li))

    c0_hi, c0_lo = _split(state_matrix(c_re, -c_im))
    rr = lax.broadcasted_iota(jnp.int32, (C_WIDTH, C_WIDTH), 0)
    cc = lax.broadcasted_iota(jnp.int32, (C_WIDTH, C_WIDTH), 1)
    for d in range(SSM_T):
        wr, wi = _cmul(pows[d][0], pows[d][1], bbr, bbi)
        w_full = state_matrix(wr, wi)
        wb_ref[SSM_T - 1 - d] = w_full.astype(BF16)
        w_hi, w_lo = _split(w_full)
        k = _dot_nt(w_hi, c0_hi) + _dot_nt(w_hi, c0_lo) + _dot_nt(w_lo, c0_hi)
        if d == 0:
            k = k + jnp.where(rr == cc, d_ref[...], 0.0)
        kd_ref[d] = k.astype(BF16)
        mr, mi = _cmul(pows[d + 1][0], pows[d + 1][1], c_re, c_im)
        wct_ref[d] = state_matrix(mr, -mi).astype(BF16)

    re_cols, im_cols = pl.ds(0, N_STATE), pl.ds(N_STATE, N_STATE)
    sr, si = pows[SSM_T]
    atab_ref[0:1, re_cols] = sr
    atab_ref[0:1, im_cols] = si
    for k in range(SSM_LOG_N):
        span = 1 << k
        astep_ref[pl.ds(k, 1), re_cols] = sr
        astep_ref[pl.ds(k, 1), im_cols] = si
        tr, ti = _cmul(atab_ref[0:span, re_cols], atab_ref[0:span, im_cols], sr, si)
        atab_ref[pl.ds(span, span), re_cols] = tr
        atab_ref[pl.ds(span, span), im_cols] = ti
        sr, si = _cmul(sr, si, sr, si)


def _tables(a_re, a_im, ldt, b_re, b_im, c_re, c_im, d_skip):
    depth = a_re.shape[0]
    vec = lambda w: pl.BlockSpec((None, 1, w), lambda l: (l, 0, 0))
    mat = lambda r, c: pl.BlockSpec((None, r, c), lambda l: (l, 0, 0))
    cube = lambda r, c: pl.BlockSpec((None, SSM_T, r, c), lambda l: (l, 0, 0, 0))
    return pl.pallas_call(
        _tables_kernel,
        grid=(depth,),
        in_specs=[vec(N_STATE), vec(N_STATE), vec(N_STATE),
                  mat(C_GROUP, N_STATE), mat(C_GROUP, N_STATE),
                  mat(C_GROUP, N_STATE), mat(C_GROUP, N_STATE), vec(C_WIDTH)],
        out_specs=[cube(C_WIDTH, C_WIDTH), cube(C_WIDTH, 2 * N_STATE), cube(C_WIDTH, 2 * N_STATE),
                   mat(SSM_LOG_N, 2 * N_STATE), mat(SSM_N, 2 * N_STATE)],
        out_shape=[jax.ShapeDtypeStruct((depth, SSM_T, C_WIDTH, C_WIDTH), BF16),
                   jax.ShapeDtypeStruct((depth, SSM_T, C_WIDTH, 2 * N_STATE), BF16),
                   jax.ShapeDtypeStruct((depth, SSM_T, C_WIDTH, 2 * N_STATE), BF16),
                   jax.ShapeDtypeStruct((depth, SSM_LOG_N, 2 * N_STATE), F32),
                   jax.ShapeDtypeStruct((depth, SSM_N, 2 * N_STATE), F32)],
        compiler_params=pltpu.CompilerParams(
            dimension_semantics=("arbitrary",), vmem_limit_bytes=VMEM_LIMIT),
        name="ssm_tables",
    )(a_re, a_im, ldt, b_re, b_im, c_re, c_im, d_skip)


def _ssm_kernel(blocks_per_seq, ua_ref, ub_ref, kd_ref, wb_ref, wct_ref, astep_ref, atab_ref,
                w1_ref, w2_ref, mg_ref, o_ref, inc_ref, sprev_ref, ya_ref, yb_ref, carry_ref):
    @pl.when((pl.program_id(0) % blocks_per_seq) == 0)
    def _():
        carry_ref[...] = jnp.zeros_like(carry_ref)

    xs = [jnp.concatenate([ua_ref[pl.ds(j, SSM_N, stride=SSM_T), :],
                           ub_ref[pl.ds(j, SSM_N, stride=SSM_T), :]], axis=1).astype(BF16)
          for j in range(SSM_T)]

    inc = _dot(xs[0], wb_ref[0])
    for j in range(1, SSM_T):
        inc = inc + _dot(xs[j], wb_ref[j])
    inc_ref[...] = inc

    row = lax.broadcasted_iota(jnp.int32, (SSM_N, LANES), 0)
    for lt in range(N_STATE // LANES):
        re_cols = pl.ds(lt * LANES, LANES)
        im_cols = pl.ds(N_STATE + lt * LANES, LANES)
        xr = inc_ref[:, re_cols]
        xi = inc_ref[:, im_cols]
        for k in range(SSM_LOG_N):
            sh = 1 << k
            ar = astep_ref[pl.ds(k, 1), re_cols]
            ai = astep_ref[pl.ds(k, 1), im_cols]
            keep = row >= sh
            sr = jnp.where(keep, pltpu.roll(xr, sh, 0), 0.0)
            si = jnp.where(keep, pltpu.roll(xi, sh, 0), 0.0)
            xr, xi = xr + ar * sr - ai * si, xi + ar * si + ai * sr
        cr = carry_ref[:, re_cols]
        ci = carry_ref[:, im_cols]
        tr = atab_ref[:, re_cols]
        ti = atab_ref[:, im_cols]
        xr, xi = xr + tr * cr - ti * ci, xi + tr * ci + ti * cr
        first = row == 0
        sprev_ref[:, re_cols] = jnp.where(first, cr, pltpu.roll(xr, 1, 0)).astype(BF16)
        sprev_ref[:, im_cols] = jnp.where(first, ci, pltpu.roll(xi, 1, 0)).astype(BF16)
        carry_ref[:, re_cols] = xr[SSM_N - 1:SSM_N, :]
        carry_ref[:, im_cols] = xi[SSM_N - 1:SSM_N, :]

    sprev = sprev_ref[...]
    for t in range(SSM_T):
        y = _dot_nt(sprev, wct_ref[t])
        for j in range(t + 1):
            y = y + _dot(xs[j], kd_ref[t - j])
        ya_ref[pl.ds(t, SSM_N, stride=SSM_T), :] = y[:, 0:LANES]
        yb_ref[pl.ds(t, SSM_N, stride=SSM_T), :] = y[:, LANES:2 * LANES]

    y = jax.nn.gelu(jnp.concatenate([ya_ref[...], yb_ref[...]], axis=1)).astype(BF16)
    yc = _dot(y, w1_ref[...]) * jax.nn.sigmoid(_dot(y, w2_ref[...]))
    o_ref[...] = _rms(yc, mg_ref[...]).astype(BF16)


def _ssm(layer, zc, kd, wb, wct, astep, atab, w1, w2, mg, seq_len):
    n = zc.shape[0]
    return pl.pallas_call(
        functools.partial(_ssm_kernel, seq_len // ROWS_SSM),
        grid=(n // ROWS_SSM,),
        in_specs=[
            pl.BlockSpec((ROWS_SSM, LANES), lambda i: (i, 0)),
            pl.BlockSpec((ROWS_SSM, LANES), lambda i: (i, 1)),
            _layer_spec((SSM_T, C_WIDTH, C_WIDTH), layer),
            _layer_spec((SSM_T, C_WIDTH, 2 * N_STATE), layer),
            _layer_spec((SSM_T, C_WIDTH, 2 * N_STATE), layer),
            _layer_spec((SSM_LOG_N, 2 * N_STATE), layer),
            _layer_spec((SSM_N, 2 * N_STATE), layer),
            _layer_spec((C_WIDTH, C_WIDTH), layer),
            _layer_spec((C_WIDTH, C_WIDTH), layer),
            _layer_spec((1, C_WIDTH), layer),
        ],
        out_specs=pl.BlockSpec((ROWS_SSM, C_WIDTH), lambda i: (i, 0)),
        out_shape=jax.ShapeDtypeStruct((n, C_WIDTH), BF16),
        scratch_shapes=[
            pltpu.VMEM((SSM_N, 2 * N_STATE), F32),
            pltpu.VMEM((SSM_N, 2 * N_STATE), BF16),
            pltpu.VMEM((ROWS_SSM, LANES), F32),
            pltpu.VMEM((ROWS_SSM, LANES), F32),
            pltpu.VMEM((1, 2 * N_STATE), F32),
        ],
        compiler_params=pltpu.CompilerParams(
            dimension_semantics=("arbitrary",), vmem_limit_bytes=VMEM_LIMIT),
        name="ssm",
    )(zc, zc, kd, wb, wct, astep, atab, w1, w2, mg)


def _post_kernel(h_ref, ya_ref, yb_ref, yc_ref, p_ref, wo_ref, g1_ref, w1_ref, w2_ref,
                 g2_ref, wg_ref, wp_ref, o_ref):
    h = h_ref[...]
    h = h + (_dot(ya_ref[...], wo_ref[0:A_WIDTH, :])
             + _dot(yb_ref[...], wo_ref[A_WIDTH:A_WIDTH + B_WIDTH, :])
             + _dot(yc_ref[...], wo_ref[A_WIDTH + B_WIDTH:, :]))
    hn = _rms(h, g1_ref[...]).astype(BF16)
    piece = D_FF // FF_SPLIT
    ff = None
    for c in range(FF_SPLIT):
        a = jnp.maximum(_dot(hn, w1_ref[:, c * piece:(c + 1) * piece]), 0.0)
        t = _dot((a * a).astype(BF16), w2_ref[c * piece:(c + 1) * piece, :])
        ff = t if ff is None else ff + t
    h = h + ff
    gate = jax.nn.sigmoid(_dot(_rms(h, g2_ref[...]).astype(BF16), wg_ref[...]))
    o_ref[...] = h + gate * _dot(p_ref[...].astype(BF16), wp_ref[...])


def _post(layer, h, ya, yb, yc, p, wo, g1, w1, w2, g2, wg, wp):
    n = h.shape[0]
    rb = lambda w: pl.BlockSpec((ROWS_PROJ, w), lambda i: (i, 0))
    return pl.pallas_call(
        _post_kernel,
        grid=(n // ROWS_PROJ,),
        in_specs=[
            rb(D_MODEL), rb(A_WIDTH), rb(B_WIDTH), rb(C_WIDTH),
            pl.BlockSpec((None, ROWS_PROJ, PLE_DIM), lambda i: (layer, i, 0)),
            _layer_spec((D_MODEL, D_MODEL), layer),
            _layer_spec((1, D_MODEL), layer),
            _layer_spec((D_MODEL, D_FF), layer),
            _layer_spec((D_FF, D_MODEL), layer),
            _layer_spec((1, D_MODEL), layer),
            _layer_spec((D_MODEL, D_MODEL), layer),
            _layer_spec((PLE_DIM, D_MODEL), layer),
        ],
        out_specs=rb(D_MODEL),
        out_shape=jax.ShapeDtypeStruct((n, D_MODEL), F32),
        compiler_params=pltpu.CompilerParams(
            dimension_semantics=("arbitrary",), vmem_limit_bytes=VMEM_LIMIT),
        name="post",
    )(h, ya, yb, yc, p, wo, g1, w1, w2, g2, wg, wp)


def _in_col_perm():
    a = np.arange(2 * A_WIDTH).reshape(A_HEADS, 2, HEAD_DIM)
    perm_a = np.concatenate([a[:, 0].reshape(-1), a[:, 1].reshape(-1)])
    return np.concatenate([perm_a, np.arange(2 * A_WIDTH, IN_COLS)])


def kernel(x, p, positions, attn_norm_g, w_in, gmlp_ln_g, gmlp_ln_b, gmlp_ws, gmlp_bs, q_norm_g, k_norm_g, sinks, ssm_a_re, ssm_a_im, ssm_log_dt, ssm_b_re, ssm_b_im, ssm_c_re, ssm_c_im, ssm_d, glu_w1, glu_w2, mix_out_g, w_out, mlp_norm_g, w_ff1, w_ff2, ple_norm_g, w_ple_gate, w_ple_proj):
    bsz, seq_len, _ = x.shape
    depth = w_in.shape[0]
    n = bsz * seq_len

    inv = 1.0 / (ROPE_THETA ** (jnp.arange(0, HEAD_DIM, 2, dtype=F32) / HEAD_DIM))
    ang = positions.astype(F32).reshape(n, 1) * inv
    cos2 = jnp.tile(jnp.cos(ang), (1, 4))
    sin2 = jnp.tile(jnp.concatenate([-jnp.sin(ang), jnp.sin(ang)], axis=1), (1, 2))

    seg = np.arange(256) // HEAD_DIM
    pm = jnp.asarray((seg[:, None] == seg[None, :]) / HEAD_DIM, dtype=BF16)

    rows = lambda v: v.reshape(depth, 1, -1).astype(F32)
    g_attn, g_mlp, g_ple = rows(attn_norm_g), rows(mlp_norm_g), rows(ple_norm_g)
    ln_g, ln_b = rows(gmlp_ln_g), rows(gmlp_ln_b)
    mg_a = rows(mix_out_g[:, :A_WIDTH])
    mg_b = rows(mix_out_g[:, A_WIDTH:A_WIDTH + B_WIDTH])
    mg_c = rows(mix_out_g[:, A_WIDTH + B_WIDTH:])
    qg = rows(jnp.tile(q_norm_g, (1, B_Q_HEADS)))
    kg = rows(jnp.tile(k_norm_g, (1, B_KV_HEADS)))
    bs = jnp.repeat(jnp.swapaxes(gmlp_bs, 1, 2), HEAD_DIM, axis=2)
    w_in_b = w_in[:, :, _in_col_perm()].astype(BF16)
    ws_b, w_out_b = gmlp_ws.astype(BF16), w_out.astype(BF16)
    w_ff1_b, w_ff2_b = w_ff1.astype(BF16), w_ff2.astype(BF16)
    w_gate_b, w_proj_b = w_ple_gate.astype(BF16), w_ple_proj.astype(BF16)
    glu1_b, glu2_b = glu_w1.astype(BF16), glu_w2.astype(BF16)
    p2 = p.reshape(depth, n, PLE_DIM)

    kd, wb, wct, astep, atab = _tables(
        rows(ssm_a_re), rows(ssm_a_im), rows(jnp.repeat(ssm_log_dt, C_STATE, axis=1)),
        jnp.transpose(ssm_b_re, (0, 3, 1, 2)).reshape(depth, C_GROUP, N_STATE),
        jnp.transpose(ssm_b_im, (0, 3, 1, 2)).reshape(depth, C_GROUP, N_STATE),
        jnp.transpose(ssm_c_re, (0, 2, 1, 3)).reshape(depth, C_GROUP, N_STATE),
        jnp.transpose(ssm_c_im, (0, 2, 1, 3)).reshape(depth, C_GROUP, N_STATE),
        rows(ssm_d))

    h = x.reshape(n, D_MODEL)
    for i in range(depth):
        ya, yb, zc = _mix(i, h, cos2, sin2, sinks, g_attn, w_in_b, ws_b, bs, ln_g, ln_b, mg_a,
                          qg, kg, mg_b, pm, seq_len)
        yc = _ssm(i, zc, kd, wb, wct, astep, atab, glu1_b, glu2_b, mg_c, seq_len)
        h = _post(i, h, ya, yb, yc, p2, w_out_b, g_mlp, w_ff1_b, w_ff2_b, g_ple, w_gate_b, w_proj_b)
    return h.reshape(bsz, seq_len, D_MODEL)
```

```python
import functools
import math

import numpy as np
import jax
import jax.numpy as jnp
from jax import lax
from jax.experimental import pallas as pl
from jax.experimental.pallas import tpu as pltpu

F32 = jnp.float32
BF16 = jnp.bfloat16

D_MODEL = 1024
HEAD_DIM = 64
A_WIDTH = 256
A_HEADS = 4
CHUNK = 128
B_WIDTH = 512
B_Q_HEADS = 8
B_KV_HEADS = 2
WINDOW = 128
ROPE_THETA = 10000.0
C_WIDTH = 256
C_GROUP = 16
C_GROUPS = 16
C_STATE = 64
N_STATE = C_GROUPS * C_STATE
IN_COLS = 1536
D_FF = 4096
PLE_DIM = 256
EPS = 1e-6
NEG_BIG = -1e30
LANES = 128

COL_A = 0
COL_Q = 512
COL_K = 1024
COL_V = 1152
COL_C = 1280

ROWS_PROJ = 512
ROWS_MIX = 512
ROWS_SSM = 2048
SSM_T = 8
SSM_N = ROWS_SSM // SSM_T
SSM_LOG_N = int(math.log2(SSM_N))
FF_SPLIT = 4
VMEM_LIMIT = 56 * 1024 * 1024


def _const_spec(shape):
    nd = len(shape)
    return pl.BlockSpec(shape, lambda *_: (0,) * nd, pipeline_mode=pl.Buffered(1))


def _layer_spec(shape, layer):
    nd = len(shape)
    return pl.BlockSpec((None,) + tuple(shape), lambda *_: (layer,) + (0,) * nd,
                        pipeline_mode=pl.Buffered(1))


def _rms(x, g):
    ms = jnp.mean(x * x, axis=-1, keepdims=True)
    return x * lax.rsqrt(ms + EPS) * g


def _dot(a, b):
    return jnp.dot(a, b, preferred_element_type=F32)


def _dot_nt(a, b):
    return lax.dot_general(a, b, (((1,), (1,)), ((), ())), preferred_element_type=F32)


def _split(x):
    hi = x.astype(BF16)
    return hi, (x - hi.astype(F32)).astype(BF16)


def _seg_mean(x, pmat):
    hi, lo = _split(x)
    return _dot(hi, pmat) + _dot(lo, pmat)


def _cmul(ar, ai, br, bi):
    return ar * br - ai * bi, ar * bi + ai * br


def _rope(x, cos, sin_signed):
    width = x.shape[-1]
    lane = lax.broadcasted_iota(jnp.int32, x.shape, 1)
    first_half = (lane % HEAD_DIM) < (HEAD_DIM // 2)
    partner = jnp.where(first_half,
                        pltpu.roll(x, width - HEAD_DIM // 2, 1),
                        pltpu.roll(x, HEAD_DIM // 2, 1))
    return x * cos + partner * sin_signed


def _gmlp_chunk(z_ref, rows, ws, bs_ref, lng_ref, lnb_ref, mg_ref, pm):
    lane = lax.broadcasted_iota(jnp.int32, (CHUNK, A_WIDTH), 1)
    u = jax.nn.gelu(z_ref[rows, COL_A:COL_A + A_WIDTH])
    v = jax.nn.gelu(z_ref[rows, COL_A + A_WIDTH:COL_A + 2 * A_WIDTH])
    vc = v - _seg_mean(v, pm)
    var = _seg_mean(vc * vc, pm)
    vn = (vc * lax.rsqrt(var + EPS) * lng_ref[...] + lnb_ref[...]).astype(BF16)
    sv = bs_ref[...]
    for h in range(A_HEADS):
        in_head = (lane >= h * HEAD_DIM) & (lane < (h + 1) * HEAD_DIM)
        sv = sv + _dot(ws[h], jnp.where(in_head, vn, jnp.zeros((), BF16)))
    return _rms(u * sv, mg_ref[...]).astype(BF16)


def _place_heads(x):
    low = lax.broadcasted_iota(jnp.int32, x.shape, 1) < HEAD_DIM
    sw = pltpu.roll(x, HEAD_DIM, 1)
    zero = jnp.zeros((), F32)
    return [[jnp.where(low, x, zero).astype(BF16), jnp.where(low, zero, sw).astype(BF16)],
            [jnp.where(low, sw, zero).astype(BF16), jnp.where(low, zero, x).astype(BF16)]]


def _mix_project(h_ref, g_ref, w_ref, z_ref):
    xn = _rms(h_ref[...], g_ref[...]).astype(BF16)
    z_ref[...] = _dot(xn, w_ref[...])


def _mix_mixers(layer, first_block, sinks_ref, cos_ref, sin_ref, ws_ref, bs_ref, lng_ref, lnb_ref,
                mga_ref, qg_ref, kg_ref, mgb_ref, pm_ref, ya_ref, yb_ref, zc_ref,
                z_ref, kprev_ref, vprev_ref):
    nq = ROWS_MIX // WINDOW
    zc_ref[...] = z_ref[:, COL_C:COL_C + C_WIDTH]

    pm = pm_ref[...]
    pm_kv = pm[0:LANES, 0:LANES]

    row = lax.broadcasted_iota(jnp.int32, (CHUNK, CHUNK), 0)
    col = lax.broadcasted_iota(jnp.int32, (CHUNK, CHUNK), 1)
    ws = [jnp.where(row >= col, ws_ref[h], jnp.zeros((), BF16)) for h in range(A_HEADS)]
    for c in range(ROWS_MIX // CHUNK):
        rows = pl.ds(c * CHUNK, CHUNK)
        ya_ref[rows, :] = _gmlp_chunk(z_ref, rows, ws, bs_ref, lng_ref, lnb_ref, mga_ref, pm)
        yield

    k_cur = z_ref[:, COL_K:COL_K + LANES]
    k_cur = k_cur * lax.rsqrt(_seg_mean(k_cur * k_cur, pm_kv) + EPS) * kg_ref[...]
    k_cur = _rope(k_cur, cos_ref[...], sin_ref[...])
    v_cur = z_ref[:, COL_V:COL_V + LANES]
    k_all = jnp.concatenate([kprev_ref[...], k_cur], axis=0)
    v_all = jnp.concatenate([vprev_ref[...], v_cur], axis=0)
    kprev_ref[...] = k_cur[ROWS_MIX - WINDOW:, :]
    vprev_ref[...] = v_cur[ROWS_MIX - WINDOW:, :]
    k_pl = _place_heads(k_all)
    v_pl = _place_heads(v_all)
    yield

    q = z_ref[:, COL_Q:COL_Q + B_WIDTH]
    qn = jnp.concatenate(
        [q[:, s:s + 256] * lax.rsqrt(_seg_mean(q[:, s:s + 256] * q[:, s:s + 256], pm) + EPS)
         for s in (0, 256)], axis=1) * qg_ref[...]
    cos_q = jnp.concatenate([cos_ref[...]] * 4, axis=1)
    sin_q = jnp.concatenate([sin_ref[...]] * 4, axis=1)
    qr = (_rope(qn, cos_q, sin_q) * (HEAD_DIM ** -0.5)).astype(BF16)
    yield

    qi = lax.broadcasted_iota(jnp.int32, (2 * WINDOW, WINDOW), 0) % WINDOW
    kc = lax.broadcasted_iota(jnp.int32, (2 * WINDOW, WINDOW), 1)
    from_cur = kc <= qi
    top = lax.broadcasted_iota(jnp.int32, (2 * WINDOW, 1), 0) < WINDOW
    prev_ok = kc >= jnp.where(first_block, WINDOW, 0)
    zero = jnp.zeros((), F32)

    for b in range(nq):
        rows = slice(b * WINDOW, (b + 1) * WINDOW)
        win = slice(b * WINDOW, (b + 2) * WINDOW)
        pairs = []
        for j in range(B_KV_HEADS):
            qa = qr[rows, (2 * j) * LANES:(2 * j + 1) * LANES]
            qb = qr[rows, (2 * j + 1) * LANES:(2 * j + 2) * LANES]
            qs = jnp.concatenate([qa, qb], axis=0)
            acc = None
            for slot in range(2):
                s = _dot_nt(qs, k_pl[j][slot][win])
                s_prev = s[:, 0:WINDOW]
                if b == 0:
                    s_prev = jnp.where(prev_ok, s_prev, NEG_BIG)
                s = jnp.where(from_cur, s[:, WINDOW:], s_prev)
                sink = jnp.where(top, sinks_ref[layer, 4 * j + slot],
                                 sinks_ref[layer, 4 * j + 2 + slot])
                m = jnp.maximum(jnp.max(s, axis=-1, keepdims=True), sink)
                pr = jnp.exp(s - m)
                denom = jnp.sum(pr, axis=-1, keepdims=True) + jnp.exp(sink - m)
                pn = pr * (1.0 / denom)
                p2 = jnp.concatenate([jnp.where(from_cur, zero, pn),
                                      jnp.where(from_cur, pn, zero)], axis=1).astype(BF16)
                o = _dot(p2, v_pl[j][slot][win])
                acc = o if acc is None else acc + o
            pairs += [acc[0:WINDOW], acc[WINDOW:2 * WINDOW]]
            if j + 1 < B_KV_HEADS:
                yield
        yb = jnp.concatenate(pairs, axis=1)
        yb_ref[rows, :] = _rms(yb, mgb_ref[...]).astype(BF16)
        yield


MIXER_STAGES = ROWS_MIX // CHUNK + 2 + 2 * (ROWS_MIX // WINDOW)
POST_STAGES = 2 + FF_SPLIT


def _run(*staged):
    total = max(n for _, n in staged)
    done = [0] * len(staged)
    for step in range(1, total + 1):
        for k, (gen, n) in enumerate(staged):
            while done[k] * total < step * n:
                next(gen, None)
                done[k] += 1
    for gen, _ in staged:
        for _ in gen:
            pass


def _mix_kernel(layer, blocks_per_seq, sinks_ref, h_ref, cos_ref, sin_ref, g_ref, w_ref, *rest):
    mixer_refs, (z_ref, kprev_ref, vprev_ref) = rest[:-3], rest[-3:]
    first_block = (pl.program_id(0) % blocks_per_seq) == 0

    @pl.when(first_block)
    def _():
        kprev_ref[...] = jnp.zeros_like(kprev_ref)
        vprev_ref[...] = jnp.zeros_like(vprev_ref)

    _mix_project(h_ref, g_ref, w_ref, z_ref)
    _run((_mix_mixers(layer, first_block, sinks_ref, cos_ref, sin_ref, *mixer_refs,
                      z_ref, kprev_ref, vprev_ref), MIXER_STAGES))


def _mix_param_specs(layer):
    return [
        _layer_spec((1, D_MODEL), layer),
        _layer_spec((D_MODEL, IN_COLS), layer),
        _layer_spec((A_HEADS, CHUNK, CHUNK), layer),
        _layer_spec((CHUNK, A_WIDTH), layer),
        _layer_spec((1, A_WIDTH), layer),
        _layer_spec((1, A_WIDTH), layer),
        _layer_spec((1, A_WIDTH), layer),
        _layer_spec((1, B_WIDTH), layer),
        _layer_spec((1, LANES), layer),
        _layer_spec((1, B_WIDTH), layer),
        _const_spec((256, 256)),
    ]


def _mix_out_shapes(n):
    return [jax.ShapeDtypeStruct((n, A_WIDTH), BF16),
            jax.ShapeDtypeStruct((n, B_WIDTH), BF16),
            jax.ShapeDtypeStruct((n, C_WIDTH), F32)]


_MIX_SCRATCH = [
    pltpu.VMEM((ROWS_MIX, IN_COLS), F32),
    pltpu.VMEM((WINDOW, LANES), F32),
    pltpu.VMEM((WINDOW, LANES), F32),
]


def _mix(layer, h, cos2, sin2, sinks, mix_params, seq_len):
    n = h.shape[0]
    rb = lambda w_: pl.BlockSpec((ROWS_MIX, w_), lambda i: (i, 0))
    return pl.pallas_call(
        functools.partial(_mix_kernel, layer, seq_len // ROWS_MIX),
        grid=(n // ROWS_MIX,),
        in_specs=[pl.BlockSpec(memory_space=pltpu.SMEM), rb(D_MODEL), rb(LANES), rb(LANES)]
        + _mix_param_specs(layer),
        out_specs=[rb(A_WIDTH), rb(B_WIDTH), rb(C_WIDTH)],
        out_shape=_mix_out_shapes(n),
        scratch_shapes=_MIX_SCRATCH,
        compiler_params=pltpu.CompilerParams(
            dimension_semantics=("arbitrary",), vmem_limit_bytes=VMEM_LIMIT),
        name="mix",
    )(sinks, h, cos2, sin2, *mix_params)


def _tables_kernel(are_ref, aim_ref, ldt_ref, bre_ref, bim_ref, cre_ref, cim_ref, d_ref,
                   kd_ref, wb_ref, wct_ref, astep_ref, atab_ref):
    a_re, a_im = are_ref[...], aim_ref[...]
    dt = jnp.exp(ldt_ref[...])
    mag = jnp.exp(a_re * dt)
    lr, li = mag * jnp.cos(a_im * dt), mag * jnp.sin(a_im * dt)
    den = a_re * a_re + a_im * a_im
    fr, fi = _cmul(lr - 1.0, li, a_re / den, -a_im / den)
    bbr, bbi = _cmul(fr, fi, bre_ref[...], bim_ref[...])
    c_re, c_im = cre_ref[...], cim_ref[...]

    row_g = lax.broadcasted_iota(jnp.int32, (C_WIDTH, N_STATE), 0) // C_GROUP
    col_g = lax.broadcasted_iota(jnp.int32, (C_WIDTH, N_STATE), 1) // C_STATE
    same_group = row_g == col_g

    def blockdiag(m):
        return jnp.where(same_group, jnp.concatenate([m] * C_GROUPS, axis=0), 0.0)

    def state_matrix(mr, mi):
        return jnp.concatenate([blockdiag(mr), blockdiag(mi)], axis=1)

    pows = [(jnp.ones_like(lr), jnp.zeros_like(li))]
    for _ in range(SSM_T):
        pows.append(_cmul(pows[-1][0], pows[-1][1], lr, li))

    c0_hi, c0_lo = _split(state_matrix(c_re, -c_im))
    rr = lax.broadcasted_iota(jnp.int32, (C_WIDTH, C_WIDTH), 0)
    cc = lax.broadcasted_iota(jnp.int32, (C_WIDTH, C_WIDTH), 1)
    for d in range(SSM_T):
        wr, wi = _cmul(pows[d][0], pows[d][1], bbr, bbi)
        w_full = state_matrix(wr, wi)
        wb_ref[SSM_T - 1 - d] = w_full.astype(BF16)
        w_hi, w_lo = _split(w_full)
        k = _dot_nt(w_hi, c0_hi) + _dot_nt(w_hi, c0_lo) + _dot_nt(w_lo, c0_hi)
        if d == 0:
            k = k + jnp.where(rr == cc, d_ref[...], 0.0)
        kd_ref[d] = k.astype(BF16)
        mr, mi = _cmul(pows[d + 1][0], pows[d + 1][1], c_re, c_im)
        wct_ref[d] = state_matrix(mr, -mi).astype(BF16)

    re_cols, im_cols = pl.ds(0, N_STATE), pl.ds(N_STATE, N_STATE)
    sr, si = pows[SSM_T]
    atab_ref[0:1, re_cols] = sr
    atab_ref[0:1, im_cols] = si
    for k in range(SSM_LOG_N):
        span = 1 << k
        astep_ref[pl.ds(k, 1), re_cols] = sr
        astep_ref[pl.ds(k, 1), im_cols] = si
        tr, ti = _cmul(atab_ref[0:span, re_cols], atab_ref[0:span, im_cols], sr, si)
        atab_ref[pl.ds(span, span), re_cols] = tr
        atab_ref[pl.ds(span, span), im_cols] = ti
        sr, si = _cmul(sr, si, sr, si)


def _tables(a_re, a_im, ldt, b_re, b_im, c_re, c_im, d_skip):
    depth = a_re.shape[0]
    vec = lambda w: pl.BlockSpec((None, 1, w), lambda l: (l, 0, 0))
    mat = lambda r, c: pl.BlockSpec((None, r, c), lambda l: (l, 0, 0))
    cube = lambda r, c: pl.BlockSpec((None, SSM_T, r, c), lambda l: (l, 0, 0, 0))
    return pl.pallas_call(
        _tables_kernel,
        grid=(depth,),
        in_specs=[vec(N_STATE), vec(N_STATE), vec(N_STATE),
                  mat(C_GROUP, N_STATE), mat(C_GROUP, N_STATE),
                  mat(C_GROUP, N_STATE), mat(C_GROUP, N_STATE), vec(C_WIDTH)],
        out_specs=[cube(C_WIDTH, C_WIDTH), cube(C_WIDTH, 2 * N_STATE), cube(C_WIDTH, 2 * N_STATE),
                   mat(SSM_LOG_N, 2 * N_STATE), mat(SSM_N, 2 * N_STATE)],
        out_shape=[jax.ShapeDtypeStruct((depth, SSM_T, C_WIDTH, C_WIDTH), BF16),
                   jax.ShapeDtypeStruct((depth, SSM_T, C_WIDTH, 2 * N_STATE), BF16),
                   jax.ShapeDtypeStruct((depth, SSM_T, C_WIDTH, 2 * N_STATE), BF16),
                   jax.ShapeDtypeStruct((depth, SSM_LOG_N, 2 * N_STATE), F32),
                   jax.ShapeDtypeStruct((depth, SSM_N, 2 * N_STATE), F32)],
        compiler_params=pltpu.CompilerParams(
            dimension_semantics=("arbitrary",), vmem_limit_bytes=VMEM_LIMIT),
        name="ssm_tables",
    )(a_re, a_im, ldt, b_re, b_im, c_re, c_im, d_skip)


def _ssm_kernel(blocks_per_seq, ua_ref, ub_ref, kd_ref, wb_ref, wct_ref, astep_ref, atab_ref,
                w1_ref, w2_ref, mg_ref, o_ref, inc_ref, sprev_ref, ya_ref, yb_ref, carry_ref):
    @pl.when((pl.program_id(0) % blocks_per_seq) == 0)
    def _():
        carry_ref[...] = jnp.zeros_like(carry_ref)

    xs = [jnp.concatenate([ua_ref[pl.ds(j, SSM_N, stride=SSM_T), :],
                           ub_ref[pl.ds(j, SSM_N, stride=SSM_T), :]], axis=1).astype(BF16)
          for j in range(SSM_T)]

    inc = _dot(xs[0], wb_ref[0])
    for j in range(1, SSM_T):
        inc = inc + _dot(xs[j], wb_ref[j])
    inc_ref[...] = inc

    row = lax.broadcasted_iota(jnp.int32, (SSM_N, LANES), 0)
    for lt in range(N_STATE // LANES):
        re_cols = pl.ds(lt * LANES, LANES)
        im_cols = pl.ds(N_STATE + lt * LANES, LANES)
        xr = inc_ref[:, re_cols]
        xi = inc_ref[:, im_cols]
        for k in range(SSM_LOG_N):
            sh = 1 << k
            ar = astep_ref[pl.ds(k, 1), re_cols]
            ai = astep_ref[pl.ds(k, 1), im_cols]
            keep = row >= sh
            sr = jnp.where(keep, pltpu.roll(xr, sh, 0), 0.0)
            si = jnp.where(keep, pltpu.roll(xi, sh, 0), 0.0)
            xr, xi = xr + ar * sr - ai * si, xi + ar * si + ai * sr
        cr = carry_ref[:, re_cols]
        ci = carry_ref[:, im_cols]
        tr = atab_ref[:, re_cols]
        ti = atab_ref[:, im_cols]
        xr, xi = xr + tr * cr - ti * ci, xi + tr * ci + ti * cr
        first = row == 0
        sprev_ref[:, re_cols] = jnp.where(first, cr, pltpu.roll(xr, 1, 0)).astype(BF16)
        sprev_ref[:, im_cols] = jnp.where(first, ci, pltpu.roll(xi, 1, 0)).astype(BF16)
        carry_ref[:, re_cols] = xr[SSM_N - 1:SSM_N, :]
        carry_ref[:, im_cols] = xi[SSM_N - 1:SSM_N, :]

    sprev = sprev_ref[...]
    for t in range(SSM_T):
        y = _dot_nt(sprev, wct_ref[t])
        for j in range(t + 1):
            y = y + _dot(xs[j], kd_ref[t - j])
        ya_ref[pl.ds(t, SSM_N, stride=SSM_T), :] = y[:, 0:LANES]
        yb_ref[pl.ds(t, SSM_N, stride=SSM_T), :] = y[:, LANES:2 * LANES]

    y = jax.nn.gelu(jnp.concatenate([ya_ref[...], yb_ref[...]], axis=1)).astype(BF16)
    yc = _dot(y, w1_ref[...]) * jax.nn.sigmoid(_dot(y, w2_ref[...]))
    o_ref[...] = _rms(yc, mg_ref[...]).astype(BF16)


def _ssm(layer, zc, kd, wb, wct, astep, atab, w1, w2, mg, seq_len):
    n = zc.shape[0]
    return pl.pallas_call(
        functools.partial(_ssm_kernel, seq_len // ROWS_SSM),
        grid=(n // ROWS_SSM,),
        in_specs=[
            pl.BlockSpec((ROWS_SSM, LANES), lambda i: (i, 0)),
            pl.BlockSpec((ROWS_SSM, LANES), lambda i: (i, 1)),
            _layer_spec((SSM_T, C_WIDTH, C_WIDTH), layer),
            _layer_spec((SSM_T, C_WIDTH, 2 * N_STATE), layer),
            _layer_spec((SSM_T, C_WIDTH, 2 * N_STATE), layer),
            _layer_spec((SSM_LOG_N, 2 * N_STATE), layer),
            _layer_spec((SSM_N, 2 * N_STATE), layer),
            _layer_spec((C_WIDTH, C_WIDTH), layer),
            _layer_spec((C_WIDTH, C_WIDTH), layer),
            _layer_spec((1, C_WIDTH), layer),
        ],
        out_specs=pl.BlockSpec((ROWS_SSM, C_WIDTH), lambda i: (i, 0)),
        out_shape=jax.ShapeDtypeStruct((n, C_WIDTH), BF16),
        scratch_shapes=[
            pltpu.VMEM((SSM_N, 2 * N_STATE), F32),
            pltpu.VMEM((SSM_N, 2 * N_STATE), BF16),
            pltpu.VMEM((ROWS_SSM, LANES), F32),
            pltpu.VMEM((ROWS_SSM, LANES), F32),
            pltpu.VMEM((1, 2 * N_STATE), F32),
        ],
        compiler_params=pltpu.CompilerParams(
            dimension_semantics=("arbitrary",), vmem_limit_bytes=VMEM_LIMIT),
        name="ssm",
    )(zc, zc, kd, wb, wct, astep, atab, w1, w2, mg)


def _post_stages(h_ref, ya_ref, yb_ref, yc_ref, p_ref, wo_ref, g1_ref, w1_ref, w2_ref,
                 g2_ref, wg_ref, wp_ref, *out_refs):
    h = h_ref[...]
    h = h + (_dot(ya_ref[...], wo_ref[0:A_WIDTH, :])
             + _dot(yb_ref[...], wo_ref[A_WIDTH:A_WIDTH + B_WIDTH, :])
             + _dot(yc_ref[...], wo_ref[A_WIDTH + B_WIDTH:, :]))
    hn = _rms(h, g1_ref[...]).astype(BF16)
    yield
    piece = D_FF // FF_SPLIT
    ff = None
    for c in range(FF_SPLIT):
        a = jnp.maximum(_dot(hn, w1_ref[:, c * piece:(c + 1) * piece]), 0.0)
        t = _dot((a * a).astype(BF16), w2_ref[c * piece:(c + 1) * piece, :])
        ff = t if ff is None else ff + t
        yield
    h = h + ff
    gate = jax.nn.sigmoid(_dot(_rms(h, g2_ref[...]).astype(BF16), wg_ref[...]))
    h = h + gate * _dot(p_ref[...].astype(BF16), wp_ref[...])
    for o_ref in out_refs:
        o_ref[...] = h
    yield


N_POST_IN = 12


def _post_kernel(*refs):
    _run((_post_stages(*refs), POST_STAGES))


def _post_specs(layer, row_block):
    rb = lambda w: pl.BlockSpec((ROWS_PROJ, w), lambda i: (row_block(i), 0))
    return [
        rb(D_MODEL), rb(A_WIDTH), rb(B_WIDTH), rb(C_WIDTH),
        pl.BlockSpec((None, ROWS_PROJ, PLE_DIM), lambda i: (layer, row_block(i), 0)),
        _layer_spec((D_MODEL, D_MODEL), layer),
        _layer_spec((1, D_MODEL), layer),
        _layer_spec((D_MODEL, D_FF), layer),
        _layer_spec((D_FF, D_MODEL), layer),
        _layer_spec((1, D_MODEL), layer),
        _layer_spec((D_MODEL, D_MODEL), layer),
        _layer_spec((PLE_DIM, D_MODEL), layer),
    ]


def _post(layer, h, ya, yb, yc, p, post_params):
    n = h.shape[0]
    return pl.pallas_call(
        _post_kernel,
        grid=(n // ROWS_PROJ,),
        in_specs=_post_specs(layer, lambda i: i),
        out_specs=pl.BlockSpec((ROWS_PROJ, D_MODEL), lambda i: (i, 0)),
        out_shape=jax.ShapeDtypeStruct((n, D_MODEL), F32),
        compiler_params=pltpu.CompilerParams(
            dimension_semantics=("arbitrary",), vmem_limit_bytes=VMEM_LIMIT),
        name="post",
    )(h, ya, yb, yc, p, *post_params)


def _fused_kernel(layer, blocks_per_seq, n_blocks, sinks_ref, *refs):
    post_refs, refs = refs[:N_POST_IN], refs[N_POST_IN:]
    cos_ref, sin_ref, g_ref, w_ref = refs[:4]
    mixer_refs = refs[4:-8]
    h_out_ref, ya_ref, yb_ref, zc_ref = refs[-8:-4]
    hs_ref, z_ref, kprev_ref, vprev_ref = refs[-4:]
    i = pl.program_id(0)
    first_block = (i % blocks_per_seq) == 1 % blocks_per_seq

    @pl.when((i == 0) | first_block)
    def _():
        kprev_ref[...] = jnp.zeros_like(kprev_ref)
        vprev_ref[...] = jnp.zeros_like(vprev_ref)

    @pl.when(i == 0)
    def _():
        hs_ref[...] = jnp.zeros_like(hs_ref)

    _mix_project(hs_ref, g_ref, w_ref, z_ref)
    _run((_post_stages(*post_refs, h_out_ref, hs_ref), POST_STAGES),
         (_mix_mixers(layer, first_block, sinks_ref, cos_ref, sin_ref, *mixer_refs,
                      ya_ref, yb_ref, zc_ref, z_ref, kprev_ref, vprev_ref), MIXER_STAGES))


def _fused(layer, h, ya, yb, yc, p, post_params, cos2, sin2, sinks, mix_params, seq_len):
    n = h.shape[0]
    n_blocks = n // ROWS_MIX
    merged = lambda i: jnp.minimum(i, n_blocks - 1)
    mixed = lambda i: jnp.maximum(i - 1, 0)
    rb = lambda w_: pl.BlockSpec((ROWS_MIX, w_), lambda i: (mixed(i), 0))
    return pl.pallas_call(
        functools.partial(_fused_kernel, layer, seq_len // ROWS_MIX, n_blocks),
        grid=(n_blocks + 1,),
        in_specs=[pl.BlockSpec(memory_space=pltpu.SMEM)]
        + _post_specs(layer - 1, merged)
        + [rb(LANES), rb(LANES)] + _mix_param_specs(layer),
        out_specs=[pl.BlockSpec((ROWS_PROJ, D_MODEL), lambda i: (merged(i), 0)),
                   rb(A_WIDTH), rb(B_WIDTH), rb(C_WIDTH)],
        out_shape=[jax.ShapeDtypeStruct((n, D_MODEL), F32)] + _mix_out_shapes(n),
        scratch_shapes=[pltpu.VMEM((ROWS_MIX, D_MODEL), F32)] + _MIX_SCRATCH,
        compiler_params=pltpu.CompilerParams(
            dimension_semantics=("arbitrary",), vmem_limit_bytes=VMEM_LIMIT),
        name="fused",
    )(sinks, h, ya, yb, yc, p, *post_params, cos2, sin2, *mix_params)


def kernel(x, p, positions, attn_norm_g, w_in, gmlp_ln_g, gmlp_ln_b, gmlp_ws, gmlp_bs, q_norm_g, k_norm_g, sinks, ssm_a_re, ssm_a_im, ssm_log_dt, ssm_b_re, ssm_b_im, ssm_c_re, ssm_c_im, ssm_d, glu_w1, glu_w2, mix_out_g, w_out, mlp_norm_g, w_ff1, w_ff2, ple_norm_g, w_ple_gate, w_ple_proj):
    bsz, seq_len, _ = x.shape
    depth = w_in.shape[0]
    n = bsz * seq_len

    inv = 1.0 / (ROPE_THETA ** (jnp.arange(0, HEAD_DIM, 2, dtype=F32) / HEAD_DIM))
    ang = positions.astype(F32).reshape(n, 1) * inv
    cos2 = jnp.tile(jnp.cos(ang), (1, 4))
    sin2 = jnp.tile(jnp.concatenate([-jnp.sin(ang), jnp.sin(ang)], axis=1), (1, 2))

    seg = np.arange(256) // HEAD_DIM
    pm = jnp.asarray((seg[:, None] == seg[None, :]) / HEAD_DIM, dtype=BF16)

    rows = lambda v: v.reshape(depth, 1, -1).astype(F32)
    g_attn, g_mlp, g_ple = rows(attn_norm_g), rows(mlp_norm_g), rows(ple_norm_g)
    ln_g, ln_b = rows(gmlp_ln_g), rows(gmlp_ln_b)
    mg_a = rows(mix_out_g[:, :A_WIDTH])
    mg_b = rows(mix_out_g[:, A_WIDTH:A_WIDTH + B_WIDTH])
    mg_c = rows(mix_out_g[:, A_WIDTH + B_WIDTH:])
    qg = rows(jnp.tile(q_norm_g, (1, B_Q_HEADS)))
    kg = rows(jnp.tile(k_norm_g, (1, B_KV_HEADS)))
    bs = jnp.repeat(jnp.swapaxes(gmlp_bs, 1, 2), HEAD_DIM, axis=2)
    w_a = w_in[:, :, :2 * A_WIDTH].reshape(depth, D_MODEL, A_HEADS, 2, HEAD_DIM)
    w_a = jnp.swapaxes(w_a, 2, 3).reshape(depth, D_MODEL, 2 * A_WIDTH)
    w_in_b = jnp.concatenate([w_a, w_in[:, :, 2 * A_WIDTH:]], axis=2).astype(BF16)
    ws_b, w_out_b = gmlp_ws.astype(BF16), w_out.astype(BF16)
    w_ff1_b, w_ff2_b = w_ff1.astype(BF16), w_ff2.astype(BF16)
    w_gate_b, w_proj_b = w_ple_gate.astype(BF16), w_ple_proj.astype(BF16)
    glu1_b, glu2_b = glu_w1.astype(BF16), glu_w2.astype(BF16)
    p2 = p.reshape(depth, n, PLE_DIM)

    kd, wb, wct, astep, atab = _tables(
        rows(ssm_a_re), rows(ssm_a_im), rows(jnp.repeat(ssm_log_dt, C_STATE, axis=1)),
        jnp.transpose(ssm_b_re, (0, 3, 1, 2)).reshape(depth, C_GROUP, N_STATE),
        jnp.transpose(ssm_b_im, (0, 3, 1, 2)).reshape(depth, C_GROUP, N_STATE),
        jnp.transpose(ssm_c_re, (0, 2, 1, 3)).reshape(depth, C_GROUP, N_STATE),
        jnp.transpose(ssm_c_im, (0, 2, 1, 3)).reshape(depth, C_GROUP, N_STATE),
        rows(ssm_d))

    mix_params = (g_attn, w_in_b, ws_b, bs, ln_g, ln_b, mg_a, qg, kg, mg_b, pm)
    post_params = (w_out_b, g_mlp, w_ff1_b, w_ff2_b, g_ple, w_gate_b, w_proj_b)
    h = x.reshape(n, D_MODEL)
    ya, yb, zc = _mix(0, h, cos2, sin2, sinks, mix_params, seq_len)
    for i in range(depth):
        yc = _ssm(i, zc, kd, wb, wct, astep, atab, glu1_b, glu2_b, mg_c, seq_len)
        if i + 1 < depth:
            h, ya, yb, zc = _fused(i + 1, h, ya, yb, yc, p2, post_params, cos2, sin2, sinks,
                                   mix_params, seq_len)
        else:
            h = _post(i, h, ya, yb, yc, p2, post_params)
    return h.reshape(bsz, seq_len, D_MODEL)
```

```python
import functools
import math

import numpy as np
import jax
import jax.numpy as jnp
from jax import lax
from jax.experimental import pallas as pl
from jax.experimental.pallas import tpu as pltpu

F32 = jnp.float32
BF16 = jnp.bfloat16

D_MODEL = 1024
HEAD_DIM = 64
A_WIDTH = 256
A_HEADS = 4
CHUNK = 128
B_WIDTH = 512
B_Q_HEADS = 8
B_KV_HEADS = 2
WINDOW = 128
ROPE_THETA = 10000.0
C_WIDTH = 256
C_GROUP = 16
C_GROUPS = 16
C_STATE = 64
N_STATE = C_GROUPS * C_STATE
IN_COLS = 1536
D_FF = 4096
PLE_DIM = 256
EPS = 1e-6
NEG_BIG = -1e30
LANES = 128

COL_A = 0
COL_Q = 512
COL_K = 1024
COL_V = 1152
COL_C = 1280

ROWS_PROJ = 512
ROWS_MIX = 512
ROWS_SSM = 2048
SSM_T = 8
SSM_N = ROWS_SSM // SSM_T
SSM_LOG_N = int(math.log2(SSM_N))
FF_SPLIT = 8
VMEM_LIMIT = 56 * 1024 * 1024


def _const_spec(shape):
    nd = len(shape)
    return pl.BlockSpec(shape, lambda *_: (0,) * nd, pipeline_mode=pl.Buffered(1))


def _layer_spec(shape, layer):
    nd = len(shape)
    return pl.BlockSpec((None,) + tuple(shape), lambda *_: (layer,) + (0,) * nd,
                        pipeline_mode=pl.Buffered(1))


def _rms(x, g):
    ms = jnp.mean(x * x, axis=-1, keepdims=True)
    return x * lax.rsqrt(ms + EPS) * g


def _dot(a, b):
    return jnp.dot(a, b, preferred_element_type=F32)


def _dot_nt(a, b):
    return lax.dot_general(a, b, (((1,), (1,)), ((), ())), preferred_element_type=F32)


def _split(x):
    hi = x.astype(BF16)
    return hi, (x - hi.astype(F32)).astype(BF16)


def _seg_mean(x, pmat):
    hi, lo = _split(x)
    return _dot(hi, pmat) + _dot(lo, pmat)


def _cmul(ar, ai, br, bi):
    return ar * br - ai * bi, ar * bi + ai * br


def _rope(x, cos, sin_signed):
    width = x.shape[-1]
    lane = lax.broadcasted_iota(jnp.int32, x.shape, 1)
    first_half = (lane % HEAD_DIM) < (HEAD_DIM // 2)
    partner = jnp.where(first_half,
                        pltpu.roll(x, width - HEAD_DIM // 2, 1),
                        pltpu.roll(x, HEAD_DIM // 2, 1))
    return x * cos + partner * sin_signed


def _dup_heads(x):
    low = lax.broadcasted_iota(jnp.int32, x.shape, 1) < HEAD_DIM
    sw = pltpu.roll(x, HEAD_DIM, 1)
    return [jnp.where(low, x, sw).astype(BF16), jnp.where(low, sw, x).astype(BF16)]


def _mix_project(h_ref, g_ref, w_ref, z_ref):
    xn = _rms(h_ref[...], g_ref[...]).astype(BF16)
    z_ref[...] = _dot(xn, w_ref[...])


def _mix_mixers(layer, first_block, sinks_ref, cos_ref, sin_ref, ws_ref, bs_ref, lng_ref, lnb_ref,
                mga_ref, qg_ref, kg_ref, mgb_ref, pm_ref, ya_ref, yb_ref, zc_ref,
                z_ref, kprev_ref, vprev_ref):
    nq = ROWS_MIX // WINDOW
    zc_ref[...] = z_ref[:, COL_C:COL_C + C_WIDTH]
    pm = pm_ref[...]
    pm_kv = pm[0:LANES, 0:LANES]

    k_raw = z_ref[:, COL_K:COL_K + LANES]
    k_ms = _dot((k_raw * k_raw).astype(BF16), pm_kv)
    yield
    k_cur = _rope(k_raw * lax.rsqrt(k_ms + EPS) * kg_ref[...], cos_ref[...], sin_ref[...])
    v_cur = z_ref[:, COL_V:COL_V + LANES]
    k_dup = _dup_heads(jnp.concatenate([kprev_ref[...], k_cur], axis=0))
    v_dup = _dup_heads(jnp.concatenate([vprev_ref[...], v_cur], axis=0))
    kprev_ref[...] = k_cur[ROWS_MIX - WINDOW:, :]
    vprev_ref[...] = v_cur[ROWS_MIX - WINDOW:, :]
    q = z_ref[:, COL_Q:COL_Q + B_WIDTH]
    q_ms = jnp.concatenate(
        [_dot((q[:, s:s + 256] * q[:, s:s + 256]).astype(BF16), pm) for s in (0, 256)], axis=1)
    yield
    cos_q = jnp.concatenate([cos_ref[...]] * 4, axis=1)
    sin_q = jnp.concatenate([sin_ref[...]] * 4, axis=1)
    qr = _rope(q * lax.rsqrt(q_ms + EPS) * qg_ref[...], cos_q, sin_q) * (HEAD_DIM ** -0.5)
    low_q = (lax.broadcasted_iota(jnp.int32, qr.shape, 1) % LANES) < HEAD_DIM
    q_low = jnp.where(low_q, qr, 0.0).astype(BF16)
    q_high = jnp.where(low_q, 0.0, qr).astype(BF16)

    u = jax.nn.gelu(z_ref[:, COL_A:COL_A + A_WIDTH])
    v = jax.nn.gelu(z_ref[:, COL_A + A_WIDTH:COL_A + 2 * A_WIDTH])
    v_mean = _dot(v.astype(BF16), pm)
    yield
    vc = v - v_mean
    v_var = _dot((vc * vc).astype(BF16), pm)
    yield
    vn = (vc * lax.rsqrt(v_var + EPS) * lng_ref[...] + lnb_ref[...]).astype(BF16)
    lane = lax.broadcasted_iota(jnp.int32, (CHUNK, A_WIDTH), 1)
    in_head = [(lane >= h * HEAD_DIM) & (lane < (h + 1) * HEAD_DIM) for h in range(A_HEADS)]
    svs = []
    for c in range(ROWS_MIX // CHUNK):
        vn_c = vn[c * CHUNK:(c + 1) * CHUNK]
        stacked = jnp.concatenate(
            [jnp.where(in_head[h], vn_c, jnp.zeros((), BF16)) for h in range(A_HEADS)], axis=0)
        svs.append(_dot(ws_ref[...], stacked))
    yield
    for c in range(ROWS_MIX // CHUNK):
        rows = slice(c * CHUNK, (c + 1) * CHUNK)
        ya_ref[rows, :] = _rms(u[rows] * (svs[c] + bs_ref[...]), mga_ref[...]).astype(BF16)

    qi = lax.broadcasted_iota(jnp.int32, (4 * WINDOW, WINDOW), 0) % WINDOW
    kc = lax.broadcasted_iota(jnp.int32, (4 * WINDOW, WINDOW), 1)
    from_cur = kc <= qi
    head_slot = lax.broadcasted_iota(jnp.int32, (4 * WINDOW, 1), 0) // WINDOW
    prev_ok = kc >= jnp.where(first_block, WINDOW, 0)
    low_o = lax.broadcasted_iota(jnp.int32, (WINDOW, LANES), 1) < HEAD_DIM
    zero = jnp.zeros((), F32)

    for b in range(nq):
        rows = slice(b * WINDOW, (b + 1) * WINDOW)
        win = slice(b * WINDOW, (b + 2) * WINDOW)
        pairs = []
        for j in range(B_KV_HEADS):
            pair_a = slice((2 * j) * LANES, (2 * j + 1) * LANES)
            pair_b = slice((2 * j + 1) * LANES, (2 * j + 2) * LANES)
            qs = jnp.concatenate([q_low[rows, pair_a], q_low[rows, pair_b],
                                  q_high[rows, pair_a], q_high[rows, pair_b]], axis=0)
            s = _dot_nt(qs, k_dup[j][win])
            yield
            s_prev = s[:, 0:WINDOW]
            if b == 0:
                s_prev = jnp.where(prev_ok, s_prev, NEG_BIG)
            s = jnp.where(from_cur, s[:, WINDOW:], s_prev)
            sink = jnp.where(head_slot == 0, sinks_ref[layer, 4 * j],
                             jnp.where(head_slot == 1, sinks_ref[layer, 4 * j + 2],
                                       jnp.where(head_slot == 2, sinks_ref[layer, 4 * j + 1],
                                                 sinks_ref[layer, 4 * j + 3])))
            m = jnp.maximum(jnp.max(s, axis=-1, keepdims=True), sink)
            pr = jnp.exp(s - m)
            denom = jnp.sum(pr, axis=-1, keepdims=True) + jnp.exp(sink - m)
            pn = pr * (1.0 / denom)
            p2 = jnp.concatenate([jnp.where(from_cur, zero, pn),
                                  jnp.where(from_cur, pn, zero)], axis=1).astype(BF16)
            o = _dot(p2, v_dup[j][win])
            yield
            pairs += [jnp.where(low_o, o[0:WINDOW], o[2 * WINDOW:3 * WINDOW]),
                      jnp.where(low_o, o[WINDOW:2 * WINDOW], o[3 * WINDOW:4 * WINDOW])]
        yb = jnp.concatenate(pairs, axis=1)
        yb_ref[rows, :] = _rms(yb, mgb_ref[...]).astype(BF16)


MIXER_STAGES = 5 + 2 * B_KV_HEADS * (ROWS_MIX // WINDOW)
POST_STAGES = 4 + FF_SPLIT


def _run(*staged):
    total = max(n for _, n in staged)
    done = [0] * len(staged)
    for step in range(1, total + 1):
        for k, (gen, n) in enumerate(staged):
            while done[k] * total < step * n:
                next(gen, None)
                done[k] += 1
    for gen, _ in staged:
        for _ in gen:
            pass


def _mix_kernel(layer, blocks_per_seq, sinks_ref, h_ref, cos_ref, sin_ref, g_ref, w_ref, *rest):
    mixer_refs, (z_ref, kprev_ref, vprev_ref) = rest[:-3], rest[-3:]
    first_block = (pl.program_id(0) % blocks_per_seq) == 0

    @pl.when(first_block)
    def _():
        kprev_ref[...] = jnp.zeros_like(kprev_ref)
        vprev_ref[...] = jnp.zeros_like(vprev_ref)

    _mix_project(h_ref, g_ref, w_ref, z_ref)
    _run((_mix_mixers(layer, first_block, sinks_ref, cos_ref, sin_ref, *mixer_refs,
                      z_ref, kprev_ref, vprev_ref), MIXER_STAGES))


def _mix_param_specs(layer):
    return [
        _layer_spec((1, D_MODEL), layer),
        _layer_spec((D_MODEL, IN_COLS), layer),
        _layer_spec((CHUNK, A_HEADS * CHUNK), layer),
        _layer_spec((CHUNK, A_WIDTH), layer),
        _layer_spec((1, A_WIDTH), layer),
        _layer_spec((1, A_WIDTH), layer),
        _layer_spec((1, A_WIDTH), layer),
        _layer_spec((1, B_WIDTH), layer),
        _layer_spec((1, LANES), layer),
        _layer_spec((1, B_WIDTH), layer),
        _const_spec((256, 256)),
    ]


def _mix_out_shapes(n):
    return [jax.ShapeDtypeStruct((n, A_WIDTH), BF16),
            jax.ShapeDtypeStruct((n, B_WIDTH), BF16),
            jax.ShapeDtypeStruct((n, C_WIDTH), F32)]


_MIX_SCRATCH = [
    pltpu.VMEM((ROWS_MIX, IN_COLS), F32),
    pltpu.VMEM((WINDOW, LANES), F32),
    pltpu.VMEM((WINDOW, LANES), F32),
]


def _mix(layer, h, cos2, sin2, sinks, mix_params, seq_len):
    n = h.shape[0]
    rb = lambda w_: pl.BlockSpec((ROWS_MIX, w_), lambda i: (i, 0))
    return pl.pallas_call(
        functools.partial(_mix_kernel, layer, seq_len // ROWS_MIX),
        grid=(n // ROWS_MIX,),
        in_specs=[pl.BlockSpec(memory_space=pltpu.SMEM), rb(D_MODEL), rb(LANES), rb(LANES)]
        + _mix_param_specs(layer),
        out_specs=[rb(A_WIDTH), rb(B_WIDTH), rb(C_WIDTH)],
        out_shape=_mix_out_shapes(n),
        scratch_shapes=_MIX_SCRATCH,
        compiler_params=pltpu.CompilerParams(
            dimension_semantics=("arbitrary",), vmem_limit_bytes=VMEM_LIMIT),
        name="mix",
    )(sinks, h, cos2, sin2, *mix_params)


def _tables_kernel(are_ref, aim_ref, ldt_ref, bre_ref, bim_ref, cre_ref, cim_ref, d_ref,
                   kd_ref, wb_ref, wct_ref, astep_ref, atab_ref):
    a_re, a_im = are_ref[...], aim_ref[...]
    dt = jnp.exp(ldt_ref[...])
    mag = jnp.exp(a_re * dt)
    lr, li = mag * jnp.cos(a_im * dt), mag * jnp.sin(a_im * dt)
    den = a_re * a_re + a_im * a_im
    fr, fi = _cmul(lr - 1.0, li, a_re / den, -a_im / den)
    bbr, bbi = _cmul(fr, fi, bre_ref[...], bim_ref[...])
    c_re, c_im = cre_ref[...], cim_ref[...]

    row_g = lax.broadcasted_iota(jnp.int32, (C_WIDTH, N_STATE), 0) // C_GROUP
    col_g = lax.broadcasted_iota(jnp.int32, (C_WIDTH, N_STATE), 1) // C_STATE
    same_group = row_g == col_g

    def blockdiag(m):
        return jnp.where(same_group, jnp.concatenate([m] * C_GROUPS, axis=0), 0.0)

    def state_matrix(mr, mi):
        return jnp.concatenate([blockdiag(mr), blockdiag(mi)], axis=1)

    pows = [(jnp.ones_like(lr), jnp.zeros_like(li))]
    for _ in range(SSM_T):
        pows.append(_cmul(pows[-1][0], pows[-1][1], lr, li))

    c0_hi, c0_lo = _split(state_matrix(c_re, -c_im))
    rr = lax.broadcasted_iota(jnp.int32, (C_WIDTH, C_WIDTH), 0)
    cc = lax.broadcasted_iota(jnp.int32, (C_WIDTH, C_WIDTH), 1)
    for d in range(SSM_T):
        wr, wi = _cmul(pows[d][0], pows[d][1], bbr, bbi)
        w_full = state_matrix(wr, wi)
        wb_ref[SSM_T - 1 - d] = w_full.astype(BF16)
        w_hi, w_lo = _split(w_full)
        k = _dot_nt(w_hi, c0_hi) + _dot_nt(w_hi, c0_lo) + _dot_nt(w_lo, c0_hi)
        if d == 0:
            k = k + jnp.where(rr == cc, d_ref[...], 0.0)
        kd_ref[d] = k.astype(BF16)
        mr, mi = _cmul(pows[d + 1][0], pows[d + 1][1], c_re, c_im)
        wct_ref[d] = state_matrix(mr, -mi).astype(BF16)

    re_cols, im_cols = pl.ds(0, N_STATE), pl.ds(N_STATE, N_STATE)
    sr, si = pows[SSM_T]
    atab_ref[0:1, re_cols] = sr
    atab_ref[0:1, im_cols] = si
    for k in range(SSM_LOG_N):
        span = 1 << k
        astep_ref[pl.ds(k, 1), re_cols] = sr
        astep_ref[pl.ds(k, 1), im_cols] = si
        tr, ti = _cmul(atab_ref[0:span, re_cols], atab_ref[0:span, im_cols], sr, si)
        atab_ref[pl.ds(span, span), re_cols] = tr
        atab_ref[pl.ds(span, span), im_cols] = ti
        sr, si = _cmul(sr, si, sr, si)


def _tables(a_re, a_im, ldt, b_re, b_im, c_re, c_im, d_skip):
    depth = a_re.shape[0]
    vec = lambda w: pl.BlockSpec((None, 1, w), lambda l: (l, 0, 0))
    mat = lambda r, c: pl.BlockSpec((None, r, c), lambda l: (l, 0, 0))
    cube = lambda r, c: pl.BlockSpec((None, SSM_T, r, c), lambda l: (l, 0, 0, 0))
    return pl.pallas_call(
        _tables_kernel,
        grid=(depth,),
        in_specs=[vec(N_STATE), vec(N_STATE), vec(N_STATE),
                  mat(C_GROUP, N_STATE), mat(C_GROUP, N_STATE),
                  mat(C_GROUP, N_STATE), mat(C_GROUP, N_STATE), vec(C_WIDTH)],
        out_specs=[cube(C_WIDTH, C_WIDTH), cube(C_WIDTH, 2 * N_STATE), cube(C_WIDTH, 2 * N_STATE),
                   mat(SSM_LOG_N, 2 * N_STATE), mat(SSM_N, 2 * N_STATE)],
        out_shape=[jax.ShapeDtypeStruct((depth, SSM_T, C_WIDTH, C_WIDTH), BF16),
                   jax.ShapeDtypeStruct((depth, SSM_T, C_WIDTH, 2 * N_STATE), BF16),
                   jax.ShapeDtypeStruct((depth, SSM_T, C_WIDTH, 2 * N_STATE), BF16),
                   jax.ShapeDtypeStruct((depth, SSM_LOG_N, 2 * N_STATE), F32),
                   jax.ShapeDtypeStruct((depth, SSM_N, 2 * N_STATE), F32)],
        compiler_params=pltpu.CompilerParams(
            dimension_semantics=("arbitrary",), vmem_limit_bytes=VMEM_LIMIT),
        name="ssm_tables",
    )(a_re, a_im, ldt, b_re, b_im, c_re, c_im, d_skip)


def _ssm_kernel(blocks_per_seq, ua_ref, ub_ref, kd_ref, wb_ref, wct_ref, astep_ref, atab_ref,
                w1_ref, w2_ref, mg_ref, o_ref, inc_ref, sprev_ref, ya_ref, yb_ref, carry_ref):
    @pl.when((pl.program_id(0) % blocks_per_seq) == 0)
    def _():
        carry_ref[...] = jnp.zeros_like(carry_ref)

    xs = [jnp.concatenate([ua_ref[pl.ds(j, SSM_N, stride=SSM_T), :],
                           ub_ref[pl.ds(j, SSM_N, stride=SSM_T), :]], axis=1).astype(BF16)
          for j in range(SSM_T)]

    inc = _dot(xs[0], wb_ref[0])
    for j in range(1, SSM_T):
        inc = inc + _dot(xs[j], wb_ref[j])
    inc_ref[...] = inc

    row = lax.broadcasted_iota(jnp.int32, (SSM_N, LANES), 0)
    for lt in range(N_STATE // LANES):
        re_cols = pl.ds(lt * LANES, LANES)
        im_cols = pl.ds(N_STATE + lt * LANES, LANES)
        xr = inc_ref[:, re_cols]
        xi = inc_ref[:, im_cols]
        for k in range(SSM_LOG_N):
            sh = 1 << k
            ar = astep_ref[pl.ds(k, 1), re_cols]
            ai = astep_ref[pl.ds(k, 1), im_cols]
            keep = row >= sh
            sr = jnp.where(keep, pltpu.roll(xr, sh, 0), 0.0)
            si = jnp.where(keep, pltpu.roll(xi, sh, 0), 0.0)
            xr, xi = xr + ar * sr - ai * si, xi + ar * si + ai * sr
        cr = carry_ref[:, re_cols]
        ci = carry_ref[:, im_cols]
        tr = atab_ref[:, re_cols]
        ti = atab_ref[:, im_cols]
        xr, xi = xr + tr * cr - ti * ci, xi + tr * ci + ti * cr
        first = row == 0
        sprev_ref[:, re_cols] = jnp.where(first, cr, pltpu.roll(xr, 1, 0)).astype(BF16)
        sprev_ref[:, im_cols] = jnp.where(first, ci, pltpu.roll(xi, 1, 0)).astype(BF16)
        carry_ref[:, re_cols] = xr[SSM_N - 1:SSM_N, :]
        carry_ref[:, im_cols] = xi[SSM_N - 1:SSM_N, :]

    sprev = sprev_ref[...]
    for t in range(SSM_T):
        y = _dot_nt(sprev, wct_ref[t])
        for j in range(t + 1):
            y = y + _dot(xs[j], kd_ref[t - j])
        ya_ref[pl.ds(t, SSM_N, stride=SSM_T), :] = y[:, 0:LANES]
        yb_ref[pl.ds(t, SSM_N, stride=SSM_T), :] = y[:, LANES:2 * LANES]

    y = jax.nn.gelu(jnp.concatenate([ya_ref[...], yb_ref[...]], axis=1)).astype(BF16)
    yc = _dot(y, w1_ref[...]) * jax.nn.sigmoid(_dot(y, w2_ref[...]))
    o_ref[...] = _rms(yc, mg_ref[...]).astype(BF16)


def _ssm(layer, zc, kd, wb, wct, astep, atab, w1, w2, mg, seq_len):
    n = zc.shape[0]
    return pl.pallas_call(
        functools.partial(_ssm_kernel, seq_len // ROWS_SSM),
        grid=(n // ROWS_SSM,),
        in_specs=[
            pl.BlockSpec((ROWS_SSM, LANES), lambda i: (i, 0)),
            pl.BlockSpec((ROWS_SSM, LANES), lambda i: (i, 1)),
            _layer_spec((SSM_T, C_WIDTH, C_WIDTH), layer),
            _layer_spec((SSM_T, C_WIDTH, 2 * N_STATE), layer),
            _layer_spec((SSM_T, C_WIDTH, 2 * N_STATE), layer),
            _layer_spec((SSM_LOG_N, 2 * N_STATE), layer),
            _layer_spec((SSM_N, 2 * N_STATE), layer),
            _layer_spec((C_WIDTH, C_WIDTH), layer),
            _layer_spec((C_WIDTH, C_WIDTH), layer),
            _layer_spec((1, C_WIDTH), layer),
        ],
        out_specs=pl.BlockSpec((ROWS_SSM, C_WIDTH), lambda i: (i, 0)),
        out_shape=jax.ShapeDtypeStruct((n, C_WIDTH), BF16),
        scratch_shapes=[
            pltpu.VMEM((SSM_N, 2 * N_STATE), F32),
            pltpu.VMEM((SSM_N, 2 * N_STATE), BF16),
            pltpu.VMEM((ROWS_SSM, LANES), F32),
            pltpu.VMEM((ROWS_SSM, LANES), F32),
            pltpu.VMEM((1, 2 * N_STATE), F32),
        ],
        compiler_params=pltpu.CompilerParams(
            dimension_semantics=("arbitrary",), vmem_limit_bytes=VMEM_LIMIT),
        name="ssm",
    )(zc, zc, kd, wb, wct, astep, atab, w1, w2, mg)


def _post_stages(h_ref, ya_ref, yb_ref, yc_ref, p_ref, wo_ref, g1_ref, w1_ref, w2_ref,
                 g2_ref, wg_ref, wp_ref, *out_refs):
    mixed = (_dot(ya_ref[...], wo_ref[0:A_WIDTH, :])
             + _dot(yb_ref[...], wo_ref[A_WIDTH:A_WIDTH + B_WIDTH, :])
             + _dot(yc_ref[...], wo_ref[A_WIDTH + B_WIDTH:, :]))
    yield
    h = h_ref[...] + mixed
    hn = _rms(h, g1_ref[...]).astype(BF16)
    ple = _dot(p_ref[...].astype(BF16), wp_ref[...])
    piece = D_FF // FF_SPLIT
    up = _dot(hn, w1_ref[:, 0:piece])
    yield
    ff = None
    for c in range(FF_SPLIT):
        a = jnp.maximum(up, 0.0)
        down = _dot((a * a).astype(BF16), w2_ref[c * piece:(c + 1) * piece, :])
        if c + 1 < FF_SPLIT:
            up = _dot(hn, w1_ref[:, (c + 1) * piece:(c + 2) * piece])
        yield
        ff = down if ff is None else ff + down
    h = h + ff
    gate = _dot(_rms(h, g2_ref[...]).astype(BF16), wg_ref[...])
    yield
    h = h + jax.nn.sigmoid(gate) * ple
    for o_ref in out_refs:
        o_ref[...] = h
    yield


N_POST_IN = 12


def _post_kernel(*refs):
    _run((_post_stages(*refs), POST_STAGES))


def _post_specs(layer, row_block):
    rb = lambda w: pl.BlockSpec((ROWS_PROJ, w), lambda i: (row_block(i), 0))
    return [
        rb(D_MODEL), rb(A_WIDTH), rb(B_WIDTH), rb(C_WIDTH),
        pl.BlockSpec((None, ROWS_PROJ, PLE_DIM), lambda i: (layer, row_block(i), 0)),
        _layer_spec((D_MODEL, D_MODEL), layer),
        _layer_spec((1, D_MODEL), layer),
        _layer_spec((D_MODEL, D_FF), layer),
        _layer_spec((D_FF, D_MODEL), layer),
        _layer_spec((1, D_MODEL), layer),
        _layer_spec((D_MODEL, D_MODEL), layer),
        _layer_spec((PLE_DIM, D_MODEL), layer),
    ]


def _post(layer, h, ya, yb, yc, p, post_params):
    n = h.shape[0]
    return pl.pallas_call(
        _post_kernel,
        grid=(n // ROWS_PROJ,),
        in_specs=_post_specs(layer, lambda i: i),
        out_specs=pl.BlockSpec((ROWS_PROJ, D_MODEL), lambda i: (i, 0)),
        out_shape=jax.ShapeDtypeStruct((n, D_MODEL), F32),
        compiler_params=pltpu.CompilerParams(
            dimension_semantics=("arbitrary",), vmem_limit_bytes=VMEM_LIMIT),
        name="post",
    )(h, ya, yb, yc, p, *post_params)


def _fused_kernel(layer, blocks_per_seq, n_blocks, sinks_ref, *refs):
    post_refs, refs = refs[:N_POST_IN], refs[N_POST_IN:]
    cos_ref, sin_ref, g_ref, w_ref = refs[:4]
    mixer_refs = refs[4:-8]
    h_out_ref, ya_ref, yb_ref, zc_ref = refs[-8:-4]
    hs_ref, z_ref, kprev_ref, vprev_ref = refs[-4:]
    i = pl.program_id(0)
    first_block = (i % blocks_per_seq) == 1 % blocks_per_seq

    @pl.when((i == 0) | first_block)
    def _():
        kprev_ref[...] = jnp.zeros_like(kprev_ref)
        vprev_ref[...] = jnp.zeros_like(vprev_ref)

    @pl.when(i == 0)
    def _():
        hs_ref[...] = jnp.zeros_like(hs_ref)

    _mix_project(hs_ref, g_ref, w_ref, z_ref)
    _run((_post_stages(*post_refs, h_out_ref, hs_ref), POST_STAGES),
         (_mix_mixers(layer, first_block, sinks_ref, cos_ref, sin_ref, *mixer_refs,
                      ya_ref, yb_ref, zc_ref, z_ref, kprev_ref, vprev_ref), MIXER_STAGES))


def _fused(layer, h, ya, yb, yc, p, post_params, cos2, sin2, sinks, mix_params, seq_len):
    n = h.shape[0]
    n_blocks = n // ROWS_MIX
    merged = lambda i: jnp.minimum(i, n_blocks - 1)
    mixed = lambda i: jnp.maximum(i - 1, 0)
    rb = lambda w_: pl.BlockSpec((ROWS_MIX, w_), lambda i: (mixed(i), 0))
    return pl.pallas_call(
        functools.partial(_fused_kernel, layer, seq_len // ROWS_MIX, n_blocks),
        grid=(n_blocks + 1,),
        in_specs=[pl.BlockSpec(memory_space=pltpu.SMEM)]
        + _post_specs(layer - 1, merged)
        + [rb(LANES), rb(LANES)] + _mix_param_specs(layer),
        out_specs=[pl.BlockSpec((ROWS_PROJ, D_MODEL), lambda i: (merged(i), 0)),
                   rb(A_WIDTH), rb(B_WIDTH), rb(C_WIDTH)],
        out_shape=[jax.ShapeDtypeStruct((n, D_MODEL), F32)] + _mix_out_shapes(n),
        scratch_shapes=[pltpu.VMEM((ROWS_MIX, D_MODEL), F32)] + _MIX_SCRATCH,
        compiler_params=pltpu.CompilerParams(
            dimension_semantics=("arbitrary",), vmem_limit_bytes=VMEM_LIMIT),
        name="fused",
    )(sinks, h, ya, yb, yc, p, *post_params, cos2, sin2, *mix_params)


def kernel(x, p, positions, attn_norm_g, w_in, gmlp_ln_g, gmlp_ln_b, gmlp_ws, gmlp_bs, q_norm_g, k_norm_g, sinks, ssm_a_re, ssm_a_im, ssm_log_dt, ssm_b_re, ssm_b_im, ssm_c_re, ssm_c_im, ssm_d, glu_w1, glu_w2, mix_out_g, w_out, mlp_norm_g, w_ff1, w_ff2, ple_norm_g, w_ple_gate, w_ple_proj):
    bsz, seq_len, _ = x.shape
    depth = w_in.shape[0]
    n = bsz * seq_len

    inv = 1.0 / (ROPE_THETA ** (jnp.arange(0, HEAD_DIM, 2, dtype=F32) / HEAD_DIM))
    ang = positions.astype(F32).reshape(n, 1) * inv
    cos2 = jnp.tile(jnp.cos(ang), (1, 4))
    sin2 = jnp.tile(jnp.concatenate([-jnp.sin(ang), jnp.sin(ang)], axis=1), (1, 2))

    seg = np.arange(256) // HEAD_DIM
    pm = jnp.asarray((seg[:, None] == seg[None, :]) / HEAD_DIM, dtype=BF16)

    rows = lambda v: v.reshape(depth, 1, -1).astype(F32)
    g_attn, g_mlp, g_ple = rows(attn_norm_g), rows(mlp_norm_g), rows(ple_norm_g)
    ln_g, ln_b = rows(gmlp_ln_g), rows(gmlp_ln_b)
    mg_a = rows(mix_out_g[:, :A_WIDTH])
    mg_b = rows(mix_out_g[:, A_WIDTH:A_WIDTH + B_WIDTH])
    mg_c = rows(mix_out_g[:, A_WIDTH + B_WIDTH:])
    qg = rows(jnp.tile(q_norm_g, (1, B_Q_HEADS)))
    kg = rows(jnp.tile(k_norm_g, (1, B_KV_HEADS)))
    bs = jnp.repeat(jnp.swapaxes(gmlp_bs, 1, 2), HEAD_DIM, axis=2)
    w_a = w_in[:, :, :2 * A_WIDTH].reshape(depth, D_MODEL, A_HEADS, 2, HEAD_DIM)
    w_a = jnp.swapaxes(w_a, 2, 3).reshape(depth, D_MODEL, 2 * A_WIDTH)
    w_in_b = jnp.concatenate([w_a, w_in[:, :, 2 * A_WIDTH:]], axis=2).astype(BF16)
    causal = np.tril(np.ones((CHUNK, CHUNK), dtype=bool))
    ws_b = jnp.swapaxes(jnp.where(causal, gmlp_ws, 0.0), 1, 2).reshape(
        depth, CHUNK, A_HEADS * CHUNK).astype(BF16)
    w_out_b = w_out.astype(BF16)
    w_ff1_b, w_ff2_b = w_ff1.astype(BF16), w_ff2.astype(BF16)
    w_gate_b, w_proj_b = w_ple_gate.astype(BF16), w_ple_proj.astype(BF16)
    glu1_b, glu2_b = glu_w1.astype(BF16), glu_w2.astype(BF16)
    p2 = p.reshape(depth, n, PLE_DIM)

    kd, wb, wct, astep, atab = _tables(
        rows(ssm_a_re), rows(ssm_a_im), rows(jnp.repeat(ssm_log_dt, C_STATE, axis=1)),
        jnp.transpose(ssm_b_re, (0, 3, 1, 2)).reshape(depth, C_GROUP, N_STATE),
        jnp.transpose(ssm_b_im, (0, 3, 1, 2)).reshape(depth, C_GROUP, N_STATE),
        jnp.transpose(ssm_c_re, (0, 2, 1, 3)).reshape(depth, C_GROUP, N_STATE),
        jnp.transpose(ssm_c_im, (0, 2, 1, 3)).reshape(depth, C_GROUP, N_STATE),
        rows(ssm_d))

    mix_params = (g_attn, w_in_b, ws_b, bs, ln_g, ln_b, mg_a, qg, kg, mg_b, pm)
    post_params = (w_out_b, g_mlp, w_ff1_b, w_ff2_b, g_ple, w_gate_b, w_proj_b)
    h = x.reshape(n, D_MODEL)
    ya, yb, zc = _mix(0, h, cos2, sin2, sinks, mix_params, seq_len)
    for i in range(depth):
        yc = _ssm(i, zc, kd, wb, wct, astep, atab, glu1_b, glu2_b, mg_c, seq_len)
        if i + 1 < depth:
            h, ya, yb, zc = _fused(i + 1, h, ya, yb, yc, p2, post_params, cos2, sin2, sinks,
                                   mix_params, seq_len)
        else:
            h = _post(i, h, ya, yb, yc, p2, post_params)
    return h.reshape(bsz, seq_len, D_MODEL)
```

```python
import functools
import math

import numpy as np
import jax
import jax.numpy as jnp
from jax import lax
from jax.experimental import pallas as pl
from jax.experimental.pallas import tpu as pltpu

F32 = jnp.float32
BF16 = jnp.bfloat16

D_MODEL = 1024
HEAD_DIM = 64
A_WIDTH = 256
A_HEADS = 4
CHUNK = 128
B_WIDTH = 512
B_Q_HEADS = 8
B_KV_HEADS = 2
WINDOW = 128
ROPE_THETA = 10000.0
C_WIDTH = 256
C_GROUP = 16
C_GROUPS = 16
C_STATE = 64
N_STATE = C_GROUPS * C_STATE
IN_COLS = 1536
D_FF = 4096
PLE_DIM = 256
EPS = 1e-6
NEG_BIG = -1e30
LANES = 128

COL_A = 0
COL_Q = 512
COL_K = 1024
COL_V = 1152
COL_C = 1280

ROWS_PROJ = 512
ROWS_MIX = 512
ROWS_SSM = 2048
SSM_T = 8
SSM_N = ROWS_SSM // SSM_T
SUBLANES = 8
SCAN_STEPS = int(math.log2(SUBLANES))
FF_SPLIT = 8
VMEM_LIMIT = 56 * 1024 * 1024


def _const_spec(shape):
    nd = len(shape)
    return pl.BlockSpec(shape, lambda *_: (0,) * nd, pipeline_mode=pl.Buffered(1))


def _layer_spec(shape, layer):
    nd = len(shape)
    return pl.BlockSpec((None,) + tuple(shape), lambda *_: (layer,) + (0,) * nd,
                        pipeline_mode=pl.Buffered(1))


def _rms(x, g):
    ms = jnp.mean(x * x, axis=-1, keepdims=True)
    return x * lax.rsqrt(ms + EPS) * g


def _dot(a, b):
    return jnp.dot(a, b, preferred_element_type=F32)


def _dot_nt(a, b):
    return lax.dot_general(a, b, (((1,), (1,)), ((), ())), preferred_element_type=F32)


def _split(x):
    hi = x.astype(BF16)
    return hi, (x - hi.astype(F32)).astype(BF16)


def _seg_mean(x, pmat):
    hi, lo = _split(x)
    return _dot(hi, pmat) + _dot(lo, pmat)


def _cmul(ar, ai, br, bi):
    return ar * br - ai * bi, ar * bi + ai * br


def _rope(x, cos, sin_signed):
    width = x.shape[-1]
    lane = lax.broadcasted_iota(jnp.int32, x.shape, 1)
    first_half = (lane % HEAD_DIM) < (HEAD_DIM // 2)
    partner = jnp.where(first_half,
                        pltpu.roll(x, width - HEAD_DIM // 2, 1),
                        pltpu.roll(x, HEAD_DIM // 2, 1))
    return x * cos + partner * sin_signed


def _dup_heads(x):
    low = lax.broadcasted_iota(jnp.int32, x.shape, 1) < HEAD_DIM
    sw = pltpu.roll(x, HEAD_DIM, 1)
    return [jnp.where(low, x, sw).astype(BF16), jnp.where(low, sw, x).astype(BF16)]


def _mix_project(h_ref, g_ref, w_ref, z_ref):
    xn = _rms(h_ref[...], g_ref[...]).astype(BF16)
    z_ref[...] = _dot(xn, w_ref[...])


def _mix_mixers(layer, first_block, sinks_ref, cos_ref, sin_ref, ws_ref, bs_ref, lng_ref, lnb_ref,
                mga_ref, qg_ref, kg_ref, mgb_ref, pm_ref, ya_ref, yb_ref, zc_ref,
                z_ref, kprev_ref, vprev_ref):
    nq = ROWS_MIX // WINDOW
    zc_ref[...] = z_ref[:, COL_C:COL_C + C_WIDTH]
    pm = pm_ref[...]
    pm_kv = pm[0:LANES, 0:LANES]

    k_raw = z_ref[:, COL_K:COL_K + LANES]
    k_ms = _dot((k_raw * k_raw).astype(BF16), pm_kv)
    yield
    k_cur = _rope(k_raw * lax.rsqrt(k_ms + EPS) * kg_ref[...], cos_ref[...], sin_ref[...])
    v_cur = z_ref[:, COL_V:COL_V + LANES]
    k_dup = _dup_heads(jnp.concatenate([kprev_ref[...], k_cur], axis=0))
    v_dup = _dup_heads(jnp.concatenate([vprev_ref[...], v_cur], axis=0))
    kprev_ref[...] = k_cur[ROWS_MIX - WINDOW:, :]
    vprev_ref[...] = v_cur[ROWS_MIX - WINDOW:, :]
    q = z_ref[:, COL_Q:COL_Q + B_WIDTH]
    q_ms = jnp.concatenate(
        [_dot((q[:, s:s + 256] * q[:, s:s + 256]).astype(BF16), pm) for s in (0, 256)], axis=1)
    yield
    cos_q = jnp.concatenate([cos_ref[...]] * 4, axis=1)
    sin_q = jnp.concatenate([sin_ref[...]] * 4, axis=1)
    qr = _rope(q * lax.rsqrt(q_ms + EPS) * qg_ref[...], cos_q, sin_q) * (HEAD_DIM ** -0.5)
    low_q = (lax.broadcasted_iota(jnp.int32, qr.shape, 1) % LANES) < HEAD_DIM
    q_low = jnp.where(low_q, qr, 0.0).astype(BF16)
    q_high = jnp.where(low_q, 0.0, qr).astype(BF16)

    u = jax.nn.gelu(z_ref[:, COL_A:COL_A + A_WIDTH])
    v = jax.nn.gelu(z_ref[:, COL_A + A_WIDTH:COL_A + 2 * A_WIDTH])
    v_mean = _dot(v.astype(BF16), pm)
    yield
    vc = v - v_mean
    v_var = _dot((vc * vc).astype(BF16), pm)
    yield
    vn = (vc * lax.rsqrt(v_var + EPS) * lng_ref[...] + lnb_ref[...]).astype(BF16)
    lane = lax.broadcasted_iota(jnp.int32, (CHUNK, A_WIDTH), 1)
    in_head = [(lane >= h * HEAD_DIM) & (lane < (h + 1) * HEAD_DIM) for h in range(A_HEADS)]
    svs = []
    for c in range(ROWS_MIX // CHUNK):
        vn_c = vn[c * CHUNK:(c + 1) * CHUNK]
        stacked = jnp.concatenate(
            [jnp.where(in_head[h], vn_c, jnp.zeros((), BF16)) for h in range(A_HEADS)], axis=0)
        svs.append(_dot(ws_ref[...], stacked))
    yield
    for c in range(ROWS_MIX // CHUNK):
        rows = slice(c * CHUNK, (c + 1) * CHUNK)
        ya_ref[rows, :] = _rms(u[rows] * (svs[c] + bs_ref[...]), mga_ref[...]).astype(BF16)

    qi = lax.broadcasted_iota(jnp.int32, (4 * WINDOW, WINDOW), 0) % WINDOW
    kc = lax.broadcasted_iota(jnp.int32, (4 * WINDOW, WINDOW), 1)
    from_cur = kc <= qi
    head_slot = lax.broadcasted_iota(jnp.int32, (4 * WINDOW, 1), 0) // WINDOW
    prev_ok = kc >= jnp.where(first_block, WINDOW, 0)
    low_o = lax.broadcasted_iota(jnp.int32, (WINDOW, LANES), 1) < HEAD_DIM
    zero = jnp.zeros((), F32)

    for b in range(nq):
        rows = slice(b * WINDOW, (b + 1) * WINDOW)
        win = slice(b * WINDOW, (b + 2) * WINDOW)
        pairs = []
        for j in range(B_KV_HEADS):
            pair_a = slice((2 * j) * LANES, (2 * j + 1) * LANES)
            pair_b = slice((2 * j + 1) * LANES, (2 * j + 2) * LANES)
            qs = jnp.concatenate([q_low[rows, pair_a], q_low[rows, pair_b],
                                  q_high[rows, pair_a], q_high[rows, pair_b]], axis=0)
            s = _dot_nt(qs, k_dup[j][win])
            yield
            s_prev = s[:, 0:WINDOW]
            if b == 0:
                s_prev = jnp.where(prev_ok, s_prev, NEG_BIG)
            s = jnp.where(from_cur, s[:, WINDOW:], s_prev)
            sink = jnp.where(head_slot == 0, sinks_ref[layer, 4 * j],
                             jnp.where(head_slot == 1, sinks_ref[layer, 4 * j + 2],
                                       jnp.where(head_slot == 2, sinks_ref[layer, 4 * j + 1],
                                                 sinks_ref[layer, 4 * j + 3])))
            m = jnp.maximum(jnp.max(s, axis=-1, keepdims=True), sink)
            pr = jnp.exp(s - m)
            denom = jnp.sum(pr, axis=-1, keepdims=True) + jnp.exp(sink - m)
            pn = pr * (1.0 / denom)
            p2 = jnp.concatenate([jnp.where(from_cur, zero, pn),
                                  jnp.where(from_cur, pn, zero)], axis=1).astype(BF16)
            o = _dot(p2, v_dup[j][win])
            yield
            pairs += [jnp.where(low_o, o[0:WINDOW], o[2 * WINDOW:3 * WINDOW]),
                      jnp.where(low_o, o[WINDOW:2 * WINDOW], o[3 * WINDOW:4 * WINDOW])]
        yb = jnp.concatenate(pairs, axis=1)
        yb_ref[rows, :] = _rms(yb, mgb_ref[...]).astype(BF16)


MIXER_STAGES = 5 + 2 * B_KV_HEADS * (ROWS_MIX // WINDOW)
POST_STAGES = 4 + FF_SPLIT


def _run(*staged):
    total = max(n for _, n in staged)
    done = [0] * len(staged)
    for step in range(1, total + 1):
        for k, (gen, n) in enumerate(staged):
            while done[k] * total < step * n:
                next(gen, None)
                done[k] += 1
    for gen, _ in staged:
        for _ in gen:
            pass


def _mix_kernel(layer, blocks_per_seq, sinks_ref, h_ref, cos_ref, sin_ref, g_ref, w_ref, *rest):
    mixer_refs, (z_ref, kprev_ref, vprev_ref) = rest[:-3], rest[-3:]
    first_block = (pl.program_id(0) % blocks_per_seq) == 0

    @pl.when(first_block)
    def _():
        kprev_ref[...] = jnp.zeros_like(kprev_ref)
        vprev_ref[...] = jnp.zeros_like(vprev_ref)

    _mix_project(h_ref, g_ref, w_ref, z_ref)
    _run((_mix_mixers(layer, first_block, sinks_ref, cos_ref, sin_ref, *mixer_refs,
                      z_ref, kprev_ref, vprev_ref), MIXER_STAGES))


def _mix_param_specs(layer):
    return [
        _layer_spec((1, D_MODEL), layer),
        _layer_spec((D_MODEL, IN_COLS), layer),
        _layer_spec((CHUNK, A_HEADS * CHUNK), layer),
        _layer_spec((CHUNK, A_WIDTH), layer),
        _layer_spec((1, A_WIDTH), layer),
        _layer_spec((1, A_WIDTH), layer),
        _layer_spec((1, A_WIDTH), layer),
        _layer_spec((1, B_WIDTH), layer),
        _layer_spec((1, LANES), layer),
        _layer_spec((1, B_WIDTH), layer),
        _const_spec((256, 256)),
    ]


def _mix_out_shapes(n):
    return [jax.ShapeDtypeStruct((n, A_WIDTH), BF16),
            jax.ShapeDtypeStruct((n, B_WIDTH), BF16),
            jax.ShapeDtypeStruct((n, C_WIDTH), F32)]


_MIX_SCRATCH = [
    pltpu.VMEM((ROWS_MIX, IN_COLS), F32),
    pltpu.VMEM((WINDOW, LANES), F32),
    pltpu.VMEM((WINDOW, LANES), F32),
]


def _mix(layer, h, cos2, sin2, sinks, mix_params, seq_len):
    n = h.shape[0]
    rb = lambda w_: pl.BlockSpec((ROWS_MIX, w_), lambda i: (i, 0))
    return pl.pallas_call(
        functools.partial(_mix_kernel, layer, seq_len // ROWS_MIX),
        grid=(n // ROWS_MIX,),
        in_specs=[pl.BlockSpec(memory_space=pltpu.SMEM), rb(D_MODEL), rb(LANES), rb(LANES)]
        + _mix_param_specs(layer),
        out_specs=[rb(A_WIDTH), rb(B_WIDTH), rb(C_WIDTH)],
        out_shape=_mix_out_shapes(n),
        scratch_shapes=_MIX_SCRATCH,
        compiler_params=pltpu.CompilerParams(
            dimension_semantics=("arbitrary",), vmem_limit_bytes=VMEM_LIMIT),
        name="mix",
    )(sinks, h, cos2, sin2, *mix_params)


def _tables_kernel(are_ref, aim_ref, ldt_ref, bre_ref, bim_ref, cre_ref, cim_ref, d_ref,
                   kd_ref, wb_ref, wct_ref, astep_ref, atab_ref):
    a_re, a_im = are_ref[...], aim_ref[...]
    dt = jnp.exp(ldt_ref[...])
    mag = jnp.exp(a_re * dt)
    lr, li = mag * jnp.cos(a_im * dt), mag * jnp.sin(a_im * dt)
    den = a_re * a_re + a_im * a_im
    fr, fi = _cmul(lr - 1.0, li, a_re / den, -a_im / den)
    bbr, bbi = _cmul(fr, fi, bre_ref[...], bim_ref[...])
    c_re, c_im = cre_ref[...], cim_ref[...]

    row_g = lax.broadcasted_iota(jnp.int32, (C_WIDTH, N_STATE), 0) // C_GROUP
    col_g = lax.broadcasted_iota(jnp.int32, (C_WIDTH, N_STATE), 1) // C_STATE
    same_group = row_g == col_g

    def blockdiag(m):
        return jnp.where(same_group, jnp.concatenate([m] * C_GROUPS, axis=0), 0.0)

    def state_matrix(mr, mi):
        return jnp.concatenate([blockdiag(mr), blockdiag(mi)], axis=1)

    pows = [(jnp.ones_like(lr), jnp.zeros_like(li))]
    for _ in range(SSM_T):
        pows.append(_cmul(pows[-1][0], pows[-1][1], lr, li))

    c0_hi, c0_lo = _split(state_matrix(c_re, -c_im))
    rr = lax.broadcasted_iota(jnp.int32, (C_WIDTH, C_WIDTH), 0)
    cc = lax.broadcasted_iota(jnp.int32, (C_WIDTH, C_WIDTH), 1)
    for d in range(SSM_T):
        wr, wi = _cmul(pows[d][0], pows[d][1], bbr, bbi)
        w_full = state_matrix(wr, wi)
        wb_ref[SSM_T - 1 - d] = w_full.astype(BF16)
        w_hi, w_lo = _split(w_full)
        k = _dot_nt(w_hi, c0_hi) + _dot_nt(w_hi, c0_lo) + _dot_nt(w_lo, c0_hi)
        if d == 0:
            k = k + jnp.where(rr == cc, d_ref[...], 0.0)
        kd_ref[d] = k.astype(BF16)
        mr, mi = _cmul(pows[d + 1][0], pows[d + 1][1], c_re, c_im)
        wct_ref[d] = state_matrix(mr, -mi).astype(BF16)

    re_cols, im_cols = pl.ds(0, N_STATE), pl.ds(N_STATE, N_STATE)
    sr, si = pows[SSM_T]
    atab_ref[0:1, re_cols] = sr
    atab_ref[0:1, im_cols] = si
    for k in range(SCAN_STEPS):
        span = 1 << k
        astep_ref[pl.ds(k, 1), re_cols] = sr
        astep_ref[pl.ds(k, 1), im_cols] = si
        tr, ti = _cmul(atab_ref[0:span, re_cols], atab_ref[0:span, im_cols], sr, si)
        atab_ref[pl.ds(span, span), re_cols] = tr
        atab_ref[pl.ds(span, span), im_cols] = ti
        sr, si = _cmul(sr, si, sr, si)


def _tables(a_re, a_im, ldt, b_re, b_im, c_re, c_im, d_skip):
    depth = a_re.shape[0]
    vec = lambda w: pl.BlockSpec((None, 1, w), lambda l: (l, 0, 0))
    mat = lambda r, c: pl.BlockSpec((None, r, c), lambda l: (l, 0, 0))
    cube = lambda r, c: pl.BlockSpec((None, SSM_T, r, c), lambda l: (l, 0, 0, 0))
    return pl.pallas_call(
        _tables_kernel,
        grid=(depth,),
        in_specs=[vec(N_STATE), vec(N_STATE), vec(N_STATE),
                  mat(C_GROUP, N_STATE), mat(C_GROUP, N_STATE),
                  mat(C_GROUP, N_STATE), mat(C_GROUP, N_STATE), vec(C_WIDTH)],
        out_specs=[cube(C_WIDTH, C_WIDTH), cube(C_WIDTH, 2 * N_STATE), cube(C_WIDTH, 2 * N_STATE),
                   mat(SCAN_STEPS, 2 * N_STATE), mat(SUBLANES, 2 * N_STATE)],
        out_shape=[jax.ShapeDtypeStruct((depth, SSM_T, C_WIDTH, C_WIDTH), BF16),
                   jax.ShapeDtypeStruct((depth, SSM_T, C_WIDTH, 2 * N_STATE), BF16),
                   jax.ShapeDtypeStruct((depth, SSM_T, C_WIDTH, 2 * N_STATE), BF16),
                   jax.ShapeDtypeStruct((depth, SCAN_STEPS, 2 * N_STATE), F32),
                   jax.ShapeDtypeStruct((depth, SUBLANES, 2 * N_STATE), F32)],
        compiler_params=pltpu.CompilerParams(
            dimension_semantics=("arbitrary",), vmem_limit_bytes=VMEM_LIMIT),
        name="ssm_tables",
    )(a_re, a_im, ldt, b_re, b_im, c_re, c_im, d_skip)


def _glu_out(glu_a, glu_b, mg_ref, o_ref):
    o_ref[...] = _rms(glu_a * jax.nn.sigmoid(glu_b), mg_ref[...]).astype(BF16)


def _ssm_kernel(blocks_per_seq, n_blocks, *refs):
    w1_ref, w2_ref, mg_ref, o_ref, _, _, ya_ref, yb_ref, carry_ref = refs[-9:]
    i = pl.program_id(0)

    @pl.when(i == 0)
    def _():
        ya_ref[...] = jnp.zeros_like(ya_ref)
        yb_ref[...] = jnp.zeros_like(yb_ref)

    @pl.when((i % blocks_per_seq) == 0)
    def _():
        carry_ref[...] = jnp.zeros_like(carry_ref)

    @pl.when(i < n_blocks)
    def _():
        _ssm_block(*refs)

    @pl.when(i == n_blocks)
    def _():
        y_prev = jax.nn.gelu(jnp.concatenate([ya_ref[...], yb_ref[...]], axis=1)).astype(BF16)
        _glu_out(_dot(y_prev, w1_ref[...]), _dot(y_prev, w2_ref[...]), mg_ref, o_ref)


def _ssm_block(ua_ref, ub_ref, kd_ref, wb_ref, wct_ref, astep_ref, atab_ref,
               w1_ref, w2_ref, mg_ref, o_ref, inc_ref, sprev_ref, ya_ref, yb_ref, carry_ref):
    xs = [jnp.concatenate([ua_ref[pl.ds(j, SSM_N, stride=SSM_T), :],
                           ub_ref[pl.ds(j, SSM_N, stride=SSM_T), :]], axis=1).astype(BF16)
          for j in range(SSM_T)]

    inc = _dot(xs[0], wb_ref[0])
    for j in range(1, SSM_T // 2):
        inc = inc + _dot(xs[j], wb_ref[j])
    y_prev = jax.nn.gelu(jnp.concatenate([ya_ref[...], yb_ref[...]], axis=1)).astype(BF16)
    glu_a, glu_b = _dot(y_prev, w1_ref[...]), _dot(y_prev, w2_ref[...])
    for j in range(SSM_T // 2, SSM_T):
        inc = inc + _dot(xs[j], wb_ref[j])
    inc_ref[...] = inc
    _glu_out(glu_a, glu_b, mg_ref, o_ref)

    n_groups = SSM_N // SUBLANES
    sub = lax.broadcasted_iota(jnp.int32, (n_groups, SUBLANES, LANES), 1)
    sub2 = lax.broadcasted_iota(jnp.int32, (SUBLANES, LANES), 0)
    y_lag = []
    for lt in range(N_STATE // LANES):
        re_cols = pl.ds(lt * LANES, LANES)
        im_cols = pl.ds(N_STATE + lt * LANES, LANES)
        xr = inc_ref[:, re_cols].reshape(n_groups, SUBLANES, LANES)
        xi = inc_ref[:, im_cols].reshape(n_groups, SUBLANES, LANES)
        for k in range(SCAN_STEPS):
            sh = 1 << k
            ar = astep_ref[pl.ds(k, 1), re_cols]
            ai = astep_ref[pl.ds(k, 1), im_cols]
            keep = sub >= sh
            sr = jnp.where(keep, pltpu.roll(xr, sh, 1), 0.0)
            si = jnp.where(keep, pltpu.roll(xi, sh, 1), 0.0)
            xr, xi = xr + ar * sr - ai * si, xi + ar * si + ai * sr
        tr = atab_ref[:, re_cols]
        ti = atab_ref[:, im_cols]
        er = jnp.broadcast_to(carry_ref[:, re_cols], (SUBLANES, LANES))
        ei = jnp.broadcast_to(carry_ref[:, im_cols], (SUBLANES, LANES))
        before_r, before_i = [], []
        for g in range(n_groups):
            fr = xr[g] + tr * er - ti * ei
            fi = xi[g] + tr * ei + ti * er
            before_r.append(jnp.where(sub2 == 0, er, pltpu.roll(fr, 1, 0)))
            before_i.append(jnp.where(sub2 == 0, ei, pltpu.roll(fi, 1, 0)))
            er = jnp.broadcast_to(fr[SUBLANES - 1:SUBLANES, :], (SUBLANES, LANES))
            ei = jnp.broadcast_to(fi[SUBLANES - 1:SUBLANES, :], (SUBLANES, LANES))
        sprev_ref[:, re_cols] = jnp.concatenate(before_r, axis=0).astype(BF16)
        sprev_ref[:, im_cols] = jnp.concatenate(before_i, axis=0).astype(BF16)
        carry_ref[:, re_cols] = er[0:1, :]
        carry_ref[:, im_cols] = ei[0:1, :]

        n_tiles = N_STATE // LANES
        for t in range(lt * SSM_T // n_tiles, (lt + 1) * SSM_T // n_tiles):
            y = _dot(xs[0], kd_ref[t])
            for j in range(1, t + 1):
                y = y + _dot(xs[j], kd_ref[t - j])
            y_lag.append(y)

    sprev = sprev_ref[...]
    for t in range(SSM_T):
        y = y_lag[t] + _dot_nt(sprev, wct_ref[t])
        ya_ref[pl.ds(t, SSM_N, stride=SSM_T), :] = y[:, 0:LANES]
        yb_ref[pl.ds(t, SSM_N, stride=SSM_T), :] = y[:, LANES:2 * LANES]


def _ssm(layer, zc, kd, wb, wct, astep, atab, w1, w2, mg, seq_len):
    n = zc.shape[0]
    n_blocks = n // ROWS_SSM
    scanned = lambda i: jnp.minimum(i, n_blocks - 1)
    return pl.pallas_call(
        functools.partial(_ssm_kernel, seq_len // ROWS_SSM, n_blocks),
        grid=(n_blocks + 1,),
        in_specs=[
            pl.BlockSpec((ROWS_SSM, LANES), lambda i: (scanned(i), 0)),
            pl.BlockSpec((ROWS_SSM, LANES), lambda i: (scanned(i), 1)),
            _layer_spec((SSM_T, C_WIDTH, C_WIDTH), layer),
            _layer_spec((SSM_T, C_WIDTH, 2 * N_STATE), layer),
            _layer_spec((SSM_T, C_WIDTH, 2 * N_STATE), layer),
            _layer_spec((SCAN_STEPS, 2 * N_STATE), layer),
            _layer_spec((SUBLANES, 2 * N_STATE), layer),
            _layer_spec((C_WIDTH, C_WIDTH), layer),
            _layer_spec((C_WIDTH, C_WIDTH), layer),
            _layer_spec((1, C_WIDTH), layer),
        ],
        out_specs=pl.BlockSpec((ROWS_SSM, C_WIDTH), lambda i: (jnp.maximum(i - 1, 0), 0)),
        out_shape=jax.ShapeDtypeStruct((n, C_WIDTH), BF16),
        scratch_shapes=[
            pltpu.VMEM((SSM_N, 2 * N_STATE), F32),
            pltpu.VMEM((SSM_N, 2 * N_STATE), BF16),
            pltpu.VMEM((ROWS_SSM, LANES), F32),
            pltpu.VMEM((ROWS_SSM, LANES), F32),
            pltpu.VMEM((1, 2 * N_STATE), F32),
        ],
        compiler_params=pltpu.CompilerParams(
            dimension_semantics=("arbitrary",), vmem_limit_bytes=VMEM_LIMIT),
        name="ssm",
    )(zc, zc, kd, wb, wct, astep, atab, w1, w2, mg)


def _post_stages(h_ref, ya_ref, yb_ref, yc_ref, p_ref, wo_ref, g1_ref, w1_ref, w2_ref,
                 g2_ref, wg_ref, wp_ref, *out_refs):
    mixed = (_dot(ya_ref[...], wo_ref[0:A_WIDTH, :])
             + _dot(yb_ref[...], wo_ref[A_WIDTH:A_WIDTH + B_WIDTH, :])
             + _dot(yc_ref[...], wo_ref[A_WIDTH + B_WIDTH:, :]))
    yield
    h = h_ref[...] + mixed
    hn = _rms(h, g1_ref[...]).astype(BF16)
    ple = _dot(p_ref[...].astype(BF16), wp_ref[...])
    piece = D_FF // FF_SPLIT
    up = _dot(hn, w1_ref[:, 0:piece])
    yield
    ff = None
    for c in range(FF_SPLIT):
        a = jnp.maximum(up, 0.0)
        down = _dot((a * a).astype(BF16), w2_ref[c * piece:(c + 1) * piece, :])
        if c + 1 < FF_SPLIT:
            up = _dot(hn, w1_ref[:, (c + 1) * piece:(c + 2) * piece])
        yield
        ff = down if ff is None else ff + down
    h = h + ff
    gate = _dot(_rms(h, g2_ref[...]).astype(BF16), wg_ref[...])
    yield
    h = h + jax.nn.sigmoid(gate) * ple
    for o_ref in out_refs:
        o_ref[...] = h
    yield


N_POST_IN = 12


def _post_kernel(*refs):
    _run((_post_stages(*refs), POST_STAGES))


def _post_specs(layer, row_block):
    rb = lambda w: pl.BlockSpec((ROWS_PROJ, w), lambda i: (row_block(i), 0))
    return [
        rb(D_MODEL), rb(A_WIDTH), rb(B_WIDTH), rb(C_WIDTH),
        pl.BlockSpec((None, ROWS_PROJ, PLE_DIM), lambda i: (layer, row_block(i), 0)),
        _layer_spec((D_MODEL, D_MODEL), layer),
        _layer_spec((1, D_MODEL), layer),
        _layer_spec((D_MODEL, D_FF), layer),
        _layer_spec((D_FF, D_MODEL), layer),
        _layer_spec((1, D_MODEL), layer),
        _layer_spec((D_MODEL, D_MODEL), layer),
        _layer_spec((PLE_DIM, D_MODEL), layer),
    ]


def _post(layer, h, ya, yb, yc, p, post_params):
    n = h.shape[0]
    return pl.pallas_call(
        _post_kernel,
        grid=(n // ROWS_PROJ,),
        in_specs=_post_specs(layer, lambda i: i),
        out_specs=pl.BlockSpec((ROWS_PROJ, D_MODEL), lambda i: (i, 0)),
        out_shape=jax.ShapeDtypeStruct((n, D_MODEL), F32),
        compiler_params=pltpu.CompilerParams(
            dimension_semantics=("arbitrary",), vmem_limit_bytes=VMEM_LIMIT),
        name="post",
    )(h, ya, yb, yc, p, *post_params)


def _fused_kernel(layer, blocks_per_seq, n_blocks, sinks_ref, *refs):
    post_refs, refs = refs[:N_POST_IN], refs[N_POST_IN:]
    cos_ref, sin_ref, g_ref, w_ref = refs[:4]
    mixer_refs = refs[4:-8]
    h_out_ref, ya_ref, yb_ref, zc_ref = refs[-8:-4]
    hs_ref, z_ref, kprev_ref, vprev_ref = refs[-4:]
    i = pl.program_id(0)
    first_block = (i % blocks_per_seq) == 1 % blocks_per_seq

    @pl.when((i == 0) | first_block)
    def _():
        kprev_ref[...] = jnp.zeros_like(kprev_ref)
        vprev_ref[...] = jnp.zeros_like(vprev_ref)

    @pl.when(i == 0)
    def _():
        hs_ref[...] = jnp.zeros_like(hs_ref)

    _mix_project(hs_ref, g_ref, w_ref, z_ref)
    _run((_post_stages(*post_refs, h_out_ref, hs_ref), POST_STAGES),
         (_mix_mixers(layer, first_block, sinks_ref, cos_ref, sin_ref, *mixer_refs,
                      ya_ref, yb_ref, zc_ref, z_ref, kprev_ref, vprev_ref), MIXER_STAGES))


def _fused(layer, h, ya, yb, yc, p, post_params, cos2, sin2, sinks, mix_params, seq_len):
    n = h.shape[0]
    n_blocks = n // ROWS_MIX
    merged = lambda i: jnp.minimum(i, n_blocks - 1)
    mixed = lambda i: jnp.maximum(i - 1, 0)
    rb = lambda w_: pl.BlockSpec((ROWS_MIX, w_), lambda i: (mixed(i), 0))
    return pl.pallas_call(
        functools.partial(_fused_kernel, layer, seq_len // ROWS_MIX, n_blocks),
        grid=(n_blocks + 1,),
        in_specs=[pl.BlockSpec(memory_space=pltpu.SMEM)]
        + _post_specs(layer - 1, merged)
        + [rb(LANES), rb(LANES)] + _mix_param_specs(layer),
        out_specs=[pl.BlockSpec((ROWS_PROJ, D_MODEL), lambda i: (merged(i), 0)),
                   rb(A_WIDTH), rb(B_WIDTH), rb(C_WIDTH)],
        out_shape=[jax.ShapeDtypeStruct((n, D_MODEL), F32)] + _mix_out_shapes(n),
        scratch_shapes=[pltpu.VMEM((ROWS_MIX, D_MODEL), F32)] + _MIX_SCRATCH,
        compiler_params=pltpu.CompilerParams(
            dimension_semantics=("arbitrary",), vmem_limit_bytes=VMEM_LIMIT),
        name="fused",
    )(sinks, h, ya, yb, yc, p, *post_params, cos2, sin2, *mix_params)


def kernel(x, p, positions, attn_norm_g, w_in, gmlp_ln_g, gmlp_ln_b, gmlp_ws, gmlp_bs, q_norm_g, k_norm_g, sinks, ssm_a_re, ssm_a_im, ssm_log_dt, ssm_b_re, ssm_b_im, ssm_c_re, ssm_c_im, ssm_d, glu_w1, glu_w2, mix_out_g, w_out, mlp_norm_g, w_ff1, w_ff2, ple_norm_g, w_ple_gate, w_ple_proj):
    bsz, seq_len, _ = x.shape
    depth = w_in.shape[0]
    n = bsz * seq_len

    inv = 1.0 / (ROPE_THETA ** (jnp.arange(0, HEAD_DIM, 2, dtype=F32) / HEAD_DIM))
    ang = positions.astype(F32).reshape(n, 1) * inv
    cos2 = jnp.tile(jnp.cos(ang), (1, 4))
    sin2 = jnp.tile(jnp.concatenate([-jnp.sin(ang), jnp.sin(ang)], axis=1), (1, 2))

    seg = np.arange(256) // HEAD_DIM
    pm = jnp.asarray((seg[:, None] == seg[None, :]) / HEAD_DIM, dtype=BF16)

    rows = lambda v: v.reshape(depth, 1, -1).astype(F32)
    g_attn, g_mlp, g_ple = rows(attn_norm_g), rows(mlp_norm_g), rows(ple_norm_g)
    ln_g, ln_b = rows(gmlp_ln_g), rows(gmlp_ln_b)
    mg_a = rows(mix_out_g[:, :A_WIDTH])
    mg_b = rows(mix_out_g[:, A_WIDTH:A_WIDTH + B_WIDTH])
    mg_c = rows(mix_out_g[:, A_WIDTH + B_WIDTH:])
    qg = rows(jnp.tile(q_norm_g, (1, B_Q_HEADS)))
    kg = rows(jnp.tile(k_norm_g, (1, B_KV_HEADS)))
    bs = jnp.repeat(jnp.swapaxes(gmlp_bs, 1, 2), HEAD_DIM, axis=2)
    w_a = w_in[:, :, :2 * A_WIDTH].reshape(depth, D_MODEL, A_HEADS, 2, HEAD_DIM)
    w_a = jnp.swapaxes(w_a, 2, 3).reshape(depth, D_MODEL, 2 * A_WIDTH)
    w_in_b = jnp.concatenate([w_a, w_in[:, :, 2 * A_WIDTH:]], axis=2).astype(BF16)
    causal = np.tril(np.ones((CHUNK, CHUNK), dtype=bool))
    ws_b = jnp.swapaxes(jnp.where(causal, gmlp_ws, 0.0), 1, 2).reshape(
        depth, CHUNK, A_HEADS * CHUNK).astype(BF16)
    w_out_b = w_out.astype(BF16)
    w_ff1_b, w_ff2_b = w_ff1.astype(BF16), w_ff2.astype(BF16)
    w_gate_b, w_proj_b = w_ple_gate.astype(BF16), w_ple_proj.astype(BF16)
    glu1_b, glu2_b = glu_w1.astype(BF16), glu_w2.astype(BF16)
    p2 = p.reshape(depth, n, PLE_DIM)

    kd, wb, wct, astep, atab = _tables(
        rows(ssm_a_re), rows(ssm_a_im), rows(jnp.repeat(ssm_log_dt, C_STATE, axis=1)),
        jnp.transpose(ssm_b_re, (0, 3, 1, 2)).reshape(depth, C_GROUP, N_STATE),
        jnp.transpose(ssm_b_im, (0, 3, 1, 2)).reshape(depth, C_GROUP, N_STATE),
        jnp.transpose(ssm_c_re, (0, 2, 1, 3)).reshape(depth, C_GROUP, N_STATE),
        jnp.transpose(ssm_c_im, (0, 2, 1, 3)).reshape(depth, C_GROUP, N_STATE),
        rows(ssm_d))

    mix_params = (g_attn, w_in_b, ws_b, bs, ln_g, ln_b, mg_a, qg, kg, mg_b, pm)
    post_params = (w_out_b, g_mlp, w_ff1_b, w_ff2_b, g_ple, w_gate_b, w_proj_b)
    h = x.reshape(n, D_MODEL)
    ya, yb, zc = _mix(0, h, cos2, sin2, sinks, mix_params, seq_len)
    for i in range(depth):
        yc = _ssm(i, zc, kd, wb, wct, astep, atab, glu1_b, glu2_b, mg_c, seq_len)
        if i + 1 < depth:
            h, ya, yb, zc = _fused(i + 1, h, ya, yb, yc, p2, post_params, cos2, sin2, sinks,
                                   mix_params, seq_len)
        else:
            h = _post(i, h, ya, yb, yc, p2, post_params)
    return h.reshape(bsz, seq_len, D_MODEL)
```

```python
import functools
import itertools
import math

import numpy as np
import jax
import jax.numpy as jnp
from jax import lax
from jax.experimental import pallas as pl
from jax.experimental.pallas import tpu as pltpu

F32 = jnp.float32
BF16 = jnp.bfloat16

D_MODEL = 1024
HEAD_DIM = 64
A_WIDTH = 256
A_HEADS = 4
CHUNK = 128
B_WIDTH = 512
B_Q_HEADS = 8
B_KV_HEADS = 2
WINDOW = 128
ROPE_THETA = 10000.0
C_WIDTH = 256
C_GROUP = 16
C_GROUPS = 16
C_STATE = 64
N_STATE = C_GROUPS * C_STATE
IN_COLS = 1536
D_FF = 4096
PLE_DIM = 256
EPS = 1e-6
NEG_BIG = -1e30
LANES = 128

COL_A = 0
COL_Q = 512
COL_K = 1024
COL_V = 1152
COL_C = 1280

ROWS_PROJ = 512
ROWS_MIX = 512
ROWS_SSM = 2048
SSM_T = 8
SSM_N = ROWS_SSM // SSM_T
SUBLANES = 8
SCAN_STEPS = int(math.log2(SUBLANES))
FF_SPLIT = 8
VMEM_LIMIT = 56 * 1024 * 1024


def _const_spec(shape):
    nd = len(shape)
    return pl.BlockSpec(shape, lambda *_: (0,) * nd, pipeline_mode=pl.Buffered(1))


def _layer_spec(shape, layer):
    nd = len(shape)
    return pl.BlockSpec((None,) + tuple(shape), lambda *_: (layer,) + (0,) * nd,
                        pipeline_mode=pl.Buffered(1))


def _rms(x, g):
    ms = jnp.mean(x * x, axis=-1, keepdims=True)
    return x * lax.rsqrt(ms + EPS) * g


def _dot(a, b):
    return jnp.dot(a, b, preferred_element_type=F32)


def _dot_nt(a, b):
    return lax.dot_general(a, b, (((1,), (1,)), ((), ())), preferred_element_type=F32)


def _split(x):
    hi = x.astype(BF16)
    return hi, (x - hi.astype(F32)).astype(BF16)


def _seg_mean(x, pmat):
    hi, lo = _split(x)
    return _dot(hi, pmat) + _dot(lo, pmat)


def _cmul(ar, ai, br, bi):
    return ar * br - ai * bi, ar * bi + ai * br


def _rope(x, cos, sin_signed):
    width = x.shape[-1]
    lane = lax.broadcasted_iota(jnp.int32, x.shape, 1)
    first_half = (lane % HEAD_DIM) < (HEAD_DIM // 2)
    partner = jnp.where(first_half,
                        pltpu.roll(x, width - HEAD_DIM // 2, 1),
                        pltpu.roll(x, HEAD_DIM // 2, 1))
    return x * cos + partner * sin_signed


def _dup_heads(x):
    low = lax.broadcasted_iota(jnp.int32, x.shape, 1) < HEAD_DIM
    sw = pltpu.roll(x, HEAD_DIM, 1)
    return [jnp.where(low, x, sw).astype(BF16), jnp.where(low, sw, x).astype(BF16)]


PROJECT_PIECE = 256
PROJECT_STAGES = IN_COLS // PROJECT_PIECE


def _mix_project_stages(h_ref, g_ref, w_ref, z_ref):
    xn = _rms(h_ref[...], g_ref[...]).astype(BF16)
    for c in range(PROJECT_STAGES):
        cols = slice(c * PROJECT_PIECE, (c + 1) * PROJECT_PIECE)
        z_ref[:, cols] = _dot(xn, w_ref[:, cols])
        yield


def _mix_mixers(layer, first_block, sinks_ref, cos_ref, sin_ref, ws_ref, bs_ref, lng_ref, lnb_ref,
                mga_ref, qg_ref, kg_ref, mgb_ref, pm_ref, ya_ref, yb_ref, zc_ref,
                z_ref, kprev_ref, vprev_ref):
    nq = ROWS_MIX // WINDOW
    zc_ref[...] = z_ref[:, COL_C:COL_C + C_WIDTH]
    pm = pm_ref[...]
    pm_kv = pm[0:LANES, 0:LANES]

    k_raw = z_ref[:, COL_K:COL_K + LANES]
    k_ms = _dot((k_raw * k_raw).astype(BF16), pm_kv)
    yield
    k_cur = _rope(k_raw * lax.rsqrt(k_ms + EPS) * kg_ref[...], cos_ref[...], sin_ref[...])
    v_cur = z_ref[:, COL_V:COL_V + LANES]
    k_dup = _dup_heads(jnp.concatenate([kprev_ref[...], k_cur], axis=0))
    v_dup = _dup_heads(jnp.concatenate([vprev_ref[...], v_cur], axis=0))
    kprev_ref[...] = k_cur[ROWS_MIX - WINDOW:, :]
    vprev_ref[...] = v_cur[ROWS_MIX - WINDOW:, :]
    q = z_ref[:, COL_Q:COL_Q + B_WIDTH]
    q_ms = jnp.concatenate(
        [_dot((q[:, s:s + 256] * q[:, s:s + 256]).astype(BF16), pm) for s in (0, 256)], axis=1)
    yield
    cos_q = jnp.concatenate([cos_ref[...]] * 4, axis=1)
    sin_q = jnp.concatenate([sin_ref[...]] * 4, axis=1)
    qr = _rope(q * lax.rsqrt(q_ms + EPS) * qg_ref[...], cos_q, sin_q) * (HEAD_DIM ** -0.5)
    low_q = (lax.broadcasted_iota(jnp.int32, qr.shape, 1) % LANES) < HEAD_DIM
    q_low = jnp.where(low_q, qr, 0.0).astype(BF16)
    q_high = jnp.where(low_q, 0.0, qr).astype(BF16)

    u = jax.nn.gelu(z_ref[:, COL_A:COL_A + A_WIDTH])
    v = jax.nn.gelu(z_ref[:, COL_A + A_WIDTH:COL_A + 2 * A_WIDTH])
    v_mean = _dot(v.astype(BF16), pm)
    yield
    vc = v - v_mean
    v_var = _dot((vc * vc).astype(BF16), pm)
    yield
    vn = (vc * lax.rsqrt(v_var + EPS) * lng_ref[...] + lnb_ref[...]).astype(BF16)
    lane = lax.broadcasted_iota(jnp.int32, (CHUNK, A_WIDTH), 1)
    in_head = [(lane >= h * HEAD_DIM) & (lane < (h + 1) * HEAD_DIM) for h in range(A_HEADS)]
    svs = []
    for c in range(ROWS_MIX // CHUNK):
        vn_c = vn[c * CHUNK:(c + 1) * CHUNK]
        stacked = jnp.concatenate(
            [jnp.where(in_head[h], vn_c, jnp.zeros((), BF16)) for h in range(A_HEADS)], axis=0)
        svs.append(_dot(ws_ref[...], stacked))
    yield
    for c in range(ROWS_MIX // CHUNK):
        rows = slice(c * CHUNK, (c + 1) * CHUNK)
        ya_ref[rows, :] = _rms(u[rows] * (svs[c] + bs_ref[...]), mga_ref[...]).astype(BF16)

    qi = lax.broadcasted_iota(jnp.int32, (4 * WINDOW, WINDOW), 0) % WINDOW
    kc = lax.broadcasted_iota(jnp.int32, (4 * WINDOW, WINDOW), 1)
    from_cur = kc <= qi
    head_slot = lax.broadcasted_iota(jnp.int32, (4 * WINDOW, 1), 0) // WINDOW
    prev_ok = kc >= jnp.where(first_block, WINDOW, 0)
    low_o = lax.broadcasted_iota(jnp.int32, (WINDOW, LANES), 1) < HEAD_DIM
    zero = jnp.zeros((), F32)

    for b in range(nq):
        rows = slice(b * WINDOW, (b + 1) * WINDOW)
        win = slice(b * WINDOW, (b + 2) * WINDOW)
        pairs = []
        for j in range(B_KV_HEADS):
            pair_a = slice((2 * j) * LANES, (2 * j + 1) * LANES)
            pair_b = slice((2 * j + 1) * LANES, (2 * j + 2) * LANES)
            qs = jnp.concatenate([q_low[rows, pair_a], q_low[rows, pair_b],
                                  q_high[rows, pair_a], q_high[rows, pair_b]], axis=0)
            s = _dot_nt(qs, k_dup[j][win])
            yield
            s_prev = s[:, 0:WINDOW]
            if b == 0:
                s_prev = jnp.where(prev_ok, s_prev, NEG_BIG)
            s = jnp.where(from_cur, s[:, WINDOW:], s_prev)
            sink = jnp.where(head_slot == 0, sinks_ref[layer, 4 * j],
                             jnp.where(head_slot == 1, sinks_ref[layer, 4 * j + 2],
                                       jnp.where(head_slot == 2, sinks_ref[layer, 4 * j + 1],
                                                 sinks_ref[layer, 4 * j + 3])))
            m = jnp.maximum(jnp.max(s, axis=-1, keepdims=True), sink)
            pr = jnp.exp(s - m)
            denom = jnp.sum(pr, axis=-1, keepdims=True) + jnp.exp(sink - m)
            pn = pr * (1.0 / denom)
            p2 = jnp.concatenate([jnp.where(from_cur, zero, pn),
                                  jnp.where(from_cur, pn, zero)], axis=1).astype(BF16)
            o = _dot(p2, v_dup[j][win])
            yield
            pairs += [jnp.where(low_o, o[0:WINDOW], o[2 * WINDOW:3 * WINDOW]),
                      jnp.where(low_o, o[WINDOW:2 * WINDOW], o[3 * WINDOW:4 * WINDOW])]
        yb = jnp.concatenate(pairs, axis=1)
        yb_ref[rows, :] = _rms(yb, mgb_ref[...]).astype(BF16)


MIXER_STAGES = 5 + 2 * B_KV_HEADS * (ROWS_MIX // WINDOW)
MIXER_Z_STAGES = 3
POST_STAGES = 4 + FF_SPLIT


def _run(*staged):
    total = max(n for _, n in staged)
    done = [0] * len(staged)
    for step in range(1, total + 1):
        for k, (gen, n) in enumerate(staged):
            while done[k] * total < step * n:
                next(gen, None)
                done[k] += 1
    for gen, _ in staged:
        for _ in gen:
            pass


def _mix_kernel(layer, blocks_per_seq, sinks_ref, h_ref, cos_ref, sin_ref, g_ref, w_ref, *rest):
    mixer_refs, (z_ref, kprev_ref, vprev_ref) = rest[:-3], rest[-3:]
    i = pl.program_id(0)
    first_block = (i % blocks_per_seq) == 1 % blocks_per_seq

    @pl.when((i == 0) | first_block)
    def _():
        kprev_ref[...] = jnp.zeros_like(kprev_ref)
        vprev_ref[...] = jnp.zeros_like(vprev_ref)

    @pl.when(i == 0)
    def _():
        z_ref[...] = jnp.zeros_like(z_ref)

    mixers = _mix_mixers(layer, first_block, sinks_ref, cos_ref, sin_ref, *mixer_refs,
                         z_ref, kprev_ref, vprev_ref)
    for _ in range(MIXER_Z_STAGES):
        next(mixers)
    _run((mixers, MIXER_STAGES - MIXER_Z_STAGES),
         (_mix_project_stages(h_ref, g_ref, w_ref, z_ref), PROJECT_STAGES))


def _mix_param_specs(layer):
    return [
        _layer_spec((1, D_MODEL), layer),
        _layer_spec((D_MODEL, IN_COLS), layer),
        _layer_spec((CHUNK, A_HEADS * CHUNK), layer),
        _layer_spec((CHUNK, A_WIDTH), layer),
        _layer_spec((1, A_WIDTH), layer),
        _layer_spec((1, A_WIDTH), layer),
        _layer_spec((1, A_WIDTH), layer),
        _layer_spec((1, B_WIDTH), layer),
        _layer_spec((1, LANES), layer),
        _layer_spec((1, B_WIDTH), layer),
        _const_spec((256, 256)),
    ]


def _mix_out_shapes(n):
    return [jax.ShapeDtypeStruct((n, A_WIDTH), BF16),
            jax.ShapeDtypeStruct((n, B_WIDTH), BF16),
            jax.ShapeDtypeStruct((n, C_WIDTH), F32)]


_MIX_SCRATCH = [
    pltpu.VMEM((ROWS_MIX, IN_COLS), F32),
    pltpu.VMEM((WINDOW, LANES), F32),
    pltpu.VMEM((WINDOW, LANES), F32),
]


def _mix(layer, h, cos2, sin2, sinks, mix_params, seq_len):
    n = h.shape[0]
    n_blocks = n // ROWS_MIX
    rb = lambda w_: pl.BlockSpec((ROWS_MIX, w_), lambda i: (jnp.maximum(i - 1, 0), 0))
    h_spec = pl.BlockSpec((ROWS_MIX, D_MODEL), lambda i: (jnp.minimum(i, n_blocks - 1), 0))
    return pl.pallas_call(
        functools.partial(_mix_kernel, layer, seq_len // ROWS_MIX),
        grid=(n_blocks + 1,),
        in_specs=[pl.BlockSpec(memory_space=pltpu.SMEM), h_spec, rb(LANES), rb(LANES)]
        + _mix_param_specs(layer),
        out_specs=[rb(A_WIDTH), rb(B_WIDTH), rb(C_WIDTH)],
        out_shape=_mix_out_shapes(n),
        scratch_shapes=_MIX_SCRATCH,
        compiler_params=pltpu.CompilerParams(
            dimension_semantics=("arbitrary",), vmem_limit_bytes=VMEM_LIMIT),
        name="mix",
    )(sinks, h, cos2, sin2, *mix_params)


def _tables_kernel(are_ref, aim_ref, ldt_ref, bre_ref, bim_ref, cre_ref, cim_ref, d_ref,
                   kd_ref, wb_ref, wct_ref, astep_ref, atab_ref):
    a_re, a_im = are_ref[...], aim_ref[...]
    dt = jnp.exp(ldt_ref[...])
    mag = jnp.exp(a_re * dt)
    lr, li = mag * jnp.cos(a_im * dt), mag * jnp.sin(a_im * dt)
    den = a_re * a_re + a_im * a_im
    fr, fi = _cmul(lr - 1.0, li, a_re / den, -a_im / den)
    bbr, bbi = _cmul(fr, fi, bre_ref[...], bim_ref[...])
    c_re, c_im = cre_ref[...], cim_ref[...]

    row_g = lax.broadcasted_iota(jnp.int32, (C_WIDTH, N_STATE), 0) // C_GROUP
    col_g = lax.broadcasted_iota(jnp.int32, (C_WIDTH, N_STATE), 1) // C_STATE
    same_group = row_g == col_g

    def blockdiag(m):
        return jnp.where(same_group, jnp.concatenate([m] * C_GROUPS, axis=0), 0.0)

    def state_matrix(mr, mi):
        return jnp.concatenate([blockdiag(mr), blockdiag(mi)], axis=1)

    pows = [(jnp.ones_like(lr), jnp.zeros_like(li))]
    for _ in range(SSM_T):
        pows.append(_cmul(pows[-1][0], pows[-1][1], lr, li))

    c0_hi, c0_lo = _split(state_matrix(c_re, -c_im))
    rr = lax.broadcasted_iota(jnp.int32, (C_WIDTH, C_WIDTH), 0)
    cc = lax.broadcasted_iota(jnp.int32, (C_WIDTH, C_WIDTH), 1)
    for d in range(SSM_T):
        wr, wi = _cmul(pows[d][0], pows[d][1], bbr, bbi)
        w_full = state_matrix(wr, wi)
        wb_ref[SSM_T - 1 - d] = w_full.astype(BF16)
        w_hi, w_lo = _split(w_full)
        k = _dot_nt(w_hi, c0_hi) + _dot_nt(w_hi, c0_lo) + _dot_nt(w_lo, c0_hi)
        if d == 0:
            k = k + jnp.where(rr == cc, d_ref[...], 0.0)
        kd_ref[d] = k.astype(BF16)
        mr, mi = _cmul(pows[d + 1][0], pows[d + 1][1], c_re, c_im)
        wct_ref[d] = state_matrix(mr, -mi).astype(BF16)

    re_cols, im_cols = pl.ds(0, N_STATE), pl.ds(N_STATE, N_STATE)
    sr, si = pows[SSM_T]
    atab_ref[0:1, re_cols] = sr
    atab_ref[0:1, im_cols] = si
    for k in range(SCAN_STEPS):
        span = 1 << k
        astep_ref[pl.ds(k, 1), re_cols] = sr
        astep_ref[pl.ds(k, 1), im_cols] = si
        tr, ti = _cmul(atab_ref[0:span, re_cols], atab_ref[0:span, im_cols], sr, si)
        atab_ref[pl.ds(span, span), re_cols] = tr
        atab_ref[pl.ds(span, span), im_cols] = ti
        sr, si = _cmul(sr, si, sr, si)


def _tables(a_re, a_im, ldt, b_re, b_im, c_re, c_im, d_skip):
    depth = a_re.shape[0]
    vec = lambda w: pl.BlockSpec((None, 1, w), lambda l: (l, 0, 0))
    mat = lambda r, c: pl.BlockSpec((None, r, c), lambda l: (l, 0, 0))
    cube = lambda r, c: pl.BlockSpec((None, SSM_T, r, c), lambda l: (l, 0, 0, 0))
    return pl.pallas_call(
        _tables_kernel,
        grid=(depth,),
        in_specs=[vec(N_STATE), vec(N_STATE), vec(N_STATE),
                  mat(C_GROUP, N_STATE), mat(C_GROUP, N_STATE),
                  mat(C_GROUP, N_STATE), mat(C_GROUP, N_STATE), vec(C_WIDTH)],
        out_specs=[cube(C_WIDTH, C_WIDTH), cube(C_WIDTH, 2 * N_STATE), cube(C_WIDTH, 2 * N_STATE),
                   mat(SCAN_STEPS, 2 * N_STATE), mat(SUBLANES, 2 * N_STATE)],
        out_shape=[jax.ShapeDtypeStruct((depth, SSM_T, C_WIDTH, C_WIDTH), BF16),
                   jax.ShapeDtypeStruct((depth, SSM_T, C_WIDTH, 2 * N_STATE), BF16),
                   jax.ShapeDtypeStruct((depth, SSM_T, C_WIDTH, 2 * N_STATE), BF16),
                   jax.ShapeDtypeStruct((depth, SCAN_STEPS, 2 * N_STATE), F32),
                   jax.ShapeDtypeStruct((depth, SUBLANES, 2 * N_STATE), F32)],
        compiler_params=pltpu.CompilerParams(
            dimension_semantics=("arbitrary",), vmem_limit_bytes=VMEM_LIMIT),
        name="ssm_tables",
    )(a_re, a_im, ldt, b_re, b_im, c_re, c_im, d_skip)


def _glu_out(glu_a, glu_b, mg_ref, o_ref):
    o_ref[...] = _rms(glu_a * jax.nn.sigmoid(glu_b), mg_ref[...]).astype(BF16)


def _ssm_kernel(blocks_per_seq, n_blocks, *refs):
    w1_ref, w2_ref, mg_ref, o_ref, _, _, ya_ref, yb_ref, carry_ref = refs[-9:]
    i = pl.program_id(0)

    @pl.when(i == 0)
    def _():
        ya_ref[...] = jnp.zeros_like(ya_ref)
        yb_ref[...] = jnp.zeros_like(yb_ref)

    @pl.when((i % blocks_per_seq) == 0)
    def _():
        carry_ref[...] = jnp.zeros_like(carry_ref)

    @pl.when(i < n_blocks)
    def _():
        _ssm_block(*refs)

    @pl.when(i == n_blocks)
    def _():
        y_prev = jax.nn.gelu(jnp.concatenate([ya_ref[...], yb_ref[...]], axis=1)).astype(BF16)
        _glu_out(_dot(y_prev, w1_ref[...]), _dot(y_prev, w2_ref[...]), mg_ref, o_ref)


def _ssm_block(ua_ref, ub_ref, kd_ref, wb_ref, wct_ref, astep_ref, atab_ref,
               w1_ref, w2_ref, mg_ref, o_ref, inc_ref, sprev_ref, ya_ref, yb_ref, carry_ref):
    xs = [jnp.concatenate([ua_ref[pl.ds(j, SSM_N, stride=SSM_T), :],
                           ub_ref[pl.ds(j, SSM_N, stride=SSM_T), :]], axis=1).astype(BF16)
          for j in range(SSM_T)]

    inc = _dot(xs[0], wb_ref[0])
    for j in range(1, SSM_T // 2):
        inc = inc + _dot(xs[j], wb_ref[j])
    y_prev = jax.nn.gelu(jnp.concatenate([ya_ref[...], yb_ref[...]], axis=1)).astype(BF16)
    glu_a, glu_b = _dot(y_prev, w1_ref[...]), _dot(y_prev, w2_ref[...])
    for j in range(SSM_T // 2, SSM_T):
        inc = inc + _dot(xs[j], wb_ref[j])
    inc_ref[...] = inc
    _glu_out(glu_a, glu_b, mg_ref, o_ref)

    n_groups = SSM_N // SUBLANES
    sub = lax.broadcasted_iota(jnp.int32, (n_groups, SUBLANES, LANES), 1)
    sub2 = lax.broadcasted_iota(jnp.int32, (SUBLANES, LANES), 0)
    y_lag = []
    for lt in range(N_STATE // LANES):
        re_cols = pl.ds(lt * LANES, LANES)
        im_cols = pl.ds(N_STATE + lt * LANES, LANES)
        xr = inc_ref[:, re_cols].reshape(n_groups, SUBLANES, LANES)
        xi = inc_ref[:, im_cols].reshape(n_groups, SUBLANES, LANES)
        for k in range(SCAN_STEPS):
            sh = 1 << k
            ar = astep_ref[pl.ds(k, 1), re_cols]
            ai = astep_ref[pl.ds(k, 1), im_cols]
            keep = sub >= sh
            sr = jnp.where(keep, pltpu.roll(xr, sh, 1), 0.0)
            si = jnp.where(keep, pltpu.roll(xi, sh, 1), 0.0)
            xr, xi = xr + ar * sr - ai * si, xi + ar * si + ai * sr
        tr = atab_ref[:, re_cols]
        ti = atab_ref[:, im_cols]
        er = jnp.broadcast_to(carry_ref[:, re_cols], (SUBLANES, LANES))
        ei = jnp.broadcast_to(carry_ref[:, im_cols], (SUBLANES, LANES))
        before_r, before_i = [], []
        for g in range(n_groups):
            fr = xr[g] + tr * er - ti * ei
            fi = xi[g] + tr * ei + ti * er
            before_r.append(jnp.where(sub2 == 0, er, pltpu.roll(fr, 1, 0)))
            before_i.append(jnp.where(sub2 == 0, ei, pltpu.roll(fi, 1, 0)))
            er = jnp.broadcast_to(fr[SUBLANES - 1:SUBLANES, :], (SUBLANES, LANES))
            ei = jnp.broadcast_to(fi[SUBLANES - 1:SUBLANES, :], (SUBLANES, LANES))
        sprev_ref[:, re_cols] = jnp.concatenate(before_r, axis=0).astype(BF16)
        sprev_ref[:, im_cols] = jnp.concatenate(before_i, axis=0).astype(BF16)
        carry_ref[:, re_cols] = er[0:1, :]
        carry_ref[:, im_cols] = ei[0:1, :]

        n_tiles = N_STATE // LANES
        for t in range(lt * SSM_T // n_tiles, (lt + 1) * SSM_T // n_tiles):
            y = _dot(xs[0], kd_ref[t])
            for j in range(1, t + 1):
                y = y + _dot(xs[j], kd_ref[t - j])
            y_lag.append(y)

    sprev = sprev_ref[...]
    for t in range(SSM_T):
        y = y_lag[t] + _dot_nt(sprev, wct_ref[t])
        ya_ref[pl.ds(t, SSM_N, stride=SSM_T), :] = y[:, 0:LANES]
        yb_ref[pl.ds(t, SSM_N, stride=SSM_T), :] = y[:, LANES:2 * LANES]


def _ssm(layer, zc, kd, wb, wct, astep, atab, w1, w2, mg, seq_len):
    n = zc.shape[0]
    n_blocks = n // ROWS_SSM
    scanned = lambda i: jnp.minimum(i, n_blocks - 1)
    return pl.pallas_call(
        functools.partial(_ssm_kernel, seq_len // ROWS_SSM, n_blocks),
        grid=(n_blocks + 1,),
        in_specs=[
            pl.BlockSpec((ROWS_SSM, LANES), lambda i: (scanned(i), 0)),
            pl.BlockSpec((ROWS_SSM, LANES), lambda i: (scanned(i), 1)),
            _layer_spec((SSM_T, C_WIDTH, C_WIDTH), layer),
            _layer_spec((SSM_T, C_WIDTH, 2 * N_STATE), layer),
            _layer_spec((SSM_T, C_WIDTH, 2 * N_STATE), layer),
            _layer_spec((SCAN_STEPS, 2 * N_STATE), layer),
            _layer_spec((SUBLANES, 2 * N_STATE), layer),
            _layer_spec((C_WIDTH, C_WIDTH), layer),
            _layer_spec((C_WIDTH, C_WIDTH), layer),
            _layer_spec((1, C_WIDTH), layer),
        ],
        out_specs=pl.BlockSpec((ROWS_SSM, C_WIDTH), lambda i: (jnp.maximum(i - 1, 0), 0)),
        out_shape=jax.ShapeDtypeStruct((n, C_WIDTH), BF16),
        scratch_shapes=[
            pltpu.VMEM((SSM_N, 2 * N_STATE), F32),
            pltpu.VMEM((SSM_N, 2 * N_STATE), BF16),
            pltpu.VMEM((ROWS_SSM, LANES), F32),
            pltpu.VMEM((ROWS_SSM, LANES), F32),
            pltpu.VMEM((1, 2 * N_STATE), F32),
        ],
        compiler_params=pltpu.CompilerParams(
            dimension_semantics=("arbitrary",), vmem_limit_bytes=VMEM_LIMIT),
        name="ssm",
    )(zc, zc, kd, wb, wct, astep, atab, w1, w2, mg)


def _post_stages(h_ref, ya_ref, yb_ref, yc_ref, p_ref, wo_ref, g1_ref, w1_ref, w2_ref,
                 g2_ref, wg_ref, wp_ref, *out_refs):
    mixed = (_dot(ya_ref[...], wo_ref[0:A_WIDTH, :])
             + _dot(yb_ref[...], wo_ref[A_WIDTH:A_WIDTH + B_WIDTH, :])
             + _dot(yc_ref[...], wo_ref[A_WIDTH + B_WIDTH:, :]))
    yield
    h = h_ref[...] + mixed
    hn = _rms(h, g1_ref[...]).astype(BF16)
    ple = _dot(p_ref[...].astype(BF16), wp_ref[...])
    piece = D_FF // FF_SPLIT
    up = _dot(hn, w1_ref[:, 0:piece])
    yield
    ff = None
    for c in range(FF_SPLIT):
        a = jnp.maximum(up, 0.0)
        down = _dot((a * a).astype(BF16), w2_ref[c * piece:(c + 1) * piece, :])
        if c + 1 < FF_SPLIT:
            up = _dot(hn, w1_ref[:, (c + 1) * piece:(c + 2) * piece])
        yield
        ff = down if ff is None else ff + down
    h = h + ff
    gate = _dot(_rms(h, g2_ref[...]).astype(BF16), wg_ref[...])
    yield
    h = h + jax.nn.sigmoid(gate) * ple
    for o_ref in out_refs:
        o_ref[...] = h
    yield


N_POST_IN = 12


def _post_kernel(*refs):
    _run((_post_stages(*refs), POST_STAGES))


def _post_specs(layer, row_block):
    rb = lambda w: pl.BlockSpec((ROWS_PROJ, w), lambda i: (row_block(i), 0))
    return [
        rb(D_MODEL), rb(A_WIDTH), rb(B_WIDTH), rb(C_WIDTH),
        pl.BlockSpec((None, ROWS_PROJ, PLE_DIM), lambda i: (layer, row_block(i), 0)),
        _layer_spec((D_MODEL, D_MODEL), layer),
        _layer_spec((1, D_MODEL), layer),
        _layer_spec((D_MODEL, D_FF), layer),
        _layer_spec((D_FF, D_MODEL), layer),
        _layer_spec((1, D_MODEL), layer),
        _layer_spec((D_MODEL, D_MODEL), layer),
        _layer_spec((PLE_DIM, D_MODEL), layer),
    ]


def _post(layer, h, ya, yb, yc, p, post_params):
    n = h.shape[0]
    return pl.pallas_call(
        _post_kernel,
        grid=(n // ROWS_PROJ,),
        in_specs=_post_specs(layer, lambda i: i),
        out_specs=pl.BlockSpec((ROWS_PROJ, D_MODEL), lambda i: (i, 0)),
        out_shape=jax.ShapeDtypeStruct((n, D_MODEL), F32),
        compiler_params=pltpu.CompilerParams(
            dimension_semantics=("arbitrary",), vmem_limit_bytes=VMEM_LIMIT),
        name="post",
    )(h, ya, yb, yc, p, *post_params)


def _fused_kernel(layer, blocks_per_seq, n_blocks, sinks_ref, *refs):
    post_refs, refs = refs[:N_POST_IN], refs[N_POST_IN:]
    cos_ref, sin_ref, g_ref, w_ref = refs[:4]
    mixer_refs = refs[4:-8]
    h_out_ref, ya_ref, yb_ref, zc_ref = refs[-8:-4]
    hs_ref, z_ref, kprev_ref, vprev_ref = refs[-4:]
    i = pl.program_id(0)
    first_block = (i % blocks_per_seq) == 1 % blocks_per_seq

    @pl.when((i == 0) | first_block)
    def _():
        kprev_ref[...] = jnp.zeros_like(kprev_ref)
        vprev_ref[...] = jnp.zeros_like(vprev_ref)

    @pl.when(i == 0)
    def _():
        hs_ref[...] = jnp.zeros_like(hs_ref)

    mix = itertools.chain(
        _mix_project_stages(hs_ref, g_ref, w_ref, z_ref),
        _mix_mixers(layer, first_block, sinks_ref, cos_ref, sin_ref, *mixer_refs,
                    ya_ref, yb_ref, zc_ref, z_ref, kprev_ref, vprev_ref))
    _run((_post_stages(*post_refs, h_out_ref, hs_ref), POST_STAGES),
         (mix, PROJECT_STAGES + MIXER_STAGES))


def _fused(layer, h, ya, yb, yc, p, post_params, cos2, sin2, sinks, mix_params, seq_len):
    n = h.shape[0]
    n_blocks = n // ROWS_MIX
    merged = lambda i: jnp.minimum(i, n_blocks - 1)
    mixed = lambda i: jnp.maximum(i - 1, 0)
    rb = lambda w_: pl.BlockSpec((ROWS_MIX, w_), lambda i: (mixed(i), 0))
    return pl.pallas_call(
        functools.partial(_fused_kernel, layer, seq_len // ROWS_MIX, n_blocks),
        grid=(n_blocks + 1,),
        in_specs=[pl.BlockSpec(memory_space=pltpu.SMEM)]
        + _post_specs(layer - 1, merged)
        + [rb(LANES), rb(LANES)] + _mix_param_specs(layer),
        out_specs=[pl.BlockSpec((ROWS_PROJ, D_MODEL), lambda i: (merged(i), 0)),
                   rb(A_WIDTH), rb(B_WIDTH), rb(C_WIDTH)],
        out_shape=[jax.ShapeDtypeStruct((n, D_MODEL), F32)] + _mix_out_shapes(n),
        scratch_shapes=[pltpu.VMEM((ROWS_MIX, D_MODEL), F32)] + _MIX_SCRATCH,
        compiler_params=pltpu.CompilerParams(
            dimension_semantics=("arbitrary",), vmem_limit_bytes=VMEM_LIMIT),
        name="fused",
    )(sinks, h, ya, yb, yc, p, *post_params, cos2, sin2, *mix_params)


def kernel(x, p, positions, attn_norm_g, w_in, gmlp_ln_g, gmlp_ln_b, gmlp_ws, gmlp_bs, q_norm_g, k_norm_g, sinks, ssm_a_re, ssm_a_im, ssm_log_dt, ssm_b_re, ssm_b_im, ssm_c_re, ssm_c_im, ssm_d, glu_w1, glu_w2, mix_out_g, w_out, mlp_norm_g, w_ff1, w_ff2, ple_norm_g, w_ple_gate, w_ple_proj):
    bsz, seq_len, _ = x.shape
    depth = w_in.shape[0]
    n = bsz * seq_len

    inv = 1.0 / (ROPE_THETA ** (jnp.arange(0, HEAD_DIM, 2, dtype=F32) / HEAD_DIM))
    ang = positions.astype(F32).reshape(n, 1) * inv
    cos2 = jnp.tile(jnp.cos(ang), (1, 4))
    sin2 = jnp.tile(jnp.concatenate([-jnp.sin(ang), jnp.sin(ang)], axis=1), (1, 2))

    seg = np.arange(256) // HEAD_DIM
    pm = jnp.asarray((seg[:, None] == seg[None, :]) / HEAD_DIM, dtype=BF16)

    rows = lambda v: v.reshape(depth, 1, -1).astype(F32)
    g_attn, g_mlp, g_ple = rows(attn_norm_g), rows(mlp_norm_g), rows(ple_norm_g)
    ln_g, ln_b = rows(gmlp_ln_g), rows(gmlp_ln_b)
    mg_a = rows(mix_out_g[:, :A_WIDTH])
    mg_b = rows(mix_out_g[:, A_WIDTH:A_WIDTH + B_WIDTH])
    mg_c = rows(mix_out_g[:, A_WIDTH + B_WIDTH:])
    qg = rows(jnp.tile(q_norm_g, (1, B_Q_HEADS)))
    kg = rows(jnp.tile(k_norm_g, (1, B_KV_HEADS)))
    bs = jnp.repeat(jnp.swapaxes(gmlp_bs, 1, 2), HEAD_DIM, axis=2)
    w_a = w_in[:, :, :2 * A_WIDTH].reshape(depth, D_MODEL, A_HEADS, 2, HEAD_DIM)
    w_a = jnp.swapaxes(w_a, 2, 3).reshape(depth, D_MODEL, 2 * A_WIDTH)
    w_in_b = jnp.concatenate([w_a, w_in[:, :, 2 * A_WIDTH:]], axis=2).astype(BF16)
    causal = np.tril(np.ones((CHUNK, CHUNK), dtype=bool))
    ws_b = jnp.swapaxes(jnp.where(causal, gmlp_ws, 0.0), 1, 2).reshape(
        depth, CHUNK, A_HEADS * CHUNK).astype(BF16)
    w_out_b = w_out.astype(BF16)
    w_ff1_b, w_ff2_b = w_ff1.astype(BF16), w_ff2.astype(BF16)
    w_gate_b, w_proj_b = w_ple_gate.astype(BF16), w_ple_proj.astype(BF16)
    glu1_b, glu2_b = glu_w1.astype(BF16), glu_w2.astype(BF16)
    p2 = p.reshape(depth, n, PLE_DIM)

    kd, wb, wct, astep, atab = _tables(
        rows(ssm_a_re), rows(ssm_a_im), rows(jnp.repeat(ssm_log_dt, C_STATE, axis=1)),
        jnp.transpose(ssm_b_re, (0, 3, 1, 2)).reshape(depth, C_GROUP, N_STATE),
        jnp.transpose(ssm_b_im, (0, 3, 1, 2)).reshape(depth, C_GROUP, N_STATE),
        jnp.transpose(ssm_c_re, (0, 2, 1, 3)).reshape(depth, C_GROUP, N_STATE),
        jnp.transpose(ssm_c_im, (0, 2, 1, 3)).reshape(depth, C_GROUP, N_STATE),
        rows(ssm_d))

    mix_params = (g_attn, w_in_b, ws_b, bs, ln_g, ln_b, mg_a, qg, kg, mg_b, pm)
    post_params = (w_out_b, g_mlp, w_ff1_b, w_ff2_b, g_ple, w_gate_b, w_proj_b)
    h = x.reshape(n, D_MODEL)
    ya, yb, zc = _mix(0, h, cos2, sin2, sinks, mix_params, seq_len)
    for i in range(depth):
        yc = _ssm(i, zc, kd, wb, wct, astep, atab, glu1_b, glu2_b, mg_c, seq_len)
        if i + 1 < depth:
            h, ya, yb, zc = _fused(i + 1, h, ya, yb, yc, p2, post_params, cos2, sin2, sinks,
                                   mix_params, seq_len)
        else:
            h = _post(i, h, ya, yb, yc, p2, post_params)
    return h.reshape(bsz, seq_len, D_MODEL)
```

```python
import functools
import math

import numpy as np
import jax
import jax.numpy as jnp
from jax import lax
from jax.experimental import pallas as pl
from jax.experimental.pallas import tpu as pltpu

F32 = jnp.float32
BF16 = jnp.bfloat16

D_MODEL = 1024
HEAD_DIM = 64
A_WIDTH = 256
A_HEADS = 4
CHUNK = 128
B_WIDTH = 512
B_Q_HEADS = 8
B_KV_HEADS = 2
WINDOW = 128
ROPE_THETA = 10000.0
C_WIDTH = 256
C_GROUP = 16
C_GROUPS = 16
C_STATE = 64
N_STATE = C_GROUPS * C_STATE
IN_COLS = 1536
D_FF = 4096
PLE_DIM = 256
EPS = 1e-6
NEG_BIG = -1e30
LANES = 128

COL_A = 0
COL_Q = 512
COL_K = 1024
COL_V = 1152
COL_C = 1280

ROWS_PROJ = 512
ROWS_MIX = 512
ROWS_SSM = 2048
SSM_T = 8
SSM_N = ROWS_SSM // SSM_T
QUAD_GROUPS = 4
QUADS = C_GROUPS // QUAD_GROUPS
QUAD_WIDTH = QUAD_GROUPS * C_GROUP
QUAD_STATE = 2 * QUAD_GROUPS * C_STATE
PACKED = SSM_T * QUAD_WIDTH
SUBLANES = 8
SCAN_STEPS = int(math.log2(SUBLANES))
FF_SPLIT = 8
VMEM_LIMIT = 56 * 1024 * 1024


def _const_spec(shape):
    nd = len(shape)
    return pl.BlockSpec(shape, lambda *_: (0,) * nd, pipeline_mode=pl.Buffered(1))


def _layer_spec(shape, layer):
    nd = len(shape)
    return pl.BlockSpec((None,) + tuple(shape), lambda *_: (layer,) + (0,) * nd,
                        pipeline_mode=pl.Buffered(1))


def _rms(x, g):
    ms = jnp.mean(x * x, axis=-1, keepdims=True)
    return x * lax.rsqrt(ms + EPS) * g


def _dot(a, b):
    return jnp.dot(a, b, preferred_element_type=F32)


def _dot_nt(a, b):
    return lax.dot_general(a, b, (((1,), (1,)), ((), ())), preferred_element_type=F32)


def _split(x):
    hi = x.astype(BF16)
    return hi, (x - hi.astype(F32)).astype(BF16)


def _seg_mean(x, pmat):
    hi, lo = _split(x)
    return _dot(hi, pmat) + _dot(lo, pmat)


def _cmul(ar, ai, br, bi):
    return ar * br - ai * bi, ar * bi + ai * br


def _rope(x, cos, sin_signed):
    width = x.shape[-1]
    lane = lax.broadcasted_iota(jnp.int32, x.shape, 1)
    first_half = (lane % HEAD_DIM) < (HEAD_DIM // 2)
    partner = jnp.where(first_half,
                        pltpu.roll(x, width - HEAD_DIM // 2, 1),
                        pltpu.roll(x, HEAD_DIM // 2, 1))
    return x * cos + partner * sin_signed


def _dup_heads(x):
    low = lax.broadcasted_iota(jnp.int32, x.shape, 1) < HEAD_DIM
    sw = pltpu.roll(x, HEAD_DIM, 1)
    return [jnp.where(low, x, sw).astype(BF16), jnp.where(low, sw, x).astype(BF16)]


def _mix_project(h_ref, g_ref, w_ref, z_ref):
    xn = _rms(h_ref[...], g_ref[...]).astype(BF16)
    z_ref[...] = _dot(xn, w_ref[...])


def _mix_mixers(layer, first_block, sinks_ref, cos_ref, sin_ref, ws_ref, bs_ref, lng_ref, lnb_ref,
                mga_ref, qg_ref, kg_ref, mgb_ref, pm_ref, ya_ref, yb_ref, zc_ref,
                z_ref, kprev_ref, vprev_ref):
    nq = ROWS_MIX // WINDOW
    zc_ref[...] = z_ref[:, COL_C:COL_C + C_WIDTH]
    pm = pm_ref[...]
    pm_kv = pm[0:LANES, 0:LANES]

    k_raw = z_ref[:, COL_K:COL_K + LANES]
    k_ms = _dot((k_raw * k_raw).astype(BF16), pm_kv)
    yield
    k_cur = _rope(k_raw * lax.rsqrt(k_ms + EPS) * kg_ref[...], cos_ref[...], sin_ref[...])
    v_cur = z_ref[:, COL_V:COL_V + LANES]
    k_dup = _dup_heads(jnp.concatenate([kprev_ref[...], k_cur], axis=0))
    v_dup = _dup_heads(jnp.concatenate([vprev_ref[...], v_cur], axis=0))
    kprev_ref[...] = k_cur[ROWS_MIX - WINDOW:, :]
    vprev_ref[...] = v_cur[ROWS_MIX - WINDOW:, :]
    q = z_ref[:, COL_Q:COL_Q + B_WIDTH]
    q_ms = jnp.concatenate(
        [_dot((q[:, s:s + 256] * q[:, s:s + 256]).astype(BF16), pm) for s in (0, 256)], axis=1)
    yield
    cos_q = jnp.concatenate([cos_ref[...]] * 4, axis=1)
    sin_q = jnp.concatenate([sin_ref[...]] * 4, axis=1)
    qr = _rope(q * lax.rsqrt(q_ms + EPS) * qg_ref[...], cos_q, sin_q) * (HEAD_DIM ** -0.5)
    low_q = (lax.broadcasted_iota(jnp.int32, qr.shape, 1) % LANES) < HEAD_DIM
    q_low = jnp.where(low_q, qr, 0.0).astype(BF16)
    q_high = jnp.where(low_q, 0.0, qr).astype(BF16)

    u = jax.nn.gelu(z_ref[:, COL_A:COL_A + A_WIDTH])
    v = jax.nn.gelu(z_ref[:, COL_A + A_WIDTH:COL_A + 2 * A_WIDTH])
    v_mean = _dot(v.astype(BF16), pm)
    yield
    vc = v - v_mean
    v_var = _dot((vc * vc).astype(BF16), pm)
    yield
    vn = (vc * lax.rsqrt(v_var + EPS) * lng_ref[...] + lnb_ref[...]).astype(BF16)
    lane = lax.broadcasted_iota(jnp.int32, (CHUNK, A_WIDTH), 1)
    in_head = [(lane >= h * HEAD_DIM) & (lane < (h + 1) * HEAD_DIM) for h in range(A_HEADS)]
    svs = []
    for c in range(ROWS_MIX // CHUNK):
        vn_c = vn[c * CHUNK:(c + 1) * CHUNK]
        stacked = jnp.concatenate(
            [jnp.where(in_head[h], vn_c, jnp.zeros((), BF16)) for h in range(A_HEADS)], axis=0)
        svs.append(_dot(ws_ref[...], stacked))
    yield
    for c in range(ROWS_MIX // CHUNK):
        rows = slice(c * CHUNK, (c + 1) * CHUNK)
        ya_ref[rows, :] = _rms(u[rows] * (svs[c] + bs_ref[...]), mga_ref[...]).astype(BF16)

    qi = lax.broadcasted_iota(jnp.int32, (4 * WINDOW, WINDOW), 0) % WINDOW
    kc = lax.broadcasted_iota(jnp.int32, (4 * WINDOW, WINDOW), 1)
    from_cur = kc <= qi
    head_slot = lax.broadcasted_iota(jnp.int32, (4 * WINDOW, 1), 0) // WINDOW
    prev_ok = kc >= jnp.where(first_block, WINDOW, 0)
    low_o = lax.broadcasted_iota(jnp.int32, (WINDOW, LANES), 1) < HEAD_DIM
    zero = jnp.zeros((), F32)

    for b in range(nq):
        rows = slice(b * WINDOW, (b + 1) * WINDOW)
        win = slice(b * WINDOW, (b + 2) * WINDOW)
        pairs = []
        for j in range(B_KV_HEADS):
            pair_a = slice((2 * j) * LANES, (2 * j + 1) * LANES)
            pair_b = slice((2 * j + 1) * LANES, (2 * j + 2) * LANES)
            qs = jnp.concatenate([q_low[rows, pair_a], q_low[rows, pair_b],
                                  q_high[rows, pair_a], q_high[rows, pair_b]], axis=0)
            s = _dot_nt(qs, k_dup[j][win])
            yield
            s_prev = s[:, 0:WINDOW]
            if b == 0:
                s_prev = jnp.where(prev_ok, s_prev, NEG_BIG)
            s = jnp.where(from_cur, s[:, WINDOW:], s_prev)
            sink = jnp.where(head_slot == 0, sinks_ref[layer, 4 * j],
                             jnp.where(head_slot == 1, sinks_ref[layer, 4 * j + 2],
                                       jnp.where(head_slot == 2, sinks_ref[layer, 4 * j + 1],
                                                 sinks_ref[layer, 4 * j + 3])))
            m = jnp.maximum(jnp.max(s, axis=-1, keepdims=True), sink)
            pr = jnp.exp(s - m)
            denom = jnp.sum(pr, axis=-1, keepdims=True) + jnp.exp(sink - m)
            pn = pr * (1.0 / denom)
            p2 = jnp.concatenate([jnp.where(from_cur, zero, pn),
                                  jnp.where(from_cur, pn, zero)], axis=1).astype(BF16)
            o = _dot(p2, v_dup[j][win])
            yield
            pairs += [jnp.where(low_o, o[0:WINDOW], o[2 * WINDOW:3 * WINDOW]),
                      jnp.where(low_o, o[WINDOW:2 * WINDOW], o[3 * WINDOW:4 * WINDOW])]
        yb = jnp.concatenate(pairs, axis=1)
        yb_ref[rows, :] = _rms(yb, mgb_ref[...]).astype(BF16)


MIXER_STAGES = 5 + 2 * B_KV_HEADS * (ROWS_MIX // WINDOW)
POST_STAGES = 4 + FF_SPLIT


def _run(*staged):
    total = max(n for _, n in staged)
    done = [0] * len(staged)
    for step in range(1, total + 1):
        for k, (gen, n) in enumerate(staged):
            while done[k] * total < step * n:
                next(gen, None)
                done[k] += 1
    for gen, _ in staged:
        for _ in gen:
            pass


def _mix_kernel(layer, blocks_per_seq, sinks_ref, h_ref, cos_ref, sin_ref, g_ref, w_ref, *rest):
    mixer_refs, (z_ref, kprev_ref, vprev_ref) = rest[:-3], rest[-3:]
    first_block = (pl.program_id(0) % blocks_per_seq) == 0

    @pl.when(first_block)
    def _():
        kprev_ref[...] = jnp.zeros_like(kprev_ref)
        vprev_ref[...] = jnp.zeros_like(vprev_ref)

    _mix_project(h_ref, g_ref, w_ref, z_ref)
    _run((_mix_mixers(layer, first_block, sinks_ref, cos_ref, sin_ref, *mixer_refs,
                      z_ref, kprev_ref, vprev_ref), MIXER_STAGES))


def _mix_param_specs(layer):
    return [
        _layer_spec((1, D_MODEL), layer),
        _layer_spec((D_MODEL, IN_COLS), layer),
        _layer_spec((CHUNK, A_HEADS * CHUNK), layer),
        _layer_spec((CHUNK, A_WIDTH), layer),
        _layer_spec((1, A_WIDTH), layer),
        _layer_spec((1, A_WIDTH), layer),
        _layer_spec((1, A_WIDTH), layer),
        _layer_spec((1, B_WIDTH), layer),
        _layer_spec((1, LANES), layer),
        _layer_spec((1, B_WIDTH), layer),
        _const_spec((256, 256)),
    ]


def _mix_out_shapes(n):
    return [jax.ShapeDtypeStruct((n, A_WIDTH), BF16),
            jax.ShapeDtypeStruct((n, B_WIDTH), BF16),
            jax.ShapeDtypeStruct((n, C_WIDTH), F32)]


_MIX_SCRATCH = [
    pltpu.VMEM((ROWS_MIX, IN_COLS), F32),
    pltpu.VMEM((WINDOW, LANES), F32),
    pltpu.VMEM((WINDOW, LANES), F32),
]


def _mix(layer, h, cos2, sin2, sinks, mix_params, seq_len):
    n = h.shape[0]
    rb = lambda w_: pl.BlockSpec((ROWS_MIX, w_), lambda i: (i, 0))
    return pl.pallas_call(
        functools.partial(_mix_kernel, layer, seq_len // ROWS_MIX),
        grid=(n // ROWS_MIX,),
        in_specs=[pl.BlockSpec(memory_space=pltpu.SMEM), rb(D_MODEL), rb(LANES), rb(LANES)]
        + _mix_param_specs(layer),
        out_specs=[rb(A_WIDTH), rb(B_WIDTH), rb(C_WIDTH)],
        out_shape=_mix_out_shapes(n),
        scratch_shapes=_MIX_SCRATCH,
        compiler_params=pltpu.CompilerParams(
            dimension_semantics=("arbitrary",), vmem_limit_bytes=VMEM_LIMIT),
        name="mix",
    )(sinks, h, cos2, sin2, *mix_params)


def _tables_kernel(are_ref, aim_ref, ldt_ref, bre_ref, bim_ref, cre_ref, cim_ref, d_ref,
                   kt_ref, wb_ref, wct_ref, astep_ref, atab_ref):
    a_re, a_im = are_ref[...], aim_ref[...]
    dt = jnp.exp(ldt_ref[...])
    mag = jnp.exp(a_re * dt)
    lr, li = mag * jnp.cos(a_im * dt), mag * jnp.sin(a_im * dt)
    den = a_re * a_re + a_im * a_im
    fr, fi = _cmul(lr - 1.0, li, a_re / den, -a_im / den)
    bbr, bbi = _cmul(fr, fi, bre_ref[...], bim_ref[...])
    c_re, c_im = cre_ref[...], cim_ref[...]

    row_g = lax.broadcasted_iota(jnp.int32, (QUAD_WIDTH, QUAD_STATE // 2), 0) // C_GROUP
    col_g = lax.broadcasted_iota(jnp.int32, (QUAD_WIDTH, QUAD_STATE // 2), 1) // C_STATE
    same_group = row_g == col_g

    def quad_block(mr, mi, q):
        cols = slice(q * QUAD_STATE // 2, (q + 1) * QUAD_STATE // 2)
        tile = lambda m: jnp.where(same_group, jnp.concatenate([m[:, cols]] * QUAD_GROUPS, axis=0), 0.0)
        return jnp.concatenate([tile(mr), tile(mi)], axis=1)

    pows = [(jnp.ones_like(lr), jnp.zeros_like(li))]
    for _ in range(SSM_T):
        pows.append(_cmul(pows[-1][0], pows[-1][1], lr, li))
    norm2 = lr * lr + li * li
    ir, ii = lr / norm2, -li / norm2
    inv_pows = [(jnp.ones_like(lr), jnp.zeros_like(li))]
    for _ in range(SSM_T - 1):
        inv_pows.append(_cmul(inv_pows[-1][0], inv_pows[-1][1], ir, ii))

    rr = lax.broadcasted_iota(jnp.int32, (PACKED, PACKED), 0)
    cc = lax.broadcasted_iota(jnp.int32, (PACKED, PACKED), 1)
    for q in range(QUADS):
        into_state, from_state, lag_in, lag_out = [], [], [], []
        for s in range(SSM_T):
            wr, wi = _cmul(pows[SSM_T - 1 - s][0], pows[SSM_T - 1 - s][1], bbr, bbi)
            into_state.append(quad_block(wr, wi, q))
            mr, mi = _cmul(pows[s + 1][0], pows[s + 1][1], c_re, c_im)
            from_state.append(quad_block(mr, -mi, q))
            wr, wi = _cmul(inv_pows[s][0], inv_pows[s][1], bbr, bbi)
            lag_in.append(quad_block(wr, wi, q))
            mr, mi = _cmul(pows[s][0], pows[s][1], c_re, c_im)
            lag_out.append(quad_block(mr, -mi, q))
        wb_ref[q] = jnp.concatenate(into_state, axis=0).astype(BF16)
        wct_ref[q] = jnp.concatenate(from_state, axis=0).astype(BF16)
        a_hi, a_lo = _split(jnp.concatenate(lag_in, axis=0))
        b_hi, b_lo = _split(jnp.concatenate(lag_out, axis=0))
        k = _dot_nt(a_hi, b_hi) + _dot_nt(a_hi, b_lo) + _dot_nt(a_lo, b_hi)
        k = jnp.where(cc // QUAD_WIDTH >= rr // QUAD_WIDTH, k, 0.0)
        kt_ref[q] = (k + jnp.where(rr == cc, d_ref[q], 0.0)).astype(BF16)

    half = QUAD_STATE // 2
    for q in range(QUADS):
        src = slice(q * half, (q + 1) * half)
        re_cols, im_cols = pl.ds(q * QUAD_STATE, half), pl.ds(q * QUAD_STATE + half, half)
        sr, si = pows[SSM_T][0][:, src], pows[SSM_T][1][:, src]
        atab_ref[0:1, re_cols] = sr
        atab_ref[0:1, im_cols] = si
        for k in range(SCAN_STEPS):
            span = 1 << k
            astep_ref[pl.ds(k, 1), re_cols] = sr
            astep_ref[pl.ds(k, 1), im_cols] = si
            tr, ti = _cmul(atab_ref[0:span, re_cols], atab_ref[0:span, im_cols], sr, si)
            atab_ref[pl.ds(span, span), re_cols] = tr
            atab_ref[pl.ds(span, span), im_cols] = ti
            sr, si = _cmul(sr, si, sr, si)


def _tables(a_re, a_im, ldt, b_re, b_im, c_re, c_im, d_skip):
    depth = a_re.shape[0]
    vec = lambda w: pl.BlockSpec((None, 1, w), lambda l: (l, 0, 0))
    mat = lambda r, c: pl.BlockSpec((None, r, c), lambda l: (l, 0, 0))
    cube = lambda r, c: pl.BlockSpec((None, QUADS, r, c), lambda l: (l, 0, 0, 0))
    packed = jax.ShapeDtypeStruct((depth, QUADS, PACKED, PACKED), BF16)
    return pl.pallas_call(
        _tables_kernel,
        grid=(depth,),
        in_specs=[vec(N_STATE), vec(N_STATE), vec(N_STATE),
                  mat(C_GROUP, N_STATE), mat(C_GROUP, N_STATE),
                  mat(C_GROUP, N_STATE), mat(C_GROUP, N_STATE), cube(1, PACKED)],
        out_specs=[cube(PACKED, PACKED), cube(PACKED, PACKED), cube(PACKED, PACKED),
                   mat(SCAN_STEPS, 2 * N_STATE), mat(SUBLANES, 2 * N_STATE)],
        out_shape=[packed, packed, packed,
                   jax.ShapeDtypeStruct((depth, SCAN_STEPS, 2 * N_STATE), F32),
                   jax.ShapeDtypeStruct((depth, SUBLANES, 2 * N_STATE), F32)],
        compiler_params=pltpu.CompilerParams(
            dimension_semantics=("arbitrary",), vmem_limit_bytes=VMEM_LIMIT),
        name="ssm_tables",
    )(a_re, a_im, ldt, b_re, b_im, c_re, c_im, d_skip)


def _glu_out(glu_a, glu_b, mg_ref, o_ref):
    o_ref[...] = _rms(glu_a * jax.nn.sigmoid(glu_b), mg_ref[...]).astype(BF16)


def _ssm_kernel(blocks_per_seq, n_blocks, *refs):
    w1_ref, w2_ref, mg_ref, o_ref, _, _, ya_ref, yb_ref, carry_ref = refs[-9:]
    i = pl.program_id(0)

    @pl.when(i == 0)
    def _():
        ya_ref[...] = jnp.zeros_like(ya_ref)
        yb_ref[...] = jnp.zeros_like(yb_ref)

    @pl.when((i % blocks_per_seq) == 0)
    def _():
        carry_ref[...] = jnp.zeros_like(carry_ref)

    @pl.when(i < n_blocks)
    def _():
        _ssm_block(*refs)

    @pl.when(i == n_blocks)
    def _():
        y_prev = jax.nn.gelu(jnp.concatenate([ya_ref[...], yb_ref[...]], axis=1)).astype(BF16)
        _glu_out(_dot(y_prev, w1_ref[...]), _dot(y_prev, w2_ref[...]), mg_ref, o_ref)


def _ssm_block(ua_ref, ub_ref, kt_ref, wb_ref, wct_ref, astep_ref, atab_ref,
               w1_ref, w2_ref, mg_ref, o_ref, inc_ref, sprev_ref, ya_ref, yb_ref, carry_ref):
    low = lax.broadcasted_iota(jnp.int32, (SSM_N, LANES), 1) < QUAD_WIDTH

    halves = [[ref[pl.ds(j, SSM_N, stride=SSM_T), :] for j in range(SSM_T)]
              for ref in (ua_ref, ub_ref)]

    def pack(q):
        src, upper = halves[q // 2], q % 2 == 1
        cols = []
        for m in range(SSM_T // 2):
            even, odd = src[2 * m], src[2 * m + 1]
            if upper:
                even = pltpu.roll(even, QUAD_WIDTH, 1)
            else:
                odd = pltpu.roll(odd, QUAD_WIDTH, 1)
            cols.append(jnp.where(low, even, odd))
        return jnp.concatenate(cols, axis=1).astype(BF16)

    xs = [pack(q) for q in range(QUADS)]

    for q in range(QUADS // 2):
        inc_ref[:, q * QUAD_STATE:(q + 1) * QUAD_STATE] = _dot(xs[q], wb_ref[q])
    y_prev = jax.nn.gelu(jnp.concatenate([ya_ref[...], yb_ref[...]], axis=1)).astype(BF16)
    glu_a, glu_b = _dot(y_prev, w1_ref[...]), _dot(y_prev, w2_ref[...])
    for q in range(QUADS // 2, QUADS):
        inc_ref[:, q * QUAD_STATE:(q + 1) * QUAD_STATE] = _dot(xs[q], wb_ref[q])
    _glu_out(glu_a, glu_b, mg_ref, o_ref)

    n_groups = SSM_N // SUBLANES
    sub = lax.broadcasted_iota(jnp.int32, (n_groups, SUBLANES, LANES), 1)
    sub2 = lax.broadcasted_iota(jnp.int32, (SUBLANES, LANES), 0)
    tiles_per_quad = QUAD_STATE // 2 // LANES
    y_lag = []
    for lt in range(N_STATE // LANES):
        q, part = lt // tiles_per_quad, lt % tiles_per_quad
        re_cols = pl.ds(q * QUAD_STATE + part * LANES, LANES)
        im_cols = pl.ds(q * QUAD_STATE + QUAD_STATE // 2 + part * LANES, LANES)
        xr = inc_ref[:, re_cols].reshape(n_groups, SUBLANES, LANES)
        xi = inc_ref[:, im_cols].reshape(n_groups, SUBLANES, LANES)
        for k in range(SCAN_STEPS):
            sh = 1 << k
            ar = astep_ref[pl.ds(k, 1), re_cols]
            ai = astep_ref[pl.ds(k, 1), im_cols]
            keep = sub >= sh
            sr = jnp.where(keep, pltpu.roll(xr, sh, 1), 0.0)
            si = jnp.where(keep, pltpu.roll(xi, sh, 1), 0.0)
            xr, xi = xr + ar * sr - ai * si, xi + ar * si + ai * sr
        tr = atab_ref[:, re_cols]
        ti = atab_ref[:, im_cols]
        er = jnp.broadcast_to(carry_ref[:, re_cols], (SUBLANES, LANES))
        ei = jnp.broadcast_to(carry_ref[:, im_cols], (SUBLANES, LANES))
        before_r, before_i = [], []
        for g in range(n_groups):
            fr = xr[g] + tr * er - ti * ei
            fi = xi[g] + tr * ei + ti * er
            before_r.append(jnp.where(sub2 == 0, er, pltpu.roll(fr, 1, 0)))
            before_i.append(jnp.where(sub2 == 0, ei, pltpu.roll(fi, 1, 0)))
            er = jnp.broadcast_to(fr[SUBLANES - 1:SUBLANES, :], (SUBLANES, LANES))
            ei = jnp.broadcast_to(fi[SUBLANES - 1:SUBLANES, :], (SUBLANES, LANES))
        sprev_ref[:, re_cols] = jnp.concatenate(before_r, axis=0).astype(BF16)
        sprev_ref[:, im_cols] = jnp.concatenate(before_i, axis=0).astype(BF16)
        carry_ref[:, re_cols] = er[0:1, :]
        carry_ref[:, im_cols] = ei[0:1, :]

        if part == tiles_per_quad - 1:
            y_lag.append(_dot(xs[q], kt_ref[q]))

    ys = [y_lag[q] + _dot_nt(sprev_ref[:, q * QUAD_STATE:(q + 1) * QUAD_STATE], wct_ref[q])
          for q in range(QUADS)]
    for t in range(SSM_T):
        cols = slice((t // 2) * LANES, (t // 2 + 1) * LANES)
        for y_ref, (qa, qb) in ((ya_ref, (0, 1)), (yb_ref, (2, 3))):
            lower, upper = ys[qa][:, cols], ys[qb][:, cols]
            if t % 2 == 1:
                lower = pltpu.roll(lower, QUAD_WIDTH, 1)
            else:
                upper = pltpu.roll(upper, QUAD_WIDTH, 1)
            y_ref[pl.ds(t, SSM_N, stride=SSM_T), :] = jnp.where(low, lower, upper)


def _ssm(layer, zc, kd, wb, wct, astep, atab, w1, w2, mg, seq_len):
    n = zc.shape[0]
    n_blocks = n // ROWS_SSM
    scanned = lambda i: jnp.minimum(i, n_blocks - 1)
    return pl.pallas_call(
        functools.partial(_ssm_kernel, seq_len // ROWS_SSM, n_blocks),
        grid=(n_blocks + 1,),
        in_specs=[
            pl.BlockSpec((ROWS_SSM, LANES), lambda i: (scanned(i), 0)),
            pl.BlockSpec((ROWS_SSM, LANES), lambda i: (scanned(i), 1)),
            _layer_spec((QUADS, PACKED, PACKED), layer),
            _layer_spec((QUADS, PACKED, QUAD_STATE), layer),
            _layer_spec((QUADS, PACKED, QUAD_STATE), layer),
            _layer_spec((SCAN_STEPS, 2 * N_STATE), layer),
            _layer_spec((SUBLANES, 2 * N_STATE), layer),
            _layer_spec((C_WIDTH, C_WIDTH), layer),
            _layer_spec((C_WIDTH, C_WIDTH), layer),
            _layer_spec((1, C_WIDTH), layer),
        ],
        out_specs=pl.BlockSpec((ROWS_SSM, C_WIDTH), lambda i: (jnp.maximum(i - 1, 0), 0)),
        out_shape=jax.ShapeDtypeStruct((n, C_WIDTH), BF16),
        scratch_shapes=[
            pltpu.VMEM((SSM_N, 2 * N_STATE), F32),
            pltpu.VMEM((SSM_N, 2 * N_STATE), BF16),
            pltpu.VMEM((ROWS_SSM, LANES), F32),
            pltpu.VMEM((ROWS_SSM, LANES), F32),
            pltpu.VMEM((1, 2 * N_STATE), F32),
        ],
        compiler_params=pltpu.CompilerParams(
            dimension_semantics=("arbitrary",), vmem_limit_bytes=VMEM_LIMIT),
        name="ssm",
    )(zc, zc, kd, wb, wct, astep, atab, w1, w2, mg)


def _post_stages(h_ref, ya_ref, yb_ref, yc_ref, p_ref, wo_ref, g1_ref, w1_ref, w2_ref,
                 g2_ref, wg_ref, wp_ref, *out_refs):
    mixed = (_dot(ya_ref[...], wo_ref[0:A_WIDTH, :])
             + _dot(yb_ref[...], wo_ref[A_WIDTH:A_WIDTH + B_WIDTH, :])
             + _dot(yc_ref[...], wo_ref[A_WIDTH + B_WIDTH:, :]))
    yield
    h = h_ref[...] + mixed
    hn = _rms(h, g1_ref[...]).astype(BF16)
    ple = _dot(p_ref[...].astype(BF16), wp_ref[...])
    piece = D_FF // FF_SPLIT
    up = _dot(hn, w1_ref[:, 0:piece])
    yield
    ff = None
    for c in range(FF_SPLIT):
        a = jnp.maximum(up, 0.0)
        down = _dot((a * a).astype(BF16), w2_ref[c * piece:(c + 1) * piece, :])
        if c + 1 < FF_SPLIT:
            up = _dot(hn, w1_ref[:, (c + 1) * piece:(c + 2) * piece])
        yield
        ff = down if ff is None else ff + down
    h = h + ff
    gate = _dot(_rms(h, g2_ref[...]).astype(BF16), wg_ref[...])
    yield
    h = h + jax.nn.sigmoid(gate) * ple
    for o_ref in out_refs:
        o_ref[...] = h
    yield


N_POST_IN = 12


def _post_kernel(*refs):
    _run((_post_stages(*refs), POST_STAGES))


def _post_specs(layer, row_block):
    rb = lambda w: pl.BlockSpec((ROWS_PROJ, w), lambda i: (row_block(i), 0))
    return [
        rb(D_MODEL), rb(A_WIDTH), rb(B_WIDTH), rb(C_WIDTH),
        pl.BlockSpec((None, ROWS_PROJ, PLE_DIM), lambda i: (layer, row_block(i), 0)),
        _layer_spec((D_MODEL, D_MODEL), layer),
        _layer_spec((1, D_MODEL), layer),
        _layer_spec((D_MODEL, D_FF), layer),
        _layer_spec((D_FF, D_MODEL), layer),
        _layer_spec((1, D_MODEL), layer),
        _layer_spec((D_MODEL, D_MODEL), layer),
        _layer_spec((PLE_DIM, D_MODEL), layer),
    ]


def _post(layer, h, ya, yb, yc, p, post_params):
    n = h.shape[0]
    return pl.pallas_call(
        _post_kernel,
        grid=(n // ROWS_PROJ,),
        in_specs=_post_specs(layer, lambda i: i),
        out_specs=pl.BlockSpec((ROWS_PROJ, D_MODEL), lambda i: (i, 0)),
        out_shape=jax.ShapeDtypeStruct((n, D_MODEL), F32),
        compiler_params=pltpu.CompilerParams(
            dimension_semantics=("arbitrary",), vmem_limit_bytes=VMEM_LIMIT),
        name="post",
    )(h, ya, yb, yc, p, *post_params)


def _fused_kernel(layer, blocks_per_seq, n_blocks, sinks_ref, *refs):
    post_refs, refs = refs[:N_POST_IN], refs[N_POST_IN:]
    cos_ref, sin_ref, g_ref, w_ref = refs[:4]
    mixer_refs = refs[4:-8]
    h_out_ref, ya_ref, yb_ref, zc_ref = refs[-8:-4]
    hs_ref, z_ref, kprev_ref, vprev_ref = refs[-4:]
    i = pl.program_id(0)
    first_block = (i % blocks_per_seq) == 1 % blocks_per_seq

    @pl.when((i == 0) | first_block)
    def _():
        kprev_ref[...] = jnp.zeros_like(kprev_ref)
        vprev_ref[...] = jnp.zeros_like(vprev_ref)

    @pl.when(i == 0)
    def _():
        hs_ref[...] = jnp.zeros_like(hs_ref)

    _mix_project(hs_ref, g_ref, w_ref, z_ref)
    _run((_post_stages(*post_refs, h_out_ref, hs_ref), POST_STAGES),
         (_mix_mixers(layer, first_block, sinks_ref, cos_ref, sin_ref, *mixer_refs,
                      ya_ref, yb_ref, zc_ref, z_ref, kprev_ref, vprev_ref), MIXER_STAGES))


def _fused(layer, h, ya, yb, yc, p, post_params, cos2, sin2, sinks, mix_params, seq_len):
    n = h.shape[0]
    n_blocks = n // ROWS_MIX
    merged = lambda i: jnp.minimum(i, n_blocks - 1)
    mixed = lambda i: jnp.maximum(i - 1, 0)
    rb = lambda w_: pl.BlockSpec((ROWS_MIX, w_), lambda i: (mixed(i), 0))
    return pl.pallas_call(
        functools.partial(_fused_kernel, layer, seq_len // ROWS_MIX, n_blocks),
        grid=(n_blocks + 1,),
        in_specs=[pl.BlockSpec(memory_space=pltpu.SMEM)]
        + _post_specs(layer - 1, merged)
        + [rb(LANES), rb(LANES)] + _mix_param_specs(layer),
        out_specs=[pl.BlockSpec((ROWS_PROJ, D_MODEL), lambda i: (merged(i), 0)),
                   rb(A_WIDTH), rb(B_WIDTH), rb(C_WIDTH)],
        out_shape=[jax.ShapeDtypeStruct((n, D_MODEL), F32)] + _mix_out_shapes(n),
        scratch_shapes=[pltpu.VMEM((ROWS_MIX, D_MODEL), F32)] + _MIX_SCRATCH,
        compiler_params=pltpu.CompilerParams(
            dimension_semantics=("arbitrary",), vmem_limit_bytes=VMEM_LIMIT),
        name="fused",
    )(sinks, h, ya, yb, yc, p, *post_params, cos2, sin2, *mix_params)


def kernel(x, p, positions, attn_norm_g, w_in, gmlp_ln_g, gmlp_ln_b, gmlp_ws, gmlp_bs, q_norm_g, k_norm_g, sinks, ssm_a_re, ssm_a_im, ssm_log_dt, ssm_b_re, ssm_b_im, ssm_c_re, ssm_c_im, ssm_d, glu_w1, glu_w2, mix_out_g, w_out, mlp_norm_g, w_ff1, w_ff2, ple_norm_g, w_ple_gate, w_ple_proj):
    bsz, seq_len, _ = x.shape
    depth = w_in.shape[0]
    n = bsz * seq_len

    inv = 1.0 / (ROPE_THETA ** (jnp.arange(0, HEAD_DIM, 2, dtype=F32) / HEAD_DIM))
    ang = positions.astype(F32).reshape(n, 1) * inv
    cos2 = jnp.tile(jnp.cos(ang), (1, 4))
    sin2 = jnp.tile(jnp.concatenate([-jnp.sin(ang), jnp.sin(ang)], axis=1), (1, 2))

    seg = np.arange(256) // HEAD_DIM
    pm = jnp.asarray((seg[:, None] == seg[None, :]) / HEAD_DIM, dtype=BF16)

    rows = lambda v: v.reshape(depth, 1, -1).astype(F32)
    g_attn, g_mlp, g_ple = rows(attn_norm_g), rows(mlp_norm_g), rows(ple_norm_g)
    ln_g, ln_b = rows(gmlp_ln_g), rows(gmlp_ln_b)
    mg_a = rows(mix_out_g[:, :A_WIDTH])
    mg_b = rows(mix_out_g[:, A_WIDTH:A_WIDTH + B_WIDTH])
    mg_c = rows(mix_out_g[:, A_WIDTH + B_WIDTH:])
    qg = rows(jnp.tile(q_norm_g, (1, B_Q_HEADS)))
    kg = rows(jnp.tile(k_norm_g, (1, B_KV_HEADS)))
    bs = jnp.repeat(jnp.swapaxes(gmlp_bs, 1, 2), HEAD_DIM, axis=2)
    w_a = w_in[:, :, :2 * A_WIDTH].reshape(depth, D_MODEL, A_HEADS, 2, HEAD_DIM)
    w_a = jnp.swapaxes(w_a, 2, 3).reshape(depth, D_MODEL, 2 * A_WIDTH)
    w_in_b = jnp.concatenate([w_a, w_in[:, :, 2 * A_WIDTH:]], axis=2).astype(BF16)
    causal = np.tril(np.ones((CHUNK, CHUNK), dtype=bool))
    ws_b = jnp.swapaxes(jnp.where(causal, gmlp_ws, 0.0), 1, 2).reshape(
        depth, CHUNK, A_HEADS * CHUNK).astype(BF16)
    w_out_b = w_out.astype(BF16)
    w_ff1_b, w_ff2_b = w_ff1.astype(BF16), w_ff2.astype(BF16)
    w_gate_b, w_proj_b = w_ple_gate.astype(BF16), w_ple_proj.astype(BF16)
    glu1_b, glu2_b = glu_w1.astype(BF16), glu_w2.astype(BF16)
    p2 = p.reshape(depth, n, PLE_DIM)

    kd, wb, wct, astep, atab = _tables(
        rows(ssm_a_re), rows(ssm_a_im), rows(jnp.repeat(ssm_log_dt, C_STATE, axis=1)),
        jnp.transpose(ssm_b_re, (0, 3, 1, 2)).reshape(depth, C_GROUP, N_STATE),
        jnp.transpose(ssm_b_im, (0, 3, 1, 2)).reshape(depth, C_GROUP, N_STATE),
        jnp.transpose(ssm_c_re, (0, 2, 1, 3)).reshape(depth, C_GROUP, N_STATE),
        jnp.transpose(ssm_c_im, (0, 2, 1, 3)).reshape(depth, C_GROUP, N_STATE),
        jnp.tile(ssm_d.reshape(depth, QUADS, 1, QUAD_WIDTH).astype(F32), (1, 1, 1, SSM_T)))

    mix_params = (g_attn, w_in_b, ws_b, bs, ln_g, ln_b, mg_a, qg, kg, mg_b, pm)
    post_params = (w_out_b, g_mlp, w_ff1_b, w_ff2_b, g_ple, w_gate_b, w_proj_b)
    h = x.reshape(n, D_MODEL)
    ya, yb, zc = _mix(0, h, cos2, sin2, sinks, mix_params, seq_len)
    for i in range(depth):
        yc = _ssm(i, zc, kd, wb, wct, astep, atab, glu1_b, glu2_b, mg_c, seq_len)
        if i + 1 < depth:
            h, ya, yb, zc = _fused(i + 1, h, ya, yb, yc, p2, post_params, cos2, sin2, sinks,
                                   mix_params, seq_len)
        else:
            h = _post(i, h, ya, yb, yc, p2, post_params)
    return h.reshape(bsz, seq_len, D_MODEL)
```

```python
import functools
import math

import numpy as np
import jax
import jax.numpy as jnp
from jax import lax
from jax.experimental import pallas as pl
from jax.experimental.pallas import tpu as pltpu

F32 = jnp.float32
BF16 = jnp.bfloat16

D_MODEL = 1024
HEAD_DIM = 64
A_WIDTH = 256
A_HEADS = 4
CHUNK = 128
B_WIDTH = 512
B_Q_HEADS = 8
B_KV_HEADS = 2
WINDOW = 128
ROPE_THETA = 10000.0
C_WIDTH = 256
C_GROUP = 16
C_GROUPS = 16
C_STATE = 64
N_STATE = C_GROUPS * C_STATE
IN_COLS = 1536
D_FF = 4096
PLE_DIM = 256
EPS = 1e-6
NEG_BIG = -1e30
LANES = 128

COL_A = 0
COL_Q = 512
COL_K = 1024
COL_V = 1152
COL_C = 1280

ROWS_PROJ = 512
ROWS_MIX = 512
ROWS_SSM = 2048
SSM_T = 8
SSM_N = ROWS_SSM // SSM_T
QUAD_GROUPS = 4
QUADS = C_GROUPS // QUAD_GROUPS
QUAD_WIDTH = QUAD_GROUPS * C_GROUP
QUAD_STATE = 2 * QUAD_GROUPS * C_STATE
PACKED = SSM_T * QUAD_WIDTH
SUBLANES = 8
SCAN_STEPS = int(math.log2(SUBLANES))
FF_SPLIT = 8
VMEM_LIMIT = 56 * 1024 * 1024


def _const_spec(shape):
    nd = len(shape)
    return pl.BlockSpec(shape, lambda *_: (0,) * nd, pipeline_mode=pl.Buffered(1))


def _layer_spec(shape, layer):
    nd = len(shape)
    return pl.BlockSpec((None,) + tuple(shape), lambda *_: (layer,) + (0,) * nd,
                        pipeline_mode=pl.Buffered(1))


def _rms(x, g):
    ms = jnp.mean(x * x, axis=-1, keepdims=True)
    return x * lax.rsqrt(ms + EPS) * g


def _dot(a, b):
    return jnp.dot(a, b, preferred_element_type=F32)


def _dot_nt(a, b):
    return lax.dot_general(a, b, (((1,), (1,)), ((), ())), preferred_element_type=F32)


def _split(x):
    hi = x.astype(BF16)
    return hi, (x - hi.astype(F32)).astype(BF16)


def _seg_mean(x, pmat):
    hi, lo = _split(x)
    return _dot(hi, pmat) + _dot(lo, pmat)


def _cmul(ar, ai, br, bi):
    return ar * br - ai * bi, ar * bi + ai * br


def _rope(x, cos, sin_signed):
    width = x.shape[-1]
    lane = lax.broadcasted_iota(jnp.int32, x.shape, 1)
    first_half = (lane % HEAD_DIM) < (HEAD_DIM // 2)
    partner = jnp.where(first_half,
                        pltpu.roll(x, width - HEAD_DIM // 2, 1),
                        pltpu.roll(x, HEAD_DIM // 2, 1))
    return x * cos + partner * sin_signed


def _dup_heads(x):
    low = lax.broadcasted_iota(jnp.int32, x.shape, 1) < HEAD_DIM
    sw = pltpu.roll(x, HEAD_DIM, 1)
    return [jnp.where(low, x, sw).astype(BF16), jnp.where(low, sw, x).astype(BF16)]


def _mix_project(h_ref, g_ref, w_ref, z_ref):
    xn = _rms(h_ref[...], g_ref[...]).astype(BF16)
    z_ref[...] = _dot(xn, w_ref[...])


def _mix_mixers(layer, first_block, sinks_ref, cos_ref, sin_ref, ws_ref, bs_ref, lng_ref, lnb_ref,
                mga_ref, qg_ref, kg_ref, mgb_ref, pm_ref, ya_ref, yb_ref, zc_ref,
                z_ref, kprev_ref, vprev_ref):
    nq = ROWS_MIX // WINDOW
    zc_ref[...] = z_ref[:, COL_C:COL_C + C_WIDTH]
    pm = pm_ref[...]
    pm_kv = pm[0:LANES, 0:LANES]

    k_raw = z_ref[:, COL_K:COL_K + LANES]
    k_ms = _dot((k_raw * k_raw).astype(BF16), pm_kv)
    yield
    k_cur = _rope(k_raw * lax.rsqrt(k_ms + EPS) * kg_ref[...], cos_ref[...], sin_ref[...])
    v_cur = z_ref[:, COL_V:COL_V + LANES]
    k_dup = _dup_heads(jnp.concatenate([kprev_ref[...], k_cur], axis=0))
    v_dup = _dup_heads(jnp.concatenate([vprev_ref[...], v_cur], axis=0))
    kprev_ref[...] = k_cur[ROWS_MIX - WINDOW:, :]
    vprev_ref[...] = v_cur[ROWS_MIX - WINDOW:, :]
    q = z_ref[:, COL_Q:COL_Q + B_WIDTH]
    q_ms = jnp.concatenate(
        [_dot((q[:, s:s + 256] * q[:, s:s + 256]).astype(BF16), pm) for s in (0, 256)], axis=1)
    yield
    cos_q = jnp.concatenate([cos_ref[...]] * 4, axis=1)
    sin_q = jnp.concatenate([sin_ref[...]] * 4, axis=1)
    qr = _rope(q * lax.rsqrt(q_ms + EPS) * qg_ref[...], cos_q, sin_q) * (HEAD_DIM ** -0.5)
    low_q = (lax.broadcasted_iota(jnp.int32, qr.shape, 1) % LANES) < HEAD_DIM
    q_low = jnp.where(low_q, qr, 0.0).astype(BF16)
    q_high = jnp.where(low_q, 0.0, qr).astype(BF16)

    u = jax.nn.gelu(z_ref[:, COL_A:COL_A + A_WIDTH])
    v = jax.nn.gelu(z_ref[:, COL_A + A_WIDTH:COL_A + 2 * A_WIDTH])
    v_mean = _dot(v.astype(BF16), pm)
    yield
    vc = v - v_mean
    v_var = _dot((vc * vc).astype(BF16), pm)
    yield
    vn = (vc * lax.rsqrt(v_var + EPS) * lng_ref[...] + lnb_ref[...]).astype(BF16)
    lane = lax.broadcasted_iota(jnp.int32, (CHUNK, A_WIDTH), 1)
    in_head = [(lane >= h * HEAD_DIM) & (lane < (h + 1) * HEAD_DIM) for h in range(A_HEADS)]
    svs = []
    for c in range(ROWS_MIX // CHUNK):
        vn_c = vn[c * CHUNK:(c + 1) * CHUNK]
        stacked = jnp.concatenate(
            [jnp.where(in_head[h], vn_c, jnp.zeros((), BF16)) for h in range(A_HEADS)], axis=0)
        svs.append(_dot(ws_ref[...], stacked))
    yield
    for c in range(ROWS_MIX // CHUNK):
        rows = slice(c * CHUNK, (c + 1) * CHUNK)
        ya_ref[rows, :] = _rms(u[rows] * (svs[c] + bs_ref[...]), mga_ref[...]).astype(BF16)

    qi = lax.broadcasted_iota(jnp.int32, (4 * WINDOW, WINDOW), 0) % WINDOW
    kc = lax.broadcasted_iota(jnp.int32, (4 * WINDOW, WINDOW), 1)
    from_cur = kc <= qi
    head_slot = lax.broadcasted_iota(jnp.int32, (4 * WINDOW, 1), 0) // WINDOW
    prev_ok = kc >= jnp.where(first_block, WINDOW, 0)
    low_o = lax.broadcasted_iota(jnp.int32, (WINDOW, LANES), 1) < HEAD_DIM
    zero = jnp.zeros((), F32)

    def scores(b, j):
        rows = slice(b * WINDOW, (b + 1) * WINDOW)
        pair_a = slice((2 * j) * LANES, (2 * j + 1) * LANES)
        pair_b = slice((2 * j + 1) * LANES, (2 * j + 2) * LANES)
        qs = jnp.concatenate([q_low[rows, pair_a], q_low[rows, pair_b],
                              q_high[rows, pair_a], q_high[rows, pair_b]], axis=0)
        return _dot_nt(qs, k_dup[j][b * WINDOW:(b + 2) * WINDOW])

    units = [(b, j) for b in range(nq) for j in range(B_KV_HEADS)]
    s_next = scores(*units[0])
    yield
    pairs = []
    for u, (b, j) in enumerate(units):
        s = s_next
        if u + 1 < len(units):
            s_next = scores(*units[u + 1])
        s_prev = s[:, 0:WINDOW]
        if b == 0:
            s_prev = jnp.where(prev_ok, s_prev, NEG_BIG)
        s = jnp.where(from_cur, s[:, WINDOW:], s_prev)
        sink = jnp.where(head_slot == 0, sinks_ref[layer, 4 * j],
                         jnp.where(head_slot == 1, sinks_ref[layer, 4 * j + 2],
                                   jnp.where(head_slot == 2, sinks_ref[layer, 4 * j + 1],
                                             sinks_ref[layer, 4 * j + 3])))
        m = jnp.maximum(jnp.max(s, axis=-1, keepdims=True), sink)
        pr = jnp.exp(s - m)
        denom = jnp.sum(pr, axis=-1, keepdims=True) + jnp.exp(sink - m)
        pn = pr * (1.0 / denom)
        p2 = jnp.concatenate([jnp.where(from_cur, zero, pn),
                              jnp.where(from_cur, pn, zero)], axis=1).astype(BF16)
        yield
        o = _dot(p2, v_dup[j][b * WINDOW:(b + 2) * WINDOW])
        yield
        pairs += [jnp.where(low_o, o[0:WINDOW], o[2 * WINDOW:3 * WINDOW]),
                  jnp.where(low_o, o[WINDOW:2 * WINDOW], o[3 * WINDOW:4 * WINDOW])]
        if j == B_KV_HEADS - 1:
            yb = jnp.concatenate(pairs, axis=1)
            yb_ref[b * WINDOW:(b + 1) * WINDOW, :] = _rms(yb, mgb_ref[...]).astype(BF16)
            pairs = []


MIXER_STAGES = 6 + 2 * B_KV_HEADS * (ROWS_MIX // WINDOW)
POST_STAGES = 4 + FF_SPLIT


def _run(*staged):
    total = max(n for _, n in staged)
    done = [0] * len(staged)
    for step in range(1, total + 1):
        for k, (gen, n) in enumerate(staged):
            while done[k] * total < step * n:
                next(gen, None)
                done[k] += 1
    for gen, _ in staged:
        for _ in gen:
            pass


def _mix_kernel(layer, blocks_per_seq, sinks_ref, h_ref, cos_ref, sin_ref, g_ref, w_ref, *rest):
    mixer_refs, (z_ref, kprev_ref, vprev_ref) = rest[:-3], rest[-3:]
    first_block = (pl.program_id(0) % blocks_per_seq) == 0

    @pl.when(first_block)
    def _():
        kprev_ref[...] = jnp.zeros_like(kprev_ref)
        vprev_ref[...] = jnp.zeros_like(vprev_ref)

    _mix_project(h_ref, g_ref, w_ref, z_ref)
    _run((_mix_mixers(layer, first_block, sinks_ref, cos_ref, sin_ref, *mixer_refs,
                      z_ref, kprev_ref, vprev_ref), MIXER_STAGES))


def _mix_param_specs(layer):
    return [
        _layer_spec((1, D_MODEL), layer),
        _layer_spec((D_MODEL, IN_COLS), layer),
        _layer_spec((CHUNK, A_HEADS * CHUNK), layer),
        _layer_spec((CHUNK, A_WIDTH), layer),
        _layer_spec((1, A_WIDTH), layer),
        _layer_spec((1, A_WIDTH), layer),
        _layer_spec((1, A_WIDTH), layer),
        _layer_spec((1, B_WIDTH), layer),
        _layer_spec((1, LANES), layer),
        _layer_spec((1, B_WIDTH), layer),
        _const_spec((256, 256)),
    ]


def _mix_out_shapes(n):
    return [jax.ShapeDtypeStruct((n, A_WIDTH), BF16),
            jax.ShapeDtypeStruct((n, B_WIDTH), BF16),
            jax.ShapeDtypeStruct((n, C_WIDTH), F32)]


_MIX_SCRATCH = [
    pltpu.VMEM((ROWS_MIX, IN_COLS), F32),
    pltpu.VMEM((WINDOW, LANES), F32),
    pltpu.VMEM((WINDOW, LANES), F32),
]


def _mix(layer, h, cos2, sin2, sinks, mix_params, seq_len):
    n = h.shape[0]
    rb = lambda w_: pl.BlockSpec((ROWS_MIX, w_), lambda i: (i, 0))
    return pl.pallas_call(
        functools.partial(_mix_kernel, layer, seq_len // ROWS_MIX),
        grid=(n // ROWS_MIX,),
        in_specs=[pl.BlockSpec(memory_space=pltpu.SMEM), rb(D_MODEL), rb(LANES), rb(LANES)]
        + _mix_param_specs(layer),
        out_specs=[rb(A_WIDTH), rb(B_WIDTH), rb(C_WIDTH)],
        out_shape=_mix_out_shapes(n),
        scratch_shapes=_MIX_SCRATCH,
        compiler_params=pltpu.CompilerParams(
            dimension_semantics=("arbitrary",), vmem_limit_bytes=VMEM_LIMIT),
        name="mix",
    )(sinks, h, cos2, sin2, *mix_params)


def _tables_kernel(are_ref, aim_ref, ldt_ref, bre_ref, bim_ref, cre_ref, cim_ref, d_ref,
                   kt_ref, wb_ref, wct_ref, astep_ref, atab_ref):
    a_re, a_im = are_ref[...], aim_ref[...]
    dt = jnp.exp(ldt_ref[...])
    mag = jnp.exp(a_re * dt)
    lr, li = mag * jnp.cos(a_im * dt), mag * jnp.sin(a_im * dt)
    den = a_re * a_re + a_im * a_im
    fr, fi = _cmul(lr - 1.0, li, a_re / den, -a_im / den)
    bbr, bbi = _cmul(fr, fi, bre_ref[...], bim_ref[...])
    c_re, c_im = cre_ref[...], cim_ref[...]

    row_g = lax.broadcasted_iota(jnp.int32, (QUAD_WIDTH, QUAD_STATE // 2), 0) // C_GROUP
    col_g = lax.broadcasted_iota(jnp.int32, (QUAD_WIDTH, QUAD_STATE // 2), 1) // C_STATE
    same_group = row_g == col_g

    def quad_block(mr, mi, q):
        cols = slice(q * QUAD_STATE // 2, (q + 1) * QUAD_STATE // 2)
        tile = lambda m: jnp.where(same_group, jnp.concatenate([m[:, cols]] * QUAD_GROUPS, axis=0), 0.0)
        return jnp.concatenate([tile(mr), tile(mi)], axis=1)

    pows = [(jnp.ones_like(lr), jnp.zeros_like(li))]
    for _ in range(SSM_T):
        pows.append(_cmul(pows[-1][0], pows[-1][1], lr, li))

    rr = lax.broadcasted_iota(jnp.int32, (PACKED, PACKED), 0)
    cc = lax.broadcasted_iota(jnp.int32, (PACKED, PACKED), 1)
    for q in range(QUADS):
        into_state, from_state, lag_out = [], [], []
        for s in range(SSM_T):
            wr, wi = _cmul(pows[SSM_T - 1 - s][0], pows[SSM_T - 1 - s][1], bbr, bbi)
            into_state.append(quad_block(wr, wi, q))
            mr, mi = _cmul(pows[s + 1][0], pows[s + 1][1], c_re, c_im)
            from_state.append(quad_block(mr, -mi, q))
            mr, mi = _cmul(pows[s][0], pows[s][1], c_re, c_im)
            lag_out.append(quad_block(mr, -mi, q))
        wb_ref[q] = jnp.concatenate(into_state, axis=0).astype(BF16)
        wct_ref[q] = jnp.concatenate(from_state, axis=0).astype(BF16)
        a_hi, a_lo = _split(quad_block(bbr, bbi, q))
        b_hi, b_lo = _split(jnp.concatenate(lag_out, axis=0))
        lags = _dot_nt(a_hi, b_hi) + _dot_nt(a_hi, b_lo) + _dot_nt(a_lo, b_hi)
        lane = lax.broadcasted_iota(jnp.int32, lags.shape, 1)
        k = jnp.concatenate(
            [lags] + [jnp.where(lane >= s * QUAD_WIDTH, pltpu.roll(lags, s * QUAD_WIDTH, 1), 0.0)
                      for s in range(1, SSM_T)], axis=0)
        kt_ref[q] = (k + jnp.where(rr == cc, d_ref[q], 0.0)).astype(BF16)

    half = QUAD_STATE // 2
    for q in range(QUADS):
        src = slice(q * half, (q + 1) * half)
        re_cols, im_cols = pl.ds(q * QUAD_STATE, half), pl.ds(q * QUAD_STATE + half, half)
        sr, si = pows[SSM_T][0][:, src], pows[SSM_T][1][:, src]
        atab_ref[0:1, re_cols] = sr
        atab_ref[0:1, im_cols] = si
        for k in range(SCAN_STEPS):
            span = 1 << k
            astep_ref[pl.ds(k, 1), re_cols] = sr
            astep_ref[pl.ds(k, 1), im_cols] = si
            tr, ti = _cmul(atab_ref[0:span, re_cols], atab_ref[0:span, im_cols], sr, si)
            atab_ref[pl.ds(span, span), re_cols] = tr
            atab_ref[pl.ds(span, span), im_cols] = ti
            sr, si = _cmul(sr, si, sr, si)


def _tables(a_re, a_im, ldt, b_re, b_im, c_re, c_im, d_skip):
    depth = a_re.shape[0]
    vec = lambda w: pl.BlockSpec((None, 1, w), lambda l: (l, 0, 0))
    mat = lambda r, c: pl.BlockSpec((None, r, c), lambda l: (l, 0, 0))
    cube = lambda r, c: pl.BlockSpec((None, QUADS, r, c), lambda l: (l, 0, 0, 0))
    packed = jax.ShapeDtypeStruct((depth, QUADS, PACKED, PACKED), BF16)
    return pl.pallas_call(
        _tables_kernel,
        grid=(depth,),
        in_specs=[vec(N_STATE), vec(N_STATE), vec(N_STATE),
                  mat(C_GROUP, N_STATE), mat(C_GROUP, N_STATE),
                  mat(C_GROUP, N_STATE), mat(C_GROUP, N_STATE), cube(1, PACKED)],
        out_specs=[cube(PACKED, PACKED), cube(PACKED, PACKED), cube(PACKED, PACKED),
                   mat(SCAN_STEPS, 2 * N_STATE), mat(SUBLANES, 2 * N_STATE)],
        out_shape=[packed, packed, packed,
                   jax.ShapeDtypeStruct((depth, SCAN_STEPS, 2 * N_STATE), F32),
                   jax.ShapeDtypeStruct((depth, SUBLANES, 2 * N_STATE), F32)],
        compiler_params=pltpu.CompilerParams(
            dimension_semantics=("arbitrary",), vmem_limit_bytes=VMEM_LIMIT),
        name="ssm_tables",
    )(a_re, a_im, ldt, b_re, b_im, c_re, c_im, d_skip)


def _glu_out(glu_a, glu_b, mg_ref, o_ref):
    o_ref[...] = _rms(glu_a * jax.nn.sigmoid(glu_b), mg_ref[...]).astype(BF16)


def _ssm_kernel(blocks_per_seq, n_blocks, *refs):
    w1_ref, w2_ref, mg_ref, o_ref, _, _, ya_ref, yb_ref, carry_ref = refs[-9:]
    i = pl.program_id(0)

    @pl.when(i == 0)
    def _():
        ya_ref[...] = jnp.zeros_like(ya_ref)
        yb_ref[...] = jnp.zeros_like(yb_ref)

    @pl.when((i % blocks_per_seq) == 0)
    def _():
        carry_ref[...] = jnp.zeros_like(carry_ref)

    @pl.when(i < n_blocks)
    def _():
        _ssm_block(*refs)

    @pl.when(i == n_blocks)
    def _():
        y_prev = jax.nn.gelu(jnp.concatenate([ya_ref[...], yb_ref[...]], axis=1)).astype(BF16)
        _glu_out(_dot(y_prev, w1_ref[...]), _dot(y_prev, w2_ref[...]), mg_ref, o_ref)


def _ssm_block(ua_ref, ub_ref, kt_ref, wb_ref, wct_ref, astep_ref, atab_ref,
               w1_ref, w2_ref, mg_ref, o_ref, inc_ref, sprev_ref, ya_ref, yb_ref, carry_ref):
    low = lax.broadcasted_iota(jnp.int32, (SSM_N, LANES), 1) < QUAD_WIDTH

    halves = [[ref[pl.ds(j, SSM_N, stride=SSM_T), :] for j in range(SSM_T)]
              for ref in (ua_ref, ub_ref)]

    def pack(q):
        src, upper = halves[q // 2], q % 2 == 1
        cols = []
        for m in range(SSM_T // 2):
            even, odd = src[2 * m], src[2 * m + 1]
            if upper:
                even = pltpu.roll(even, QUAD_WIDTH, 1)
            else:
                odd = pltpu.roll(odd, QUAD_WIDTH, 1)
            cols.append(jnp.where(low, even, odd))
        return jnp.concatenate(cols, axis=1).astype(BF16)

    xs = [pack(q) for q in range(QUADS)]

    for q in range(QUADS // 2):
        inc_ref[:, q * QUAD_STATE:(q + 1) * QUAD_STATE] = _dot(xs[q], wb_ref[q])
    y_prev = jax.nn.gelu(jnp.concatenate([ya_ref[...], yb_ref[...]], axis=1)).astype(BF16)
    glu_a, glu_b = _dot(y_prev, w1_ref[...]), _dot(y_prev, w2_ref[...])
    for q in range(QUADS // 2, QUADS):
        inc_ref[:, q * QUAD_STATE:(q + 1) * QUAD_STATE] = _dot(xs[q], wb_ref[q])
    _glu_out(glu_a, glu_b, mg_ref, o_ref)

    n_groups = SSM_N // SUBLANES
    sub = lax.broadcasted_iota(jnp.int32, (n_groups, SUBLANES, LANES), 1)
    sub2 = lax.broadcasted_iota(jnp.int32, (SUBLANES, LANES), 0)
    tiles_per_quad = QUAD_STATE // 2 // LANES
    y_lag = []
    for lt in range(N_STATE // LANES):
        q, part = lt // tiles_per_quad, lt % tiles_per_quad
        re_cols = pl.ds(q * QUAD_STATE + part * LANES, LANES)
        im_cols = pl.ds(q * QUAD_STATE + QUAD_STATE // 2 + part * LANES, LANES)
        xr = inc_ref[:, re_cols].reshape(n_groups, SUBLANES, LANES)
        xi = inc_ref[:, im_cols].reshape(n_groups, SUBLANES, LANES)
        for k in range(SCAN_STEPS):
            sh = 1 << k
            ar = astep_ref[pl.ds(k, 1), re_cols]
            ai = astep_ref[pl.ds(k, 1), im_cols]
            keep = sub >= sh
            sr = jnp.where(keep, pltpu.roll(xr, sh, 1), 0.0)
            si = jnp.where(keep, pltpu.roll(xi, sh, 1), 0.0)
            xr, xi = xr + ar * sr - ai * si, xi + ar * si + ai * sr
        tr = atab_ref[:, re_cols]
        ti = atab_ref[:, im_cols]
        er = jnp.broadcast_to(carry_ref[:, re_cols], (SUBLANES, LANES))
        ei = jnp.broadcast_to(carry_ref[:, im_cols], (SUBLANES, LANES))
        before_r, before_i = [], []
        for g in range(n_groups):
            fr = xr[g] + tr * er - ti * ei
            fi = xi[g] + tr * ei + ti * er
            before_r.append(jnp.where(sub2 == 0, er, pltpu.roll(fr, 1, 0)))
            before_i.append(jnp.where(sub2 == 0, ei, pltpu.roll(fi, 1, 0)))
            er = jnp.broadcast_to(fr[SUBLANES - 1:SUBLANES, :], (SUBLANES, LANES))
            ei = jnp.broadcast_to(fi[SUBLANES - 1:SUBLANES, :], (SUBLANES, LANES))
        sprev_ref[:, re_cols] = jnp.concatenate(before_r, axis=0).astype(BF16)
        sprev_ref[:, im_cols] = jnp.concatenate(before_i, axis=0).astype(BF16)
        carry_ref[:, re_cols] = er[0:1, :]
        carry_ref[:, im_cols] = ei[0:1, :]

        if part == tiles_per_quad - 1:
            y_lag.append(_dot(xs[q], kt_ref[q]))

    ys = [y_lag[q] + _dot_nt(sprev_ref[:, q * QUAD_STATE:(q + 1) * QUAD_STATE], wct_ref[q])
          for q in range(QUADS)]
    for t in range(SSM_T):
        cols = slice((t // 2) * LANES, (t // 2 + 1) * LANES)
        for y_ref, (qa, qb) in ((ya_ref, (0, 1)), (yb_ref, (2, 3))):
            lower, upper = ys[qa][:, cols], ys[qb][:, cols]
            if t % 2 == 1:
                lower = pltpu.roll(lower, QUAD_WIDTH, 1)
            else:
                upper = pltpu.roll(upper, QUAD_WIDTH, 1)
            y_ref[pl.ds(t, SSM_N, stride=SSM_T), :] = jnp.where(low, lower, upper)


def _ssm(layer, zc, kd, wb, wct, astep, atab, w1, w2, mg, seq_len):
    n = zc.shape[0]
    n_blocks = n // ROWS_SSM
    scanned = lambda i: jnp.minimum(i, n_blocks - 1)
    return pl.pallas_call(
        functools.partial(_ssm_kernel, seq_len // ROWS_SSM, n_blocks),
        grid=(n_blocks + 1,),
        in_specs=[
            pl.BlockSpec((ROWS_SSM, LANES), lambda i: (scanned(i), 0)),
            pl.BlockSpec((ROWS_SSM, LANES), lambda i: (scanned(i), 1)),
            _layer_spec((QUADS, PACKED, PACKED), layer),
            _layer_spec((QUADS, PACKED, QUAD_STATE), layer),
            _layer_spec((QUADS, PACKED, QUAD_STATE), layer),
            _layer_spec((SCAN_STEPS, 2 * N_STATE), layer),
            _layer_spec((SUBLANES, 2 * N_STATE), layer),
            _layer_spec((C_WIDTH, C_WIDTH), layer),
            _layer_spec((C_WIDTH, C_WIDTH), layer),
            _layer_spec((1, C_WIDTH), layer),
        ],
        out_specs=pl.BlockSpec((ROWS_SSM, C_WIDTH), lambda i: (jnp.maximum(i - 1, 0), 0)),
        out_shape=jax.ShapeDtypeStruct((n, C_WIDTH), BF16),
        scratch_shapes=[
            pltpu.VMEM((SSM_N, 2 * N_STATE), F32),
            pltpu.VMEM((SSM_N, 2 * N_STATE), BF16),
            pltpu.VMEM((ROWS_SSM, LANES), F32),
            pltpu.VMEM((ROWS_SSM, LANES), F32),
            pltpu.VMEM((1, 2 * N_STATE), F32),
        ],
        compiler_params=pltpu.CompilerParams(
            dimension_semantics=("arbitrary",), vmem_limit_bytes=VMEM_LIMIT),
        name="ssm",
    )(zc, zc, kd, wb, wct, astep, atab, w1, w2, mg)


def _post_stages(h_ref, ya_ref, yb_ref, yc_ref, p_ref, wo_ref, g1_ref, w1_ref, w2_ref,
                 g2_ref, wg_ref, wp_ref, *emit):
    mixed = (_dot(ya_ref[...], wo_ref[0:A_WIDTH, :])
             + _dot(yb_ref[...], wo_ref[A_WIDTH:A_WIDTH + B_WIDTH, :])
             + _dot(yc_ref[...], wo_ref[A_WIDTH + B_WIDTH:, :]))
    yield
    h = h_ref[...] + mixed
    hn = _rms(h, g1_ref[...]).astype(BF16)
    ple = _dot(p_ref[...].astype(BF16), wp_ref[...])
    piece = D_FF // FF_SPLIT
    up = _dot(hn, w1_ref[:, 0:piece])
    yield
    ff = None
    for c in range(FF_SPLIT):
        a = jnp.maximum(up, 0.0)
        down = _dot((a * a).astype(BF16), w2_ref[c * piece:(c + 1) * piece, :])
        if c + 1 < FF_SPLIT:
            up = _dot(hn, w1_ref[:, (c + 1) * piece:(c + 2) * piece])
        yield
        ff = down if ff is None else ff + down
    h = h + ff
    gate = _dot(_rms(h, g2_ref[...]).astype(BF16), wg_ref[...])
    yield
    h = h + jax.nn.sigmoid(gate) * ple
    for fn in emit:
        fn(h)
    yield


N_POST_IN = 12


def _store_to(ref):
    def store(value):
        ref[...] = value
    return store


def _post_kernel(*refs):
    _run((_post_stages(*refs[:N_POST_IN], _store_to(refs[N_POST_IN])), POST_STAGES))


def _post_specs(layer, row_block):
    rb = lambda w: pl.BlockSpec((ROWS_PROJ, w), lambda i: (row_block(i), 0))
    return [
        rb(D_MODEL), rb(A_WIDTH), rb(B_WIDTH), rb(C_WIDTH),
        pl.BlockSpec((None, ROWS_PROJ, PLE_DIM), lambda i: (layer, row_block(i), 0)),
        _layer_spec((D_MODEL, D_MODEL), layer),
        _layer_spec((1, D_MODEL), layer),
        _layer_spec((D_MODEL, D_FF), layer),
        _layer_spec((D_FF, D_MODEL), layer),
        _layer_spec((1, D_MODEL), layer),
        _layer_spec((D_MODEL, D_MODEL), layer),
        _layer_spec((PLE_DIM, D_MODEL), layer),
    ]


def _post(layer, h, ya, yb, yc, p, post_params):
    n = h.shape[0]
    return pl.pallas_call(
        _post_kernel,
        grid=(n // ROWS_PROJ,),
        in_specs=_post_specs(layer, lambda i: i),
        out_specs=pl.BlockSpec((ROWS_PROJ, D_MODEL), lambda i: (i, 0)),
        out_shape=jax.ShapeDtypeStruct((n, D_MODEL), F32),
        compiler_params=pltpu.CompilerParams(
            dimension_semantics=("arbitrary",), vmem_limit_bytes=VMEM_LIMIT),
        name="post",
    )(h, ya, yb, yc, p, *post_params)


def _fused_kernel(layer, blocks_per_seq, n_blocks, sinks_ref, *refs):
    post_refs, refs = refs[:N_POST_IN], refs[N_POST_IN:]
    cos_ref, sin_ref, g_ref, w_ref = refs[:4]
    mixer_refs = refs[4:-8]
    h_out_ref, ya_ref, yb_ref, zc_ref = refs[-8:-4]
    hs_ref, z_ref, kprev_ref, vprev_ref = refs[-4:]
    i = pl.program_id(0)
    first_block = (i % blocks_per_seq) == 1 % blocks_per_seq

    @pl.when((i == 0) | first_block)
    def _():
        kprev_ref[...] = jnp.zeros_like(kprev_ref)
        vprev_ref[...] = jnp.zeros_like(vprev_ref)

    @pl.when(i == 0)
    def _():
        hs_ref[...] = jnp.zeros_like(hs_ref)

    _mix_project(hs_ref, g_ref, w_ref, z_ref)
    _run((_post_stages(*post_refs, _store_to(h_out_ref), _store_to(hs_ref)), POST_STAGES),
         (_mix_mixers(layer, first_block, sinks_ref, cos_ref, sin_ref, *mixer_refs,
                      ya_ref, yb_ref, zc_ref, z_ref, kprev_ref, vprev_ref), MIXER_STAGES))


def _fused(layer, h, ya, yb, yc, p, post_params, cos2, sin2, sinks, mix_params, seq_len):
    n = h.shape[0]
    n_blocks = n // ROWS_MIX
    merged = lambda i: jnp.minimum(i, n_blocks - 1)
    mixed = lambda i: jnp.maximum(i - 1, 0)
    rb = lambda w_: pl.BlockSpec((ROWS_MIX, w_), lambda i: (mixed(i), 0))
    return pl.pallas_call(
        functools.partial(_fused_kernel, layer, seq_len // ROWS_MIX, n_blocks),
        grid=(n_blocks + 1,),
        in_specs=[pl.BlockSpec(memory_space=pltpu.SMEM)]
        + _post_specs(layer - 1, merged)
        + [rb(LANES), rb(LANES)] + _mix_param_specs(layer),
        out_specs=[pl.BlockSpec((ROWS_PROJ, D_MODEL), lambda i: (merged(i), 0)),
                   rb(A_WIDTH), rb(B_WIDTH), rb(C_WIDTH)],
        out_shape=[jax.ShapeDtypeStruct((n, D_MODEL), F32)] + _mix_out_shapes(n),
        scratch_shapes=[pltpu.VMEM((ROWS_MIX, D_MODEL), F32)] + _MIX_SCRATCH,
        compiler_params=pltpu.CompilerParams(
            dimension_semantics=("arbitrary",), vmem_limit_bytes=VMEM_LIMIT),
        name="fused",
    )(sinks, h, ya, yb, yc, p, *post_params, cos2, sin2, *mix_params)


def kernel(x, p, positions, attn_norm_g, w_in, gmlp_ln_g, gmlp_ln_b, gmlp_ws, gmlp_bs, q_norm_g, k_norm_g, sinks, ssm_a_re, ssm_a_im, ssm_log_dt, ssm_b_re, ssm_b_im, ssm_c_re, ssm_c_im, ssm_d, glu_w1, glu_w2, mix_out_g, w_out, mlp_norm_g, w_ff1, w_ff2, ple_norm_g, w_ple_gate, w_ple_proj):
    bsz, seq_len, _ = x.shape
    depth = w_in.shape[0]
    n = bsz * seq_len

    inv = 1.0 / (ROPE_THETA ** (jnp.arange(0, HEAD_DIM, 2, dtype=F32) / HEAD_DIM))
    ang = positions.astype(F32).reshape(n, 1) * inv
    cos2 = jnp.tile(jnp.cos(ang), (1, 4))
    sin2 = jnp.tile(jnp.concatenate([-jnp.sin(ang), jnp.sin(ang)], axis=1), (1, 2))

    seg = np.arange(256) // HEAD_DIM
    pm = jnp.asarray((seg[:, None] == seg[None, :]) / HEAD_DIM, dtype=BF16)

    rows = lambda v: v.reshape(depth, 1, -1).astype(F32)
    g_attn, g_mlp, g_ple = rows(attn_norm_g), rows(mlp_norm_g), rows(ple_norm_g)
    ln_g, ln_b = rows(gmlp_ln_g), rows(gmlp_ln_b)
    mg_a = rows(mix_out_g[:, :A_WIDTH])
    mg_b = rows(mix_out_g[:, A_WIDTH:A_WIDTH + B_WIDTH])
    mg_c = rows(mix_out_g[:, A_WIDTH + B_WIDTH:])
    qg = rows(jnp.tile(q_norm_g, (1, B_Q_HEADS)))
    kg = rows(jnp.tile(k_norm_g, (1, B_KV_HEADS)))
    bs = jnp.repeat(jnp.swapaxes(gmlp_bs, 1, 2), HEAD_DIM, axis=2)
    w_a = w_in[:, :, :2 * A_WIDTH].reshape(depth, D_MODEL, A_HEADS, 2, HEAD_DIM)
    w_a = jnp.swapaxes(w_a, 2, 3).reshape(depth, D_MODEL, 2 * A_WIDTH)
    w_in_b = jnp.concatenate([w_a, w_in[:, :, 2 * A_WIDTH:]], axis=2).astype(BF16)
    causal = np.tril(np.ones((CHUNK, CHUNK), dtype=bool))
    ws_b = jnp.swapaxes(jnp.where(causal, gmlp_ws, 0.0), 1, 2).reshape(
        depth, CHUNK, A_HEADS * CHUNK).astype(BF16)
    w_out_b = w_out.astype(BF16)
    w_ff1_b, w_ff2_b = w_ff1.astype(BF16), w_ff2.astype(BF16)
    w_gate_b, w_proj_b = w_ple_gate.astype(BF16), w_ple_proj.astype(BF16)
    glu1_b, glu2_b = glu_w1.astype(BF16), glu_w2.astype(BF16)
    p2 = p.reshape(depth, n, PLE_DIM)

    kd, wb, wct, astep, atab = _tables(
        rows(ssm_a_re), rows(ssm_a_im), rows(jnp.repeat(ssm_log_dt, C_STATE, axis=1)),
        jnp.transpose(ssm_b_re, (0, 3, 1, 2)).reshape(depth, C_GROUP, N_STATE),
        jnp.transpose(ssm_b_im, (0, 3, 1, 2)).reshape(depth, C_GROUP, N_STATE),
        jnp.transpose(ssm_c_re, (0, 2, 1, 3)).reshape(depth, C_GROUP, N_STATE),
        jnp.transpose(ssm_c_im, (0, 2, 1, 3)).reshape(depth, C_GROUP, N_STATE),
        jnp.tile(ssm_d.reshape(depth, QUADS, 1, QUAD_WIDTH).astype(F32), (1, 1, 1, SSM_T)))

    mix_params = (g_attn, w_in_b, ws_b, bs, ln_g, ln_b, mg_a, qg, kg, mg_b, pm)
    post_params = (w_out_b, g_mlp, w_ff1_b, w_ff2_b, g_ple, w_gate_b, w_proj_b)
    h = x.reshape(n, D_MODEL)
    ya, yb, zc = _mix(0, h, cos2, sin2, sinks, mix_params, seq_len)
    for i in range(depth):
        yc = _ssm(i, zc, kd, wb, wct, astep, atab, glu1_b, glu2_b, mg_c, seq_len)
        if i + 1 < depth:
            h, ya, yb, zc = _fused(i + 1, h, ya, yb, yc, p2, post_params, cos2, sin2, sinks,
                                   mix_params, seq_len)
        else:
            h = _post(i, h, ya, yb, yc, p2, post_params)
    return h.reshape(bsz, seq_len, D_MODEL)
```

```python
import functools
import math

import numpy as np
import jax
import jax.numpy as jnp
from jax import lax
from jax.experimental import pallas as pl
from jax.experimental.pallas import tpu as pltpu

F32 = jnp.float32
BF16 = jnp.bfloat16

D_MODEL = 1024
HEAD_DIM = 64
A_WIDTH = 256
A_HEADS = 4
CHUNK = 128
B_WIDTH = 512
B_Q_HEADS = 8
B_KV_HEADS = 2
WINDOW = 128
ROPE_THETA = 10000.0
C_WIDTH = 256
C_GROUP = 16
C_GROUPS = 16
C_STATE = 64
N_STATE = C_GROUPS * C_STATE
IN_COLS = 1536
D_FF = 4096
PLE_DIM = 256
EPS = 1e-6
NEG_BIG = -1e30
LANES = 128

COL_A = 0
COL_Q = 512
COL_K = 1024
COL_V = 1152
COL_C = 1280

ROWS_PROJ = 512
ROWS_MIX = 512
ROWS_SSM = 2048
SSM_T = 8
SSM_N = ROWS_SSM // SSM_T
QUAD_GROUPS = 4
QUADS = C_GROUPS // QUAD_GROUPS
QUAD_WIDTH = QUAD_GROUPS * C_GROUP
QUAD_STATE = 2 * QUAD_GROUPS * C_STATE
PACKED = SSM_T * QUAD_WIDTH
SUBLANES = 8
SCAN_STEPS = int(math.log2(SUBLANES))
FF_SPLIT = 8
VMEM_LIMIT = 56 * 1024 * 1024


def _const_spec(shape):
    nd = len(shape)
    return pl.BlockSpec(shape, lambda *_: (0,) * nd, pipeline_mode=pl.Buffered(1))


def _layer_spec(shape, layer):
    nd = len(shape)
    return pl.BlockSpec((None,) + tuple(shape), lambda *_: (layer,) + (0,) * nd,
                        pipeline_mode=pl.Buffered(1))


def _rms(x, g):
    ms = jnp.mean(x * x, axis=-1, keepdims=True)
    return x * lax.rsqrt(ms + EPS) * g


def _dot(a, b):
    return jnp.dot(a, b, preferred_element_type=F32)


def _dot_nt(a, b):
    return lax.dot_general(a, b, (((1,), (1,)), ((), ())), preferred_element_type=F32)


def _split(x):
    hi = x.astype(BF16)
    return hi, (x - hi.astype(F32)).astype(BF16)


def _seg_mean(x, pmat):
    hi, lo = _split(x)
    return _dot(hi, pmat) + _dot(lo, pmat)


def _cmul(ar, ai, br, bi):
    return ar * br - ai * bi, ar * bi + ai * br


def _rope(x, cos, sin_signed):
    width = x.shape[-1]
    lane = lax.broadcasted_iota(jnp.int32, x.shape, 1)
    first_half = (lane % HEAD_DIM) < (HEAD_DIM // 2)
    partner = jnp.where(first_half,
                        pltpu.roll(x, width - HEAD_DIM // 2, 1),
                        pltpu.roll(x, HEAD_DIM // 2, 1))
    return x * cos + partner * sin_signed


def _dup_heads(x):
    low = lax.broadcasted_iota(jnp.int32, x.shape, 1) < HEAD_DIM
    sw = pltpu.roll(x, HEAD_DIM, 1)
    return [jnp.where(low, x, sw).astype(BF16), jnp.where(low, sw, x).astype(BF16)]


def _mix_project(h_ref, g_ref, w_refs, z_ref):
    wa_ref, wr_ref = w_refs
    xn = _rms(h_ref[...], g_ref[...]).astype(BF16)
    z_ref[:, COL_A:COL_Q] = _dot(xn, wa_ref[...])
    z_ref[:, COL_Q:] = _dot(xn, wr_ref[...])


def _mix_mixers(layer, first_block, sinks_ref, cos_ref, sin_ref, ws_ref, bs_ref, lng_ref, lnb_ref,
                mga_ref, qg_ref, kg_ref, mgb_ref, pm_ref, ya_ref, yb_ref, zc_ref,
                z_ref, kprev_ref, vprev_ref):
    nq = ROWS_MIX // WINDOW
    zc_ref[...] = z_ref[:, COL_C:COL_C + C_WIDTH]
    pm = pm_ref[...]
    pm_kv = pm[0:LANES, 0:LANES]

    k_raw = z_ref[:, COL_K:COL_K + LANES]
    k_ms = _dot((k_raw * k_raw).astype(BF16), pm_kv)
    yield
    k_cur = _rope(k_raw * lax.rsqrt(k_ms + EPS) * kg_ref[...], cos_ref[...], sin_ref[...])
    v_cur = z_ref[:, COL_V:COL_V + LANES]
    k_dup = _dup_heads(jnp.concatenate([kprev_ref[...], k_cur], axis=0))
    v_dup = _dup_heads(jnp.concatenate([vprev_ref[...], v_cur], axis=0))
    kprev_ref[...] = k_cur[ROWS_MIX - WINDOW:, :]
    vprev_ref[...] = v_cur[ROWS_MIX - WINDOW:, :]
    q = z_ref[:, COL_Q:COL_Q + B_WIDTH]
    q_ms = jnp.concatenate(
        [_dot((q[:, s:s + 256] * q[:, s:s + 256]).astype(BF16), pm) for s in (0, 256)], axis=1)
    yield
    cos_q = jnp.concatenate([cos_ref[...]] * 4, axis=1)
    sin_q = jnp.concatenate([sin_ref[...]] * 4, axis=1)
    qr = _rope(q * lax.rsqrt(q_ms + EPS) * qg_ref[...], cos_q, sin_q) * (HEAD_DIM ** -0.5)
    low_q = (lax.broadcasted_iota(jnp.int32, qr.shape, 1) % LANES) < HEAD_DIM
    q_low = jnp.where(low_q, qr, 0.0).astype(BF16)
    q_high = jnp.where(low_q, 0.0, qr).astype(BF16)

    u = jax.nn.gelu(z_ref[:, COL_A:COL_A + A_WIDTH])
    v = jax.nn.gelu(z_ref[:, COL_A + A_WIDTH:COL_A + 2 * A_WIDTH])
    v_mean = _dot(v.astype(BF16), pm)
    yield
    vc = v - v_mean
    v_var = _dot((vc * vc).astype(BF16), pm)
    yield
    vn = (vc * lax.rsqrt(v_var + EPS) * lng_ref[...] + lnb_ref[...]).astype(BF16)
    lane = lax.broadcasted_iota(jnp.int32, (CHUNK, A_WIDTH), 1)
    in_head = [(lane >= h * HEAD_DIM) & (lane < (h + 1) * HEAD_DIM) for h in range(A_HEADS)]
    svs = []
    for c in range(ROWS_MIX // CHUNK):
        vn_c = vn[c * CHUNK:(c + 1) * CHUNK]
        stacked = jnp.concatenate(
            [jnp.where(in_head[h], vn_c, jnp.zeros((), BF16)) for h in range(A_HEADS)], axis=0)
        svs.append(_dot(ws_ref[...], stacked))
    yield
    for c in range(ROWS_MIX // CHUNK):
        rows = slice(c * CHUNK, (c + 1) * CHUNK)
        ya_ref[rows, :] = _rms(u[rows] * (svs[c] + bs_ref[...]), mga_ref[...]).astype(BF16)

    qi = lax.broadcasted_iota(jnp.int32, (4 * WINDOW, WINDOW), 0) % WINDOW
    kc = lax.broadcasted_iota(jnp.int32, (4 * WINDOW, WINDOW), 1)
    from_cur = kc <= qi
    head_slot = lax.broadcasted_iota(jnp.int32, (4 * WINDOW, 1), 0) // WINDOW
    prev_ok = kc >= jnp.where(first_block, WINDOW, 0)
    low_o = lax.broadcasted_iota(jnp.int32, (WINDOW, LANES), 1) < HEAD_DIM
    zero = jnp.zeros((), F32)

    def scores(b, j):
        rows = slice(b * WINDOW, (b + 1) * WINDOW)
        pair_a = slice((2 * j) * LANES, (2 * j + 1) * LANES)
        pair_b = slice((2 * j + 1) * LANES, (2 * j + 2) * LANES)
        qs = jnp.concatenate([q_low[rows, pair_a], q_low[rows, pair_b],
                              q_high[rows, pair_a], q_high[rows, pair_b]], axis=0)
        return _dot_nt(qs, k_dup[j][b * WINDOW:(b + 2) * WINDOW])

    units = [(b, j) for b in range(nq) for j in range(B_KV_HEADS)]
    s_next = scores(*units[0])
    yield
    pairs = []
    for u, (b, j) in enumerate(units):
        s = s_next
        if u + 1 < len(units):
            s_next = scores(*units[u + 1])
        s_prev = s[:, 0:WINDOW]
        if b == 0:
            s_prev = jnp.where(prev_ok, s_prev, NEG_BIG)
        s = jnp.where(from_cur, s[:, WINDOW:], s_prev)
        sink = jnp.where(head_slot == 0, sinks_ref[layer, 4 * j],
                         jnp.where(head_slot == 1, sinks_ref[layer, 4 * j + 2],
                                   jnp.where(head_slot == 2, sinks_ref[layer, 4 * j + 1],
                                             sinks_ref[layer, 4 * j + 3])))
        m = jnp.maximum(jnp.max(s, axis=-1, keepdims=True), sink)
        pr = jnp.exp(s - m)
        denom = jnp.sum(pr, axis=-1, keepdims=True) + jnp.exp(sink - m)
        pn = pr * (1.0 / denom)
        p2 = jnp.concatenate([jnp.where(from_cur, zero, pn),
                              jnp.where(from_cur, pn, zero)], axis=1).astype(BF16)
        yield
        o = _dot(p2, v_dup[j][b * WINDOW:(b + 2) * WINDOW])
        yield
        pairs += [jnp.where(low_o, o[0:WINDOW], o[2 * WINDOW:3 * WINDOW]),
                  jnp.where(low_o, o[WINDOW:2 * WINDOW], o[3 * WINDOW:4 * WINDOW])]
        if j == B_KV_HEADS - 1:
            yb = jnp.concatenate(pairs, axis=1)
            yb_ref[b * WINDOW:(b + 1) * WINDOW, :] = _rms(yb, mgb_ref[...]).astype(BF16)
            pairs = []


MIXER_STAGES = 6 + 2 * B_KV_HEADS * (ROWS_MIX // WINDOW)
POST_STAGES = 4 + FF_SPLIT


def _run(*staged):
    total = max(n for _, n in staged)
    done = [0] * len(staged)
    for step in range(1, total + 1):
        for k, (gen, n) in enumerate(staged):
            while done[k] * total < step * n:
                next(gen, None)
                done[k] += 1
    for gen, _ in staged:
        for _ in gen:
            pass


def _mix_kernel(layer, blocks_per_seq, sinks_ref, h_ref, cos_ref, sin_ref, g_ref, wa_ref, wr_ref,
                *rest):
    w_ref = (wa_ref, wr_ref)
    mixer_refs, (z_ref, kprev_ref, vprev_ref) = rest[:-3], rest[-3:]
    first_block = (pl.program_id(0) % blocks_per_seq) == 0

    @pl.when(first_block)
    def _():
        kprev_ref[...] = jnp.zeros_like(kprev_ref)
        vprev_ref[...] = jnp.zeros_like(vprev_ref)

    _mix_project(h_ref, g_ref, w_ref, z_ref)
    _run((_mix_mixers(layer, first_block, sinks_ref, cos_ref, sin_ref, *mixer_refs,
                      z_ref, kprev_ref, vprev_ref), MIXER_STAGES))


def _mix_param_specs(layer):
    return [
        _layer_spec((1, D_MODEL), layer),
        _layer_spec((D_MODEL, COL_Q), layer),
        _layer_spec((D_MODEL, IN_COLS - COL_Q), layer),
        _layer_spec((CHUNK, A_HEADS * CHUNK), layer),
        _layer_spec((CHUNK, A_WIDTH), layer),
        _layer_spec((1, A_WIDTH), layer),
        _layer_spec((1, A_WIDTH), layer),
        _layer_spec((1, A_WIDTH), layer),
        _layer_spec((1, B_WIDTH), layer),
        _layer_spec((1, LANES), layer),
        _layer_spec((1, B_WIDTH), layer),
        _const_spec((256, 256)),
    ]


def _mix_out_shapes(n):
    return [jax.ShapeDtypeStruct((n, A_WIDTH), BF16),
            jax.ShapeDtypeStruct((n, B_WIDTH), BF16),
            jax.ShapeDtypeStruct((n, C_WIDTH), F32)]


_MIX_SCRATCH = [
    pltpu.VMEM((ROWS_MIX, IN_COLS), F32),
    pltpu.VMEM((WINDOW, LANES), F32),
    pltpu.VMEM((WINDOW, LANES), F32),
]


def _mix(layer, h, cos2, sin2, sinks, mix_params, seq_len):
    n = h.shape[0]
    rb = lambda w_: pl.BlockSpec((ROWS_MIX, w_), lambda i: (i, 0))
    return pl.pallas_call(
        functools.partial(_mix_kernel, layer, seq_len // ROWS_MIX),
        grid=(n // ROWS_MIX,),
        in_specs=[pl.BlockSpec(memory_space=pltpu.SMEM), rb(D_MODEL), rb(LANES), rb(LANES)]
        + _mix_param_specs(layer),
        out_specs=[rb(A_WIDTH), rb(B_WIDTH), rb(C_WIDTH)],
        out_shape=_mix_out_shapes(n),
        scratch_shapes=_MIX_SCRATCH,
        compiler_params=pltpu.CompilerParams(
            dimension_semantics=("arbitrary",), vmem_limit_bytes=VMEM_LIMIT),
        name="mix",
    )(sinks, h, cos2, sin2, *mix_params)


def _tables_kernel(are_ref, aim_ref, ldt_ref, bre_ref, bim_ref, cre_ref, cim_ref, d_ref,
                   kt_ref, wb_ref, wct_ref, astep_ref, atab_ref):
    a_re, a_im = are_ref[...], aim_ref[...]
    dt = jnp.exp(ldt_ref[...])
    mag = jnp.exp(a_re * dt)
    lr, li = mag * jnp.cos(a_im * dt), mag * jnp.sin(a_im * dt)
    den = a_re * a_re + a_im * a_im
    fr, fi = _cmul(lr - 1.0, li, a_re / den, -a_im / den)
    bbr, bbi = _cmul(fr, fi, bre_ref[...], bim_ref[...])
    c_re, c_im = cre_ref[...], cim_ref[...]

    row_g = lax.broadcasted_iota(jnp.int32, (QUAD_WIDTH, QUAD_STATE // 2), 0) // C_GROUP
    col_g = lax.broadcasted_iota(jnp.int32, (QUAD_WIDTH, QUAD_STATE // 2), 1) // C_STATE
    same_group = row_g == col_g

    def quad_block(mr, mi, q):
        cols = slice(q * QUAD_STATE // 2, (q + 1) * QUAD_STATE // 2)
        tile = lambda m: jnp.where(same_group, jnp.concatenate([m[:, cols]] * QUAD_GROUPS, axis=0), 0.0)
        return jnp.concatenate([tile(mr), tile(mi)], axis=1)

    pows = [(jnp.ones_like(lr), jnp.zeros_like(li))]
    for _ in range(SSM_T):
        pows.append(_cmul(pows[-1][0], pows[-1][1], lr, li))

    rr = lax.broadcasted_iota(jnp.int32, (PACKED, PACKED), 0)
    cc = lax.broadcasted_iota(jnp.int32, (PACKED, PACKED), 1)
    for q in range(QUADS):
        into_state, from_state, lag_out = [], [], []
        for s in range(SSM_T):
            wr, wi = _cmul(pows[SSM_T - 1 - s][0], pows[SSM_T - 1 - s][1], bbr, bbi)
            into_state.append(quad_block(wr, wi, q))
            mr, mi = _cmul(pows[s + 1][0], pows[s + 1][1], c_re, c_im)
            from_state.append(quad_block(mr, -mi, q))
            mr, mi = _cmul(pows[s][0], pows[s][1], c_re, c_im)
            lag_out.append(quad_block(mr, -mi, q))
        wb_ref[q] = jnp.concatenate(into_state, axis=0).astype(BF16)
        wct_ref[q] = jnp.concatenate(from_state, axis=0).astype(BF16)
        a_hi, a_lo = _split(quad_block(bbr, bbi, q))
        b_hi, b_lo = _split(jnp.concatenate(lag_out, axis=0))
        lags = _dot_nt(a_hi, b_hi) + _dot_nt(a_hi, b_lo) + _dot_nt(a_lo, b_hi)
        lane = lax.broadcasted_iota(jnp.int32, lags.shape, 1)
        k = jnp.concatenate(
            [lags] + [jnp.where(lane >= s * QUAD_WIDTH, pltpu.roll(lags, s * QUAD_WIDTH, 1), 0.0)
                      for s in range(1, SSM_T)], axis=0)
        kt_ref[q] = (k + jnp.where(rr == cc, d_ref[q], 0.0)).astype(BF16)

    half = QUAD_STATE // 2
    for q in range(QUADS):
        src = slice(q * half, (q + 1) * half)
        re_cols, im_cols = pl.ds(q * QUAD_STATE, half), pl.ds(q * QUAD_STATE + half, half)
        sr, si = pows[SSM_T][0][:, src], pows[SSM_T][1][:, src]
        atab_ref[0:1, re_cols] = sr
        atab_ref[0:1, im_cols] = si
        for k in range(SCAN_STEPS):
            span = 1 << k
            astep_ref[pl.ds(k, 1), re_cols] = sr
            astep_ref[pl.ds(k, 1), im_cols] = si
            tr, ti = _cmul(atab_ref[0:span, re_cols], atab_ref[0:span, im_cols], sr, si)
            atab_ref[pl.ds(span, span), re_cols] = tr
            atab_ref[pl.ds(span, span), im_cols] = ti
            sr, si = _cmul(sr, si, sr, si)


def _tables(a_re, a_im, ldt, b_re, b_im, c_re, c_im, d_skip):
    depth = a_re.shape[0]
    vec = lambda w: pl.BlockSpec((None, 1, w), lambda l: (l, 0, 0))
    mat = lambda r, c: pl.BlockSpec((None, r, c), lambda l: (l, 0, 0))
    cube = lambda r, c: pl.BlockSpec((None, QUADS, r, c), lambda l: (l, 0, 0, 0))
    packed = jax.ShapeDtypeStruct((depth, QUADS, PACKED, PACKED), BF16)
    return pl.pallas_call(
        _tables_kernel,
        grid=(depth,),
        in_specs=[vec(N_STATE), vec(N_STATE), vec(N_STATE),
                  mat(C_GROUP, N_STATE), mat(C_GROUP, N_STATE),
                  mat(C_GROUP, N_STATE), mat(C_GROUP, N_STATE), cube(1, PACKED)],
        out_specs=[cube(PACKED, PACKED), cube(PACKED, PACKED), cube(PACKED, PACKED),
                   mat(SCAN_STEPS, 2 * N_STATE), mat(SUBLANES, 2 * N_STATE)],
        out_shape=[packed, packed, packed,
                   jax.ShapeDtypeStruct((depth, SCAN_STEPS, 2 * N_STATE), F32),
                   jax.ShapeDtypeStruct((depth, SUBLANES, 2 * N_STATE), F32)],
        compiler_params=pltpu.CompilerParams(
            dimension_semantics=("arbitrary",), vmem_limit_bytes=VMEM_LIMIT),
        name="ssm_tables",
    )(a_re, a_im, ldt, b_re, b_im, c_re, c_im, d_skip)


def _glu_out(glu_a, glu_b, mg_ref, o_ref):
    o_ref[...] = _rms(glu_a * jax.nn.sigmoid(glu_b), mg_ref[...]).astype(BF16)


def _ssm_kernel(blocks_per_seq, n_blocks, *refs):
    w1_ref, w2_ref, mg_ref, o_ref, _, _, ya_ref, yb_ref, carry_ref = refs[-9:]
    i = pl.program_id(0)

    @pl.when(i == 0)
    def _():
        ya_ref[...] = jnp.zeros_like(ya_ref)
        yb_ref[...] = jnp.zeros_like(yb_ref)

    @pl.when((i % blocks_per_seq) == 0)
    def _():
        carry_ref[...] = jnp.zeros_like(carry_ref)

    @pl.when(i < n_blocks)
    def _():
        _ssm_block(*refs)

    @pl.when(i == n_blocks)
    def _():
        y_prev = jax.nn.gelu(jnp.concatenate([ya_ref[...], yb_ref[...]], axis=1)).astype(BF16)
        _glu_out(_dot(y_prev, w1_ref[...]), _dot(y_prev, w2_ref[...]), mg_ref, o_ref)


def _ssm_block(ua_ref, ub_ref, kt_ref, wb_ref, wct_ref, astep_ref, atab_ref,
               w1_ref, w2_ref, mg_ref, o_ref, inc_ref, sprev_ref, ya_ref, yb_ref, carry_ref):
    low = lax.broadcasted_iota(jnp.int32, (SSM_N, LANES), 1) < QUAD_WIDTH

    halves = [[ref[pl.ds(j, SSM_N, stride=SSM_T), :] for j in range(SSM_T)]
              for ref in (ua_ref, ub_ref)]

    def pack(q):
        src, upper = halves[q // 2], q % 2 == 1
        cols = []
        for m in range(SSM_T // 2):
            even, odd = src[2 * m], src[2 * m + 1]
            if upper:
                even = pltpu.roll(even, QUAD_WIDTH, 1)
            else:
                odd = pltpu.roll(odd, QUAD_WIDTH, 1)
            cols.append(jnp.where(low, even, odd))
        return jnp.concatenate(cols, axis=1).astype(BF16)

    xs = [pack(q) for q in range(QUADS)]

    for q in range(QUADS // 2):
        inc_ref[:, q * QUAD_STATE:(q + 1) * QUAD_STATE] = _dot(xs[q], wb_ref[q])
    y_prev = jax.nn.gelu(jnp.concatenate([ya_ref[...], yb_ref[...]], axis=1)).astype(BF16)
    glu_a, glu_b = _dot(y_prev, w1_ref[...]), _dot(y_prev, w2_ref[...])
    for q in range(QUADS // 2, QUADS):
        inc_ref[:, q * QUAD_STATE:(q + 1) * QUAD_STATE] = _dot(xs[q], wb_ref[q])
    _glu_out(glu_a, glu_b, mg_ref, o_ref)

    n_groups = SSM_N // SUBLANES
    sub = lax.broadcasted_iota(jnp.int32, (n_groups, SUBLANES, LANES), 1)
    sub2 = lax.broadcasted_iota(jnp.int32, (SUBLANES, LANES), 0)
    tiles_per_quad = QUAD_STATE // 2 // LANES
    y_lag = []
    for lt in range(N_STATE // LANES):
        q, part = lt // tiles_per_quad, lt % tiles_per_quad
        re_cols = pl.ds(q * QUAD_STATE + part * LANES, LANES)
        im_cols = pl.ds(q * QUAD_STATE + QUAD_STATE // 2 + part * LANES, LANES)
        xr = inc_ref[:, re_cols].reshape(n_groups, SUBLANES, LANES)
        xi = inc_ref[:, im_cols].reshape(n_groups, SUBLANES, LANES)
        for k in range(SCAN_STEPS):
            sh = 1 << k
            ar = astep_ref[pl.ds(k, 1), re_cols]
            ai = astep_ref[pl.ds(k, 1), im_cols]
            keep = sub >= sh
            sr = jnp.where(keep, pltpu.roll(xr, sh, 1), 0.0)
            si = jnp.where(keep, pltpu.roll(xi, sh, 1), 0.0)
            xr, xi = xr + ar * sr - ai * si, xi + ar * si + ai * sr
        tr = atab_ref[:, re_cols]
        ti = atab_ref[:, im_cols]
        er = jnp.broadcast_to(carry_ref[:, re_cols], (SUBLANES, LANES))
        ei = jnp.broadcast_to(carry_ref[:, im_cols], (SUBLANES, LANES))
        before_r, before_i = [], []
        for g in range(n_groups):
            fr = xr[g] + tr * er - ti * ei
            fi = xi[g] + tr * ei + ti * er
            before_r.append(jnp.where(sub2 == 0, er, pltpu.roll(fr, 1, 0)))
            before_i.append(jnp.where(sub2 == 0, ei, pltpu.roll(fi, 1, 0)))
            er = jnp.broadcast_to(fr[SUBLANES - 1:SUBLANES, :], (SUBLANES, LANES))
            ei = jnp.broadcast_to(fi[SUBLANES - 1:SUBLANES, :], (SUBLANES, LANES))
        sprev_ref[:, re_cols] = jnp.concatenate(before_r, axis=0).astype(BF16)
        sprev_ref[:, im_cols] = jnp.concatenate(before_i, axis=0).astype(BF16)
        carry_ref[:, re_cols] = er[0:1, :]
        carry_ref[:, im_cols] = ei[0:1, :]

        if part == tiles_per_quad - 1:
            y_lag.append(_dot(xs[q], kt_ref[q]))

    ys = [y_lag[q] + _dot_nt(sprev_ref[:, q * QUAD_STATE:(q + 1) * QUAD_STATE], wct_ref[q])
          for q in range(QUADS)]
    for t in range(SSM_T):
        cols = slice((t // 2) * LANES, (t // 2 + 1) * LANES)
        for y_ref, (qa, qb) in ((ya_ref, (0, 1)), (yb_ref, (2, 3))):
            lower, upper = ys[qa][:, cols], ys[qb][:, cols]
            if t % 2 == 1:
                lower = pltpu.roll(lower, QUAD_WIDTH, 1)
            else:
                upper = pltpu.roll(upper, QUAD_WIDTH, 1)
            y_ref[pl.ds(t, SSM_N, stride=SSM_T), :] = jnp.where(low, lower, upper)


def _ssm(layer, zc, kd, wb, wct, astep, atab, w1, w2, mg, seq_len):
    n = zc.shape[0]
    n_blocks = n // ROWS_SSM
    scanned = lambda i: jnp.minimum(i, n_blocks - 1)
    return pl.pallas_call(
        functools.partial(_ssm_kernel, seq_len // ROWS_SSM, n_blocks),
        grid=(n_blocks + 1,),
        in_specs=[
            pl.BlockSpec((ROWS_SSM, LANES), lambda i: (scanned(i), 0)),
            pl.BlockSpec((ROWS_SSM, LANES), lambda i: (scanned(i), 1)),
            _layer_spec((QUADS, PACKED, PACKED), layer),
            _layer_spec((QUADS, PACKED, QUAD_STATE), layer),
            _layer_spec((QUADS, PACKED, QUAD_STATE), layer),
            _layer_spec((SCAN_STEPS, 2 * N_STATE), layer),
            _layer_spec((SUBLANES, 2 * N_STATE), layer),
            _layer_spec((C_WIDTH, C_WIDTH), layer),
            _layer_spec((C_WIDTH, C_WIDTH), layer),
            _layer_spec((1, C_WIDTH), layer),
        ],
        out_specs=pl.BlockSpec((ROWS_SSM, C_WIDTH), lambda i: (jnp.maximum(i - 1, 0), 0)),
        out_shape=jax.ShapeDtypeStruct((n, C_WIDTH), BF16),
        scratch_shapes=[
            pltpu.VMEM((SSM_N, 2 * N_STATE), F32),
            pltpu.VMEM((SSM_N, 2 * N_STATE), BF16),
            pltpu.VMEM((ROWS_SSM, LANES), F32),
            pltpu.VMEM((ROWS_SSM, LANES), F32),
            pltpu.VMEM((1, 2 * N_STATE), F32),
        ],
        compiler_params=pltpu.CompilerParams(
            dimension_semantics=("arbitrary",), vmem_limit_bytes=VMEM_LIMIT),
        name="ssm",
    )(zc, zc, kd, wb, wct, astep, atab, w1, w2, mg)


def _post_stages(h_ref, ya_ref, yb_ref, yc_ref, p_ref, wo_ref, g1_ref, w1_ref, w2_ref,
                 g2_ref, wg_ref, wp_ref, *emit):
    mixed = (_dot(ya_ref[...], wo_ref[0:A_WIDTH, :])
             + _dot(yb_ref[...], wo_ref[A_WIDTH:A_WIDTH + B_WIDTH, :])
             + _dot(yc_ref[...], wo_ref[A_WIDTH + B_WIDTH:, :]))
    yield
    h = h_ref[...] + mixed
    hn = _rms(h, g1_ref[...]).astype(BF16)
    ple = _dot(p_ref[...].astype(BF16), wp_ref[...])
    piece = D_FF // FF_SPLIT
    up = _dot(hn, w1_ref[:, 0:piece])
    yield
    ff = None
    for c in range(FF_SPLIT):
        a = jnp.maximum(up, 0.0)
        down = _dot((a * a).astype(BF16), w2_ref[c * piece:(c + 1) * piece, :])
        if c + 1 < FF_SPLIT:
            up = _dot(hn, w1_ref[:, (c + 1) * piece:(c + 2) * piece])
        yield
        ff = down if ff is None else ff + down
    h = h + ff
    gate = _dot(_rms(h, g2_ref[...]).astype(BF16), wg_ref[...])
    yield
    h = h + jax.nn.sigmoid(gate) * ple
    for fn in emit:
        fn(h)
    yield


N_POST_IN = 12


def _store_to(ref):
    def store(value):
        ref[...] = value
    return store


def _post_kernel(*refs):
    _run((_post_stages(*refs[:N_POST_IN], _store_to(refs[N_POST_IN])), POST_STAGES))


def _post_specs(layer, row_block):
    rb = lambda w: pl.BlockSpec((ROWS_PROJ, w), lambda i: (row_block(i), 0))
    return [
        rb(D_MODEL), rb(A_WIDTH), rb(B_WIDTH), rb(C_WIDTH),
        pl.BlockSpec((None, ROWS_PROJ, PLE_DIM), lambda i: (layer, row_block(i), 0)),
        _layer_spec((D_MODEL, D_MODEL), layer),
        _layer_spec((1, D_MODEL), layer),
        _layer_spec((D_MODEL, D_FF), layer),
        _layer_spec((D_FF, D_MODEL), layer),
        _layer_spec((1, D_MODEL), layer),
        _layer_spec((D_MODEL, D_MODEL), layer),
        _layer_spec((PLE_DIM, D_MODEL), layer),
    ]


def _post(layer, h, ya, yb, yc, p, post_params):
    n = h.shape[0]
    return pl.pallas_call(
        _post_kernel,
        grid=(n // ROWS_PROJ,),
        in_specs=_post_specs(layer, lambda i: i),
        out_specs=pl.BlockSpec((ROWS_PROJ, D_MODEL), lambda i: (i, 0)),
        out_shape=jax.ShapeDtypeStruct((n, D_MODEL), F32),
        compiler_params=pltpu.CompilerParams(
            dimension_semantics=("arbitrary",), vmem_limit_bytes=VMEM_LIMIT),
        name="post",
    )(h, ya, yb, yc, p, *post_params)


def _fused_kernel(layer, blocks_per_seq, n_blocks, sinks_ref, *refs):
    post_refs, refs = refs[:N_POST_IN], refs[N_POST_IN:]
    cos_ref, sin_ref, g_ref = refs[:3]
    w_ref = refs[3:5]
    mixer_refs = refs[5:-8]
    h_out_ref, ya_ref, yb_ref, zc_ref = refs[-8:-4]
    hs_ref, z_ref, kprev_ref, vprev_ref = refs[-4:]
    i = pl.program_id(0)
    first_block = (i % blocks_per_seq) == 1 % blocks_per_seq

    @pl.when((i == 0) | first_block)
    def _():
        kprev_ref[...] = jnp.zeros_like(kprev_ref)
        vprev_ref[...] = jnp.zeros_like(vprev_ref)

    @pl.when(i == 0)
    def _():
        hs_ref[...] = jnp.zeros_like(hs_ref)

    def mixers():
        return (_mix_mixers(layer, first_block, sinks_ref, cos_ref, sin_ref, *mixer_refs,
                            ya_ref, yb_ref, zc_ref, z_ref, kprev_ref, vprev_ref), MIXER_STAGES)

    @pl.when(i < n_blocks)
    def _():
        _mix_project(hs_ref, g_ref, w_ref, z_ref)
        _run((_post_stages(*post_refs, _store_to(h_out_ref), _store_to(hs_ref)), POST_STAGES),
             mixers())

    @pl.when(i == n_blocks)
    def _():
        _mix_project(hs_ref, g_ref, w_ref, z_ref)
        _run(mixers())


def _fused(layer, h, ya, yb, yc, p, post_params, cos2, sin2, sinks, mix_params, seq_len):
    n = h.shape[0]
    n_blocks = n // ROWS_MIX
    merged = lambda i: jnp.minimum(i, n_blocks - 1)
    mixed = lambda i: jnp.maximum(i - 1, 0)
    rb = lambda w_: pl.BlockSpec((ROWS_MIX, w_), lambda i: (mixed(i), 0))
    return pl.pallas_call(
        functools.partial(_fused_kernel, layer, seq_len // ROWS_MIX, n_blocks),
        grid=(n_blocks + 1,),
        in_specs=[pl.BlockSpec(memory_space=pltpu.SMEM)]
        + _post_specs(layer - 1, merged)
        + [rb(LANES), rb(LANES)] + _mix_param_specs(layer),
        out_specs=[pl.BlockSpec((ROWS_PROJ, D_MODEL), lambda i: (merged(i), 0)),
                   rb(A_WIDTH), rb(B_WIDTH), rb(C_WIDTH)],
        out_shape=[jax.ShapeDtypeStruct((n, D_MODEL), F32)] + _mix_out_shapes(n),
        scratch_shapes=[pltpu.VMEM((ROWS_MIX, D_MODEL), F32)] + _MIX_SCRATCH,
        compiler_params=pltpu.CompilerParams(
            dimension_semantics=("arbitrary",), vmem_limit_bytes=VMEM_LIMIT),
        name="fused",
    )(sinks, h, ya, yb, yc, p, *post_params, cos2, sin2, *mix_params)


def kernel(x, p, positions, attn_norm_g, w_in, gmlp_ln_g, gmlp_ln_b, gmlp_ws, gmlp_bs, q_norm_g, k_norm_g, sinks, ssm_a_re, ssm_a_im, ssm_log_dt, ssm_b_re, ssm_b_im, ssm_c_re, ssm_c_im, ssm_d, glu_w1, glu_w2, mix_out_g, w_out, mlp_norm_g, w_ff1, w_ff2, ple_norm_g, w_ple_gate, w_ple_proj):
    bsz, seq_len, _ = x.shape
    depth = w_in.shape[0]
    n = bsz * seq_len

    inv = 1.0 / (ROPE_THETA ** (jnp.arange(0, HEAD_DIM, 2, dtype=F32) / HEAD_DIM))
    ang = positions.astype(F32).reshape(n, 1) * inv
    cos2 = jnp.tile(jnp.cos(ang), (1, 4))
    sin2 = jnp.tile(jnp.concatenate([-jnp.sin(ang), jnp.sin(ang)], axis=1), (1, 2))

    seg = np.arange(256) // HEAD_DIM
    pm = jnp.asarray((seg[:, None] == seg[None, :]) / HEAD_DIM, dtype=BF16)

    rows = lambda v: v.reshape(depth, 1, -1).astype(F32)
    g_attn, g_mlp, g_ple = rows(attn_norm_g), rows(mlp_norm_g), rows(ple_norm_g)
    ln_g, ln_b = rows(gmlp_ln_g), rows(gmlp_ln_b)
    mg_a = rows(mix_out_g[:, :A_WIDTH])
    mg_b = rows(mix_out_g[:, A_WIDTH:A_WIDTH + B_WIDTH])
    mg_c = rows(mix_out_g[:, A_WIDTH + B_WIDTH:])
    qg = rows(jnp.tile(q_norm_g, (1, B_Q_HEADS)))
    kg = rows(jnp.tile(k_norm_g, (1, B_KV_HEADS)))
    bs = jnp.repeat(jnp.swapaxes(gmlp_bs, 1, 2), HEAD_DIM, axis=2)
    w_a = w_in[:, :, :2 * A_WIDTH].reshape(depth, D_MODEL, A_HEADS, 2, HEAD_DIM)
    w_a = jnp.swapaxes(w_a, 2, 3).reshape(depth, D_MODEL, 2 * A_WIDTH)
    w_in_a, w_in_rest = w_a.astype(BF16), w_in[:, :, 2 * A_WIDTH:].astype(BF16)
    causal = np.tril(np.ones((CHUNK, CHUNK), dtype=bool))
    ws_b = jnp.swapaxes(jnp.where(causal, gmlp_ws, 0.0), 1, 2).reshape(
        depth, CHUNK, A_HEADS * CHUNK).astype(BF16)
    w_out_b = w_out.astype(BF16)
    w_ff1_b, w_ff2_b = w_ff1.astype(BF16), w_ff2.astype(BF16)
    w_gate_b, w_proj_b = w_ple_gate.astype(BF16), w_ple_proj.astype(BF16)
    glu1_b, glu2_b = glu_w1.astype(BF16), glu_w2.astype(BF16)
    p2 = p.reshape(depth, n, PLE_DIM)

    kd, wb, wct, astep, atab = _tables(
        rows(ssm_a_re), rows(ssm_a_im), rows(jnp.repeat(ssm_log_dt, C_STATE, axis=1)),
        jnp.transpose(ssm_b_re, (0, 3, 1, 2)).reshape(depth, C_GROUP, N_STATE),
        jnp.transpose(ssm_b_im, (0, 3, 1, 2)).reshape(depth, C_GROUP, N_STATE),
        jnp.transpose(ssm_c_re, (0, 2, 1, 3)).reshape(depth, C_GROUP, N_STATE),
        jnp.transpose(ssm_c_im, (0, 2, 1, 3)).reshape(depth, C_GROUP, N_STATE),
        jnp.tile(ssm_d.reshape(depth, QUADS, 1, QUAD_WIDTH).astype(F32), (1, 1, 1, SSM_T)))

    mix_params = (g_attn, w_in_a, w_in_rest, ws_b, bs, ln_g, ln_b, mg_a, qg, kg, mg_b, pm)
    post_params = (w_out_b, g_mlp, w_ff1_b, w_ff2_b, g_ple, w_gate_b, w_proj_b)
    h = x.reshape(n, D_MODEL)
    ya, yb, zc = _mix(0, h, cos2, sin2, sinks, mix_params, seq_len)
    for i in range(depth):
        yc = _ssm(i, zc, kd, wb, wct, astep, atab, glu1_b, glu2_b, mg_c, seq_len)
        if i + 1 < depth:
            h, ya, yb, zc = _fused(i + 1, h, ya, yb, yc, p2, post_params, cos2, sin2, sinks,
                                   mix_params, seq_len)
        else:
            h = _post(i, h, ya, yb, yc, p2, post_params)
    return h.reshape(bsz, seq_len, D_MODEL)
```

```python
import functools
import math

import numpy as np
import jax
import jax.numpy as jnp
from jax import lax
from jax.experimental import pallas as pl
from jax.experimental.pallas import tpu as pltpu

F32 = jnp.float32
BF16 = jnp.bfloat16

D_MODEL = 1024
HEAD_DIM = 64
A_WIDTH = 256
A_HEADS = 4
CHUNK = 128
B_WIDTH = 512
B_Q_HEADS = 8
B_KV_HEADS = 2
WINDOW = 128
ROPE_THETA = 10000.0
C_WIDTH = 256
C_GROUP = 16
C_GROUPS = 16
C_STATE = 64
N_STATE = C_GROUPS * C_STATE
IN_COLS = 1536
D_FF = 4096
PLE_DIM = 256
EPS = 1e-6
NEG_BIG = -1e30
LANES = 128

COL_A = 0
COL_Q = 512
COL_K = 1024
COL_V = 1152
COL_C = 1280

ROWS_PROJ = 512
ROWS_MIX = 512
ROWS_SSM = 2048
SSM_T = 8
SSM_N = ROWS_SSM // SSM_T
QUAD_GROUPS = 4
QUADS = C_GROUPS // QUAD_GROUPS
QUAD_WIDTH = QUAD_GROUPS * C_GROUP
QUAD_STATE = 2 * QUAD_GROUPS * C_STATE
PACKED = SSM_T * QUAD_WIDTH
SUBLANES = 8
SCAN_STEPS = int(math.log2(SUBLANES))
FF_SPLIT = 8
VMEM_LIMIT = 60 * 1024 * 1024


def _const_spec(shape):
    nd = len(shape)
    return pl.BlockSpec(shape, lambda *_: (0,) * nd, pipeline_mode=pl.Buffered(1))


def _layer_spec(shape, layer):
    nd = len(shape)
    return pl.BlockSpec((None,) + tuple(shape), lambda *_: (layer,) + (0,) * nd,
                        pipeline_mode=pl.Buffered(1))


def _rms(x, g):
    ms = jnp.mean(x * x, axis=-1, keepdims=True)
    return x * lax.rsqrt(ms + EPS) * g


def _dot(a, b):
    return jnp.dot(a, b, preferred_element_type=F32)


def _dot_nt(a, b):
    return lax.dot_general(a, b, (((1,), (1,)), ((), ())), preferred_element_type=F32)


def _split(x):
    hi = x.astype(BF16)
    return hi, (x - hi.astype(F32)).astype(BF16)


def _seg_mean(x, pmat):
    hi, lo = _split(x)
    return _dot(hi, pmat) + _dot(lo, pmat)


def _cmul(ar, ai, br, bi):
    return ar * br - ai * bi, ar * bi + ai * br


def _rope(x, cos, sin_signed):
    width = x.shape[-1]
    lane = lax.broadcasted_iota(jnp.int32, x.shape, 1)
    first_half = (lane % HEAD_DIM) < (HEAD_DIM // 2)
    partner = jnp.where(first_half,
                        pltpu.roll(x, width - HEAD_DIM // 2, 1),
                        pltpu.roll(x, HEAD_DIM // 2, 1))
    return x * cos + partner * sin_signed


def _dup_heads(x):
    low = lax.broadcasted_iota(jnp.int32, x.shape, 1) < HEAD_DIM
    sw = pltpu.roll(x, HEAD_DIM, 1)
    return [jnp.where(low, x, sw).astype(BF16), jnp.where(low, sw, x).astype(BF16)]


def _mix_project(h_ref, g_ref, w_refs, z_ref):
    wa_ref, wr_ref = w_refs
    xn = _rms(h_ref[...], g_ref[...]).astype(BF16)
    z_ref[:, COL_A:COL_Q] = _dot(xn, wa_ref[...])
    z_ref[:, COL_Q:] = _dot(xn, wr_ref[...])


def _mix_mixers(layer, first_block, sinks_ref, cos_ref, sin_ref, ws_ref, bs_ref, lng_ref, lnb_ref,
                mga_ref, qg_ref, kg_ref, mgb_ref, pm_ref, ya_ref, yb_ref, zc_ref,
                z_ref, kprev_ref, vprev_ref):
    nq = ROWS_MIX // WINDOW
    zc_ref[...] = z_ref[:, COL_C:COL_C + C_WIDTH]
    pm = pm_ref[...]
    pm_kv = pm[0:LANES, 0:LANES]

    k_raw = z_ref[:, COL_K:COL_K + LANES]
    k_ms = _dot((k_raw * k_raw).astype(BF16), pm_kv)
    yield
    k_cur = _rope(k_raw * lax.rsqrt(k_ms + EPS) * kg_ref[...], cos_ref[...], sin_ref[...])
    v_cur = z_ref[:, COL_V:COL_V + LANES]
    k_dup = _dup_heads(jnp.concatenate([kprev_ref[...], k_cur], axis=0))
    v_dup = _dup_heads(jnp.concatenate([vprev_ref[...], v_cur], axis=0))
    kprev_ref[...] = k_cur[ROWS_MIX - WINDOW:, :]
    vprev_ref[...] = v_cur[ROWS_MIX - WINDOW:, :]
    q = z_ref[:, COL_Q:COL_Q + B_WIDTH]
    q_ms = jnp.concatenate(
        [_dot((q[:, s:s + 256] * q[:, s:s + 256]).astype(BF16), pm) for s in (0, 256)], axis=1)
    yield
    cos_q = jnp.concatenate([cos_ref[...]] * 4, axis=1)
    sin_q = jnp.concatenate([sin_ref[...]] * 4, axis=1)
    qr = _rope(q * lax.rsqrt(q_ms + EPS) * qg_ref[...], cos_q, sin_q) * (HEAD_DIM ** -0.5)
    low_q = (lax.broadcasted_iota(jnp.int32, qr.shape, 1) % LANES) < HEAD_DIM
    q_low = jnp.where(low_q, qr, 0.0).astype(BF16)
    q_high = jnp.where(low_q, 0.0, qr).astype(BF16)

    u = jax.nn.gelu(z_ref[:, COL_A:COL_A + A_WIDTH])
    v = jax.nn.gelu(z_ref[:, COL_A + A_WIDTH:COL_A + 2 * A_WIDTH])
    v_mean = _dot(v.astype(BF16), pm)
    yield
    vc = v - v_mean
    v_var = _dot((vc * vc).astype(BF16), pm)
    yield
    vn = (vc * lax.rsqrt(v_var + EPS) * lng_ref[...] + lnb_ref[...]).astype(BF16)
    lane = lax.broadcasted_iota(jnp.int32, (CHUNK, A_WIDTH), 1)
    in_head = [(lane >= h * HEAD_DIM) & (lane < (h + 1) * HEAD_DIM) for h in range(A_HEADS)]
    svs = []
    for c in range(ROWS_MIX // CHUNK):
        vn_c = vn[c * CHUNK:(c + 1) * CHUNK]
        stacked = jnp.concatenate(
            [jnp.where(in_head[h], vn_c, jnp.zeros((), BF16)) for h in range(A_HEADS)], axis=0)
        svs.append(_dot(ws_ref[...], stacked))
    yield
    for c in range(ROWS_MIX // CHUNK):
        rows = slice(c * CHUNK, (c + 1) * CHUNK)
        ya_ref[rows, :] = _rms(u[rows] * (svs[c] + bs_ref[...]), mga_ref[...]).astype(BF16)

    qi = lax.broadcasted_iota(jnp.int32, (4 * WINDOW, WINDOW), 0) % WINDOW
    kc = lax.broadcasted_iota(jnp.int32, (4 * WINDOW, WINDOW), 1)
    from_cur = kc <= qi
    head_slot = lax.broadcasted_iota(jnp.int32, (4 * WINDOW, 1), 0) // WINDOW
    prev_ok = kc >= jnp.where(first_block, WINDOW, 0)
    low_o = lax.broadcasted_iota(jnp.int32, (WINDOW, LANES), 1) < HEAD_DIM
    zero = jnp.zeros((), F32)

    def scores(b, j):
        rows = slice(b * WINDOW, (b + 1) * WINDOW)
        pair_a = slice((2 * j) * LANES, (2 * j + 1) * LANES)
        pair_b = slice((2 * j + 1) * LANES, (2 * j + 2) * LANES)
        qs = jnp.concatenate([q_low[rows, pair_a], q_low[rows, pair_b],
                              q_high[rows, pair_a], q_high[rows, pair_b]], axis=0)
        return _dot_nt(qs, k_dup[j][b * WINDOW:(b + 2) * WINDOW])

    units = [(b, j) for b in range(nq) for j in range(B_KV_HEADS)]
    s_next = scores(*units[0])
    yield
    pairs = []
    for u, (b, j) in enumerate(units):
        s = s_next
        if u + 1 < len(units):
            s_next = scores(*units[u + 1])
        s_prev = s[:, 0:WINDOW]
        if b == 0:
            s_prev = jnp.where(prev_ok, s_prev, NEG_BIG)
        s = jnp.where(from_cur, s[:, WINDOW:], s_prev)
        sink = jnp.where(head_slot == 0, sinks_ref[layer, 4 * j],
                         jnp.where(head_slot == 1, sinks_ref[layer, 4 * j + 2],
                                   jnp.where(head_slot == 2, sinks_ref[layer, 4 * j + 1],
                                             sinks_ref[layer, 4 * j + 3])))
        m = jnp.maximum(jnp.max(s, axis=-1, keepdims=True), sink)
        pr = jnp.exp(s - m)
        denom = jnp.sum(pr, axis=-1, keepdims=True) + jnp.exp(sink - m)
        pn = pr * (1.0 / denom)
        p2 = jnp.concatenate([jnp.where(from_cur, zero, pn),
                              jnp.where(from_cur, pn, zero)], axis=1).astype(BF16)
        yield
        o = _dot(p2, v_dup[j][b * WINDOW:(b + 2) * WINDOW])
        yield
        pairs += [jnp.where(low_o, o[0:WINDOW], o[2 * WINDOW:3 * WINDOW]),
                  jnp.where(low_o, o[WINDOW:2 * WINDOW], o[3 * WINDOW:4 * WINDOW])]
        if j == B_KV_HEADS - 1:
            yb = jnp.concatenate(pairs, axis=1)
            yb_ref[b * WINDOW:(b + 1) * WINDOW, :] = _rms(yb, mgb_ref[...]).astype(BF16)
            pairs = []


MIXER_STAGES = 6 + 2 * B_KV_HEADS * (ROWS_MIX // WINDOW)
POST_STAGES = 4 + FF_SPLIT


def _run(*staged):
    total = max(n for _, n in staged)
    done = [0] * len(staged)
    for step in range(1, total + 1):
        for k, (gen, n) in enumerate(staged):
            while done[k] * total < step * n:
                next(gen, None)
                done[k] += 1
    for gen, _ in staged:
        for _ in gen:
            pass


def _cast_specs(items, n_steps):
    step = lambda i: jnp.minimum(i, n_steps - 1)
    in_specs, out_specs, out_shapes = [], [], []
    for arr, layer, col0 in items:
        _, rows, cols = arr.shape
        chunk = rows // n_steps
        in_specs.append(
            pl.BlockSpec((None, chunk, cols), lambda i, layer=layer: (layer, step(i), 0)))
        out_specs.append(pl.BlockSpec((chunk, cols - col0), lambda i: (step(i), 0)))
        out_shapes.append(jax.ShapeDtypeStruct((rows, cols - col0), BF16))
    return in_specs, out_specs, out_shapes


def _cast_chunks(src_refs, dst_refs):
    for src, dst in zip(src_refs, dst_refs):
        x = src[...]
        dst[...] = x[:, x.shape[1] - dst.shape[1]:].astype(BF16)


N_MIXER_IN = 9


def _mix_kernel(layer, blocks_per_seq, sinks_ref, h_ref, cos_ref, sin_ref, g_ref, wa_ref, wr_ref,
                *rest):
    w_ref = (wa_ref, wr_ref)
    mixer_in, rest = rest[:N_MIXER_IN], rest[N_MIXER_IN:]
    n_cast = (len(rest) - 6) // 2
    cast_src, outs, cast_dst = rest[:n_cast], rest[n_cast:n_cast + 3], rest[n_cast + 3:-3]
    mixer_refs = mixer_in + outs
    z_ref, kprev_ref, vprev_ref = rest[-3:]
    first_block = (pl.program_id(0) % blocks_per_seq) == 0

    @pl.when(first_block)
    def _():
        kprev_ref[...] = jnp.zeros_like(kprev_ref)
        vprev_ref[...] = jnp.zeros_like(vprev_ref)

    _cast_chunks(cast_src, cast_dst)
    _mix_project(h_ref, g_ref, w_ref, z_ref)
    _run((_mix_mixers(layer, first_block, sinks_ref, cos_ref, sin_ref, *mixer_refs,
                      z_ref, kprev_ref, vprev_ref), MIXER_STAGES))


def _mix_param_specs(layer):
    return [
        _layer_spec((1, D_MODEL), layer),
        _layer_spec((D_MODEL, COL_Q), layer),
        _const_spec((D_MODEL, IN_COLS - COL_Q)),
        _layer_spec((CHUNK, A_HEADS * CHUNK), layer),
        _layer_spec((CHUNK, A_WIDTH), layer),
        _layer_spec((1, A_WIDTH), layer),
        _layer_spec((1, A_WIDTH), layer),
        _layer_spec((1, A_WIDTH), layer),
        _layer_spec((1, B_WIDTH), layer),
        _layer_spec((1, LANES), layer),
        _layer_spec((1, B_WIDTH), layer),
        _const_spec((256, 256)),
    ]


def _mix_out_shapes(n):
    return [jax.ShapeDtypeStruct((n, A_WIDTH), BF16),
            jax.ShapeDtypeStruct((n, B_WIDTH), BF16),
            jax.ShapeDtypeStruct((n, C_WIDTH), F32)]


_MIX_SCRATCH = [
    pltpu.VMEM((ROWS_MIX, IN_COLS), F32),
    pltpu.VMEM((WINDOW, LANES), F32),
    pltpu.VMEM((WINDOW, LANES), F32),
]


def _mix(layer, h, cos2, sin2, sinks, mix_params, casts, seq_len):
    n = h.shape[0]
    n_blocks = n // ROWS_MIX
    rb = lambda w_: pl.BlockSpec((ROWS_MIX, w_), lambda i: (i, 0))
    cast_in, cast_out, cast_shapes = _cast_specs(casts, n_blocks)
    return pl.pallas_call(
        functools.partial(_mix_kernel, layer, seq_len // ROWS_MIX),
        grid=(n_blocks,),
        in_specs=[pl.BlockSpec(memory_space=pltpu.SMEM), rb(D_MODEL), rb(LANES), rb(LANES)]
        + _mix_param_specs(layer) + cast_in,
        out_specs=[rb(A_WIDTH), rb(B_WIDTH), rb(C_WIDTH)] + cast_out,
        out_shape=_mix_out_shapes(n) + cast_shapes,
        scratch_shapes=_MIX_SCRATCH,
        compiler_params=pltpu.CompilerParams(
            dimension_semantics=("arbitrary",), vmem_limit_bytes=VMEM_LIMIT),
        name="mix",
    )(sinks, h, cos2, sin2, *mix_params, *[c[0] for c in casts])


def _tables_kernel(are_ref, aim_ref, ldt_ref, bre_ref, bim_ref, cre_ref, cim_ref, d_ref,
                   kt_ref, wb_ref, wct_ref, astep_ref, atab_ref):
    a_re, a_im = are_ref[...], aim_ref[...]
    dt = jnp.exp(ldt_ref[...])
    mag = jnp.exp(a_re * dt)
    lr, li = mag * jnp.cos(a_im * dt), mag * jnp.sin(a_im * dt)
    den = a_re * a_re + a_im * a_im
    fr, fi = _cmul(lr - 1.0, li, a_re / den, -a_im / den)
    bbr, bbi = _cmul(fr, fi, bre_ref[...], bim_ref[...])
    c_re, c_im = cre_ref[...], cim_ref[...]

    row_g = lax.broadcasted_iota(jnp.int32, (QUAD_WIDTH, QUAD_STATE // 2), 0) // C_GROUP
    col_g = lax.broadcasted_iota(jnp.int32, (QUAD_WIDTH, QUAD_STATE // 2), 1) // C_STATE
    same_group = row_g == col_g

    def quad_block(mr, mi, q):
        cols = slice(q * QUAD_STATE // 2, (q + 1) * QUAD_STATE // 2)
        tile = lambda m: jnp.where(same_group, jnp.concatenate([m[:, cols]] * QUAD_GROUPS, axis=0), 0.0)
        return jnp.concatenate([tile(mr), tile(mi)], axis=1)

    pows = [(jnp.ones_like(lr), jnp.zeros_like(li))]
    for _ in range(SSM_T):
        pows.append(_cmul(pows[-1][0], pows[-1][1], lr, li))

    rr = lax.broadcasted_iota(jnp.int32, (PACKED, PACKED), 0)
    cc = lax.broadcasted_iota(jnp.int32, (PACKED, PACKED), 1)
    for q in range(QUADS):
        into_state, from_state, lag_out = [], [], []
        for s in range(SSM_T):
            wr, wi = _cmul(pows[SSM_T - 1 - s][0], pows[SSM_T - 1 - s][1], bbr, bbi)
            into_state.append(quad_block(wr, wi, q))
            mr, mi = _cmul(pows[s + 1][0], pows[s + 1][1], c_re, c_im)
            from_state.append(quad_block(mr, -mi, q))
            mr, mi = _cmul(pows[s][0], pows[s][1], c_re, c_im)
            lag_out.append(quad_block(mr, -mi, q))
        wb_ref[q] = jnp.concatenate(into_state, axis=0).astype(BF16)
        wct_ref[q] = jnp.concatenate(from_state, axis=0).astype(BF16)
        a_hi, a_lo = _split(quad_block(bbr, bbi, q))
        b_hi, b_lo = _split(jnp.concatenate(lag_out, axis=0))
        lags = _dot_nt(a_hi, b_hi) + _dot_nt(a_hi, b_lo) + _dot_nt(a_lo, b_hi)
        lane = lax.broadcasted_iota(jnp.int32, lags.shape, 1)
        k = jnp.concatenate(
            [lags] + [jnp.where(lane >= s * QUAD_WIDTH, pltpu.roll(lags, s * QUAD_WIDTH, 1), 0.0)
                      for s in range(1, SSM_T)], axis=0)
        kt_ref[q] = (k + jnp.where(rr == cc, d_ref[q], 0.0)).astype(BF16)

    half = QUAD_STATE // 2
    for q in range(QUADS):
        src = slice(q * half, (q + 1) * half)
        re_cols, im_cols = pl.ds(q * QUAD_STATE, half), pl.ds(q * QUAD_STATE + half, half)
        sr, si = pows[SSM_T][0][:, src], pows[SSM_T][1][:, src]
        atab_ref[0:1, re_cols] = sr
        atab_ref[0:1, im_cols] = si
        for k in range(SCAN_STEPS):
            span = 1 << k
            astep_ref[pl.ds(k, 1), re_cols] = sr
            astep_ref[pl.ds(k, 1), im_cols] = si
            tr, ti = _cmul(atab_ref[0:span, re_cols], atab_ref[0:span, im_cols], sr, si)
            atab_ref[pl.ds(span, span), re_cols] = tr
            atab_ref[pl.ds(span, span), im_cols] = ti
            sr, si = _cmul(sr, si, sr, si)


def _tables(a_re, a_im, ldt, b_re, b_im, c_re, c_im, d_skip):
    depth = a_re.shape[0]
    vec = lambda w: pl.BlockSpec((None, 1, w), lambda l: (l, 0, 0))
    mat = lambda r, c: pl.BlockSpec((None, r, c), lambda l: (l, 0, 0))
    cube = lambda r, c: pl.BlockSpec((None, QUADS, r, c), lambda l: (l, 0, 0, 0))
    packed = jax.ShapeDtypeStruct((depth, QUADS, PACKED, PACKED), BF16)
    return pl.pallas_call(
        _tables_kernel,
        grid=(depth,),
        in_specs=[vec(N_STATE), vec(N_STATE), vec(N_STATE),
                  mat(C_GROUP, N_STATE), mat(C_GROUP, N_STATE),
                  mat(C_GROUP, N_STATE), mat(C_GROUP, N_STATE), cube(1, PACKED)],
        out_specs=[cube(PACKED, PACKED), cube(PACKED, PACKED), cube(PACKED, PACKED),
                   mat(SCAN_STEPS, 2 * N_STATE), mat(SUBLANES, 2 * N_STATE)],
        out_shape=[packed, packed, packed,
                   jax.ShapeDtypeStruct((depth, SCAN_STEPS, 2 * N_STATE), F32),
                   jax.ShapeDtypeStruct((depth, SUBLANES, 2 * N_STATE), F32)],
        compiler_params=pltpu.CompilerParams(
            dimension_semantics=("arbitrary",), vmem_limit_bytes=VMEM_LIMIT),
        name="ssm_tables",
    )(a_re, a_im, ldt, b_re, b_im, c_re, c_im, d_skip)


def _glu_out(glu_a, glu_b, mg_ref, o_ref):
    o_ref[...] = _rms(glu_a * jax.nn.sigmoid(glu_b), mg_ref[...]).astype(BF16)


def _ssm_kernel(blocks_per_seq, n_blocks, *refs):
    w1_ref, w2_ref, mg_ref, o_ref, _, _, ya_ref, yb_ref, carry_ref = refs[-9:]
    i = pl.program_id(0)

    @pl.when(i == 0)
    def _():
        ya_ref[...] = jnp.zeros_like(ya_ref)
        yb_ref[...] = jnp.zeros_like(yb_ref)

    @pl.when((i % blocks_per_seq) == 0)
    def _():
        carry_ref[...] = jnp.zeros_like(carry_ref)

    @pl.when(i < n_blocks)
    def _():
        _ssm_block(*refs)

    @pl.when(i == n_blocks)
    def _():
        y_prev = jax.nn.gelu(jnp.concatenate([ya_ref[...], yb_ref[...]], axis=1)).astype(BF16)
        _glu_out(_dot(y_prev, w1_ref[...]), _dot(y_prev, w2_ref[...]), mg_ref, o_ref)


def _ssm_block(ua_ref, ub_ref, kt_ref, wb_ref, wct_ref, astep_ref, atab_ref,
               w1_ref, w2_ref, mg_ref, o_ref, inc_ref, sprev_ref, ya_ref, yb_ref, carry_ref):
    low = lax.broadcasted_iota(jnp.int32, (SSM_N, LANES), 1) < QUAD_WIDTH

    halves = [[ref[pl.ds(j, SSM_N, stride=SSM_T), :] for j in range(SSM_T)]
              for ref in (ua_ref, ub_ref)]

    def pack(q):
        src, upper = halves[q // 2], q % 2 == 1
        cols = []
        for m in range(SSM_T // 2):
            even, odd = src[2 * m], src[2 * m + 1]
            if upper:
                even = pltpu.roll(even, QUAD_WIDTH, 1)
            else:
                odd = pltpu.roll(odd, QUAD_WIDTH, 1)
            cols.append(jnp.where(low, even, odd))
        return jnp.concatenate(cols, axis=1).astype(BF16)

    xs = [pack(q) for q in range(QUADS)]

    for q in range(QUADS // 2):
        inc_ref[:, q * QUAD_STATE:(q + 1) * QUAD_STATE] = _dot(xs[q], wb_ref[q])
    y_prev = jax.nn.gelu(jnp.concatenate([ya_ref[...], yb_ref[...]], axis=1)).astype(BF16)
    glu_a, glu_b = _dot(y_prev, w1_ref[...]), _dot(y_prev, w2_ref[...])
    for q in range(QUADS // 2, QUADS):
        inc_ref[:, q * QUAD_STATE:(q + 1) * QUAD_STATE] = _dot(xs[q], wb_ref[q])
    _glu_out(glu_a, glu_b, mg_ref, o_ref)

    n_groups = SSM_N // SUBLANES
    sub = lax.broadcasted_iota(jnp.int32, (n_groups, SUBLANES, LANES), 1)
    sub2 = lax.broadcasted_iota(jnp.int32, (SUBLANES, LANES), 0)
    tiles_per_quad = QUAD_STATE // 2 // LANES
    y_lag = []
    for lt in range(N_STATE // LANES):
        q, part = lt // tiles_per_quad, lt % tiles_per_quad
        re_cols = pl.ds(q * QUAD_STATE + part * LANES, LANES)
        im_cols = pl.ds(q * QUAD_STATE + QUAD_STATE // 2 + part * LANES, LANES)
        xr = inc_ref[:, re_cols].reshape(n_groups, SUBLANES, LANES)
        xi = inc_ref[:, im_cols].reshape(n_groups, SUBLANES, LANES)
        for k in range(SCAN_STEPS):
            sh = 1 << k
            ar = astep_ref[pl.ds(k, 1), re_cols]
            ai = astep_ref[pl.ds(k, 1), im_cols]
            keep = sub >= sh
            sr = jnp.where(keep, pltpu.roll(xr, sh, 1), 0.0)
            si = jnp.where(keep, pltpu.roll(xi, sh, 1), 0.0)
            xr, xi = xr + ar * sr - ai * si, xi + ar * si + ai * sr
        tr = atab_ref[:, re_cols]
        ti = atab_ref[:, im_cols]
        er = jnp.broadcast_to(carry_ref[:, re_cols], (SUBLANES, LANES))
        ei = jnp.broadcast_to(carry_ref[:, im_cols], (SUBLANES, LANES))
        before_r, before_i = [], []
        for g in range(n_groups):
            fr = xr[g] + tr * er - ti * ei
            fi = xi[g] + tr * ei + ti * er
            before_r.append(jnp.where(sub2 == 0, er, pltpu.roll(fr, 1, 0)))
            before_i.append(jnp.where(sub2 == 0, ei, pltpu.roll(fi, 1, 0)))
            er = jnp.broadcast_to(fr[SUBLANES - 1:SUBLANES, :], (SUBLANES, LANES))
            ei = jnp.broadcast_to(fi[SUBLANES - 1:SUBLANES, :], (SUBLANES, LANES))
        sprev_ref[:, re_cols] = jnp.concatenate(before_r, axis=0).astype(BF16)
        sprev_ref[:, im_cols] = jnp.concatenate(before_i, axis=0).astype(BF16)
        carry_ref[:, re_cols] = er[0:1, :]
        carry_ref[:, im_cols] = ei[0:1, :]

        if part == tiles_per_quad - 1:
            y_lag.append(_dot(xs[q], kt_ref[q]))

    ys = [y_lag[q] + _dot_nt(sprev_ref[:, q * QUAD_STATE:(q + 1) * QUAD_STATE], wct_ref[q])
          for q in range(QUADS)]
    for t in range(SSM_T):
        cols = slice((t // 2) * LANES, (t // 2 + 1) * LANES)
        for y_ref, (qa, qb) in ((ya_ref, (0, 1)), (yb_ref, (2, 3))):
            lower, upper = ys[qa][:, cols], ys[qb][:, cols]
            if t % 2 == 1:
                lower = pltpu.roll(lower, QUAD_WIDTH, 1)
            else:
                upper = pltpu.roll(upper, QUAD_WIDTH, 1)
            y_ref[pl.ds(t, SSM_N, stride=SSM_T), :] = jnp.where(low, lower, upper)


def _ssm(layer, zc, kd, wb, wct, astep, atab, w1, w2, mg, seq_len):
    n = zc.shape[0]
    n_blocks = n // ROWS_SSM
    scanned = lambda i: jnp.minimum(i, n_blocks - 1)
    return pl.pallas_call(
        functools.partial(_ssm_kernel, seq_len // ROWS_SSM, n_blocks),
        grid=(n_blocks + 1,),
        in_specs=[
            pl.BlockSpec((ROWS_SSM, LANES), lambda i: (scanned(i), 0)),
            pl.BlockSpec((ROWS_SSM, LANES), lambda i: (scanned(i), 1)),
            _layer_spec((QUADS, PACKED, PACKED), layer),
            _layer_spec((QUADS, PACKED, QUAD_STATE), layer),
            _layer_spec((QUADS, PACKED, QUAD_STATE), layer),
            _layer_spec((SCAN_STEPS, 2 * N_STATE), layer),
            _layer_spec((SUBLANES, 2 * N_STATE), layer),
            _layer_spec((C_WIDTH, C_WIDTH), layer),
            _layer_spec((C_WIDTH, C_WIDTH), layer),
            _layer_spec((1, C_WIDTH), layer),
        ],
        out_specs=pl.BlockSpec((ROWS_SSM, C_WIDTH), lambda i: (jnp.maximum(i - 1, 0), 0)),
        out_shape=jax.ShapeDtypeStruct((n, C_WIDTH), BF16),
        scratch_shapes=[
            pltpu.VMEM((SSM_N, 2 * N_STATE), F32),
            pltpu.VMEM((SSM_N, 2 * N_STATE), BF16),
            pltpu.VMEM((ROWS_SSM, LANES), F32),
            pltpu.VMEM((ROWS_SSM, LANES), F32),
            pltpu.VMEM((1, 2 * N_STATE), F32),
        ],
        compiler_params=pltpu.CompilerParams(
            dimension_semantics=("arbitrary",), vmem_limit_bytes=VMEM_LIMIT),
        name="ssm",
    )(zc, zc, kd, wb, wct, astep, atab, w1, w2, mg)


def _post_stages(h_ref, ya_ref, yb_ref, yc_ref, p_ref, wo_ref, g1_ref, w1_ref, w2_ref,
                 g2_ref, wg_ref, wp_ref, *emit):
    mixed = (_dot(ya_ref[...], wo_ref[0:A_WIDTH, :])
             + _dot(yb_ref[...], wo_ref[A_WIDTH:A_WIDTH + B_WIDTH, :])
             + _dot(yc_ref[...], wo_ref[A_WIDTH + B_WIDTH:, :]))
    yield
    h = h_ref[...] + mixed
    hn = _rms(h, g1_ref[...]).astype(BF16)
    ple = _dot(p_ref[...].astype(BF16), wp_ref[...])
    piece = D_FF // FF_SPLIT
    up = _dot(hn, w1_ref[:, 0:piece])
    yield
    ff = None
    for c in range(FF_SPLIT):
        a = jnp.maximum(up, 0.0)
        down = _dot((a * a).astype(BF16), w2_ref[c * piece:(c + 1) * piece, :])
        if c + 1 < FF_SPLIT:
            up = _dot(hn, w1_ref[:, (c + 1) * piece:(c + 2) * piece])
        yield
        ff = down if ff is None else ff + down
    h = h + ff
    gate = _dot(_rms(h, g2_ref[...]).astype(BF16), wg_ref[...])
    yield
    h = h + jax.nn.sigmoid(gate) * ple
    for fn in emit:
        fn(h)
    yield


N_POST_IN = 12


def _store_to(ref):
    def store(value):
        ref[...] = value
    return store


def _post_kernel(*refs):
    _run((_post_stages(*refs[:N_POST_IN], _store_to(refs[N_POST_IN])), POST_STAGES))


def _post_specs(layer, row_block):
    rb = lambda w: pl.BlockSpec((ROWS_PROJ, w), lambda i: (row_block(i), 0))
    return [
        rb(D_MODEL), rb(A_WIDTH), rb(B_WIDTH), rb(C_WIDTH),
        pl.BlockSpec((None, ROWS_PROJ, PLE_DIM), lambda i: (layer, row_block(i), 0)),
        _const_spec((D_MODEL, D_MODEL)),
        _layer_spec((1, D_MODEL), layer),
        _const_spec((D_MODEL, D_FF)),
        _const_spec((D_FF, D_MODEL)),
        _layer_spec((1, D_MODEL), layer),
        _const_spec((D_MODEL, D_MODEL)),
        _layer_spec((PLE_DIM, D_MODEL), layer),
    ]


def _post(layer, h, ya, yb, yc, p, post_params):
    n = h.shape[0]
    return pl.pallas_call(
        _post_kernel,
        grid=(n // ROWS_PROJ,),
        in_specs=_post_specs(layer, lambda i: i),
        out_specs=pl.BlockSpec((ROWS_PROJ, D_MODEL), lambda i: (i, 0)),
        out_shape=jax.ShapeDtypeStruct((n, D_MODEL), F32),
        compiler_params=pltpu.CompilerParams(
            dimension_semantics=("arbitrary",), vmem_limit_bytes=VMEM_LIMIT),
        name="post",
    )(h, ya, yb, yc, p, *post_params)


def _fused_kernel(layer, blocks_per_seq, n_blocks, sinks_ref, *refs):
    post_refs, refs = refs[:N_POST_IN], refs[N_POST_IN:]
    cos_ref, sin_ref, g_ref = refs[:3]
    w_ref = refs[3:5]
    mixer_refs, refs = refs[5:5 + N_MIXER_IN], refs[5 + N_MIXER_IN:]
    n_cast = (len(refs) - 8) // 2
    cast_src, cast_dst = refs[:n_cast], refs[n_cast + 4:-4]
    h_out_ref, ya_ref, yb_ref, zc_ref = refs[n_cast:n_cast + 4]
    hs_ref, z_ref, kprev_ref, vprev_ref = refs[-4:]
    i = pl.program_id(0)
    first_block = (i % blocks_per_seq) == 1 % blocks_per_seq

    @pl.when((i == 0) | first_block)
    def _():
        kprev_ref[...] = jnp.zeros_like(kprev_ref)
        vprev_ref[...] = jnp.zeros_like(vprev_ref)

    @pl.when(i == 0)
    def _():
        hs_ref[...] = jnp.zeros_like(hs_ref)

    def mixers():
        return (_mix_mixers(layer, first_block, sinks_ref, cos_ref, sin_ref, *mixer_refs,
                            ya_ref, yb_ref, zc_ref, z_ref, kprev_ref, vprev_ref), MIXER_STAGES)

    @pl.when(i < n_blocks)
    def _():
        _cast_chunks(cast_src, cast_dst)
        _mix_project(hs_ref, g_ref, w_ref, z_ref)
        _run((_post_stages(*post_refs, _store_to(h_out_ref), _store_to(hs_ref)), POST_STAGES),
             mixers())

    @pl.when(i == n_blocks)
    def _():
        _mix_project(hs_ref, g_ref, w_ref, z_ref)
        _run(mixers())


def _fused(layer, h, ya, yb, yc, p, post_params, cos2, sin2, sinks, mix_params, casts, seq_len):
    n = h.shape[0]
    n_blocks = n // ROWS_MIX
    merged = lambda i: jnp.minimum(i, n_blocks - 1)
    mixed = lambda i: jnp.maximum(i - 1, 0)
    rb = lambda w_: pl.BlockSpec((ROWS_MIX, w_), lambda i: (mixed(i), 0))
    cast_in, cast_out, cast_shapes = _cast_specs(casts, n_blocks)
    return pl.pallas_call(
        functools.partial(_fused_kernel, layer, seq_len // ROWS_MIX, n_blocks),
        grid=(n_blocks + 1,),
        in_specs=[pl.BlockSpec(memory_space=pltpu.SMEM)]
        + _post_specs(layer - 1, merged)
        + [rb(LANES), rb(LANES)] + _mix_param_specs(layer) + cast_in,
        out_specs=[pl.BlockSpec((ROWS_PROJ, D_MODEL), lambda i: (merged(i), 0)),
                   rb(A_WIDTH), rb(B_WIDTH), rb(C_WIDTH)] + cast_out,
        out_shape=[jax.ShapeDtypeStruct((n, D_MODEL), F32)] + _mix_out_shapes(n) + cast_shapes,
        scratch_shapes=[pltpu.VMEM((ROWS_MIX, D_MODEL), F32)] + _MIX_SCRATCH,
        compiler_params=pltpu.CompilerParams(
            dimension_semantics=("arbitrary",), vmem_limit_bytes=VMEM_LIMIT),
        name="fused",
    )(sinks, h, ya, yb, yc, p, *post_params, cos2, sin2, *mix_params, *[c[0] for c in casts])


def kernel(x, p, positions, attn_norm_g, w_in, gmlp_ln_g, gmlp_ln_b, gmlp_ws, gmlp_bs, q_norm_g, k_norm_g, sinks, ssm_a_re, ssm_a_im, ssm_log_dt, ssm_b_re, ssm_b_im, ssm_c_re, ssm_c_im, ssm_d, glu_w1, glu_w2, mix_out_g, w_out, mlp_norm_g, w_ff1, w_ff2, ple_norm_g, w_ple_gate, w_ple_proj):
    bsz, seq_len, _ = x.shape
    depth = w_in.shape[0]
    n = bsz * seq_len

    inv = 1.0 / (ROPE_THETA ** (jnp.arange(0, HEAD_DIM, 2, dtype=F32) / HEAD_DIM))
    ang = positions.astype(F32).reshape(n, 1) * inv
    cos2 = jnp.tile(jnp.cos(ang), (1, 4))
    sin2 = jnp.tile(jnp.concatenate([-jnp.sin(ang), jnp.sin(ang)], axis=1), (1, 2))

    seg = np.arange(256) // HEAD_DIM
    pm = jnp.asarray((seg[:, None] == seg[None, :]) / HEAD_DIM, dtype=BF16)

    rows = lambda v: v.reshape(depth, 1, -1).astype(F32)
    g_attn, g_mlp, g_ple = rows(attn_norm_g), rows(mlp_norm_g), rows(ple_norm_g)
    ln_g, ln_b = rows(gmlp_ln_g), rows(gmlp_ln_b)
    mg_a = rows(mix_out_g[:, :A_WIDTH])
    mg_b = rows(mix_out_g[:, A_WIDTH:A_WIDTH + B_WIDTH])
    mg_c = rows(mix_out_g[:, A_WIDTH + B_WIDTH:])
    qg = rows(jnp.tile(q_norm_g, (1, B_Q_HEADS)))
    kg = rows(jnp.tile(k_norm_g, (1, B_KV_HEADS)))
    bs = jnp.repeat(jnp.swapaxes(gmlp_bs, 1, 2), HEAD_DIM, axis=2)
    w_a = w_in[:, :, :2 * A_WIDTH].reshape(depth, D_MODEL, A_HEADS, 2, HEAD_DIM)
    w_a = jnp.swapaxes(w_a, 2, 3).reshape(depth, D_MODEL, 2 * A_WIDTH)
    w_in_a, w_in_rest0 = w_a.astype(BF16), w_in[0, :, 2 * A_WIDTH:].astype(BF16)
    causal = np.tril(np.ones((CHUNK, CHUNK), dtype=bool))
    ws_b = jnp.swapaxes(jnp.where(causal, gmlp_ws, 0.0), 1, 2).reshape(
        depth, CHUNK, A_HEADS * CHUNK).astype(BF16)
    w_proj_b = w_ple_proj.astype(BF16)
    glu1_b, glu2_b = glu_w1.astype(BF16), glu_w2.astype(BF16)
    p2 = p.reshape(depth, n, PLE_DIM)

    kd, wb, wct, astep, atab = _tables(
        rows(ssm_a_re), rows(ssm_a_im), rows(jnp.repeat(ssm_log_dt, C_STATE, axis=1)),
        jnp.transpose(ssm_b_re, (0, 3, 1, 2)).reshape(depth, C_GROUP, N_STATE),
        jnp.transpose(ssm_b_im, (0, 3, 1, 2)).reshape(depth, C_GROUP, N_STATE),
        jnp.transpose(ssm_c_re, (0, 2, 1, 3)).reshape(depth, C_GROUP, N_STATE),
        jnp.transpose(ssm_c_im, (0, 2, 1, 3)).reshape(depth, C_GROUP, N_STATE),
        jnp.tile(ssm_d.reshape(depth, QUADS, 1, QUAD_WIDTH).astype(F32), (1, 1, 1, SSM_T)))

    def casts_for(layer):
        items = [(w_out, layer, 0), (w_ff1, layer, 0), (w_ff2, layer, 0), (w_ple_gate, layer, 0)]
        if layer + 1 < depth:
            items.append((w_in, layer + 1, 2 * A_WIDTH))
        return items

    def mix_params(w_in_rest):
        return (g_attn, w_in_a, w_in_rest, ws_b, bs, ln_g, ln_b, mg_a, qg, kg, mg_b, pm)

    def post_params(cast):
        w_out_b, w_ff1_b, w_ff2_b, w_gate_b = cast[:4]
        return (w_out_b, g_mlp, w_ff1_b, w_ff2_b, g_ple, w_gate_b, w_proj_b)

    h = x.reshape(n, D_MODEL)
    ya, yb, zc, *cast = _mix(0, h, cos2, sin2, sinks, mix_params(w_in_rest0), casts_for(0),
                             seq_len)
    for i in range(depth):
        yc = _ssm(i, zc, kd, wb, wct, astep, atab, glu1_b, glu2_b, mg_c, seq_len)
        if i + 1 < depth:
            h, ya, yb, zc, *cast = _fused(i + 1, h, ya, yb, yc, p2, post_params(cast), cos2, sin2,
                                          sinks, mix_params(cast[4]), casts_for(i + 1), seq_len)
        else:
            h = _post(i, h, ya, yb, yc, p2, post_params(cast))
    return h.reshape(bsz, seq_len, D_MODEL)
```

```python
import functools
import math

import numpy as np
import jax
import jax.numpy as jnp
from jax import lax
from jax.experimental import pallas as pl
from jax.experimental.pallas import tpu as pltpu

F32 = jnp.float32
BF16 = jnp.bfloat16

D_MODEL = 1024
HEAD_DIM = 64
A_WIDTH = 256
A_HEADS = 4
CHUNK = 128
B_WIDTH = 512
B_Q_HEADS = 8
B_KV_HEADS = 2
WINDOW = 128
ROPE_THETA = 10000.0
C_WIDTH = 256
C_GROUP = 16
C_GROUPS = 16
C_STATE = 64
N_STATE = C_GROUPS * C_STATE
IN_COLS = 1536
D_FF = 4096
PLE_DIM = 256
EPS = 1e-6
NEG_BIG = -1e30
LANES = 128

COL_A = 0
COL_Q = 512
COL_K = 1024
COL_V = 1152
COL_C = 1280

ROWS_PROJ = 1024
ROWS_MIX = 512
ROWS_SSM = 2048
SSM_T = 8
SSM_N = ROWS_SSM // SSM_T
QUAD_GROUPS = 4
QUADS = C_GROUPS // QUAD_GROUPS
QUAD_WIDTH = QUAD_GROUPS * C_GROUP
QUAD_STATE = 2 * QUAD_GROUPS * C_STATE
PACKED = SSM_T * QUAD_WIDTH
SUBLANES = 8
SCAN_STEPS = int(math.log2(SUBLANES))
FF_SPLIT = 8
VMEM_LIMIT = 60 * 1024 * 1024


def _const_spec(shape):
    nd = len(shape)
    return pl.BlockSpec(shape, lambda *_: (0,) * nd, pipeline_mode=pl.Buffered(1))


def _layer_spec(shape, layer):
    nd = len(shape)
    return pl.BlockSpec((None,) + tuple(shape), lambda *_: (layer,) + (0,) * nd,
                        pipeline_mode=pl.Buffered(1))


def _rms(x, g):
    ms = jnp.mean(x * x, axis=-1, keepdims=True)
    return x * lax.rsqrt(ms + EPS) * g


def _dot(a, b):
    return jnp.dot(a, b, preferred_element_type=F32)


def _dot_nt(a, b):
    return lax.dot_general(a, b, (((1,), (1,)), ((), ())), preferred_element_type=F32)


def _split(x):
    hi = x.astype(BF16)
    return hi, (x - hi.astype(F32)).astype(BF16)


def _seg_mean(x, pmat):
    hi, lo = _split(x)
    return _dot(hi, pmat) + _dot(lo, pmat)


def _cmul(ar, ai, br, bi):
    return ar * br - ai * bi, ar * bi + ai * br


def _rope(x, cos, sin_signed):
    width = x.shape[-1]
    lane = lax.broadcasted_iota(jnp.int32, x.shape, 1)
    first_half = (lane % HEAD_DIM) < (HEAD_DIM // 2)
    partner = jnp.where(first_half,
                        pltpu.roll(x, width - HEAD_DIM // 2, 1),
                        pltpu.roll(x, HEAD_DIM // 2, 1))
    return x * cos + partner * sin_signed


def _dup_heads(x):
    low = lax.broadcasted_iota(jnp.int32, x.shape, 1) < HEAD_DIM
    sw = pltpu.roll(x, HEAD_DIM, 1)
    return [jnp.where(low, x, sw).astype(BF16), jnp.where(low, sw, x).astype(BF16)]


def _mix_project(h_ref, g_ref, w_refs, z_ref):
    wa_ref, wr_ref = w_refs
    xn = _rms(h_ref[...], g_ref[...]).astype(BF16)
    z_ref[:, COL_A:COL_Q] = _dot(xn, wa_ref[...])
    z_ref[:, COL_Q:] = _dot(xn, wr_ref[...])


def _mix_mixers(layer, first_block, sinks_ref, cos_ref, sin_ref, ws_ref, bs_ref, lng_ref, lnb_ref,
                mga_ref, qg_ref, kg_ref, mgb_ref, pm_ref, ya_ref, yb_ref, zc_ref,
                z_ref, kprev_ref, vprev_ref):
    nq = ROWS_MIX // WINDOW
    zc_ref[...] = z_ref[:, COL_C:COL_C + C_WIDTH]
    pm = pm_ref[...]
    pm_kv = pm[0:LANES, 0:LANES]

    k_raw = z_ref[:, COL_K:COL_K + LANES]
    k_ms = _dot((k_raw * k_raw).astype(BF16), pm_kv)
    yield
    half = HEAD_DIM // 2
    cos2 = jnp.concatenate([cos_ref[...]] * (LANES // half), axis=1)
    sin2 = jnp.concatenate([sin_ref[...]] * (LANES // half), axis=1)
    lane2 = lax.broadcasted_iota(jnp.int32, sin2.shape, 1)
    sin2 = jnp.where((lane2 % HEAD_DIM) < half, -sin2, sin2)
    k_cur = _rope(k_raw * lax.rsqrt(k_ms + EPS) * kg_ref[...], cos2, sin2)
    v_cur = z_ref[:, COL_V:COL_V + LANES]
    k_dup = _dup_heads(jnp.concatenate([kprev_ref[...], k_cur], axis=0))
    v_dup = _dup_heads(jnp.concatenate([vprev_ref[...], v_cur], axis=0))
    kprev_ref[...] = k_cur[ROWS_MIX - WINDOW:, :]
    vprev_ref[...] = v_cur[ROWS_MIX - WINDOW:, :]
    q = z_ref[:, COL_Q:COL_Q + B_WIDTH]
    q_ms = jnp.concatenate(
        [_dot((q[:, s:s + 256] * q[:, s:s + 256]).astype(BF16), pm) for s in (0, 256)], axis=1)
    yield
    cos_q = jnp.concatenate([cos2] * 4, axis=1)
    sin_q = jnp.concatenate([sin2] * 4, axis=1)
    qr = _rope(q * lax.rsqrt(q_ms + EPS) * qg_ref[...], cos_q, sin_q) * (HEAD_DIM ** -0.5)
    low_q = (lax.broadcasted_iota(jnp.int32, qr.shape, 1) % LANES) < HEAD_DIM
    q_low = jnp.where(low_q, qr, 0.0).astype(BF16)
    q_high = jnp.where(low_q, 0.0, qr).astype(BF16)

    u = jax.nn.gelu(z_ref[:, COL_A:COL_A + A_WIDTH])
    v = jax.nn.gelu(z_ref[:, COL_A + A_WIDTH:COL_A + 2 * A_WIDTH])
    v_mean = _dot(v.astype(BF16), pm)
    yield
    vc = v - v_mean
    v_var = _dot((vc * vc).astype(BF16), pm)
    yield
    vn = (vc * lax.rsqrt(v_var + EPS) * lng_ref[...] + lnb_ref[...]).astype(BF16)
    lane = lax.broadcasted_iota(jnp.int32, (CHUNK, A_WIDTH), 1)
    in_head = [(lane >= h * HEAD_DIM) & (lane < (h + 1) * HEAD_DIM) for h in range(A_HEADS)]
    svs = []
    for c in range(ROWS_MIX // CHUNK):
        vn_c = vn[c * CHUNK:(c + 1) * CHUNK]
        stacked = jnp.concatenate(
            [jnp.where(in_head[h], vn_c, jnp.zeros((), BF16)) for h in range(A_HEADS)], axis=0)
        svs.append(_dot(ws_ref[...], stacked))
    yield
    for c in range(ROWS_MIX // CHUNK):
        rows = slice(c * CHUNK, (c + 1) * CHUNK)
        ya_ref[rows, :] = _rms(u[rows] * (svs[c] + bs_ref[...]), mga_ref[...]).astype(BF16)

    qi = lax.broadcasted_iota(jnp.int32, (4 * WINDOW, WINDOW), 0) % WINDOW
    kc = lax.broadcasted_iota(jnp.int32, (4 * WINDOW, WINDOW), 1)
    from_cur = kc <= qi
    head_slot = lax.broadcasted_iota(jnp.int32, (4 * WINDOW, 1), 0) // WINDOW
    prev_ok = kc >= jnp.where(first_block, WINDOW, 0)
    low_o = lax.broadcasted_iota(jnp.int32, (WINDOW, LANES), 1) < HEAD_DIM
    zero = jnp.zeros((), F32)

    def scores(b, j):
        rows = slice(b * WINDOW, (b + 1) * WINDOW)
        pair_a = slice((2 * j) * LANES, (2 * j + 1) * LANES)
        pair_b = slice((2 * j + 1) * LANES, (2 * j + 2) * LANES)
        qs = jnp.concatenate([q_low[rows, pair_a], q_low[rows, pair_b],
                              q_high[rows, pair_a], q_high[rows, pair_b]], axis=0)
        return _dot_nt(qs, k_dup[j][b * WINDOW:(b + 2) * WINDOW])

    units = [(b, j) for b in range(nq) for j in range(B_KV_HEADS)]
    s_next = scores(*units[0])
    yield
    pairs = []
    for u, (b, j) in enumerate(units):
        s = s_next
        if u + 1 < len(units):
            s_next = scores(*units[u + 1])
        s_prev = s[:, 0:WINDOW]
        if b == 0:
            s_prev = jnp.where(prev_ok, s_prev, NEG_BIG)
        s = jnp.where(from_cur, s[:, WINDOW:], s_prev)
        sink = jnp.where(head_slot == 0, sinks_ref[layer, 4 * j],
                         jnp.where(head_slot == 1, sinks_ref[layer, 4 * j + 2],
                                   jnp.where(head_slot == 2, sinks_ref[layer, 4 * j + 1],
                                             sinks_ref[layer, 4 * j + 3])))
        m = jnp.maximum(jnp.max(s, axis=-1, keepdims=True), sink)
        pr = jnp.exp(s - m)
        denom = jnp.sum(pr, axis=-1, keepdims=True) + jnp.exp(sink - m)
        pn = pr * (1.0 / denom)
        p2 = jnp.concatenate([jnp.where(from_cur, zero, pn),
                              jnp.where(from_cur, pn, zero)], axis=1).astype(BF16)
        yield
        o = _dot(p2, v_dup[j][b * WINDOW:(b + 2) * WINDOW])
        yield
        pairs += [jnp.where(low_o, o[0:WINDOW], o[2 * WINDOW:3 * WINDOW]),
                  jnp.where(low_o, o[WINDOW:2 * WINDOW], o[3 * WINDOW:4 * WINDOW])]
        if j == B_KV_HEADS - 1:
            yb = jnp.concatenate(pairs, axis=1)
            yb_ref[b * WINDOW:(b + 1) * WINDOW, :] = _rms(yb, mgb_ref[...]).astype(BF16)
            pairs = []


MIXER_STAGES = 6 + 2 * B_KV_HEADS * (ROWS_MIX // WINDOW)
POST_STAGES = 4 + FF_SPLIT


def _run(*staged):
    total = max(n for _, n in staged)
    done = [0] * len(staged)
    for step in range(1, total + 1):
        for k, (gen, n) in enumerate(staged):
            while done[k] * total < step * n:
                next(gen, None)
                done[k] += 1
    for gen, _ in staged:
        for _ in gen:
            pass


def _cast_specs(items, n_steps):
    step = lambda i: jnp.minimum(i, n_steps - 1)
    in_specs, out_specs, out_shapes = [], [], []
    for arr, layer, col0 in items:
        _, rows, cols = arr.shape
        chunk = rows // n_steps
        in_specs.append(
            pl.BlockSpec((None, chunk, cols), lambda i, layer=layer: (layer, step(i), 0)))
        out_specs.append(pl.BlockSpec((chunk, cols - col0), lambda i: (step(i), 0)))
        out_shapes.append(jax.ShapeDtypeStruct((rows, cols - col0), BF16))
    return in_specs, out_specs, out_shapes


def _cast_chunks(src_refs, dst_refs):
    for src, dst in zip(src_refs, dst_refs):
        x = src[...]
        dst[...] = x[:, x.shape[1] - dst.shape[1]:].astype(BF16)


N_MIXER_IN = 9


def _mix_kernel(layer, blocks_per_seq, sinks_ref, h_ref, cos_ref, sin_ref, g_ref, wa_ref, wr_ref,
                *rest):
    w_ref = (wa_ref, wr_ref)
    mixer_in, rest = rest[:N_MIXER_IN], rest[N_MIXER_IN:]
    n_cast = (len(rest) - 6) // 2
    cast_src, outs, cast_dst = rest[:n_cast], rest[n_cast:n_cast + 3], rest[n_cast + 3:-3]
    mixer_refs = mixer_in + outs
    z_ref, kprev_ref, vprev_ref = rest[-3:]
    first_block = (pl.program_id(0) % blocks_per_seq) == 0

    @pl.when(first_block)
    def _():
        kprev_ref[...] = jnp.zeros_like(kprev_ref)
        vprev_ref[...] = jnp.zeros_like(vprev_ref)

    _cast_chunks(cast_src, cast_dst)
    _mix_project(h_ref, g_ref, w_ref, z_ref)
    _run((_mix_mixers(layer, first_block, sinks_ref, cos_ref, sin_ref, *mixer_refs,
                      z_ref, kprev_ref, vprev_ref), MIXER_STAGES))


def _mix_param_specs(layer):
    return [
        _layer_spec((1, D_MODEL), layer),
        _layer_spec((D_MODEL, COL_Q), layer),
        _const_spec((D_MODEL, IN_COLS - COL_Q)),
        _layer_spec((CHUNK, A_HEADS * CHUNK), layer),
        _layer_spec((CHUNK, A_WIDTH), layer),
        _layer_spec((1, A_WIDTH), layer),
        _layer_spec((1, A_WIDTH), layer),
        _layer_spec((1, A_WIDTH), layer),
        _layer_spec((1, B_WIDTH), layer),
        _layer_spec((1, LANES), layer),
        _layer_spec((1, B_WIDTH), layer),
        _const_spec((256, 256)),
    ]


def _mix_out_shapes(n):
    return [jax.ShapeDtypeStruct((n, A_WIDTH), BF16),
            jax.ShapeDtypeStruct((n, B_WIDTH), BF16),
            jax.ShapeDtypeStruct((n, C_WIDTH), F32)]


_MIX_SCRATCH = [
    pltpu.VMEM((ROWS_MIX, IN_COLS), F32),
    pltpu.VMEM((WINDOW, LANES), F32),
    pltpu.VMEM((WINDOW, LANES), F32),
]


def _mix(layer, h, cos2, sin2, sinks, mix_params, casts, seq_len):
    n = h.shape[0]
    n_blocks = n // ROWS_MIX
    rb = lambda w_: pl.BlockSpec((ROWS_MIX, w_), lambda i: (i, 0))
    cast_in, cast_out, cast_shapes = _cast_specs(casts, n_blocks)
    return pl.pallas_call(
        functools.partial(_mix_kernel, layer, seq_len // ROWS_MIX),
        grid=(n_blocks,),
        in_specs=[pl.BlockSpec(memory_space=pltpu.SMEM), rb(D_MODEL),
                  rb(HEAD_DIM // 2), rb(HEAD_DIM // 2)]
        + _mix_param_specs(layer) + cast_in,
        out_specs=[rb(A_WIDTH), rb(B_WIDTH), rb(C_WIDTH)] + cast_out,
        out_shape=_mix_out_shapes(n) + cast_shapes,
        scratch_shapes=_MIX_SCRATCH,
        compiler_params=pltpu.CompilerParams(
            dimension_semantics=("arbitrary",), vmem_limit_bytes=VMEM_LIMIT),
        name="mix",
    )(sinks, h, cos2, sin2, *mix_params, *[c[0] for c in casts])


def _tables_kernel(are_ref, aim_ref, ldt_ref, bre_ref, bim_ref, cre_ref, cim_ref, d_ref,
                   kt_ref, wb_ref, wct_ref, astep_ref, atab_ref):
    a_re, a_im = are_ref[...], aim_ref[...]
    dt = jnp.exp(ldt_ref[...])
    mag = jnp.exp(a_re * dt)
    lr, li = mag * jnp.cos(a_im * dt), mag * jnp.sin(a_im * dt)
    den = a_re * a_re + a_im * a_im
    fr, fi = _cmul(lr - 1.0, li, a_re / den, -a_im / den)
    bbr, bbi = _cmul(fr, fi, bre_ref[...], bim_ref[...])
    c_re, c_im = cre_ref[...], cim_ref[...]

    row_g = lax.broadcasted_iota(jnp.int32, (QUAD_WIDTH, QUAD_STATE // 2), 0) // C_GROUP
    col_g = lax.broadcasted_iota(jnp.int32, (QUAD_WIDTH, QUAD_STATE // 2), 1) // C_STATE
    same_group = row_g == col_g

    def quad_block(mr, mi, q):
        cols = slice(q * QUAD_STATE // 2, (q + 1) * QUAD_STATE // 2)
        tile = lambda m: jnp.where(same_group, jnp.concatenate([m[:, cols]] * QUAD_GROUPS, axis=0), 0.0)
        return jnp.concatenate([tile(mr), tile(mi)], axis=1)

    pows = [(jnp.ones_like(lr), jnp.zeros_like(li))]
    for _ in range(SSM_T):
        pows.append(_cmul(pows[-1][0], pows[-1][1], lr, li))

    rr = lax.broadcasted_iota(jnp.int32, (PACKED, PACKED), 0)
    cc = lax.broadcasted_iota(jnp.int32, (PACKED, PACKED), 1)
    for q in range(QUADS):
        into_state, from_state, lag_out = [], [], []
        for s in range(SSM_T):
            wr, wi = _cmul(pows[SSM_T - 1 - s][0], pows[SSM_T - 1 - s][1], bbr, bbi)
            into_state.append(quad_block(wr, wi, q))
            mr, mi = _cmul(pows[s + 1][0], pows[s + 1][1], c_re, c_im)
            from_state.append(quad_block(mr, -mi, q))
            mr, mi = _cmul(pows[s][0], pows[s][1], c_re, c_im)
            lag_out.append(quad_block(mr, -mi, q))
        wb_ref[q] = jnp.concatenate(into_state, axis=0).astype(BF16)
        wct_ref[q] = jnp.concatenate(from_state, axis=0).astype(BF16)
        a_hi, a_lo = _split(quad_block(bbr, bbi, q))
        b_hi, b_lo = _split(jnp.concatenate(lag_out, axis=0))
        lags = _dot_nt(a_hi, b_hi) + _dot_nt(a_hi, b_lo) + _dot_nt(a_lo, b_hi)
        lane = lax.broadcasted_iota(jnp.int32, lags.shape, 1)
        k = jnp.concatenate(
            [lags] + [jnp.where(lane >= s * QUAD_WIDTH, pltpu.roll(lags, s * QUAD_WIDTH, 1), 0.0)
                      for s in range(1, SSM_T)], axis=0)
        kt_ref[q] = (k + jnp.where(rr == cc, d_ref[q], 0.0)).astype(BF16)

    half = QUAD_STATE // 2
    for q in range(QUADS):
        src = slice(q * half, (q + 1) * half)
        re_cols, im_cols = pl.ds(q * QUAD_STATE, half), pl.ds(q * QUAD_STATE + half, half)
        sr, si = pows[SSM_T][0][:, src], pows[SSM_T][1][:, src]
        atab_ref[0:1, re_cols] = sr
        atab_ref[0:1, im_cols] = si
        for k in range(SCAN_STEPS):
            span = 1 << k
            astep_ref[pl.ds(k, 1), re_cols] = sr
            astep_ref[pl.ds(k, 1), im_cols] = si
            tr, ti = _cmul(atab_ref[0:span, re_cols], atab_ref[0:span, im_cols], sr, si)
            atab_ref[pl.ds(span, span), re_cols] = tr
            atab_ref[pl.ds(span, span), im_cols] = ti
            sr, si = _cmul(sr, si, sr, si)


def _tables(a_re, a_im, ldt, b_re, b_im, c_re, c_im, d_skip):
    depth = a_re.shape[0]
    vec = lambda w: pl.BlockSpec((None, 1, w), lambda l: (l, 0, 0))
    mat = lambda r, c: pl.BlockSpec((None, r, c), lambda l: (l, 0, 0))
    cube = lambda r, c: pl.BlockSpec((None, QUADS, r, c), lambda l: (l, 0, 0, 0))
    packed = jax.ShapeDtypeStruct((depth, QUADS, PACKED, PACKED), BF16)
    return pl.pallas_call(
        _tables_kernel,
        grid=(depth,),
        in_specs=[vec(N_STATE), vec(N_STATE), vec(N_STATE),
                  mat(C_GROUP, N_STATE), mat(C_GROUP, N_STATE),
                  mat(C_GROUP, N_STATE), mat(C_GROUP, N_STATE), cube(1, PACKED)],
        out_specs=[cube(PACKED, PACKED), cube(PACKED, PACKED), cube(PACKED, PACKED),
                   mat(SCAN_STEPS, 2 * N_STATE), mat(SUBLANES, 2 * N_STATE)],
        out_shape=[packed, packed, packed,
                   jax.ShapeDtypeStruct((depth, SCAN_STEPS, 2 * N_STATE), F32),
                   jax.ShapeDtypeStruct((depth, SUBLANES, 2 * N_STATE), F32)],
        compiler_params=pltpu.CompilerParams(
            dimension_semantics=("arbitrary",), vmem_limit_bytes=VMEM_LIMIT),
        name="ssm_tables",
    )(a_re, a_im, ldt, b_re, b_im, c_re, c_im, d_skip)


def _glu_out(glu_a, glu_b, mg_ref, o_ref):
    o_ref[...] = _rms(glu_a * jax.nn.sigmoid(glu_b), mg_ref[...]).astype(BF16)


def _ssm_kernel(blocks_per_seq, n_blocks, *refs):
    w1_ref, w2_ref, mg_ref, o_ref, _, _, ya_ref, yb_ref, carry_ref = refs[-9:]
    i = pl.program_id(0)

    @pl.when(i == 0)
    def _():
        ya_ref[...] = jnp.zeros_like(ya_ref)
        yb_ref[...] = jnp.zeros_like(yb_ref)

    @pl.when((i % blocks_per_seq) == 0)
    def _():
        carry_ref[...] = jnp.zeros_like(carry_ref)

    @pl.when(i < n_blocks)
    def _():
        _ssm_block(*refs)

    @pl.when(i == n_blocks)
    def _():
        y_prev = jax.nn.gelu(jnp.concatenate([ya_ref[...], yb_ref[...]], axis=1)).astype(BF16)
        _glu_out(_dot(y_prev, w1_ref[...]), _dot(y_prev, w2_ref[...]), mg_ref, o_ref)


def _ssm_block(ua_ref, ub_ref, kt_ref, wb_ref, wct_ref, astep_ref, atab_ref,
               w1_ref, w2_ref, mg_ref, o_ref, inc_ref, sprev_ref, ya_ref, yb_ref, carry_ref):
    low = lax.broadcasted_iota(jnp.int32, (SSM_N, LANES), 1) < QUAD_WIDTH

    halves = [[ref[pl.ds(j, SSM_N, stride=SSM_T), :] for j in range(SSM_T)]
              for ref in (ua_ref, ub_ref)]

    def pack(q):
        src, upper = halves[q // 2], q % 2 == 1
        cols = []
        for m in range(SSM_T // 2):
            even, odd = src[2 * m], src[2 * m + 1]
            if upper:
                even = pltpu.roll(even, QUAD_WIDTH, 1)
            else:
                odd = pltpu.roll(odd, QUAD_WIDTH, 1)
            cols.append(jnp.where(low, even, odd))
        return jnp.concatenate(cols, axis=1).astype(BF16)

    xs = [pack(q) for q in range(QUADS)]

    for q in range(QUADS // 2):
        inc_ref[:, q * QUAD_STATE:(q + 1) * QUAD_STATE] = _dot(xs[q], wb_ref[q])
    y_prev = jax.nn.gelu(jnp.concatenate([ya_ref[...], yb_ref[...]], axis=1)).astype(BF16)
    glu_a, glu_b = _dot(y_prev, w1_ref[...]), _dot(y_prev, w2_ref[...])
    for q in range(QUADS // 2, QUADS):
        inc_ref[:, q * QUAD_STATE:(q + 1) * QUAD_STATE] = _dot(xs[q], wb_ref[q])
    _glu_out(glu_a, glu_b, mg_ref, o_ref)

    n_groups = SSM_N // SUBLANES
    sub = lax.broadcasted_iota(jnp.int32, (n_groups, SUBLANES, LANES), 1)
    sub2 = lax.broadcasted_iota(jnp.int32, (SUBLANES, LANES), 0)
    tiles_per_quad = QUAD_STATE // 2 // LANES
    y_lag = []
    for lt in range(N_STATE // LANES):
        q, part = lt // tiles_per_quad, lt % tiles_per_quad
        re_cols = pl.ds(q * QUAD_STATE + part * LANES, LANES)
        im_cols = pl.ds(q * QUAD_STATE + QUAD_STATE // 2 + part * LANES, LANES)
        xr = inc_ref[:, re_cols].reshape(n_groups, SUBLANES, LANES)
        xi = inc_ref[:, im_cols].reshape(n_groups, SUBLANES, LANES)
        for k in range(SCAN_STEPS):
            sh = 1 << k
            ar = astep_ref[pl.ds(k, 1), re_cols]
            ai = astep_ref[pl.ds(k, 1), im_cols]
            keep = sub >= sh
            sr = jnp.where(keep, pltpu.roll(xr, sh, 1), 0.0)
            si = jnp.where(keep, pltpu.roll(xi, sh, 1), 0.0)
            xr, xi = xr + ar * sr - ai * si, xi + ar * si + ai * sr
        tr = atab_ref[:, re_cols]
        ti = atab_ref[:, im_cols]
        er = jnp.broadcast_to(carry_ref[:, re_cols], (SUBLANES, LANES))
        ei = jnp.broadcast_to(carry_ref[:, im_cols], (SUBLANES, LANES))
        before_r, before_i = [], []
        for g in range(n_groups):
            fr = xr[g] + tr * er - ti * ei
            fi = xi[g] + tr * ei + ti * er
            before_r.append(jnp.where(sub2 == 0, er, pltpu.roll(fr, 1, 0)))
            before_i.append(jnp.where(sub2 == 0, ei, pltpu.roll(fi, 1, 0)))
            er = jnp.broadcast_to(fr[SUBLANES - 1:SUBLANES, :], (SUBLANES, LANES))
            ei = jnp.broadcast_to(fi[SUBLANES - 1:SUBLANES, :], (SUBLANES, LANES))
        sprev_ref[:, re_cols] = jnp.concatenate(before_r, axis=0).astype(BF16)
        sprev_ref[:, im_cols] = jnp.concatenate(before_i, axis=0).astype(BF16)
        carry_ref[:, re_cols] = er[0:1, :]
        carry_ref[:, im_cols] = ei[0:1, :]

        if part == tiles_per_quad - 1:
            y_lag.append(_dot(xs[q], kt_ref[q]))

    ys = [y_lag[q] + _dot_nt(sprev_ref[:, q * QUAD_STATE:(q + 1) * QUAD_STATE], wct_ref[q])
          for q in range(QUADS)]
    for t in range(SSM_T):
        cols = slice((t // 2) * LANES, (t // 2 + 1) * LANES)
        for y_ref, (qa, qb) in ((ya_ref, (0, 1)), (yb_ref, (2, 3))):
            lower, upper = ys[qa][:, cols], ys[qb][:, cols]
            if t % 2 == 1:
                lower = pltpu.roll(lower, QUAD_WIDTH, 1)
            else:
                upper = pltpu.roll(upper, QUAD_WIDTH, 1)
            y_ref[pl.ds(t, SSM_N, stride=SSM_T), :] = jnp.where(low, lower, upper)


def _ssm(layer, zc, kd, wb, wct, astep, atab, w1, w2, mg, seq_len):
    n = zc.shape[0]
    n_blocks = n // ROWS_SSM
    scanned = lambda i: jnp.minimum(i, n_blocks - 1)
    return pl.pallas_call(
        functools.partial(_ssm_kernel, seq_len // ROWS_SSM, n_blocks),
        grid=(n_blocks + 1,),
        in_specs=[
            pl.BlockSpec((ROWS_SSM, LANES), lambda i: (scanned(i), 0)),
            pl.BlockSpec((ROWS_SSM, LANES), lambda i: (scanned(i), 1)),
            _layer_spec((QUADS, PACKED, PACKED), layer),
            _layer_spec((QUADS, PACKED, QUAD_STATE), layer),
            _layer_spec((QUADS, PACKED, QUAD_STATE), layer),
            _layer_spec((SCAN_STEPS, 2 * N_STATE), layer),
            _layer_spec((SUBLANES, 2 * N_STATE), layer),
            _layer_spec((C_WIDTH, C_WIDTH), layer),
            _layer_spec((C_WIDTH, C_WIDTH), layer),
            _layer_spec((1, C_WIDTH), layer),
        ],
        out_specs=pl.BlockSpec((ROWS_SSM, C_WIDTH), lambda i: (jnp.maximum(i - 1, 0), 0)),
        out_shape=jax.ShapeDtypeStruct((n, C_WIDTH), BF16),
        scratch_shapes=[
            pltpu.VMEM((SSM_N, 2 * N_STATE), F32),
            pltpu.VMEM((SSM_N, 2 * N_STATE), BF16),
            pltpu.VMEM((ROWS_SSM, LANES), F32),
            pltpu.VMEM((ROWS_SSM, LANES), F32),
            pltpu.VMEM((1, 2 * N_STATE), F32),
        ],
        compiler_params=pltpu.CompilerParams(
            dimension_semantics=("arbitrary",), vmem_limit_bytes=VMEM_LIMIT),
        name="ssm",
    )(zc, zc, kd, wb, wct, astep, atab, w1, w2, mg)


def _post_stages(h_ref, ya_ref, yb_ref, yc_ref, p_ref, wo_ref, g1_ref, w1_ref, w2_ref,
                 g2_ref, wg_ref, wp_ref, *emit):
    mixed = (_dot(ya_ref[...], wo_ref[0:A_WIDTH, :])
             + _dot(yb_ref[...], wo_ref[A_WIDTH:A_WIDTH + B_WIDTH, :])
             + _dot(yc_ref[...], wo_ref[A_WIDTH + B_WIDTH:, :]))
    yield
    h = h_ref[...] + mixed
    hn = _rms(h, g1_ref[...]).astype(BF16)
    ple = _dot(p_ref[...].astype(BF16), wp_ref[...])
    piece = D_FF // FF_SPLIT
    up = _dot(hn, w1_ref[:, 0:piece])
    yield
    ff = None
    for c in range(FF_SPLIT):
        a = jnp.maximum(up, 0.0)
        down = _dot((a * a).astype(BF16), w2_ref[c * piece:(c + 1) * piece, :])
        if c + 1 < FF_SPLIT:
            up = _dot(hn, w1_ref[:, (c + 1) * piece:(c + 2) * piece])
        yield
        ff = down if ff is None else ff + down
    h = h + ff
    gate = _dot(_rms(h, g2_ref[...]).astype(BF16), wg_ref[...])
    yield
    h = h + jax.nn.sigmoid(gate) * ple
    for fn in emit:
        fn(h)
    yield


N_POST_IN = 12


def _store_to(ref):
    def store(value):
        ref[...] = value
    return store


def _post_kernel(*refs):
    _run((_post_stages(*refs[:N_POST_IN], _store_to(refs[N_POST_IN])), POST_STAGES))


def _post_specs(layer, row_block, rows=ROWS_MIX):
    rb = lambda w: pl.BlockSpec((rows, w), lambda i: (row_block(i), 0))
    return [
        rb(D_MODEL), rb(A_WIDTH), rb(B_WIDTH), rb(C_WIDTH),
        pl.BlockSpec((None, rows, PLE_DIM), lambda i: (layer, row_block(i), 0)),
        _const_spec((D_MODEL, D_MODEL)),
        _layer_spec((1, D_MODEL), layer),
        _const_spec((D_MODEL, D_FF)),
        _const_spec((D_FF, D_MODEL)),
        _layer_spec((1, D_MODEL), layer),
        _const_spec((D_MODEL, D_MODEL)),
        _layer_spec((PLE_DIM, D_MODEL), layer),
    ]


def _post(layer, h, ya, yb, yc, p, post_params):
    n = h.shape[0]
    return pl.pallas_call(
        _post_kernel,
        grid=(n // ROWS_PROJ,),
        in_specs=_post_specs(layer, lambda i: i, ROWS_PROJ),
        out_specs=pl.BlockSpec((ROWS_PROJ, D_MODEL), lambda i: (i, 0)),
        out_shape=jax.ShapeDtypeStruct((n, D_MODEL), F32),
        compiler_params=pltpu.CompilerParams(
            dimension_semantics=("arbitrary",), vmem_limit_bytes=VMEM_LIMIT),
        name="post",
    )(h, ya, yb, yc, p, *post_params)


def _fused_kernel(layer, blocks_per_seq, n_blocks, sinks_ref, *refs):
    post_refs, refs = refs[:N_POST_IN], refs[N_POST_IN:]
    cos_ref, sin_ref, g_ref = refs[:3]
    w_ref = refs[3:5]
    mixer_refs, refs = refs[5:5 + N_MIXER_IN], refs[5 + N_MIXER_IN:]
    n_cast = (len(refs) - 8) // 2
    cast_src, cast_dst = refs[:n_cast], refs[n_cast + 4:-4]
    h_out_ref, ya_ref, yb_ref, zc_ref = refs[n_cast:n_cast + 4]
    hs_ref, z_ref, kprev_ref, vprev_ref = refs[-4:]
    i = pl.program_id(0)
    first_block = (i % blocks_per_seq) == 1 % blocks_per_seq

    @pl.when((i == 0) | first_block)
    def _():
        kprev_ref[...] = jnp.zeros_like(kprev_ref)
        vprev_ref[...] = jnp.zeros_like(vprev_ref)

    @pl.when(i == 0)
    def _():
        hs_ref[...] = jnp.zeros_like(hs_ref)

    def mixers():
        return (_mix_mixers(layer, first_block, sinks_ref, cos_ref, sin_ref, *mixer_refs,
                            ya_ref, yb_ref, zc_ref, z_ref, kprev_ref, vprev_ref), MIXER_STAGES)

    @pl.when(i < n_blocks)
    def _():
        _cast_chunks(cast_src, cast_dst)
        _mix_project(hs_ref, g_ref, w_ref, z_ref)
        _run((_post_stages(*post_refs, _store_to(h_out_ref), _store_to(hs_ref)), POST_STAGES),
             mixers())

    @pl.when(i == n_blocks)
    def _():
        _mix_project(hs_ref, g_ref, w_ref, z_ref)
        _run(mixers())


def _fused(layer, h, ya, yb, yc, p, post_params, cos2, sin2, sinks, mix_params, casts, seq_len):
    n = h.shape[0]
    n_blocks = n // ROWS_MIX
    merged = lambda i: jnp.minimum(i, n_blocks - 1)
    mixed = lambda i: jnp.maximum(i - 1, 0)
    rb = lambda w_: pl.BlockSpec((ROWS_MIX, w_), lambda i: (mixed(i), 0))
    cast_in, cast_out, cast_shapes = _cast_specs(casts, n_blocks)
    return pl.pallas_call(
        functools.partial(_fused_kernel, layer, seq_len // ROWS_MIX, n_blocks),
        grid=(n_blocks + 1,),
        in_specs=[pl.BlockSpec(memory_space=pltpu.SMEM)]
        + _post_specs(layer - 1, merged)
        + [rb(HEAD_DIM // 2), rb(HEAD_DIM // 2)] + _mix_param_specs(layer) + cast_in,
        out_specs=[pl.BlockSpec((ROWS_MIX, D_MODEL), lambda i: (merged(i), 0)),
                   rb(A_WIDTH), rb(B_WIDTH), rb(C_WIDTH)] + cast_out,
        out_shape=[jax.ShapeDtypeStruct((n, D_MODEL), F32)] + _mix_out_shapes(n) + cast_shapes,
        scratch_shapes=[pltpu.VMEM((ROWS_MIX, D_MODEL), F32)] + _MIX_SCRATCH,
        compiler_params=pltpu.CompilerParams(
            dimension_semantics=("arbitrary",), vmem_limit_bytes=VMEM_LIMIT),
        name="fused",
    )(sinks, h, ya, yb, yc, p, *post_params, cos2, sin2, *mix_params, *[c[0] for c in casts])


def kernel(x, p, positions, attn_norm_g, w_in, gmlp_ln_g, gmlp_ln_b, gmlp_ws, gmlp_bs, q_norm_g, k_norm_g, sinks, ssm_a_re, ssm_a_im, ssm_log_dt, ssm_b_re, ssm_b_im, ssm_c_re, ssm_c_im, ssm_d, glu_w1, glu_w2, mix_out_g, w_out, mlp_norm_g, w_ff1, w_ff2, ple_norm_g, w_ple_gate, w_ple_proj):
    bsz, seq_len, _ = x.shape
    depth = w_in.shape[0]
    n = bsz * seq_len

    inv = 1.0 / (ROPE_THETA ** (jnp.arange(0, HEAD_DIM, 2, dtype=F32) / HEAD_DIM))
    ang = positions.astype(F32).reshape(n, 1) * inv
    cos2, sin2 = jnp.cos(ang), jnp.sin(ang)

    seg = np.arange(256) // HEAD_DIM
    pm = jnp.asarray((seg[:, None] == seg[None, :]) / HEAD_DIM, dtype=BF16)

    rows = lambda v: v.reshape(depth, 1, -1).astype(F32)
    g_attn, g_mlp, g_ple = rows(attn_norm_g), rows(mlp_norm_g), rows(ple_norm_g)
    ln_g, ln_b = rows(gmlp_ln_g), rows(gmlp_ln_b)
    mg_a = rows(mix_out_g[:, :A_WIDTH])
    mg_b = rows(mix_out_g[:, A_WIDTH:A_WIDTH + B_WIDTH])
    mg_c = rows(mix_out_g[:, A_WIDTH + B_WIDTH:])
    qg = rows(jnp.tile(q_norm_g, (1, B_Q_HEADS)))
    kg = rows(jnp.tile(k_norm_g, (1, B_KV_HEADS)))
    bs = jnp.repeat(jnp.swapaxes(gmlp_bs, 1, 2), HEAD_DIM, axis=2)
    w_a = w_in[:, :, :2 * A_WIDTH].reshape(depth, D_MODEL, A_HEADS, 2, HEAD_DIM)
    w_a = jnp.swapaxes(w_a, 2, 3).reshape(depth, D_MODEL, 2 * A_WIDTH)
    w_in_a, w_in_rest0 = w_a.astype(BF16), w_in[0, :, 2 * A_WIDTH:].astype(BF16)
    causal = np.tril(np.ones((CHUNK, CHUNK), dtype=bool))
    ws_b = jnp.swapaxes(jnp.where(causal, gmlp_ws, 0.0), 1, 2).reshape(
        depth, CHUNK, A_HEADS * CHUNK).astype(BF16)
    w_proj_b = w_ple_proj.astype(BF16)
    glu1_b, glu2_b = glu_w1.astype(BF16), glu_w2.astype(BF16)
    p2 = p.reshape(depth, n, PLE_DIM)

    kd, wb, wct, astep, atab = _tables(
        rows(ssm_a_re), rows(ssm_a_im), rows(jnp.repeat(ssm_log_dt, C_STATE, axis=1)),
        jnp.transpose(ssm_b_re, (0, 3, 1, 2)).reshape(depth, C_GROUP, N_STATE),
        jnp.transpose(ssm_b_im, (0, 3, 1, 2)).reshape(depth, C_GROUP, N_STATE),
        jnp.transpose(ssm_c_re, (0, 2, 1, 3)).reshape(depth, C_GROUP, N_STATE),
        jnp.transpose(ssm_c_im, (0, 2, 1, 3)).reshape(depth, C_GROUP, N_STATE),
        jnp.tile(ssm_d.reshape(depth, QUADS, 1, QUAD_WIDTH).astype(F32), (1, 1, 1, SSM_T)))

    def casts_for(layer):
        items = [(w_out, layer, 0), (w_ff1, layer, 0), (w_ff2, layer, 0), (w_ple_gate, layer, 0)]
        if layer + 1 < depth:
            items.append((w_in, layer + 1, 2 * A_WIDTH))
        return items

    def mix_params(w_in_rest):
        return (g_attn, w_in_a, w_in_rest, ws_b, bs, ln_g, ln_b, mg_a, qg, kg, mg_b, pm)

    def post_params(cast):
        w_out_b, w_ff1_b, w_ff2_b, w_gate_b = cast[:4]
        return (w_out_b, g_mlp, w_ff1_b, w_ff2_b, g_ple, w_gate_b, w_proj_b)

    h = x.reshape(n, D_MODEL)
    ya, yb, zc, *cast = _mix(0, h, cos2, sin2, sinks, mix_params(w_in_rest0), casts_for(0),
                             seq_len)
    for i in range(depth):
        yc = _ssm(i, zc, kd, wb, wct, astep, atab, glu1_b, glu2_b, mg_c, seq_len)
        if i + 1 < depth:
            h, ya, yb, zc, *cast = _fused(i + 1, h, ya, yb, yc, p2, post_params(cast), cos2, sin2,
                                          sinks, mix_params(cast[4]), casts_for(i + 1), seq_len)
        else:
            h = _post(i, h, ya, yb, yc, p2, post_params(cast))
    return h.reshape(bsz, seq_len, D_MODEL)
```

```python
import functools
import math

import numpy as np
import jax
import jax.numpy as jnp
from jax import lax
from jax.experimental import pallas as pl
from jax.experimental.pallas import tpu as pltpu

F32 = jnp.float32
BF16 = jnp.bfloat16

D_MODEL = 1024
HEAD_DIM = 64
A_WIDTH = 256
A_HEADS = 4
CHUNK = 128
B_WIDTH = 512
B_Q_HEADS = 8
B_KV_HEADS = 2
WINDOW = 128
ROPE_THETA = 10000.0
C_WIDTH = 256
C_GROUP = 16
C_GROUPS = 16
C_STATE = 64
N_STATE = C_GROUPS * C_STATE
IN_COLS = 1536
D_FF = 4096
PLE_DIM = 256
EPS = 1e-6
NEG_BIG = -1e30
LANES = 128

COL_A = 0
COL_Q = 512
COL_K = 1024
COL_V = 1152
COL_C = 1280

ROWS_PROJ = 512
ROWS_MIX = 512
ROWS_SSM = 2048
SSM_T = 8
SSM_N = ROWS_SSM // SSM_T
QUAD_GROUPS = 4
QUADS = C_GROUPS // QUAD_GROUPS
QUAD_WIDTH = QUAD_GROUPS * C_GROUP
QUAD_STATE = 2 * QUAD_GROUPS * C_STATE
PACKED = SSM_T * QUAD_WIDTH
SUBLANES = 8
SCAN_STEPS = int(math.log2(SUBLANES))
FF_SPLIT = 8
VMEM_LIMIT = 60 * 1024 * 1024


def _const_spec(shape):
    nd = len(shape)
    return pl.BlockSpec(shape, lambda *_: (0,) * nd, pipeline_mode=pl.Buffered(1))


def _layer_spec(shape, layer):
    nd = len(shape)
    return pl.BlockSpec((None,) + tuple(shape), lambda *_: (layer,) + (0,) * nd,
                        pipeline_mode=pl.Buffered(1))


def _rms(x, g):
    ms = jnp.mean(x * x, axis=-1, keepdims=True)
    return x * lax.rsqrt(ms + EPS) * g


def _dot(a, b):
    return jnp.dot(a, b, preferred_element_type=F32)


def _dot_nt(a, b):
    return lax.dot_general(a, b, (((1,), (1,)), ((), ())), preferred_element_type=F32)


def _split(x):
    hi = x.astype(BF16)
    return hi, (x - hi.astype(F32)).astype(BF16)


def _seg_mean(x, pmat):
    hi, lo = _split(x)
    return _dot(hi, pmat) + _dot(lo, pmat)


def _cmul(ar, ai, br, bi):
    return ar * br - ai * bi, ar * bi + ai * br


def _rope(x, cos, sin_signed):
    width = x.shape[-1]
    lane = lax.broadcasted_iota(jnp.int32, x.shape, 1)
    first_half = (lane % HEAD_DIM) < (HEAD_DIM // 2)
    partner = jnp.where(first_half,
                        pltpu.roll(x, width - HEAD_DIM // 2, 1),
                        pltpu.roll(x, HEAD_DIM // 2, 1))
    return x * cos + partner * sin_signed


def _dup_heads(x):
    low = lax.broadcasted_iota(jnp.int32, x.shape, 1) < HEAD_DIM
    sw = pltpu.roll(x, HEAD_DIM, 1)
    return [jnp.where(low, x, sw).astype(BF16), jnp.where(low, sw, x).astype(BF16)]


def _mix_project(h_ref, g_ref, w_refs, z_ref):
    wa_ref, wr_ref = w_refs
    xn = _rms(h_ref[...], g_ref[...]).astype(BF16)
    z_ref[:, COL_A:COL_Q] = _dot(xn, wa_ref[...])
    z_ref[:, COL_Q:] = _dot(xn, wr_ref[...])


def _mix_mixers(layer, first_block, sinks_ref, cos_ref, sin_ref, ws_ref, bs_ref, lng_ref, lnb_ref,
                mga_ref, qg_ref, kg_ref, mgb_ref, pm_ref, ya_ref, yb_ref, zc_ref,
                z_ref, kprev_ref, vprev_ref):
    nq = ROWS_MIX // WINDOW
    zc_ref[...] = z_ref[:, COL_C:COL_C + C_WIDTH]
    pm = pm_ref[...]
    pm_kv = pm[0:LANES, 0:LANES]

    k_raw = z_ref[:, COL_K:COL_K + LANES]
    k_ms = _dot((k_raw * k_raw).astype(BF16), pm_kv)
    yield
    k_cur = _rope(k_raw * lax.rsqrt(k_ms + EPS) * kg_ref[...], cos_ref[...], sin_ref[...])
    v_cur = z_ref[:, COL_V:COL_V + LANES]
    k_dup = _dup_heads(jnp.concatenate([kprev_ref[...], k_cur], axis=0))
    v_dup = _dup_heads(jnp.concatenate([vprev_ref[...], v_cur], axis=0))
    kprev_ref[...] = k_cur[ROWS_MIX - WINDOW:, :]
    vprev_ref[...] = v_cur[ROWS_MIX - WINDOW:, :]
    q = z_ref[:, COL_Q:COL_Q + B_WIDTH]
    q_ms = jnp.concatenate(
        [_dot((q[:, s:s + 256] * q[:, s:s + 256]).astype(BF16), pm) for s in (0, 256)], axis=1)
    yield
    cos_q = jnp.concatenate([cos_ref[...]] * 4, axis=1)
    sin_q = jnp.concatenate([sin_ref[...]] * 4, axis=1)
    qr = _rope(q * lax.rsqrt(q_ms + EPS) * qg_ref[...], cos_q, sin_q) * (HEAD_DIM ** -0.5)
    low_q = (lax.broadcasted_iota(jnp.int32, qr.shape, 1) % LANES) < HEAD_DIM
    q_low = jnp.where(low_q, qr, 0.0).astype(BF16)
    q_high = jnp.where(low_q, 0.0, qr).astype(BF16)

    u = jax.nn.gelu(z_ref[:, COL_A:COL_A + A_WIDTH])
    v = jax.nn.gelu(z_ref[:, COL_A + A_WIDTH:COL_A + 2 * A_WIDTH])
    v_mean = _dot(v.astype(BF16), pm)
    yield
    vc = v - v_mean
    v_var = _dot((vc * vc).astype(BF16), pm)
    yield
    vn = (vc * lax.rsqrt(v_var + EPS) * lng_ref[...] + lnb_ref[...]).astype(BF16)
    lane = lax.broadcasted_iota(jnp.int32, (CHUNK, A_WIDTH), 1)
    in_head = [(lane >= h * HEAD_DIM) & (lane < (h + 1) * HEAD_DIM) for h in range(A_HEADS)]
    svs = []
    for c in range(ROWS_MIX // CHUNK):
        vn_c = vn[c * CHUNK:(c + 1) * CHUNK]
        stacked = jnp.concatenate(
            [jnp.where(in_head[h], vn_c, jnp.zeros((), BF16)) for h in range(A_HEADS)], axis=0)
        svs.append(_dot(ws_ref[...], stacked))
    yield
    for c in range(ROWS_MIX // CHUNK):
        rows = slice(c * CHUNK, (c + 1) * CHUNK)
        ya_ref[rows, :] = _rms(u[rows] * (svs[c] + bs_ref[...]), mga_ref[...]).astype(BF16)

    qi = lax.broadcasted_iota(jnp.int32, (4 * WINDOW, WINDOW), 0) % WINDOW
    kc = lax.broadcasted_iota(jnp.int32, (4 * WINDOW, WINDOW), 1)
    from_cur = kc <= qi
    head_slot = lax.broadcasted_iota(jnp.int32, (4 * WINDOW, 1), 0) // WINDOW
    prev_ok = kc >= jnp.where(first_block, WINDOW, 0)
    low_o = lax.broadcasted_iota(jnp.int32, (WINDOW, LANES), 1) < HEAD_DIM
    zero = jnp.zeros((), F32)

    def scores(b, j):
        rows = slice(b * WINDOW, (b + 1) * WINDOW)
        pair_a = slice((2 * j) * LANES, (2 * j + 1) * LANES)
        pair_b = slice((2 * j + 1) * LANES, (2 * j + 2) * LANES)
        qs = jnp.concatenate([q_low[rows, pair_a], q_low[rows, pair_b],
                              q_high[rows, pair_a], q_high[rows, pair_b]], axis=0)
        return _dot_nt(qs, k_dup[j][b * WINDOW:(b + 2) * WINDOW])

    units = [(b, j) for b in range(nq) for j in range(B_KV_HEADS)]
    s_next = scores(*units[0])
    yield
    pairs = []
    for u, (b, j) in enumerate(units):
        s = s_next
        if u + 1 < len(units):
            s_next = scores(*units[u + 1])
        s_prev = s[:, 0:WINDOW]
        if b == 0:
            s_prev = jnp.where(prev_ok, s_prev, NEG_BIG)
        s = jnp.where(from_cur, s[:, WINDOW:], s_prev)
        sink = jnp.where(head_slot == 0, sinks_ref[layer, 4 * j],
                         jnp.where(head_slot == 1, sinks_ref[layer, 4 * j + 2],
                                   jnp.where(head_slot == 2, sinks_ref[layer, 4 * j + 1],
                                             sinks_ref[layer, 4 * j + 3])))
        m = jnp.maximum(jnp.max(s, axis=-1, keepdims=True), sink)
        pr = jnp.exp(s - m)
        denom = jnp.sum(pr, axis=-1, keepdims=True) + jnp.exp(sink - m)
        pn = pr * (1.0 / denom)
        p2 = jnp.concatenate([jnp.where(from_cur, zero, pn),
                              jnp.where(from_cur, pn, zero)], axis=1).astype(BF16)
        yield
        o = _dot(p2, v_dup[j][b * WINDOW:(b + 2) * WINDOW])
        yield
        pairs += [jnp.where(low_o, o[0:WINDOW], o[2 * WINDOW:3 * WINDOW]),
                  jnp.where(low_o, o[WINDOW:2 * WINDOW], o[3 * WINDOW:4 * WINDOW])]
        if j == B_KV_HEADS - 1:
            yb = jnp.concatenate(pairs, axis=1)
            yb_ref[b * WINDOW:(b + 1) * WINDOW, :] = _rms(yb, mgb_ref[...]).astype(BF16)
            pairs = []


MIXER_STAGES = 6 + 2 * B_KV_HEADS * (ROWS_MIX // WINDOW)
POST_STAGES = 5 + FF_SPLIT


def _run(*staged):
    total = max(n for _, n in staged)
    done = [0] * len(staged)
    for step in range(1, total + 1):
        for k, (gen, n) in enumerate(staged):
            while done[k] * total < step * n:
                next(gen, None)
                done[k] += 1
    for gen, _ in staged:
        for _ in gen:
            pass


def _cast_specs(items, n_steps):
    step = lambda i: jnp.minimum(i, n_steps - 1)
    in_specs, out_specs, out_shapes = [], [], []
    for arr, layer, col0 in items:
        _, rows, cols = arr.shape
        chunk = rows // n_steps
        in_specs.append(
            pl.BlockSpec((None, chunk, cols), lambda i, layer=layer: (layer, step(i), 0)))
        out_specs.append(pl.BlockSpec((chunk, cols - col0), lambda i: (step(i), 0)))
        out_shapes.append(jax.ShapeDtypeStruct((rows, cols - col0), BF16))
    return in_specs, out_specs, out_shapes


def _cast_chunks(src_refs, dst_refs):
    for src, dst in zip(src_refs, dst_refs):
        x = src[...]
        dst[...] = x[:, x.shape[1] - dst.shape[1]:].astype(BF16)


N_MIXER_IN = 9


def _mix_kernel(layer, blocks_per_seq, sinks_ref, h_ref, cos_ref, sin_ref, g_ref, wa_ref, wr_ref,
                *rest):
    w_ref = (wa_ref, wr_ref)
    mixer_in, rest = rest[:N_MIXER_IN], rest[N_MIXER_IN:]
    n_cast = (len(rest) - 6) // 2
    cast_src, outs, cast_dst = rest[:n_cast], rest[n_cast:n_cast + 3], rest[n_cast + 3:-3]
    mixer_refs = mixer_in + outs
    z_ref, kprev_ref, vprev_ref = rest[-3:]
    first_block = (pl.program_id(0) % blocks_per_seq) == 0

    @pl.when(first_block)
    def _():
        kprev_ref[...] = jnp.zeros_like(kprev_ref)
        vprev_ref[...] = jnp.zeros_like(vprev_ref)

    _cast_chunks(cast_src, cast_dst)
    _mix_project(h_ref, g_ref, w_ref, z_ref)
    _run((_mix_mixers(layer, first_block, sinks_ref, cos_ref, sin_ref, *mixer_refs,
                      z_ref, kprev_ref, vprev_ref), MIXER_STAGES))


def _mix_param_specs(layer):
    return [
        _layer_spec((1, D_MODEL), layer),
        _layer_spec((D_MODEL, COL_Q), layer),
        _const_spec((D_MODEL, IN_COLS - COL_Q)),
        _layer_spec((CHUNK, A_HEADS * CHUNK), layer),
        _layer_spec((CHUNK, A_WIDTH), layer),
        _layer_spec((1, A_WIDTH), layer),
        _layer_spec((1, A_WIDTH), layer),
        _layer_spec((1, A_WIDTH), layer),
        _layer_spec((1, B_WIDTH), layer),
        _layer_spec((1, LANES), layer),
        _layer_spec((1, B_WIDTH), layer),
        _const_spec((256, 256)),
    ]


def _mix_out_shapes(n):
    return [jax.ShapeDtypeStruct((n, A_WIDTH), BF16),
            jax.ShapeDtypeStruct((n, B_WIDTH), BF16),
            jax.ShapeDtypeStruct((n, C_WIDTH), F32)]


_MIX_SCRATCH = [
    pltpu.VMEM((ROWS_MIX, IN_COLS), F32),
    pltpu.VMEM((WINDOW, LANES), F32),
    pltpu.VMEM((WINDOW, LANES), F32),
]


def _mix(layer, h, cos2, sin2, sinks, mix_params, casts, seq_len):
    n = h.shape[0]
    n_blocks = n // ROWS_MIX
    rb = lambda w_: pl.BlockSpec((ROWS_MIX, w_), lambda i: (i, 0))
    cast_in, cast_out, cast_shapes = _cast_specs(casts, n_blocks)
    return pl.pallas_call(
        functools.partial(_mix_kernel, layer, seq_len // ROWS_MIX),
        grid=(n_blocks,),
        in_specs=[pl.BlockSpec(memory_space=pltpu.SMEM), rb(D_MODEL), rb(LANES), rb(LANES)]
        + _mix_param_specs(layer) + cast_in,
        out_specs=[rb(A_WIDTH), rb(B_WIDTH), rb(C_WIDTH)] + cast_out,
        out_shape=_mix_out_shapes(n) + cast_shapes,
        scratch_shapes=_MIX_SCRATCH,
        compiler_params=pltpu.CompilerParams(
            dimension_semantics=("arbitrary",), vmem_limit_bytes=VMEM_LIMIT),
        name="mix",
    )(sinks, h, cos2, sin2, *mix_params, *[c[0] for c in casts])


def _tables_kernel(are_ref, aim_ref, ldt_ref, bre_ref, bim_ref, cre_ref, cim_ref, d_ref,
                   kt_ref, wb_ref, wct_ref, astep_ref, atab_ref):
    a_re, a_im = are_ref[...], aim_ref[...]
    dt = jnp.exp(ldt_ref[...])
    mag = jnp.exp(a_re * dt)
    lr, li = mag * jnp.cos(a_im * dt), mag * jnp.sin(a_im * dt)
    den = a_re * a_re + a_im * a_im
    fr, fi = _cmul(lr - 1.0, li, a_re / den, -a_im / den)
    bbr, bbi = _cmul(fr, fi, bre_ref[...], bim_ref[...])
    c_re, c_im = cre_ref[...], cim_ref[...]

    row_g = lax.broadcasted_iota(jnp.int32, (QUAD_WIDTH, QUAD_STATE // 2), 0) // C_GROUP
    col_g = lax.broadcasted_iota(jnp.int32, (QUAD_WIDTH, QUAD_STATE // 2), 1) // C_STATE
    same_group = row_g == col_g

    def quad_block(mr, mi, q):
        cols = slice(q * QUAD_STATE // 2, (q + 1) * QUAD_STATE // 2)
        tile = lambda m: jnp.where(same_group, jnp.concatenate([m[:, cols]] * QUAD_GROUPS, axis=0), 0.0)
        return jnp.concatenate([tile(mr), tile(mi)], axis=1)

    pows = [(jnp.ones_like(lr), jnp.zeros_like(li))]
    for _ in range(SSM_T):
        pows.append(_cmul(pows[-1][0], pows[-1][1], lr, li))

    rr = lax.broadcasted_iota(jnp.int32, (PACKED, PACKED), 0)
    cc = lax.broadcasted_iota(jnp.int32, (PACKED, PACKED), 1)
    for q in range(QUADS):
        into_state, from_state, lag_out = [], [], []
        for s in range(SSM_T):
            wr, wi = _cmul(pows[SSM_T - 1 - s][0], pows[SSM_T - 1 - s][1], bbr, bbi)
            into_state.append(quad_block(wr, wi, q))
            mr, mi = _cmul(pows[s + 1][0], pows[s + 1][1], c_re, c_im)
            from_state.append(quad_block(mr, -mi, q))
            mr, mi = _cmul(pows[s][0], pows[s][1], c_re, c_im)
            lag_out.append(quad_block(mr, -mi, q))
        wb_ref[q] = jnp.concatenate(into_state, axis=0).astype(BF16)
        wct_ref[q] = jnp.concatenate(from_state, axis=0).astype(BF16)
        a_hi, a_lo = _split(quad_block(bbr, bbi, q))
        b_hi, b_lo = _split(jnp.concatenate(lag_out, axis=0))
        lags = _dot_nt(a_hi, b_hi) + _dot_nt(a_hi, b_lo) + _dot_nt(a_lo, b_hi)
        lane = lax.broadcasted_iota(jnp.int32, lags.shape, 1)
        k = jnp.concatenate(
            [lags] + [jnp.where(lane >= s * QUAD_WIDTH, pltpu.roll(lags, s * QUAD_WIDTH, 1), 0.0)
                      for s in range(1, SSM_T)], axis=0)
        kt_ref[q] = (k + jnp.where(rr == cc, d_ref[q], 0.0)).astype(BF16)

    half = QUAD_STATE // 2
    for q in range(QUADS):
        src = slice(q * half, (q + 1) * half)
        re_cols, im_cols = pl.ds(q * QUAD_STATE, half), pl.ds(q * QUAD_STATE + half, half)
        sr, si = pows[SSM_T][0][:, src], pows[SSM_T][1][:, src]
        atab_ref[0:1, re_cols] = sr
        atab_ref[0:1, im_cols] = si
        for k in range(SCAN_STEPS):
            span = 1 << k
            astep_ref[pl.ds(k, 1), re_cols] = sr
            astep_ref[pl.ds(k, 1), im_cols] = si
            tr, ti = _cmul(atab_ref[0:span, re_cols], atab_ref[0:span, im_cols], sr, si)
            atab_ref[pl.ds(span, span), re_cols] = tr
            atab_ref[pl.ds(span, span), im_cols] = ti
            sr, si = _cmul(sr, si, sr, si)


def _tables(a_re, a_im, ldt, b_re, b_im, c_re, c_im, d_skip):
    depth = a_re.shape[0]
    vec = lambda w: pl.BlockSpec((None, 1, w), lambda l: (l, 0, 0))
    mat = lambda r, c: pl.BlockSpec((None, r, c), lambda l: (l, 0, 0))
    cube = lambda r, c: pl.BlockSpec((None, QUADS, r, c), lambda l: (l, 0, 0, 0))
    packed = jax.ShapeDtypeStruct((depth, QUADS, PACKED, PACKED), BF16)
    return pl.pallas_call(
        _tables_kernel,
        grid=(depth,),
        in_specs=[vec(N_STATE), vec(N_STATE), vec(N_STATE),
                  mat(C_GROUP, N_STATE), mat(C_GROUP, N_STATE),
                  mat(C_GROUP, N_STATE), mat(C_GROUP, N_STATE), cube(1, PACKED)],
        out_specs=[cube(PACKED, PACKED), cube(PACKED, PACKED), cube(PACKED, PACKED),
                   mat(SCAN_STEPS, 2 * N_STATE), mat(SUBLANES, 2 * N_STATE)],
        out_shape=[packed, packed, packed,
                   jax.ShapeDtypeStruct((depth, SCAN_STEPS, 2 * N_STATE), F32),
                   jax.ShapeDtypeStruct((depth, SUBLANES, 2 * N_STATE), F32)],
        compiler_params=pltpu.CompilerParams(
            dimension_semantics=("arbitrary",), vmem_limit_bytes=VMEM_LIMIT),
        name="ssm_tables",
    )(a_re, a_im, ldt, b_re, b_im, c_re, c_im, d_skip)


def _ssm_kernel(blocks_per_seq, ua_ref, ub_ref, kt_ref, wb_ref, wct_ref, astep_ref, atab_ref,
                ya_ref, yb_ref, inc_ref, sprev_ref, carry_ref):
    @pl.when((pl.program_id(0) % blocks_per_seq) == 0)
    def _():
        carry_ref[...] = jnp.zeros_like(carry_ref)

    low = lax.broadcasted_iota(jnp.int32, (SSM_N, LANES), 1) < QUAD_WIDTH

    halves = [[ref[pl.ds(j, SSM_N, stride=SSM_T), :] for j in range(SSM_T)]
              for ref in (ua_ref, ub_ref)]

    def pack(q):
        src, upper = halves[q // 2], q % 2 == 1
        cols = []
        for m in range(SSM_T // 2):
            even, odd = src[2 * m], src[2 * m + 1]
            if upper:
                even = pltpu.roll(even, QUAD_WIDTH, 1)
            else:
                odd = pltpu.roll(odd, QUAD_WIDTH, 1)
            cols.append(jnp.where(low, even, odd))
        return jnp.concatenate(cols, axis=1).astype(BF16)

    xs = [pack(q) for q in range(QUADS)]

    for q in range(QUADS):
        inc_ref[:, q * QUAD_STATE:(q + 1) * QUAD_STATE] = _dot(xs[q], wb_ref[q])

    n_groups = SSM_N // SUBLANES
    sub = lax.broadcasted_iota(jnp.int32, (n_groups, SUBLANES, LANES), 1)
    sub2 = lax.broadcasted_iota(jnp.int32, (SUBLANES, LANES), 0)
    tiles_per_quad = QUAD_STATE // 2 // LANES
    y_lag = []
    for lt in range(N_STATE // LANES):
        q, part = lt // tiles_per_quad, lt % tiles_per_quad
        re_cols = pl.ds(q * QUAD_STATE + part * LANES, LANES)
        im_cols = pl.ds(q * QUAD_STATE + QUAD_STATE // 2 + part * LANES, LANES)
        xr = inc_ref[:, re_cols].reshape(n_groups, SUBLANES, LANES)
        xi = inc_ref[:, im_cols].reshape(n_groups, SUBLANES, LANES)
        for k in range(SCAN_STEPS):
            sh = 1 << k
            ar = astep_ref[pl.ds(k, 1), re_cols]
            ai = astep_ref[pl.ds(k, 1), im_cols]
            keep = sub >= sh
            sr = jnp.where(keep, pltpu.roll(xr, sh, 1), 0.0)
            si = jnp.where(keep, pltpu.roll(xi, sh, 1), 0.0)
            xr, xi = xr + ar * sr - ai * si, xi + ar * si + ai * sr
        tr = atab_ref[:, re_cols]
        ti = atab_ref[:, im_cols]
        er = jnp.broadcast_to(carry_ref[:, re_cols], (SUBLANES, LANES))
        ei = jnp.broadcast_to(carry_ref[:, im_cols], (SUBLANES, LANES))
        before_r, before_i = [], []
        for g in range(n_groups):
            fr = xr[g] + tr * er - ti * ei
            fi = xi[g] + tr * ei + ti * er
            before_r.append(jnp.where(sub2 == 0, er, pltpu.roll(fr, 1, 0)))
            before_i.append(jnp.where(sub2 == 0, ei, pltpu.roll(fi, 1, 0)))
            er = jnp.broadcast_to(fr[SUBLANES - 1:SUBLANES, :], (SUBLANES, LANES))
            ei = jnp.broadcast_to(fi[SUBLANES - 1:SUBLANES, :], (SUBLANES, LANES))
        sprev_ref[:, re_cols] = jnp.concatenate(before_r, axis=0).astype(BF16)
        sprev_ref[:, im_cols] = jnp.concatenate(before_i, axis=0).astype(BF16)
        carry_ref[:, re_cols] = er[0:1, :]
        carry_ref[:, im_cols] = ei[0:1, :]

        if part == tiles_per_quad - 1:
            y_lag.append(_dot(xs[q], kt_ref[q]))

    ys = [y_lag[q] + _dot_nt(sprev_ref[:, q * QUAD_STATE:(q + 1) * QUAD_STATE], wct_ref[q])
          for q in range(QUADS)]
    for t in range(SSM_T):
        cols = slice((t // 2) * LANES, (t // 2 + 1) * LANES)
        for y_ref, (qa, qb) in ((ya_ref, (0, 1)), (yb_ref, (2, 3))):
            lower, upper = ys[qa][:, cols], ys[qb][:, cols]
            if t % 2 == 1:
                lower = pltpu.roll(lower, QUAD_WIDTH, 1)
            else:
                upper = pltpu.roll(upper, QUAD_WIDTH, 1)
            y_ref[pl.ds(t, SSM_N, stride=SSM_T), :] = jnp.where(low, lower, upper)


def _ssm(layer, zc, kt, wb, wct, astep, atab, seq_len):
    n = zc.shape[0]
    half = lambda c: pl.BlockSpec((ROWS_SSM, LANES), lambda i: (i, c))
    return pl.pallas_call(
        functools.partial(_ssm_kernel, seq_len // ROWS_SSM),
        grid=(n // ROWS_SSM,),
        in_specs=[
            half(0), half(1),
            _layer_spec((QUADS, PACKED, PACKED), layer),
            _layer_spec((QUADS, PACKED, QUAD_STATE), layer),
            _layer_spec((QUADS, PACKED, QUAD_STATE), layer),
            _layer_spec((SCAN_STEPS, 2 * N_STATE), layer),
            _layer_spec((SUBLANES, 2 * N_STATE), layer),
        ],
        out_specs=[pl.BlockSpec((ROWS_SSM, LANES), lambda i: (i, 0)) for _ in range(2)],
        out_shape=[jax.ShapeDtypeStruct((n, LANES), F32) for _ in range(2)],
        scratch_shapes=[
            pltpu.VMEM((SSM_N, 2 * N_STATE), F32),
            pltpu.VMEM((SSM_N, 2 * N_STATE), BF16),
            pltpu.VMEM((1, 2 * N_STATE), F32),
        ],
        compiler_params=pltpu.CompilerParams(
            dimension_semantics=("arbitrary",), vmem_limit_bytes=VMEM_LIMIT),
        name="ssm",
    )(zc, zc, kt, wb, wct, astep, atab)


def _post_stages(h_ref, ya_ref, yb_ref, sa_ref, sb_ref, p_ref, glu1_ref, glu2_ref, mgc_ref,
                 wo_ref, g1_ref, w1_ref, w2_ref, g2_ref, wg_ref, wp_ref, *emit):
    y = jax.nn.gelu(jnp.concatenate([sa_ref[...], sb_ref[...]], axis=1)).astype(BF16)
    glu_a, glu_b = _dot(y, glu1_ref[...]), _dot(y, glu2_ref[...])
    mixed = (_dot(ya_ref[...], wo_ref[0:A_WIDTH, :])
             + _dot(yb_ref[...], wo_ref[A_WIDTH:A_WIDTH + B_WIDTH, :]))
    yield
    yc = _rms(glu_a * jax.nn.sigmoid(glu_b), mgc_ref[...]).astype(BF16)
    mixed = mixed + _dot(yc, wo_ref[A_WIDTH + B_WIDTH:, :])
    yield
    h = h_ref[...] + mixed
    hn = _rms(h, g1_ref[...]).astype(BF16)
    ple = _dot(p_ref[...].astype(BF16), wp_ref[...])
    piece = D_FF // FF_SPLIT
    up = _dot(hn, w1_ref[:, 0:piece])
    yield
    ff = None
    for c in range(FF_SPLIT):
        a = jnp.maximum(up, 0.0)
        down = _dot((a * a).astype(BF16), w2_ref[c * piece:(c + 1) * piece, :])
        if c + 1 < FF_SPLIT:
            up = _dot(hn, w1_ref[:, (c + 1) * piece:(c + 2) * piece])
        yield
        ff = down if ff is None else ff + down
    h = h + ff
    gate = _dot(_rms(h, g2_ref[...]).astype(BF16), wg_ref[...])
    yield
    h = h + jax.nn.sigmoid(gate) * ple
    for fn in emit:
        fn(h)
    yield


N_POST_IN = 16


def _store_to(ref):
    def store(value):
        ref[...] = value
    return store


def _post_kernel(*refs):
    _run((_post_stages(*refs[:N_POST_IN], _store_to(refs[N_POST_IN])), POST_STAGES))


def _post_specs(layer, row_block, rows=ROWS_MIX):
    rb = lambda w: pl.BlockSpec((rows, w), lambda i: (row_block(i), 0))
    return [
        rb(D_MODEL), rb(A_WIDTH), rb(B_WIDTH), rb(LANES), rb(LANES),
        pl.BlockSpec((None, rows, PLE_DIM), lambda i: (layer, row_block(i), 0)),
        _layer_spec((C_WIDTH, C_WIDTH), layer),
        _layer_spec((C_WIDTH, C_WIDTH), layer),
        _layer_spec((1, C_WIDTH), layer),
        _const_spec((D_MODEL, D_MODEL)),
        _layer_spec((1, D_MODEL), layer),
        _const_spec((D_MODEL, D_FF)),
        _const_spec((D_FF, D_MODEL)),
        _layer_spec((1, D_MODEL), layer),
        _const_spec((D_MODEL, D_MODEL)),
        _layer_spec((PLE_DIM, D_MODEL), layer),
    ]


def _post(layer, h, ya, yb, yc, p, post_params):
    n = h.shape[0]
    return pl.pallas_call(
        _post_kernel,
        grid=(n // ROWS_PROJ,),
        in_specs=_post_specs(layer, lambda i: i, ROWS_PROJ),
        out_specs=pl.BlockSpec((ROWS_PROJ, D_MODEL), lambda i: (i, 0)),
        out_shape=jax.ShapeDtypeStruct((n, D_MODEL), F32),
        compiler_params=pltpu.CompilerParams(
            dimension_semantics=("arbitrary",), vmem_limit_bytes=VMEM_LIMIT),
        name="post",
    )(h, ya, yb, *yc, p, *post_params)


def _fused_kernel(layer, blocks_per_seq, n_blocks, sinks_ref, *refs):
    post_refs, refs = refs[:N_POST_IN], refs[N_POST_IN:]
    cos_ref, sin_ref, g_ref = refs[:3]
    w_ref = refs[3:5]
    mixer_refs, refs = refs[5:5 + N_MIXER_IN], refs[5 + N_MIXER_IN:]
    n_cast = (len(refs) - 8) // 2
    cast_src, cast_dst = refs[:n_cast], refs[n_cast + 4:-4]
    h_out_ref, ya_ref, yb_ref, zc_ref = refs[n_cast:n_cast + 4]
    hs_ref, z_ref, kprev_ref, vprev_ref = refs[-4:]
    i = pl.program_id(0)
    first_block = (i % blocks_per_seq) == 1 % blocks_per_seq

    @pl.when((i == 0) | first_block)
    def _():
        kprev_ref[...] = jnp.zeros_like(kprev_ref)
        vprev_ref[...] = jnp.zeros_like(vprev_ref)

    @pl.when(i == 0)
    def _():
        hs_ref[...] = jnp.zeros_like(hs_ref)

    def mixers():
        return (_mix_mixers(layer, first_block, sinks_ref, cos_ref, sin_ref, *mixer_refs,
                            ya_ref, yb_ref, zc_ref, z_ref, kprev_ref, vprev_ref), MIXER_STAGES)

    @pl.when(i < n_blocks)
    def _():
        _cast_chunks(cast_src, cast_dst)
        _mix_project(hs_ref, g_ref, w_ref, z_ref)
        _run((_post_stages(*post_refs, _store_to(h_out_ref), _store_to(hs_ref)), POST_STAGES),
             mixers())

    @pl.when(i == n_blocks)
    def _():
        _mix_project(hs_ref, g_ref, w_ref, z_ref)
        _run(mixers())


def _fused(layer, h, ya, yb, yc, p, post_params, cos2, sin2, sinks, mix_params, casts, seq_len):
    n = h.shape[0]
    n_blocks = n // ROWS_MIX
    merged = lambda i: jnp.minimum(i, n_blocks - 1)
    mixed = lambda i: jnp.maximum(i - 1, 0)
    rb = lambda w_: pl.BlockSpec((ROWS_MIX, w_), lambda i: (mixed(i), 0))
    cast_in, cast_out, cast_shapes = _cast_specs(casts, n_blocks)
    return pl.pallas_call(
        functools.partial(_fused_kernel, layer, seq_len // ROWS_MIX, n_blocks),
        grid=(n_blocks + 1,),
        in_specs=[pl.BlockSpec(memory_space=pltpu.SMEM)]
        + _post_specs(layer - 1, merged)
        + [rb(LANES), rb(LANES)] + _mix_param_specs(layer) + cast_in,
        out_specs=[pl.BlockSpec((ROWS_MIX, D_MODEL), lambda i: (merged(i), 0)),
                   rb(A_WIDTH), rb(B_WIDTH), rb(C_WIDTH)] + cast_out,
        out_shape=[jax.ShapeDtypeStruct((n, D_MODEL), F32)] + _mix_out_shapes(n) + cast_shapes,
        scratch_shapes=[pltpu.VMEM((ROWS_MIX, D_MODEL), F32)] + _MIX_SCRATCH,
        compiler_params=pltpu.CompilerParams(
            dimension_semantics=("arbitrary",), vmem_limit_bytes=VMEM_LIMIT),
        name="fused",
    )(sinks, h, ya, yb, *yc, p, *post_params, cos2, sin2, *mix_params, *[c[0] for c in casts])


def kernel(x, p, positions, attn_norm_g, w_in, gmlp_ln_g, gmlp_ln_b, gmlp_ws, gmlp_bs, q_norm_g, k_norm_g, sinks, ssm_a_re, ssm_a_im, ssm_log_dt, ssm_b_re, ssm_b_im, ssm_c_re, ssm_c_im, ssm_d, glu_w1, glu_w2, mix_out_g, w_out, mlp_norm_g, w_ff1, w_ff2, ple_norm_g, w_ple_gate, w_ple_proj):
    bsz, seq_len, _ = x.shape
    depth = w_in.shape[0]
    n = bsz * seq_len

    inv = 1.0 / (ROPE_THETA ** (jnp.arange(0, HEAD_DIM, 2, dtype=F32) / HEAD_DIM))
    ang = positions.astype(F32).reshape(n, 1) * inv
    cos2 = jnp.tile(jnp.cos(ang), (1, 4))
    sin2 = jnp.tile(jnp.concatenate([-jnp.sin(ang), jnp.sin(ang)], axis=1), (1, 2))

    seg = np.arange(256) // HEAD_DIM
    pm = jnp.asarray((seg[:, None] == seg[None, :]) / HEAD_DIM, dtype=BF16)

    rows = lambda v: v.reshape(depth, 1, -1).astype(F32)
    g_attn, g_mlp, g_ple = rows(attn_norm_g), rows(mlp_norm_g), rows(ple_norm_g)
    ln_g, ln_b = rows(gmlp_ln_g), rows(gmlp_ln_b)
    mg_a = rows(mix_out_g[:, :A_WIDTH])
    mg_b = rows(mix_out_g[:, A_WIDTH:A_WIDTH + B_WIDTH])
    mg_c = rows(mix_out_g[:, A_WIDTH + B_WIDTH:])
    qg = rows(jnp.tile(q_norm_g, (1, B_Q_HEADS)))
    kg = rows(jnp.tile(k_norm_g, (1, B_KV_HEADS)))
    bs = jnp.repeat(jnp.swapaxes(gmlp_bs, 1, 2), HEAD_DIM, axis=2)
    w_a = w_in[:, :, :2 * A_WIDTH].reshape(depth, D_MODEL, A_HEADS, 2, HEAD_DIM)
    w_a = jnp.swapaxes(w_a, 2, 3).reshape(depth, D_MODEL, 2 * A_WIDTH)
    w_in_a, w_in_rest0 = w_a.astype(BF16), w_in[0, :, 2 * A_WIDTH:].astype(BF16)
    causal = np.tril(np.ones((CHUNK, CHUNK), dtype=bool))
    ws_b = jnp.swapaxes(jnp.where(causal, gmlp_ws, 0.0), 1, 2).reshape(
        depth, CHUNK, A_HEADS * CHUNK).astype(BF16)
    w_proj_b = w_ple_proj.astype(BF16)
    glu1_b, glu2_b = glu_w1.astype(BF16), glu_w2.astype(BF16)
    p2 = p.reshape(depth, n, PLE_DIM)

    kd, wb, wct, astep, atab = _tables(
        rows(ssm_a_re), rows(ssm_a_im), rows(jnp.repeat(ssm_log_dt, C_STATE, axis=1)),
        jnp.transpose(ssm_b_re, (0, 3, 1, 2)).reshape(depth, C_GROUP, N_STATE),
        jnp.transpose(ssm_b_im, (0, 3, 1, 2)).reshape(depth, C_GROUP, N_STATE),
        jnp.transpose(ssm_c_re, (0, 2, 1, 3)).reshape(depth, C_GROUP, N_STATE),
        jnp.transpose(ssm_c_im, (0, 2, 1, 3)).reshape(depth, C_GROUP, N_STATE),
        jnp.tile(ssm_d.reshape(depth, QUADS, 1, QUAD_WIDTH).astype(F32), (1, 1, 1, SSM_T)))

    def casts_for(layer):
        items = [(w_out, layer, 0), (w_ff1, layer, 0), (w_ff2, layer, 0), (w_ple_gate, layer, 0)]
        if layer + 1 < depth:
            items.append((w_in, layer + 1, 2 * A_WIDTH))
        return items

    def mix_params(w_in_rest):
        return (g_attn, w_in_a, w_in_rest, ws_b, bs, ln_g, ln_b, mg_a, qg, kg, mg_b, pm)

    def post_params(cast):
        w_out_b, w_ff1_b, w_ff2_b, w_gate_b = cast[:4]
        return (glu1_b, glu2_b, mg_c, w_out_b, g_mlp, w_ff1_b, w_ff2_b, g_ple, w_gate_b, w_proj_b)

    h = x.reshape(n, D_MODEL)
    ya, yb, zc, *cast = _mix(0, h, cos2, sin2, sinks, mix_params(w_in_rest0), casts_for(0),
                             seq_len)
    for i in range(depth):
        yc = _ssm(i, zc, kd, wb, wct, astep, atab, seq_len)
        if i + 1 < depth:
            h, ya, yb, zc, *cast = _fused(i + 1, h, ya, yb, yc, p2, post_params(cast), cos2, sin2,
                                          sinks, mix_params(cast[4]), casts_for(i + 1), seq_len)
        else:
            h = _post(i, h, ya, yb, yc, p2, post_params(cast))
    return h.reshape(bsz, seq_len, D_MODEL)
```

```python
import functools
import math

import numpy as np
import jax
import jax.numpy as jnp
from jax import lax
from jax.experimental import pallas as pl
from jax.experimental.pallas import tpu as pltpu

F32 = jnp.float32
BF16 = jnp.bfloat16

D_MODEL = 1024
HEAD_DIM = 64
A_WIDTH = 256
A_HEADS = 4
CHUNK = 128
B_WIDTH = 512
B_Q_HEADS = 8
B_KV_HEADS = 2
WINDOW = 128
ROPE_THETA = 10000.0
C_WIDTH = 256
C_GROUP = 16
C_GROUPS = 16
C_STATE = 64
N_STATE = C_GROUPS * C_STATE
IN_COLS = 1536
D_FF = 4096
PLE_DIM = 256
EPS = 1e-6
NEG_BIG = -1e30
LANES = 128

COL_A = 0
COL_Q = 512
COL_K = 1024
COL_V = 1152
COL_C = 1280

ROWS_PROJ = 512
ROWS_MIX = 512
ROWS_FIRST = 1024
ROWS_SSM = 2048
SSM_T = 8
SSM_N = ROWS_SSM // SSM_T
QUAD_GROUPS = 4
QUADS = C_GROUPS // QUAD_GROUPS
QUAD_WIDTH = QUAD_GROUPS * C_GROUP
QUAD_STATE = 2 * QUAD_GROUPS * C_STATE
PACKED = SSM_T * QUAD_WIDTH
SUBLANES = 8
SCAN_STEPS = int(math.log2(SUBLANES))
FF_SPLIT = 8
VMEM_LIMIT = 60 * 1024 * 1024


def _const_spec(shape):
    nd = len(shape)
    return pl.BlockSpec(shape, lambda *_: (0,) * nd, pipeline_mode=pl.Buffered(1))


def _layer_spec(shape, layer):
    nd = len(shape)
    return pl.BlockSpec((None,) + tuple(shape), lambda *_: (layer,) + (0,) * nd,
                        pipeline_mode=pl.Buffered(1))


def _rms(x, g):
    ms = jnp.mean(x * x, axis=-1, keepdims=True)
    return x * lax.rsqrt(ms + EPS) * g


def _dot(a, b):
    return jnp.dot(a, b, preferred_element_type=F32)


def _dot_nt(a, b):
    return lax.dot_general(a, b, (((1,), (1,)), ((), ())), preferred_element_type=F32)


def _split(x):
    hi = x.astype(BF16)
    return hi, (x - hi.astype(F32)).astype(BF16)


def _seg_mean(x, pmat):
    hi, lo = _split(x)
    return _dot(hi, pmat) + _dot(lo, pmat)


def _cmul(ar, ai, br, bi):
    return ar * br - ai * bi, ar * bi + ai * br


def _rope(x, cos, sin_signed):
    width = x.shape[-1]
    lane = lax.broadcasted_iota(jnp.int32, x.shape, 1)
    first_half = (lane % HEAD_DIM) < (HEAD_DIM // 2)
    partner = jnp.where(first_half,
                        pltpu.roll(x, width - HEAD_DIM // 2, 1),
                        pltpu.roll(x, HEAD_DIM // 2, 1))
    return x * cos + partner * sin_signed


def _dup_heads(x):
    low = lax.broadcasted_iota(jnp.int32, x.shape, 1) < HEAD_DIM
    sw = pltpu.roll(x, HEAD_DIM, 1)
    return [jnp.where(low, x, sw).astype(BF16), jnp.where(low, sw, x).astype(BF16)]


def _mix_project(h_ref, g_ref, w_refs, z_ref):
    wa_ref, wr_ref = w_refs
    xn = _rms(h_ref[...], g_ref[...]).astype(BF16)
    z_ref[:, COL_A:COL_Q] = _dot(xn, wa_ref[...])
    z_ref[:, COL_Q:] = _dot(xn, wr_ref[...])


def _mix_mixers(layer, first_block, sinks_ref, cos_ref, sin_ref, ws_ref, bs_ref, lng_ref, lnb_ref,
                mga_ref, qg_ref, kg_ref, mgb_ref, pm_ref, ya_ref, yb_ref, zc_ref,
                z_ref, kprev_ref, vprev_ref, rows=ROWS_MIX):
    nq = rows // WINDOW
    zc_ref[...] = z_ref[:, COL_C:COL_C + C_WIDTH]
    cos, sin = cos_ref[...], sin_ref[...]
    pm = pm_ref[...]
    pm_kv = pm[0:LANES, 0:LANES]

    k_raw = z_ref[:,COL_K:COL_K + LANES]
    k_ms = _dot((k_raw * k_raw).astype(BF16), pm_kv)
    yield
    k_cur = _rope(k_raw * lax.rsqrt(k_ms + EPS) * kg_ref[...], cos, sin)
    v_cur = z_ref[:,COL_V:COL_V + LANES]
    k_dup = _dup_heads(jnp.concatenate([kprev_ref[...], k_cur], axis=0))
    v_dup = _dup_heads(jnp.concatenate([vprev_ref[...], v_cur], axis=0))
    kprev_ref[...] = k_cur[rows - WINDOW:, :]
    vprev_ref[...] = v_cur[rows - WINDOW:, :]
    q = z_ref[:,COL_Q:COL_Q + B_WIDTH]
    q_ms = jnp.concatenate(
        [_dot((q[:, s:s + 256] * q[:, s:s + 256]).astype(BF16), pm) for s in (0, 256)], axis=1)
    yield
    cos_q = jnp.concatenate([cos] * 4, axis=1)
    sin_q = jnp.concatenate([sin] * 4, axis=1)
    qr = _rope(q * lax.rsqrt(q_ms + EPS) * qg_ref[...], cos_q, sin_q) * (HEAD_DIM ** -0.5)
    low_q = (lax.broadcasted_iota(jnp.int32, qr.shape, 1) % LANES) < HEAD_DIM
    q_low = jnp.where(low_q, qr, 0.0).astype(BF16)
    q_high = jnp.where(low_q, 0.0, qr).astype(BF16)

    u = jax.nn.gelu(z_ref[:,COL_A:COL_A + A_WIDTH])
    v = jax.nn.gelu(z_ref[:,COL_A + A_WIDTH:COL_A + 2 * A_WIDTH])
    v_mean = _dot(v.astype(BF16), pm)
    yield
    vc = v - v_mean
    v_var = _dot((vc * vc).astype(BF16), pm)
    yield
    vn = (vc * lax.rsqrt(v_var + EPS) * lng_ref[...] + lnb_ref[...]).astype(BF16)
    lane = lax.broadcasted_iota(jnp.int32, (CHUNK, A_WIDTH), 1)
    in_head = [(lane >= h * HEAD_DIM) & (lane < (h + 1) * HEAD_DIM) for h in range(A_HEADS)]
    svs = []
    for c in range(rows // CHUNK):
        vn_c = vn[c * CHUNK:(c + 1) * CHUNK]
        stacked = jnp.concatenate(
            [jnp.where(in_head[h], vn_c, jnp.zeros((), BF16)) for h in range(A_HEADS)], axis=0)
        svs.append(_dot(ws_ref[...], stacked))
    yield
    for c in range(rows // CHUNK):
        chunk = slice(c * CHUNK, (c + 1) * CHUNK)
        ya_ref[chunk, :] = _rms(u[chunk] * (svs[c] + bs_ref[...]), mga_ref[...]).astype(BF16)

    qi = lax.broadcasted_iota(jnp.int32, (4 * WINDOW, WINDOW), 0) % WINDOW
    kc = lax.broadcasted_iota(jnp.int32, (4 * WINDOW, WINDOW), 1)
    from_cur = kc <= qi
    head_slot = lax.broadcasted_iota(jnp.int32, (4 * WINDOW, 1), 0) // WINDOW
    prev_ok = kc >= jnp.where(first_block, WINDOW, 0)
    low_o = lax.broadcasted_iota(jnp.int32, (WINDOW, LANES), 1) < HEAD_DIM
    zero = jnp.zeros((), F32)

    def scores(b, j):
        qrows = slice(b * WINDOW, (b + 1) * WINDOW)
        pair_a = slice((2 * j) * LANES, (2 * j + 1) * LANES)
        pair_b = slice((2 * j + 1) * LANES, (2 * j + 2) * LANES)
        qs = jnp.concatenate([q_low[qrows, pair_a], q_low[qrows, pair_b],
                              q_high[qrows, pair_a], q_high[qrows, pair_b]], axis=0)
        return _dot_nt(qs, k_dup[j][b * WINDOW:(b + 2) * WINDOW])

    units = [(b, j) for b in range(nq) for j in range(B_KV_HEADS)]
    s_next = scores(*units[0])
    yield
    pairs = []
    for u, (b, j) in enumerate(units):
        s = s_next
        if u + 1 < len(units):
            s_next = scores(*units[u + 1])
        s_prev = s[:, 0:WINDOW]
        if b == 0:
            s_prev = jnp.where(prev_ok, s_prev, NEG_BIG)
        s = jnp.where(from_cur, s[:, WINDOW:], s_prev)
        sink = jnp.where(head_slot == 0, sinks_ref[layer, 4 * j],
                         jnp.where(head_slot == 1, sinks_ref[layer, 4 * j + 2],
                                   jnp.where(head_slot == 2, sinks_ref[layer, 4 * j + 1],
                                             sinks_ref[layer, 4 * j + 3])))
        m = jnp.maximum(jnp.max(s, axis=-1, keepdims=True), sink)
        pr = jnp.exp(s - m)
        denom = jnp.sum(pr, axis=-1, keepdims=True) + jnp.exp(sink - m)
        pn = pr * (1.0 / denom)
        p2 = jnp.concatenate([jnp.where(from_cur, zero, pn),
                              jnp.where(from_cur, pn, zero)], axis=1).astype(BF16)
        yield
        o = _dot(p2, v_dup[j][b * WINDOW:(b + 2) * WINDOW])
        yield
        pairs += [jnp.where(low_o, o[0:WINDOW], o[2 * WINDOW:3 * WINDOW]),
                  jnp.where(low_o, o[WINDOW:2 * WINDOW], o[3 * WINDOW:4 * WINDOW])]
        if j == B_KV_HEADS - 1:
            yb = jnp.concatenate(pairs, axis=1)
            yb_ref[b * WINDOW:(b + 1) * WINDOW, :] = _rms(yb, mgb_ref[...]).astype(BF16)
            pairs = []


def _mixer_stages(rows):
    return 6 + 2 * B_KV_HEADS * (rows // WINDOW)


MIXER_STAGES = _mixer_stages(ROWS_MIX)
POST_STAGES = 5 + FF_SPLIT


def _run(*staged):
    total = max(n for _, n in staged)
    done = [0] * len(staged)
    for step in range(1, total + 1):
        for k, (gen, n) in enumerate(staged):
            while done[k] * total < step * n:
                next(gen, None)
                done[k] += 1
    for gen, _ in staged:
        for _ in gen:
            pass


def _cast_specs(items, n_steps):
    step = lambda i: jnp.minimum(i, n_steps - 1)
    in_specs, out_specs, out_shapes = [], [], []
    for arr, layer, col0 in items:
        _, rows, cols = arr.shape
        chunk = rows // n_steps
        in_specs.append(
            pl.BlockSpec((None, chunk, cols), lambda i, layer=layer: (layer, step(i), 0)))
        out_specs.append(pl.BlockSpec((chunk, cols - col0), lambda i: (step(i), 0)))
        out_shapes.append(jax.ShapeDtypeStruct((rows, cols - col0), BF16))
    return in_specs, out_specs, out_shapes


def _cast_chunks(src_refs, dst_refs):
    for src, dst in zip(src_refs, dst_refs):
        x = src[...]
        dst[...] = x[:, x.shape[1] - dst.shape[1]:].astype(BF16)


N_MIXER_IN = 9


def _mix_kernel(layer, blocks_per_seq, sinks_ref, h_ref, cos_ref, sin_ref, g_ref, wa_ref, wr_ref,
                *rest):
    w_ref = (wa_ref, wr_ref)
    mixer_in, rest = rest[:N_MIXER_IN], rest[N_MIXER_IN:]
    n_cast = (len(rest) - 6) // 2
    cast_src, outs, cast_dst = rest[:n_cast], rest[n_cast:n_cast + 3], rest[n_cast + 3:-3]
    mixer_refs = mixer_in + outs
    z_ref, kprev_ref, vprev_ref = rest[-3:]
    first_block = (pl.program_id(0) % blocks_per_seq) == 0

    @pl.when(first_block)
    def _():
        kprev_ref[...] = jnp.zeros_like(kprev_ref)
        vprev_ref[...] = jnp.zeros_like(vprev_ref)

    _cast_chunks(cast_src, cast_dst)
    _mix_project(h_ref, g_ref, w_ref, z_ref)
    rows = z_ref.shape[0]
    _run((_mix_mixers(layer, first_block, sinks_ref, cos_ref, sin_ref, *mixer_refs,
                      z_ref, kprev_ref, vprev_ref, rows=rows), _mixer_stages(rows)))


def _mix_param_specs(layer):
    return [
        _layer_spec((1, D_MODEL), layer),
        _layer_spec((D_MODEL, COL_Q), layer),
        _const_spec((D_MODEL, IN_COLS - COL_Q)),
        _layer_spec((CHUNK, A_HEADS * CHUNK), layer),
        _layer_spec((CHUNK, A_WIDTH), layer),
        _layer_spec((1, A_WIDTH), layer),
        _layer_spec((1, A_WIDTH), layer),
        _layer_spec((1, A_WIDTH), layer),
        _layer_spec((1, B_WIDTH), layer),
        _layer_spec((1, LANES), layer),
        _layer_spec((1, B_WIDTH), layer),
        _const_spec((256, 256)),
    ]


def _mix_out_shapes(n):
    return [jax.ShapeDtypeStruct((n, A_WIDTH), BF16),
            jax.ShapeDtypeStruct((n, B_WIDTH), BF16),
            jax.ShapeDtypeStruct((n, C_WIDTH), F32)]


def _mix_scratch(rows):
    return [
        pltpu.VMEM((rows, IN_COLS), F32),
        pltpu.VMEM((WINDOW, LANES), F32),
        pltpu.VMEM((WINDOW, LANES), F32),
    ]


def _mix(layer, h, cos2, sin2, sinks, mix_params, casts, seq_len):
    n = h.shape[0]
    n_blocks = n // ROWS_FIRST
    rb = lambda w_: pl.BlockSpec((ROWS_FIRST, w_), lambda i: (i, 0))
    cast_in, cast_out, cast_shapes = _cast_specs(casts, n_blocks)
    return pl.pallas_call(
        functools.partial(_mix_kernel, layer, seq_len // ROWS_FIRST),
        grid=(n_blocks,),
        in_specs=[pl.BlockSpec(memory_space=pltpu.SMEM), rb(D_MODEL), rb(LANES), rb(LANES)]
        + _mix_param_specs(layer) + cast_in,
        out_specs=[rb(A_WIDTH), rb(B_WIDTH), rb(C_WIDTH)] + cast_out,
        out_shape=_mix_out_shapes(n) + cast_shapes,
        scratch_shapes=_mix_scratch(ROWS_FIRST),
        compiler_params=pltpu.CompilerParams(
            dimension_semantics=("arbitrary",), vmem_limit_bytes=VMEM_LIMIT),
        name="mix",
    )(sinks, h, cos2, sin2, *mix_params, *[c[0] for c in casts])


def _tables_kernel(are_ref, aim_ref, ldt_ref, bre_ref, bim_ref, cre_ref, cim_ref, d_ref,
                   kt_ref, wb_ref, wct_ref, astep_ref, atab_ref):
    a_re, a_im = are_ref[...], aim_ref[...]
    dt = jnp.exp(ldt_ref[...])
    mag = jnp.exp(a_re * dt)
    lr, li = mag * jnp.cos(a_im * dt), mag * jnp.sin(a_im * dt)
    den = a_re * a_re + a_im * a_im
    fr, fi = _cmul(lr - 1.0, li, a_re / den, -a_im / den)
    bbr, bbi = _cmul(fr, fi, bre_ref[...], bim_ref[...])
    c_re, c_im = cre_ref[...], cim_ref[...]

    row_g = lax.broadcasted_iota(jnp.int32, (QUAD_WIDTH, QUAD_STATE // 2), 0) // C_GROUP
    col_g = lax.broadcasted_iota(jnp.int32, (QUAD_WIDTH, QUAD_STATE // 2), 1) // C_STATE
    same_group = row_g == col_g

    def quad_block(mr, mi, q):
        cols = slice(q * QUAD_STATE // 2, (q + 1) * QUAD_STATE // 2)
        tile = lambda m: jnp.where(same_group, jnp.concatenate([m[:, cols]] * QUAD_GROUPS, axis=0), 0.0)
        return jnp.concatenate([tile(mr), tile(mi)], axis=1)

    pows = [(jnp.ones_like(lr), jnp.zeros_like(li))]
    for _ in range(SSM_T):
        pows.append(_cmul(pows[-1][0], pows[-1][1], lr, li))

    rr = lax.broadcasted_iota(jnp.int32, (PACKED, PACKED), 0)
    cc = lax.broadcasted_iota(jnp.int32, (PACKED, PACKED), 1)
    for q in range(QUADS):
        into_state, from_state, lag_out = [], [], []
        for s in range(SSM_T):
            wr, wi = _cmul(pows[SSM_T - 1 - s][0], pows[SSM_T - 1 - s][1], bbr, bbi)
            into_state.append(quad_block(wr, wi, q))
            mr, mi = _cmul(pows[s + 1][0], pows[s + 1][1], c_re, c_im)
            from_state.append(quad_block(mr, -mi, q))
            mr, mi = _cmul(pows[s][0], pows[s][1], c_re, c_im)
            lag_out.append(quad_block(mr, -mi, q))
        wb_ref[q] = jnp.concatenate(into_state, axis=0).astype(BF16)
        wct_ref[q] = jnp.concatenate(from_state, axis=0).astype(BF16)
        a_hi, a_lo = _split(quad_block(bbr, bbi, q))
        b_hi, b_lo = _split(jnp.concatenate(lag_out, axis=0))
        lags = _dot_nt(a_hi, b_hi) + _dot_nt(a_hi, b_lo) + _dot_nt(a_lo, b_hi)
        lane = lax.broadcasted_iota(jnp.int32, lags.shape, 1)
        k = jnp.concatenate(
            [lags] + [jnp.where(lane >= s * QUAD_WIDTH, pltpu.roll(lags, s * QUAD_WIDTH, 1), 0.0)
                      for s in range(1, SSM_T)], axis=0)
        kt_ref[q] = (k + jnp.where(rr == cc, d_ref[q], 0.0)).astype(BF16)

    half = QUAD_STATE // 2
    for q in range(QUADS):
        src = slice(q * half, (q + 1) * half)
        re_cols, im_cols = pl.ds(q * QUAD_STATE, half), pl.ds(q * QUAD_STATE + half, half)
        sr, si = pows[SSM_T][0][:, src], pows[SSM_T][1][:, src]
        atab_ref[0:1, re_cols] = sr
        atab_ref[0:1, im_cols] = si
        for k in range(SCAN_STEPS):
            span = 1 << k
            astep_ref[pl.ds(k, 1), re_cols] = sr
            astep_ref[pl.ds(k, 1), im_cols] = si
            tr, ti = _cmul(atab_ref[0:span, re_cols], atab_ref[0:span, im_cols], sr, si)
            atab_ref[pl.ds(span, span), re_cols] = tr
            atab_ref[pl.ds(span, span), im_cols] = ti
            sr, si = _cmul(sr, si, sr, si)


def _tables(a_re, a_im, ldt, b_re, b_im, c_re, c_im, d_skip):
    depth = a_re.shape[0]
    vec = lambda w: pl.BlockSpec((None, 1, w), lambda l: (l, 0, 0))
    mat = lambda r, c: pl.BlockSpec((None, r, c), lambda l: (l, 0, 0))
    cube = lambda r, c: pl.BlockSpec((None, QUADS, r, c), lambda l: (l, 0, 0, 0))
    packed = jax.ShapeDtypeStruct((depth, QUADS, PACKED, PACKED), BF16)
    return pl.pallas_call(
        _tables_kernel,
        grid=(depth,),
        in_specs=[vec(N_STATE), vec(N_STATE), vec(N_STATE),
                  mat(C_GROUP, N_STATE), mat(C_GROUP, N_STATE),
                  mat(C_GROUP, N_STATE), mat(C_GROUP, N_STATE), cube(1, PACKED)],
        out_specs=[cube(PACKED, PACKED), cube(PACKED, PACKED), cube(PACKED, PACKED),
                   mat(SCAN_STEPS, 2 * N_STATE), mat(SUBLANES, 2 * N_STATE)],
        out_shape=[packed, packed, packed,
                   jax.ShapeDtypeStruct((depth, SCAN_STEPS, 2 * N_STATE), F32),
                   jax.ShapeDtypeStruct((depth, SUBLANES, 2 * N_STATE), F32)],
        compiler_params=pltpu.CompilerParams(
            dimension_semantics=("arbitrary",), vmem_limit_bytes=VMEM_LIMIT),
        name="ssm_tables",
    )(a_re, a_im, ldt, b_re, b_im, c_re, c_im, d_skip)


def _ssm_kernel(blocks_per_seq, ua_ref, ub_ref, kt_ref, wb_ref, wct_ref, astep_ref, atab_ref,
                ya_ref, yb_ref, inc_ref, sprev_ref, carry_ref):
    @pl.when((pl.program_id(0) % blocks_per_seq) == 0)
    def _():
        carry_ref[...] = jnp.zeros_like(carry_ref)

    low = lax.broadcasted_iota(jnp.int32, (SSM_N, LANES), 1) < QUAD_WIDTH

    halves = [[ref[pl.ds(j, SSM_N, stride=SSM_T), :] for j in range(SSM_T)]
              for ref in (ua_ref, ub_ref)]

    def pack(q):
        src, upper = halves[q // 2], q % 2 == 1
        cols = []
        for m in range(SSM_T // 2):
            even, odd = src[2 * m], src[2 * m + 1]
            if upper:
                even = pltpu.roll(even, QUAD_WIDTH, 1)
            else:
                odd = pltpu.roll(odd, QUAD_WIDTH, 1)
            cols.append(jnp.where(low, even, odd))
        return jnp.concatenate(cols, axis=1).astype(BF16)

    xs = [pack(q) for q in range(QUADS)]

    for q in range(QUADS):
        inc_ref[:, q * QUAD_STATE:(q + 1) * QUAD_STATE] = _dot(xs[q], wb_ref[q])

    n_groups = SSM_N // SUBLANES
    sub = lax.broadcasted_iota(jnp.int32, (n_groups, SUBLANES, LANES), 1)
    sub2 = lax.broadcasted_iota(jnp.int32, (SUBLANES, LANES), 0)
    tiles_per_quad = QUAD_STATE // 2 // LANES
    y_lag = []
    for lt in range(N_STATE // LANES):
        q, part = lt // tiles_per_quad, lt % tiles_per_quad
        re_cols = pl.ds(q * QUAD_STATE + part * LANES, LANES)
        im_cols = pl.ds(q * QUAD_STATE + QUAD_STATE // 2 + part * LANES, LANES)
        xr = inc_ref[:, re_cols].reshape(n_groups, SUBLANES, LANES)
        xi = inc_ref[:, im_cols].reshape(n_groups, SUBLANES, LANES)
        for k in range(SCAN_STEPS):
            sh = 1 << k
            ar = astep_ref[pl.ds(k, 1), re_cols]
            ai = astep_ref[pl.ds(k, 1), im_cols]
            keep = sub >= sh
            sr = jnp.where(keep, pltpu.roll(xr, sh, 1), 0.0)
            si = jnp.where(keep, pltpu.roll(xi, sh, 1), 0.0)
            xr, xi = xr + ar * sr - ai * si, xi + ar * si + ai * sr
        tr = atab_ref[:, re_cols]
        ti = atab_ref[:, im_cols]
        er = jnp.broadcast_to(carry_ref[:, re_cols], (SUBLANES, LANES))
        ei = jnp.broadcast_to(carry_ref[:, im_cols], (SUBLANES, LANES))
        before_r, before_i = [], []
        for g in range(n_groups):
            fr = xr[g] + tr * er - ti * ei
            fi = xi[g] + tr * ei + ti * er
            before_r.append(jnp.where(sub2 == 0, er, pltpu.roll(fr, 1, 0)))
            before_i.append(jnp.where(sub2 == 0, ei, pltpu.roll(fi, 1, 0)))
            er = jnp.broadcast_to(fr[SUBLANES - 1:SUBLANES, :], (SUBLANES, LANES))
            ei = jnp.broadcast_to(fi[SUBLANES - 1:SUBLANES, :], (SUBLANES, LANES))
        sprev_ref[:, re_cols] = jnp.concatenate(before_r, axis=0).astype(BF16)
        sprev_ref[:, im_cols] = jnp.concatenate(before_i, axis=0).astype(BF16)
        carry_ref[:, re_cols] = er[0:1, :]
        carry_ref[:, im_cols] = ei[0:1, :]

        if part == tiles_per_quad - 1:
            y_lag.append(_dot(xs[q], kt_ref[q]))

    ys = [y_lag[q] + _dot_nt(sprev_ref[:, q * QUAD_STATE:(q + 1) * QUAD_STATE], wct_ref[q])
          for q in range(QUADS)]
    for t in range(SSM_T):
        cols = slice((t // 2) * LANES, (t // 2 + 1) * LANES)
        for y_ref, (qa, qb) in ((ya_ref, (0, 1)), (yb_ref, (2, 3))):
            lower, upper = ys[qa][:, cols], ys[qb][:, cols]
            if t % 2 == 1:
                lower = pltpu.roll(lower, QUAD_WIDTH, 1)
            else:
                upper = pltpu.roll(upper, QUAD_WIDTH, 1)
            y_ref[pl.ds(t, SSM_N, stride=SSM_T), :] = jnp.where(low, lower, upper)


def _ssm(layer, zc, kt, wb, wct, astep, atab, seq_len):
    n = zc.shape[0]
    half = lambda c: pl.BlockSpec((ROWS_SSM, LANES), lambda i: (i, c))
    return pl.pallas_call(
        functools.partial(_ssm_kernel, seq_len // ROWS_SSM),
        grid=(n // ROWS_SSM,),
        in_specs=[
            half(0), half(1),
            _layer_spec((QUADS, PACKED, PACKED), layer),
            _layer_spec((QUADS, PACKED, QUAD_STATE), layer),
            _layer_spec((QUADS, PACKED, QUAD_STATE), layer),
            _layer_spec((SCAN_STEPS, 2 * N_STATE), layer),
            _layer_spec((SUBLANES, 2 * N_STATE), layer),
        ],
        out_specs=[pl.BlockSpec((ROWS_SSM, LANES), lambda i: (i, 0)) for _ in range(2)],
        out_shape=[jax.ShapeDtypeStruct((n, LANES), F32) for _ in range(2)],
        scratch_shapes=[
            pltpu.VMEM((SSM_N, 2 * N_STATE), F32),
            pltpu.VMEM((SSM_N, 2 * N_STATE), BF16),
            pltpu.VMEM((1, 2 * N_STATE), F32),
        ],
        compiler_params=pltpu.CompilerParams(
            dimension_semantics=("arbitrary",), vmem_limit_bytes=VMEM_LIMIT),
        name="ssm",
    )(zc, zc, kt, wb, wct, astep, atab)


def _post_stages(h_ref, ya_ref, yb_ref, sa_ref, sb_ref, p_ref, glu1_ref, glu2_ref, mgc_ref,
                 wo_ref, g1_ref, w1_ref, w2_ref, g2_ref, wg_ref, wp_ref, *emit):
    y = jax.nn.gelu(jnp.concatenate([sa_ref[...], sb_ref[...]], axis=1)).astype(BF16)
    glu_a, glu_b = _dot(y, glu1_ref[...]), _dot(y, glu2_ref[...])
    mixed = (_dot(ya_ref[...], wo_ref[0:A_WIDTH, :])
             + _dot(yb_ref[...], wo_ref[A_WIDTH:A_WIDTH + B_WIDTH, :]))
    yield
    yc = _rms(glu_a * jax.nn.sigmoid(glu_b), mgc_ref[...]).astype(BF16)
    mixed = mixed + _dot(yc, wo_ref[A_WIDTH + B_WIDTH:, :])
    yield
    h = h_ref[...] + mixed
    hn = _rms(h, g1_ref[...]).astype(BF16)
    ple = _dot(p_ref[...].astype(BF16), wp_ref[...])
    piece = D_FF // FF_SPLIT
    up = _dot(hn, w1_ref[:, 0:piece])
    yield
    ff = None
    for c in range(FF_SPLIT):
        a = jnp.maximum(up, 0.0)
        down = _dot((a * a).astype(BF16), w2_ref[c * piece:(c + 1) * piece, :])
        if c + 1 < FF_SPLIT:
            up = _dot(hn, w1_ref[:, (c + 1) * piece:(c + 2) * piece])
        yield
        ff = down if ff is None else ff + down
    h = h + ff
    gate = _dot(_rms(h, g2_ref[...]).astype(BF16), wg_ref[...])
    yield
    h = h + jax.nn.sigmoid(gate) * ple
    for fn in emit:
        fn(h)
    yield


N_POST_IN = 16


def _store_to(ref):
    def store(value):
        ref[...] = value
    return store


def _post_kernel(*refs):
    _run((_post_stages(*refs[:N_POST_IN], _store_to(refs[N_POST_IN])), POST_STAGES))


def _post_specs(layer, row_block, rows=ROWS_MIX):
    rb = lambda w: pl.BlockSpec((rows, w), lambda i: (row_block(i), 0))
    return [
        rb(D_MODEL), rb(A_WIDTH), rb(B_WIDTH), rb(LANES), rb(LANES),
        pl.BlockSpec((None, rows, PLE_DIM), lambda i: (layer, row_block(i), 0)),
        _layer_spec((C_WIDTH, C_WIDTH), layer),
        _layer_spec((C_WIDTH, C_WIDTH), layer),
        _layer_spec((1, C_WIDTH), layer),
        _const_spec((D_MODEL, D_MODEL)),
        _layer_spec((1, D_MODEL), layer),
        _const_spec((D_MODEL, D_FF)),
        _const_spec((D_FF, D_MODEL)),
        _layer_spec((1, D_MODEL), layer),
        _const_spec((D_MODEL, D_MODEL)),
        _layer_spec((PLE_DIM, D_MODEL), layer),
    ]


def _post(layer, h, ya, yb, yc, p, post_params):
    n = h.shape[0]
    return pl.pallas_call(
        _post_kernel,
        grid=(n // ROWS_PROJ,),
        in_specs=_post_specs(layer, lambda i: i, ROWS_PROJ),
        out_specs=pl.BlockSpec((ROWS_PROJ, D_MODEL), lambda i: (i, 0)),
        out_shape=jax.ShapeDtypeStruct((n, D_MODEL), F32),
        compiler_params=pltpu.CompilerParams(
            dimension_semantics=("arbitrary",), vmem_limit_bytes=VMEM_LIMIT),
        name="post",
    )(h, ya, yb, *yc, p, *post_params)


def _fused_kernel(layer, blocks_per_seq, n_blocks, sinks_ref, *refs):
    post_refs, refs = refs[:N_POST_IN], refs[N_POST_IN:]
    cos_ref, sin_ref, g_ref = refs[:3]
    w_ref = refs[3:5]
    mixer_refs, refs = refs[5:5 + N_MIXER_IN], refs[5 + N_MIXER_IN:]
    n_cast = (len(refs) - 8) // 2
    cast_src, cast_dst = refs[:n_cast], refs[n_cast + 4:-4]
    h_out_ref, ya_ref, yb_ref, zc_ref = refs[n_cast:n_cast + 4]
    hs_ref, z_ref, kprev_ref, vprev_ref = refs[-4:]
    i = pl.program_id(0)
    first_block = (i % blocks_per_seq) == 1 % blocks_per_seq

    @pl.when((i == 0) | first_block)
    def _():
        kprev_ref[...] = jnp.zeros_like(kprev_ref)
        vprev_ref[...] = jnp.zeros_like(vprev_ref)

    @pl.when(i == 0)
    def _():
        hs_ref[...] = jnp.zeros_like(hs_ref)

    def mixers():
        return (_mix_mixers(layer, first_block, sinks_ref, cos_ref, sin_ref, *mixer_refs,
                            ya_ref, yb_ref, zc_ref, z_ref, kprev_ref, vprev_ref), MIXER_STAGES)

    @pl.when(i < n_blocks)
    def _():
        _cast_chunks(cast_src, cast_dst)
        _mix_project(hs_ref, g_ref, w_ref, z_ref)
        _run((_post_stages(*post_refs, _store_to(h_out_ref), _store_to(hs_ref)), POST_STAGES),
             mixers())

    @pl.when(i == n_blocks)
    def _():
        _mix_project(hs_ref, g_ref, w_ref, z_ref)
        _run(mixers())


def _fused(layer, h, ya, yb, yc, p, post_params, cos2, sin2, sinks, mix_params, casts, seq_len):
    n = h.shape[0]
    n_blocks = n // ROWS_MIX
    merged = lambda i: jnp.minimum(i, n_blocks - 1)
    mixed = lambda i: jnp.maximum(i - 1, 0)
    rb = lambda w_: pl.BlockSpec((ROWS_MIX, w_), lambda i: (mixed(i), 0))
    cast_in, cast_out, cast_shapes = _cast_specs(casts, n_blocks)
    return pl.pallas_call(
        functools.partial(_fused_kernel, layer, seq_len // ROWS_MIX, n_blocks),
        grid=(n_blocks + 1,),
        in_specs=[pl.BlockSpec(memory_space=pltpu.SMEM)]
        + _post_specs(layer - 1, merged)
        + [rb(LANES), rb(LANES)] + _mix_param_specs(layer) + cast_in,
        out_specs=[pl.BlockSpec((ROWS_MIX, D_MODEL), lambda i: (merged(i), 0)),
                   rb(A_WIDTH), rb(B_WIDTH), rb(C_WIDTH)] + cast_out,
        out_shape=[jax.ShapeDtypeStruct((n, D_MODEL), F32)] + _mix_out_shapes(n) + cast_shapes,
        scratch_shapes=[pltpu.VMEM((ROWS_MIX, D_MODEL), F32)] + _mix_scratch(ROWS_MIX),
        compiler_params=pltpu.CompilerParams(
            dimension_semantics=("arbitrary",), vmem_limit_bytes=VMEM_LIMIT),
        name="fused",
    )(sinks, h, ya, yb, *yc, p, *post_params, cos2, sin2, *mix_params, *[c[0] for c in casts])


def kernel(x, p, positions, attn_norm_g, w_in, gmlp_ln_g, gmlp_ln_b, gmlp_ws, gmlp_bs, q_norm_g, k_norm_g, sinks, ssm_a_re, ssm_a_im, ssm_log_dt, ssm_b_re, ssm_b_im, ssm_c_re, ssm_c_im, ssm_d, glu_w1, glu_w2, mix_out_g, w_out, mlp_norm_g, w_ff1, w_ff2, ple_norm_g, w_ple_gate, w_ple_proj):
    bsz, seq_len, _ = x.shape
    depth = w_in.shape[0]
    n = bsz * seq_len

    inv = 1.0 / (ROPE_THETA ** (jnp.arange(0, HEAD_DIM, 2, dtype=F32) / HEAD_DIM))
    ang = positions.astype(F32).reshape(n, 1) * inv
    cos2 = jnp.tile(jnp.cos(ang), (1, 4))
    sin2 = jnp.tile(jnp.concatenate([-jnp.sin(ang), jnp.sin(ang)], axis=1), (1, 2))

    seg = np.arange(256) // HEAD_DIM
    pm = jnp.asarray((seg[:, None] == seg[None, :]) / HEAD_DIM, dtype=BF16)

    rows = lambda v: v.reshape(depth, 1, -1).astype(F32)
    g_attn, g_mlp, g_ple = rows(attn_norm_g), rows(mlp_norm_g), rows(ple_norm_g)
    ln_g, ln_b = rows(gmlp_ln_g), rows(gmlp_ln_b)
    mg_a = rows(mix_out_g[:, :A_WIDTH])
    mg_b = rows(mix_out_g[:, A_WIDTH:A_WIDTH + B_WIDTH])
    mg_c = rows(mix_out_g[:, A_WIDTH + B_WIDTH:])
    qg = rows(jnp.tile(q_norm_g, (1, B_Q_HEADS)))
    kg = rows(jnp.tile(k_norm_g, (1, B_KV_HEADS)))
    bs = jnp.repeat(jnp.swapaxes(gmlp_bs, 1, 2), HEAD_DIM, axis=2)
    w_a = w_in[:, :, :2 * A_WIDTH].reshape(depth, D_MODEL, A_HEADS, 2, HEAD_DIM)
    w_a = jnp.swapaxes(w_a, 2, 3).reshape(depth, D_MODEL, 2 * A_WIDTH)
    w_in_a, w_in_rest0 = w_a.astype(BF16), w_in[0, :, 2 * A_WIDTH:].astype(BF16)
    causal = np.tril(np.ones((CHUNK, CHUNK), dtype=bool))
    ws_b = jnp.swapaxes(jnp.where(causal, gmlp_ws, 0.0), 1, 2).reshape(
        depth, CHUNK, A_HEADS * CHUNK).astype(BF16)
    w_proj_b = w_ple_proj.astype(BF16)
    glu1_b, glu2_b = glu_w1.astype(BF16), glu_w2.astype(BF16)
    p2 = p.reshape(depth, n, PLE_DIM)

    kd, wb, wct, astep, atab = _tables(
        rows(ssm_a_re), rows(ssm_a_im), rows(jnp.repeat(ssm_log_dt, C_STATE, axis=1)),
        jnp.transpose(ssm_b_re, (0, 3, 1, 2)).reshape(depth, C_GROUP, N_STATE),
        jnp.transpose(ssm_b_im, (0, 3, 1, 2)).reshape(depth, C_GROUP, N_STATE),
        jnp.transpose(ssm_c_re, (0, 2, 1, 3)).reshape(depth, C_GROUP, N_STATE),
        jnp.transpose(ssm_c_im, (0, 2, 1, 3)).reshape(depth, C_GROUP, N_STATE),
        jnp.tile(ssm_d.reshape(depth, QUADS, 1, QUAD_WIDTH).astype(F32), (1, 1, 1, SSM_T)))

    def casts_for(layer):
        items = [(w_out, layer, 0), (w_ff1, layer, 0), (w_ff2, layer, 0), (w_ple_gate, layer, 0)]
        if layer + 1 < depth:
            items.append((w_in, layer + 1, 2 * A_WIDTH))
        return items

    def mix_params(w_in_rest):
        return (g_attn, w_in_a, w_in_rest, ws_b, bs, ln_g, ln_b, mg_a, qg, kg, mg_b, pm)

    def post_params(cast):
        w_out_b, w_ff1_b, w_ff2_b, w_gate_b = cast[:4]
        return (glu1_b, glu2_b, mg_c, w_out_b, g_mlp, w_ff1_b, w_ff2_b, g_ple, w_gate_b, w_proj_b)

    h = x.reshape(n, D_MODEL)
    ya, yb, zc, *cast = _mix(0, h, cos2, sin2, sinks, mix_params(w_in_rest0), casts_for(0),
                             seq_len)
    for i in range(depth):
        yc = _ssm(i, zc, kd, wb, wct, astep, atab, seq_len)
        if i + 1 < depth:
            h, ya, yb, zc, *cast = _fused(i + 1, h, ya, yb, yc, p2, post_params(cast), cos2, sin2,
                                          sinks, mix_params(cast[4]), casts_for(i + 1), seq_len)
        else:
            h = _post(i, h, ya, yb, yc, p2, post_params(cast))
    return h.reshape(bsz, seq_len, D_MODEL)
```

```python
import functools
import math

import numpy as np
import jax
import jax.numpy as jnp
from jax import lax
from jax.experimental import pallas as pl
from jax.experimental.pallas import tpu as pltpu

F32 = jnp.float32
BF16 = jnp.bfloat16

D_MODEL = 1024
HEAD_DIM = 64
A_WIDTH = 256
A_HEADS = 4
CHUNK = 128
B_WIDTH = 512
B_Q_HEADS = 8
B_KV_HEADS = 2
WINDOW = 128
ROPE_THETA = 10000.0
C_WIDTH = 256
C_GROUP = 16
C_GROUPS = 16
C_STATE = 64
N_STATE = C_GROUPS * C_STATE
IN_COLS = 1536
D_FF = 4096
PLE_DIM = 256
EPS = 1e-6
NEG_BIG = -1e30
LANES = 128

COL_A = 0
COL_Q = 512
COL_K = 1024
COL_V = 1152
COL_C = 1280

ROWS_PROJ = 512
ROWS_MIX = 512
ROWS_FIRST = 1024
ROWS_SSM = 2048
SSM_T = 8
SSM_N = ROWS_SSM // SSM_T
QUAD_GROUPS = 4
QUADS = C_GROUPS // QUAD_GROUPS
QUAD_WIDTH = QUAD_GROUPS * C_GROUP
QUAD_STATE = 2 * QUAD_GROUPS * C_STATE
PACKED = SSM_T * QUAD_WIDTH
SUBLANES = 8
SCAN_STEPS = int(math.log2(SUBLANES))
FF_SPLIT = 8
VMEM_LIMIT = 60 * 1024 * 1024


def _const_spec(shape):
    nd = len(shape)
    return pl.BlockSpec(shape, lambda *_: (0,) * nd, pipeline_mode=pl.Buffered(1))


def _layer_spec(shape, layer):
    nd = len(shape)
    return pl.BlockSpec((None,) + tuple(shape), lambda *_: (layer,) + (0,) * nd,
                        pipeline_mode=pl.Buffered(1))


def _rms(x, g):
    ms = jnp.mean(x * x, axis=-1, keepdims=True)
    return x * lax.rsqrt(ms + EPS) * g


def _dot(a, b):
    return jnp.dot(a, b, preferred_element_type=F32)


def _dot_nt(a, b):
    return lax.dot_general(a, b, (((1,), (1,)), ((), ())), preferred_element_type=F32)


def _split(x):
    hi = x.astype(BF16)
    return hi, (x - hi.astype(F32)).astype(BF16)


def _cmul(ar, ai, br, bi):
    return ar * br - ai * bi, ar * bi + ai * br


def _rope(x, cos, sin_signed):
    width = x.shape[-1]
    lane = lax.broadcasted_iota(jnp.int32, x.shape, 1)
    first_half = (lane % HEAD_DIM) < (HEAD_DIM // 2)
    partner = jnp.where(first_half,
                        pltpu.roll(x, width - HEAD_DIM // 2, 1),
                        pltpu.roll(x, HEAD_DIM // 2, 1))
    return x * cos + partner * sin_signed


def _dup_heads(x):
    low = lax.broadcasted_iota(jnp.int32, x.shape, 1) < HEAD_DIM
    sw = pltpu.roll(x, HEAD_DIM, 1)
    return [jnp.where(low, x, sw).astype(BF16), jnp.where(low, sw, x).astype(BF16)]


def _mix_project(h_ref, g_ref, w_refs, z_ref):
    wa_ref, wr_ref = w_refs
    xn = _rms(h_ref[...], g_ref[...]).astype(BF16)
    z_ref[:, COL_A:COL_Q] = _dot(xn, wa_ref[...])
    z_ref[:, COL_Q:] = _dot(xn, wr_ref[...])


def _mix_mixers(layer, first_block, sinks_ref, cos_ref, sin_ref, ws_ref, bs_ref, lng_ref, lnb_ref,
                mga_ref, qg_ref, kg_ref, mgb_ref, pm_ref, ya_ref, yb_ref, zc_ref,
                z_ref, kprev_ref, vprev_ref, rows=ROWS_MIX):
    nq = rows // WINDOW
    zc_ref[...] = z_ref[:, COL_C:COL_C + C_WIDTH]
    cos, sin = cos_ref[...], sin_ref[...]
    pm = pm_ref[...]
    pm_kv = pm[0:LANES, 0:LANES]

    k_raw = z_ref[:, COL_K:COL_K + LANES]
    k_ms = _dot((k_raw * k_raw).astype(BF16), pm_kv)
    yield
    k_cur = _rope(k_raw * lax.rsqrt(k_ms + EPS) * kg_ref[...], cos, sin)
    v_cur = z_ref[:, COL_V:COL_V + LANES]
    k_dup = _dup_heads(jnp.concatenate([kprev_ref[...], k_cur], axis=0))
    v_dup = _dup_heads(jnp.concatenate([vprev_ref[...], v_cur], axis=0))
    kprev_ref[...] = k_cur[rows - WINDOW:, :]
    vprev_ref[...] = v_cur[rows - WINDOW:, :]
    q = z_ref[:, COL_Q:COL_Q + B_WIDTH]
    q_ms = jnp.concatenate(
        [_dot((q[:, s:s + 256] * q[:, s:s + 256]).astype(BF16), pm) for s in (0, 256)], axis=1)
    yield
    cos_q = jnp.concatenate([cos] * 4, axis=1)
    sin_q = jnp.concatenate([sin] * 4, axis=1)
    qr = _rope(q * lax.rsqrt(q_ms + EPS) * qg_ref[...], cos_q, sin_q) * (HEAD_DIM ** -0.5)
    low_q = (lax.broadcasted_iota(jnp.int32, qr.shape, 1) % LANES) < HEAD_DIM
    q_low = jnp.where(low_q, qr, 0.0).astype(BF16)
    q_high = jnp.where(low_q, 0.0, qr).astype(BF16)

    u = jax.nn.gelu(z_ref[:, COL_A:COL_A + A_WIDTH])
    v = jax.nn.gelu(z_ref[:, COL_A + A_WIDTH:COL_A + 2 * A_WIDTH])
    v_mean = _dot(v.astype(BF16), pm)
    yield
    vc = v - v_mean
    v_var = _dot((vc * vc).astype(BF16), pm)
    yield
    vn = (vc * lax.rsqrt(v_var + EPS) * lng_ref[...] + lnb_ref[...]).astype(BF16)
    lane = lax.broadcasted_iota(jnp.int32, (CHUNK, A_WIDTH), 1)
    in_head = [(lane >= h * HEAD_DIM) & (lane < (h + 1) * HEAD_DIM) for h in range(A_HEADS)]
    svs = []
    for c in range(rows // CHUNK):
        vn_c = vn[c * CHUNK:(c + 1) * CHUNK]
        stacked = jnp.concatenate(
            [jnp.where(in_head[h], vn_c, jnp.zeros((), BF16)) for h in range(A_HEADS)], axis=0)
        svs.append(_dot(ws_ref[...], stacked))
    yield
    for c in range(rows // CHUNK):
        chunk = slice(c * CHUNK, (c + 1) * CHUNK)
        ya_ref[chunk, :] = _rms(u[chunk] * (svs[c] + bs_ref[...]), mga_ref[...]).astype(BF16)

    qi = lax.broadcasted_iota(jnp.int32, (4 * WINDOW, WINDOW), 0) % WINDOW
    kc = lax.broadcasted_iota(jnp.int32, (4 * WINDOW, WINDOW), 1)
    from_cur = kc <= qi
    head_slot = lax.broadcasted_iota(jnp.int32, (4 * WINDOW, 1), 0) // WINDOW
    prev_ok = kc >= jnp.where(first_block, WINDOW, 0)
    low_o = lax.broadcasted_iota(jnp.int32, (WINDOW, LANES), 1) < HEAD_DIM
    zero = jnp.zeros((), F32)

    def scores(b, j):
        qrows = slice(b * WINDOW, (b + 1) * WINDOW)
        pair_a = slice((2 * j) * LANES, (2 * j + 1) * LANES)
        pair_b = slice((2 * j + 1) * LANES, (2 * j + 2) * LANES)
        qs = jnp.concatenate([q_low[qrows, pair_a], q_low[qrows, pair_b],
                              q_high[qrows, pair_a], q_high[qrows, pair_b]], axis=0)
        return _dot_nt(qs, k_dup[j][b * WINDOW:(b + 2) * WINDOW])

    units = [(b, j) for b in range(nq) for j in range(B_KV_HEADS)]
    s_next = scores(*units[0])
    yield
    pairs = []
    for u, (b, j) in enumerate(units):
        s = s_next
        if u + 1 < len(units):
            s_next = scores(*units[u + 1])
        s_prev = s[:, 0:WINDOW]
        if b == 0:
            s_prev = jnp.where(prev_ok, s_prev, NEG_BIG)
        s = jnp.where(from_cur, s[:, WINDOW:], s_prev)
        sink = jnp.where(head_slot == 0, sinks_ref[layer, 4 * j],
                         jnp.where(head_slot == 1, sinks_ref[layer, 4 * j + 2],
                                   jnp.where(head_slot == 2, sinks_ref[layer, 4 * j + 1],
                                             sinks_ref[layer, 4 * j + 3])))
        m = jnp.maximum(jnp.max(s, axis=-1, keepdims=True), sink)
        pr = jnp.exp(s - m)
        denom = jnp.sum(pr, axis=-1, keepdims=True) + jnp.exp(sink - m)
        pn = pr * (1.0 / denom)
        p2 = jnp.concatenate([jnp.where(from_cur, zero, pn),
                              jnp.where(from_cur, pn, zero)], axis=1).astype(BF16)
        yield
        o = _dot(p2, v_dup[j][b * WINDOW:(b + 2) * WINDOW])
        yield
        pairs += [jnp.where(low_o, o[0:WINDOW], o[2 * WINDOW:3 * WINDOW]),
                  jnp.where(low_o, o[WINDOW:2 * WINDOW], o[3 * WINDOW:4 * WINDOW])]
        if j == B_KV_HEADS - 1:
            yb = jnp.concatenate(pairs, axis=1)
            yb_ref[b * WINDOW:(b + 1) * WINDOW, :] = _rms(yb, mgb_ref[...]).astype(BF16)
            pairs = []


def _mixer_stages(rows):
    return 6 + 2 * B_KV_HEADS * (rows // WINDOW)


MIXER_STAGES = _mixer_stages(ROWS_MIX)
POST_STAGES = 5 + FF_SPLIT


def _run(*staged):
    total = max(n for _, n in staged)
    done = [0] * len(staged)
    for step in range(1, total + 1):
        for k, (gen, n) in enumerate(staged):
            while done[k] * total < step * n:
                next(gen, None)
                done[k] += 1
    for gen, _ in staged:
        for _ in gen:
            pass


def _cast_specs(items, n_steps):
    step = lambda i: jnp.minimum(i, n_steps - 1)
    in_specs, out_specs, out_shapes = [], [], []
    for arr, layer, col0 in items:
        _, rows, cols = arr.shape
        chunk = rows // n_steps
        in_specs.append(
            pl.BlockSpec((None, chunk, cols), lambda i, layer=layer: (layer, step(i), 0)))
        out_specs.append(pl.BlockSpec((chunk, cols - col0), lambda i: (step(i), 0)))
        out_shapes.append(jax.ShapeDtypeStruct((rows, cols - col0), BF16))
    return in_specs, out_specs, out_shapes


def _cast_chunks(src_refs, dst_refs):
    for src, dst in zip(src_refs, dst_refs):
        x = src[...]
        dst[...] = x[:, x.shape[1] - dst.shape[1]:].astype(BF16)


N_MIXER_IN = 9


def _mix_kernel(layer, blocks_per_seq, sinks_ref, h_ref, cos_ref, sin_ref, g_ref, wa_ref, wr_ref,
                *rest):
    w_ref = (wa_ref, wr_ref)
    mixer_in, rest = rest[:N_MIXER_IN], rest[N_MIXER_IN:]
    n_cast = (len(rest) - 6) // 2
    cast_src, outs, cast_dst = rest[:n_cast], rest[n_cast:n_cast + 3], rest[n_cast + 3:-3]
    mixer_refs = mixer_in + outs
    z_ref, kprev_ref, vprev_ref = rest[-3:]
    first_block = (pl.program_id(0) % blocks_per_seq) == 0

    @pl.when(first_block)
    def _():
        kprev_ref[...] = jnp.zeros_like(kprev_ref)
        vprev_ref[...] = jnp.zeros_like(vprev_ref)

    _cast_chunks(cast_src, cast_dst)
    _mix_project(h_ref, g_ref, w_ref, z_ref)
    rows = z_ref.shape[0]
    _run((_mix_mixers(layer, first_block, sinks_ref, cos_ref, sin_ref, *mixer_refs,
                      z_ref, kprev_ref, vprev_ref, rows=rows), _mixer_stages(rows)))


def _mix_param_specs(layer):
    return [
        _layer_spec((1, D_MODEL), layer),
        _layer_spec((D_MODEL, COL_Q), layer),
        _const_spec((D_MODEL, IN_COLS - COL_Q)),
        _layer_spec((CHUNK, A_HEADS * CHUNK), layer),
        _layer_spec((CHUNK, A_WIDTH), layer),
        _layer_spec((1, A_WIDTH), layer),
        _layer_spec((1, A_WIDTH), layer),
        _layer_spec((1, A_WIDTH), layer),
        _layer_spec((1, B_WIDTH), layer),
        _layer_spec((1, LANES), layer),
        _layer_spec((1, B_WIDTH), layer),
        _const_spec((256, 256)),
    ]


def _mix_out_shapes(n):
    return [jax.ShapeDtypeStruct((n, A_WIDTH), BF16),
            jax.ShapeDtypeStruct((n, B_WIDTH), BF16),
            jax.ShapeDtypeStruct((n, C_WIDTH), F32)]


def _mix_scratch(rows):
    return [
        pltpu.VMEM((rows, IN_COLS), F32),
        pltpu.VMEM((WINDOW, LANES), F32),
        pltpu.VMEM((WINDOW, LANES), F32),
    ]


def _mix(layer, h, cos2, sin2, sinks, mix_params, casts, seq_len):
    n = h.shape[0]
    n_blocks = n // ROWS_FIRST
    rb = lambda w_: pl.BlockSpec((ROWS_FIRST, w_), lambda i: (i, 0))
    cast_in, cast_out, cast_shapes = _cast_specs(casts, n_blocks)
    return pl.pallas_call(
        functools.partial(_mix_kernel, layer, seq_len // ROWS_FIRST),
        grid=(n_blocks,),
        in_specs=[pl.BlockSpec(memory_space=pltpu.SMEM), rb(D_MODEL), rb(LANES), rb(LANES)]
        + _mix_param_specs(layer) + cast_in,
        out_specs=[rb(A_WIDTH), rb(B_WIDTH), rb(C_WIDTH)] + cast_out,
        out_shape=_mix_out_shapes(n) + cast_shapes,
        scratch_shapes=_mix_scratch(ROWS_FIRST),
        compiler_params=pltpu.CompilerParams(
            dimension_semantics=("arbitrary",), vmem_limit_bytes=VMEM_LIMIT),
        name="mix",
    )(sinks, h, cos2, sin2, *mix_params, *[c[0] for c in casts])


def _tables_kernel(are_ref, aim_ref, ldt_ref, bre_ref, bim_ref, cre_ref, cim_ref, d_ref,
                   kt_ref, wb_ref, wct_ref, astep_ref, atab_ref):
    a_re, a_im = are_ref[...], aim_ref[...]
    dt = jnp.exp(ldt_ref[...])
    mag = jnp.exp(a_re * dt)
    lr, li = mag * jnp.cos(a_im * dt), mag * jnp.sin(a_im * dt)
    den = a_re * a_re + a_im * a_im
    fr, fi = _cmul(lr - 1.0, li, a_re / den, -a_im / den)
    bbr, bbi = _cmul(fr, fi, bre_ref[...], bim_ref[...])
    c_re, c_im = cre_ref[...], cim_ref[...]

    row_g = lax.broadcasted_iota(jnp.int32, (QUAD_WIDTH, QUAD_STATE // 2), 0) // C_GROUP
    col_g = lax.broadcasted_iota(jnp.int32, (QUAD_WIDTH, QUAD_STATE // 2), 1) // C_STATE
    same_group = row_g == col_g

    def quad_block(mr, mi, q):
        cols = slice(q * QUAD_STATE // 2, (q + 1) * QUAD_STATE // 2)
        tile = lambda m: jnp.where(same_group, jnp.concatenate([m[:, cols]] * QUAD_GROUPS, axis=0), 0.0)
        return jnp.concatenate([tile(mr), tile(mi)], axis=1)

    pows = [(jnp.ones_like(lr), jnp.zeros_like(li))]
    for _ in range(SSM_T):
        pows.append(_cmul(pows[-1][0], pows[-1][1], lr, li))

    rr = lax.broadcasted_iota(jnp.int32, (PACKED, PACKED), 0)
    cc = lax.broadcasted_iota(jnp.int32, (PACKED, PACKED), 1)
    for q in range(QUADS):
        into_state, from_state, lag_out = [], [], []
        for s in range(SSM_T):
            wr, wi = _cmul(pows[SSM_T - 1 - s][0], pows[SSM_T - 1 - s][1], bbr, bbi)
            into_state.append(quad_block(wr, wi, q))
            mr, mi = _cmul(pows[s + 1][0], pows[s + 1][1], c_re, c_im)
            from_state.append(quad_block(mr, -mi, q))
            mr, mi = _cmul(pows[s][0], pows[s][1], c_re, c_im)
            lag_out.append(quad_block(mr, -mi, q))
        wb_ref[q] = jnp.concatenate(into_state, axis=0).astype(BF16)
        wct_ref[q] = jnp.concatenate(from_state, axis=0).astype(BF16)
        a_hi, a_lo = _split(quad_block(bbr, bbi, q))
        b_hi, b_lo = _split(jnp.concatenate(lag_out, axis=0))
        lags = _dot_nt(a_hi, b_hi) + _dot_nt(a_hi, b_lo) + _dot_nt(a_lo, b_hi)
        lane = lax.broadcasted_iota(jnp.int32, lags.shape, 1)
        k = jnp.concatenate(
            [lags] + [jnp.where(lane >= s * QUAD_WIDTH, pltpu.roll(lags, s * QUAD_WIDTH, 1), 0.0)
                      for s in range(1, SSM_T)], axis=0)
        kt_ref[q] = (k + jnp.where(rr == cc, d_ref[q], 0.0)).astype(BF16)

    half = QUAD_STATE // 2
    for q in range(QUADS):
        src = slice(q * half, (q + 1) * half)
        re_cols, im_cols = pl.ds(q * QUAD_STATE, half), pl.ds(q * QUAD_STATE + half, half)
        sr, si = pows[SSM_T][0][:, src], pows[SSM_T][1][:, src]
        atab_ref[0:1, re_cols] = sr
        atab_ref[0:1, im_cols] = si
        for k in range(SCAN_STEPS):
            span = 1 << k
            astep_ref[pl.ds(k, 1), re_cols] = sr
            astep_ref[pl.ds(k, 1), im_cols] = si
            tr, ti = _cmul(atab_ref[0:span, re_cols], atab_ref[0:span, im_cols], sr, si)
            atab_ref[pl.ds(span, span), re_cols] = tr
            atab_ref[pl.ds(span, span), im_cols] = ti
            sr, si = _cmul(sr, si, sr, si)


def _tables(a_re, a_im, ldt, b_re, b_im, c_re, c_im, d_skip):
    depth = a_re.shape[0]
    vec = lambda w: pl.BlockSpec((None, 1, w), lambda l: (l, 0, 0))
    mat = lambda r, c: pl.BlockSpec((None, r, c), lambda l: (l, 0, 0))
    cube = lambda r, c: pl.BlockSpec((None, QUADS, r, c), lambda l: (l, 0, 0, 0))
    packed = jax.ShapeDtypeStruct((depth, QUADS, PACKED, PACKED), BF16)
    return pl.pallas_call(
        _tables_kernel,
        grid=(depth,),
        in_specs=[vec(N_STATE), vec(N_STATE), vec(N_STATE),
                  mat(C_GROUP, N_STATE), mat(C_GROUP, N_STATE),
                  mat(C_GROUP, N_STATE), mat(C_GROUP, N_STATE), cube(1, PACKED)],
        out_specs=[cube(PACKED, PACKED), cube(PACKED, PACKED), cube(PACKED, PACKED),
                   mat(SCAN_STEPS, 2 * N_STATE), mat(SUBLANES, 2 * N_STATE)],
        out_shape=[packed, packed, packed,
                   jax.ShapeDtypeStruct((depth, SCAN_STEPS, 2 * N_STATE), F32),
                   jax.ShapeDtypeStruct((depth, SUBLANES, 2 * N_STATE), F32)],
        compiler_params=pltpu.CompilerParams(
            dimension_semantics=("arbitrary",), vmem_limit_bytes=VMEM_LIMIT),
        name="ssm_tables",
    )(a_re, a_im, ldt, b_re, b_im, c_re, c_im, d_skip)


def _ssm_kernel(blocks_per_seq, ua_ref, ub_ref, kt_ref, wb_ref, wct_ref, astep_ref, atab_ref,
                ya_ref, yb_ref, inc_ref, sprev_ref, carry_ref):
    @pl.when((pl.program_id(0) % blocks_per_seq) == 0)
    def _():
        carry_ref[...] = jnp.zeros_like(carry_ref)

    low = lax.broadcasted_iota(jnp.int32, (SSM_N, LANES), 1) < QUAD_WIDTH

    halves = [[ref[pl.ds(j, SSM_N, stride=SSM_T), :] for j in range(SSM_T)]
              for ref in (ua_ref, ub_ref)]

    def pack(q):
        src, upper = halves[q // 2], q % 2 == 1
        cols = []
        for m in range(SSM_T // 2):
            even, odd = src[2 * m], src[2 * m + 1]
            if upper:
                even = pltpu.roll(even, QUAD_WIDTH, 1)
            else:
                odd = pltpu.roll(odd, QUAD_WIDTH, 1)
            cols.append(jnp.where(low, even, odd))
        return jnp.concatenate(cols, axis=1).astype(BF16)

    xs = [pack(q) for q in range(QUADS)]

    for q in range(QUADS):
        inc_ref[:, q * QUAD_STATE:(q + 1) * QUAD_STATE] = _dot(xs[q], wb_ref[q])

    n_groups = SSM_N // SUBLANES
    sub = lax.broadcasted_iota(jnp.int32, (n_groups, SUBLANES, LANES), 1)
    sub2 = lax.broadcasted_iota(jnp.int32, (SUBLANES, LANES), 0)
    tiles_per_quad = QUAD_STATE // 2 // LANES
    y_lag = []
    for lt in range(N_STATE // LANES):
        q, part = lt // tiles_per_quad, lt % tiles_per_quad
        re_cols = pl.ds(q * QUAD_STATE + part * LANES, LANES)
        im_cols = pl.ds(q * QUAD_STATE + QUAD_STATE // 2 + part * LANES, LANES)
        xr = inc_ref[:, re_cols].reshape(n_groups, SUBLANES, LANES)
        xi = inc_ref[:, im_cols].reshape(n_groups, SUBLANES, LANES)
        for k in range(SCAN_STEPS):
            sh = 1 << k
            ar = astep_ref[pl.ds(k, 1), re_cols]
            ai = astep_ref[pl.ds(k, 1), im_cols]
            keep = sub >= sh
            sr = jnp.where(keep, pltpu.roll(xr, sh, 1), 0.0)
            si = jnp.where(keep, pltpu.roll(xi, sh, 1), 0.0)
            xr, xi = xr + ar * sr - ai * si, xi + ar * si + ai * sr
        tr = atab_ref[:, re_cols]
        ti = atab_ref[:, im_cols]
        er = jnp.broadcast_to(carry_ref[:, re_cols], (SUBLANES, LANES))
        ei = jnp.broadcast_to(carry_ref[:, im_cols], (SUBLANES, LANES))
        before_r, before_i = [], []
        for g in range(n_groups):
            fr = xr[g] + tr * er - ti * ei
            fi = xi[g] + tr * ei + ti * er
            before_r.append(jnp.where(sub2 == 0, er, pltpu.roll(fr, 1, 0)))
            before_i.append(jnp.where(sub2 == 0, ei, pltpu.roll(fi, 1, 0)))
            er = jnp.broadcast_to(fr[SUBLANES - 1:SUBLANES, :], (SUBLANES, LANES))
            ei = jnp.broadcast_to(fi[SUBLANES - 1:SUBLANES, :], (SUBLANES, LANES))
        sprev_ref[:, re_cols] = jnp.concatenate(before_r, axis=0).astype(BF16)
        sprev_ref[:, im_cols] = jnp.concatenate(before_i, axis=0).astype(BF16)
        carry_ref[:, re_cols] = er[0:1, :]
        carry_ref[:, im_cols] = ei[0:1, :]

        if part == tiles_per_quad - 1:
            y_lag.append(_dot(xs[q], kt_ref[q]))

    ys = [y_lag[q] + _dot_nt(sprev_ref[:, q * QUAD_STATE:(q + 1) * QUAD_STATE], wct_ref[q])
          for q in range(QUADS)]
    for t in range(SSM_T):
        cols = slice((t // 2) * LANES, (t // 2 + 1) * LANES)
        for y_ref, (qa, qb) in ((ya_ref, (0, 1)), (yb_ref, (2, 3))):
            lower, upper = ys[qa][:, cols], ys[qb][:, cols]
            if t % 2 == 1:
                lower = pltpu.roll(lower, QUAD_WIDTH, 1)
            else:
                upper = pltpu.roll(upper, QUAD_WIDTH, 1)
            y_ref[pl.ds(t, SSM_N, stride=SSM_T), :] = jnp.where(low, lower, upper)


def _ssm(layer, zc, kt, wb, wct, astep, atab, seq_len):
    n = zc.shape[0]
    half = lambda c: pl.BlockSpec((ROWS_SSM, LANES), lambda i: (i, c))
    return pl.pallas_call(
        functools.partial(_ssm_kernel, seq_len // ROWS_SSM),
        grid=(n // ROWS_SSM,),
        in_specs=[
            half(0), half(1),
            _layer_spec((QUADS, PACKED, PACKED), layer),
            _layer_spec((QUADS, PACKED, QUAD_STATE), layer),
            _layer_spec((QUADS, PACKED, QUAD_STATE), layer),
            _layer_spec((SCAN_STEPS, 2 * N_STATE), layer),
            _layer_spec((SUBLANES, 2 * N_STATE), layer),
        ],
        out_specs=[pl.BlockSpec((ROWS_SSM, LANES), lambda i: (i, 0)) for _ in range(2)],
        out_shape=[jax.ShapeDtypeStruct((n, LANES), F32) for _ in range(2)],
        scratch_shapes=[
            pltpu.VMEM((SSM_N, 2 * N_STATE), F32),
            pltpu.VMEM((SSM_N, 2 * N_STATE), BF16),
            pltpu.VMEM((1, 2 * N_STATE), F32),
        ],
        compiler_params=pltpu.CompilerParams(
            dimension_semantics=("arbitrary",), vmem_limit_bytes=VMEM_LIMIT),
        name="ssm",
    )(zc, zc, kt, wb, wct, astep, atab)


def _post_stages(h_ref, ya_ref, yb_ref, sa_ref, sb_ref, p_ref, glu1_ref, glu2_ref, mgc_ref,
                 wo_ref, g1_ref, w1_ref, w2_ref, g2_ref, wg_ref, wp_ref, *emit):
    y = jax.nn.gelu(jnp.concatenate([sa_ref[...], sb_ref[...]], axis=1)).astype(BF16)
    glu_a, glu_b = _dot(y, glu1_ref[...]), _dot(y, glu2_ref[...])
    mixed = (_dot(ya_ref[...], wo_ref[0:A_WIDTH, :])
             + _dot(yb_ref[...], wo_ref[A_WIDTH:A_WIDTH + B_WIDTH, :]))
    yield
    yc = _rms(glu_a * jax.nn.sigmoid(glu_b), mgc_ref[...]).astype(BF16)
    mixed = mixed + _dot(yc, wo_ref[A_WIDTH + B_WIDTH:, :])
    yield
    h = h_ref[...] + mixed
    hn = _rms(h, g1_ref[...]).astype(BF16)
    ple = _dot(p_ref[...].astype(BF16), wp_ref[...])
    piece = D_FF // FF_SPLIT
    up = _dot(hn, w1_ref[:, 0:piece])
    yield
    ff = None
    for c in range(FF_SPLIT):
        a = jnp.maximum(up, 0.0)
        down = _dot((a * a).astype(BF16), w2_ref[c * piece:(c + 1) * piece, :])
        if c + 1 < FF_SPLIT:
            up = _dot(hn, w1_ref[:, (c + 1) * piece:(c + 2) * piece])
        yield
        ff = down if ff is None else ff + down
    h = h + ff
    gate = _dot(_rms(h, g2_ref[...]).astype(BF16), wg_ref[...])
    yield
    h = h + jax.nn.sigmoid(gate) * ple
    for fn in emit:
        fn(h)
    yield


N_POST_IN = 16


def _store_to(ref):
    def store(value):
        ref[...] = value
    return store


def _post_kernel(*refs):
    _run((_post_stages(*refs[:N_POST_IN], _store_to(refs[N_POST_IN])), POST_STAGES))


def _post_specs(layer, row_block, rows=ROWS_MIX):
    rb = lambda w: pl.BlockSpec((rows, w), lambda i: (row_block(i), 0))
    return [
        rb(D_MODEL), rb(A_WIDTH), rb(B_WIDTH), rb(LANES), rb(LANES),
        pl.BlockSpec((None, rows, PLE_DIM), lambda i: (layer, row_block(i), 0)),
        _layer_spec((C_WIDTH, C_WIDTH), layer),
        _layer_spec((C_WIDTH, C_WIDTH), layer),
        _layer_spec((1, C_WIDTH), layer),
        _const_spec((D_MODEL, D_MODEL)),
        _layer_spec((1, D_MODEL), layer),
        _const_spec((D_MODEL, D_FF)),
        _const_spec((D_FF, D_MODEL)),
        _layer_spec((1, D_MODEL), layer),
        _const_spec((D_MODEL, D_MODEL)),
        _layer_spec((PLE_DIM, D_MODEL), layer),
    ]


def _post(layer, h, ya, yb, yc, p, post_params):
    n = h.shape[0]
    return pl.pallas_call(
        _post_kernel,
        grid=(n // ROWS_PROJ,),
        in_specs=_post_specs(layer, lambda i: i, ROWS_PROJ),
        out_specs=pl.BlockSpec((ROWS_PROJ, D_MODEL), lambda i: (i, 0)),
        out_shape=jax.ShapeDtypeStruct((n, D_MODEL), F32),
        compiler_params=pltpu.CompilerParams(
            dimension_semantics=("arbitrary",), vmem_limit_bytes=VMEM_LIMIT),
        name="post",
    )(h, ya, yb, *yc, p, *post_params)


def _fused_kernel(layer, blocks_per_seq, n_blocks, sinks_ref, *refs):
    post_refs, refs = refs[:N_POST_IN], refs[N_POST_IN:]
    cos_ref, sin_ref, g_ref = refs[:3]
    w_ref = refs[3:5]
    mixer_refs, refs = refs[5:5 + N_MIXER_IN], refs[5 + N_MIXER_IN:]
    n_cast = (len(refs) - 8) // 2
    cast_src, cast_dst = refs[:n_cast], refs[n_cast + 4:-4]
    h_out_ref, ya_ref, yb_ref, zc_ref = refs[n_cast:n_cast + 4]
    hs_ref, z_ref, kprev_ref, vprev_ref = refs[-4:]
    i = pl.program_id(0)
    first_block = (i % blocks_per_seq) == 1 % blocks_per_seq

    @pl.when((i == 0) | first_block)
    def _():
        kprev_ref[...] = jnp.zeros_like(kprev_ref)
        vprev_ref[...] = jnp.zeros_like(vprev_ref)

    @pl.when(i == 0)
    def _():
        hs_ref[...] = jnp.zeros_like(hs_ref)

    def mixers():
        return (_mix_mixers(layer, first_block, sinks_ref, cos_ref, sin_ref, *mixer_refs,
                            ya_ref, yb_ref, zc_ref, z_ref, kprev_ref, vprev_ref), MIXER_STAGES)

    @pl.when(i < n_blocks)
    def _():
        _cast_chunks(cast_src, cast_dst)
        _mix_project(hs_ref, g_ref, w_ref, z_ref)
        _run((_post_stages(*post_refs, _store_to(h_out_ref), _store_to(hs_ref)), POST_STAGES),
             mixers())

    @pl.when(i == n_blocks)
    def _():
        _mix_project(hs_ref, g_ref, w_ref, z_ref)
        _run(mixers())


def _fused(layer, h, ya, yb, yc, p, post_params, cos2, sin2, sinks, mix_params, casts, seq_len):
    n = h.shape[0]
    n_blocks = n // ROWS_MIX
    merged = lambda i: jnp.minimum(i, n_blocks - 1)
    mixed = lambda i: jnp.maximum(i - 1, 0)
    rb = lambda w_: pl.BlockSpec((ROWS_MIX, w_), lambda i: (mixed(i), 0))
    cast_in, cast_out, cast_shapes = _cast_specs(casts, n_blocks)
    return pl.pallas_call(
        functools.partial(_fused_kernel, layer, seq_len // ROWS_MIX, n_blocks),
        grid=(n_blocks + 1,),
        in_specs=[pl.BlockSpec(memory_space=pltpu.SMEM)]
        + _post_specs(layer - 1, merged)
        + [rb(LANES), rb(LANES)] + _mix_param_specs(layer) + cast_in,
        out_specs=[pl.BlockSpec((ROWS_MIX, D_MODEL), lambda i: (merged(i), 0)),
                   rb(A_WIDTH), rb(B_WIDTH), rb(C_WIDTH)] + cast_out,
        out_shape=[jax.ShapeDtypeStruct((n, D_MODEL), F32)] + _mix_out_shapes(n) + cast_shapes,
        scratch_shapes=[pltpu.VMEM((ROWS_MIX, D_MODEL), F32)] + _mix_scratch(ROWS_MIX),
        compiler_params=pltpu.CompilerParams(
            dimension_semantics=("arbitrary",), vmem_limit_bytes=VMEM_LIMIT),
        name="fused",
    )(sinks, h, ya, yb, *yc, p, *post_params, cos2, sin2, *mix_params, *[c[0] for c in casts])


def kernel(x, p, positions, attn_norm_g, w_in, gmlp_ln_g, gmlp_ln_b, gmlp_ws, gmlp_bs, q_norm_g, k_norm_g, sinks, ssm_a_re, ssm_a_im, ssm_log_dt, ssm_b_re, ssm_b_im, ssm_c_re, ssm_c_im, ssm_d, glu_w1, glu_w2, mix_out_g, w_out, mlp_norm_g, w_ff1, w_ff2, ple_norm_g, w_ple_gate, w_ple_proj):
    bsz, seq_len, _ = x.shape
    depth = w_in.shape[0]
    n = bsz * seq_len

    inv = 1.0 / (ROPE_THETA ** (jnp.arange(0, HEAD_DIM, 2, dtype=F32) / HEAD_DIM))
    ang = positions.astype(F32).reshape(n, 1) * inv
    cos2 = jnp.tile(jnp.cos(ang), (1, 4))
    sin2 = jnp.tile(jnp.concatenate([-jnp.sin(ang), jnp.sin(ang)], axis=1), (1, 2))

    seg = np.arange(256) // HEAD_DIM
    pm = jnp.asarray((seg[:, None] == seg[None, :]) / HEAD_DIM, dtype=BF16)

    rows = lambda v: v.reshape(depth, 1, -1).astype(F32)
    g_attn, g_mlp, g_ple = rows(attn_norm_g), rows(mlp_norm_g), rows(ple_norm_g)
    ln_g, ln_b = rows(gmlp_ln_g), rows(gmlp_ln_b)
    mg_a = rows(mix_out_g[:, :A_WIDTH])
    mg_b = rows(mix_out_g[:, A_WIDTH:A_WIDTH + B_WIDTH])
    mg_c = rows(mix_out_g[:, A_WIDTH + B_WIDTH:])
    qg = rows(jnp.tile(q_norm_g, (1, B_Q_HEADS)))
    kg = rows(jnp.tile(k_norm_g, (1, B_KV_HEADS)))
    bs = jnp.repeat(jnp.swapaxes(gmlp_bs, 1, 2), HEAD_DIM, axis=2)
    w_a = w_in[:, :, :2 * A_WIDTH].reshape(depth, D_MODEL, A_HEADS, 2, HEAD_DIM)
    w_a = jnp.swapaxes(w_a, 2, 3).reshape(depth, D_MODEL, 2 * A_WIDTH)
    w_in_a, w_in_rest0 = w_a.astype(BF16), w_in[0, :, 2 * A_WIDTH:].astype(BF16)
    causal = np.tril(np.ones((CHUNK, CHUNK), dtype=bool))
    ws_b = jnp.swapaxes(jnp.where(causal, gmlp_ws, 0.0), 1, 2).reshape(
        depth, CHUNK, A_HEADS * CHUNK).astype(BF16)
    w_proj_b = w_ple_proj.astype(BF16)
    glu1_b, glu2_b = glu_w1.astype(BF16), glu_w2.astype(BF16)
    p2 = p.reshape(depth, n, PLE_DIM)

    kd, wb, wct, astep, atab = _tables(
        rows(ssm_a_re), rows(ssm_a_im), rows(jnp.repeat(ssm_log_dt, C_STATE, axis=1)),
        jnp.transpose(ssm_b_re, (0, 3, 1, 2)).reshape(depth, C_GROUP, N_STATE),
        jnp.transpose(ssm_b_im, (0, 3, 1, 2)).reshape(depth, C_GROUP, N_STATE),
        jnp.transpose(ssm_c_re, (0, 2, 1, 3)).reshape(depth, C_GROUP, N_STATE),
        jnp.transpose(ssm_c_im, (0, 2, 1, 3)).reshape(depth, C_GROUP, N_STATE),
        jnp.tile(ssm_d.reshape(depth, QUADS, 1, QUAD_WIDTH).astype(F32), (1, 1, 1, SSM_T)))

    def casts_for(layer):
        items = [(w_out, layer, 0), (w_ff1, layer, 0), (w_ff2, layer, 0), (w_ple_gate, layer, 0)]
        if layer + 1 < depth:
            items.append((w_in, layer + 1, 2 * A_WIDTH))
        return items

    def mix_params(w_in_rest):
        return (g_attn, w_in_a, w_in_rest, ws_b, bs, ln_g, ln_b, mg_a, qg, kg, mg_b, pm)

    def post_params(cast):
        w_out_b, w_ff1_b, w_ff2_b, w_gate_b = cast[:4]
        return (glu1_b, glu2_b, mg_c, w_out_b, g_mlp, w_ff1_b, w_ff2_b, g_ple, w_gate_b, w_proj_b)

    h = x.reshape(n, D_MODEL)
    ya, yb, zc, *cast = _mix(0, h, cos2, sin2, sinks, mix_params(w_in_rest0), casts_for(0),
                             seq_len)
    for i in range(depth):
        yc = _ssm(i, zc, kd, wb, wct, astep, atab, seq_len)
        if i + 1 < depth:
            h, ya, yb, zc, *cast = _fused(i + 1, h, ya, yb, yc, p2, post_params(cast), cos2, sin2,
                                          sinks, mix_params(cast[4]), casts_for(i + 1), seq_len)
        else:
            h = _post(i, h, ya, yb, yc, p2, post_params(cast))
    return h.reshape(bsz, seq_len, D_MODEL)
```

```python
import functools
import math

import numpy as np
import jax
import jax.numpy as jnp
from jax import lax
from jax.experimental import pallas as pl
from jax.experimental.pallas import tpu as pltpu

F32 = jnp.float32
BF16 = jnp.bfloat16

D_MODEL = 1024
HEAD_DIM = 64
A_WIDTH = 256
A_HEADS = 4
CHUNK = 128
B_WIDTH = 512
B_Q_HEADS = 8
B_KV_HEADS = 2
WINDOW = 128
ROPE_THETA = 10000.0
C_WIDTH = 256
C_GROUP = 16
C_GROUPS = 16
C_STATE = 64
N_STATE = C_GROUPS * C_STATE
IN_COLS = 1536
D_FF = 4096
PLE_DIM = 256
EPS = 1e-6
NEG_BIG = -1e30
LANES = 128

COL_A = 0
COL_Q = 512
COL_K = 1024
COL_V = 1152
COL_C = 1280

ROWS_PROJ = 512
ROWS_MIX = 512
ROWS_FIRST = 1024
ROWS_SSM = 2048
SSM_T = 8
SSM_N = ROWS_SSM // SSM_T
QUAD_GROUPS = 4
QUADS = C_GROUPS // QUAD_GROUPS
QUAD_WIDTH = QUAD_GROUPS * C_GROUP
QUAD_STATE = 2 * QUAD_GROUPS * C_STATE
PACKED = SSM_T * QUAD_WIDTH
SUBLANES = 8
SCAN_STEPS = int(math.log2(SUBLANES))
FF_SPLIT = 8
VMEM_LIMIT = 60 * 1024 * 1024


def _const_spec(shape):
    nd = len(shape)
    return pl.BlockSpec(shape, lambda *_: (0,) * nd, pipeline_mode=pl.Buffered(1))


def _layer_spec(shape, layer):
    nd = len(shape)
    return pl.BlockSpec((None,) + tuple(shape), lambda *_: (layer,) + (0,) * nd,
                        pipeline_mode=pl.Buffered(1))


def _rms(x, g):
    ms = jnp.mean(x * x, axis=-1, keepdims=True)
    return x * lax.rsqrt(ms + EPS) * g


def _dot(a, b):
    return jnp.dot(a, b, preferred_element_type=F32)


def _dot_nt(a, b):
    return lax.dot_general(a, b, (((1,), (1,)), ((), ())), preferred_element_type=F32)


def _split(x):
    hi = x.astype(BF16)
    return hi, (x - hi.astype(F32)).astype(BF16)


def _cmul(ar, ai, br, bi):
    return ar * br - ai * bi, ar * bi + ai * br


def _rope(x, cos, sin_signed):
    width = x.shape[-1]
    lane = lax.broadcasted_iota(jnp.int32, x.shape, 1)
    first_half = (lane % HEAD_DIM) < (HEAD_DIM // 2)
    partner = jnp.where(first_half,
                        pltpu.roll(x, width - HEAD_DIM // 2, 1),
                        pltpu.roll(x, HEAD_DIM // 2, 1))
    return x * cos + partner * sin_signed


def _dup_heads(x):
    low = lax.broadcasted_iota(jnp.int32, x.shape, 1) < HEAD_DIM
    sw = pltpu.roll(x, HEAD_DIM, 1)
    return [jnp.where(low, x, sw).astype(BF16), jnp.where(low, sw, x).astype(BF16)]


def _mix_project(h_ref, g_ref, w_refs, z_ref):
    wa_ref, wr_ref = w_refs
    xn = _rms(h_ref[...], g_ref[...]).astype(BF16)
    z_ref[:, COL_A:COL_Q] = _dot(xn, wa_ref[...])
    z_ref[:, COL_Q:] = _dot(xn, wr_ref[...])


def _mix_mixers(layer, first_block, sinks_ref, cos_ref, sin_ref, ws_ref, bs_ref, lng_ref, lnb_ref,
                mga_ref, qg_ref, kg_ref, mgb_ref, pm_ref, ya_ref, yb_ref, zc_ref,
                z_ref, kprev_ref, vprev_ref, rows=ROWS_MIX):
    nq = rows // WINDOW
    zc_ref[...] = z_ref[:, COL_C:COL_C + C_WIDTH]
    cos, sin = cos_ref[...], sin_ref[...]
    pm = pm_ref[...]
    pm_kv = pm[0:LANES, 0:LANES]

    k_raw = z_ref[:, COL_K:COL_K + LANES]
    k_ms = _dot((k_raw * k_raw).astype(BF16), pm_kv)
    yield
    k_cur = _rope(k_raw * lax.rsqrt(k_ms + EPS) * kg_ref[...], cos, sin)
    v_cur = z_ref[:, COL_V:COL_V + LANES]
    k_dup = _dup_heads(jnp.concatenate([kprev_ref[...], k_cur], axis=0))
    v_dup = _dup_heads(jnp.concatenate([vprev_ref[...], v_cur], axis=0))
    kprev_ref[...] = k_cur[rows - WINDOW:, :]
    vprev_ref[...] = v_cur[rows - WINDOW:, :]
    q = z_ref[:, COL_Q:COL_Q + B_WIDTH]
    q_ms = jnp.concatenate(
        [_dot((q[:, s:s + 256] * q[:, s:s + 256]).astype(BF16), pm) for s in (0, 256)], axis=1)
    yield
    cos_q = jnp.concatenate([cos] * 4, axis=1)
    sin_q = jnp.concatenate([sin] * 4, axis=1)
    qr = _rope(q * lax.rsqrt(q_ms + EPS) * qg_ref[...], cos_q, sin_q) * (HEAD_DIM ** -0.5)
    low_q = (lax.broadcasted_iota(jnp.int32, qr.shape, 1) % LANES) < HEAD_DIM
    q_low = jnp.where(low_q, qr, 0.0).astype(BF16)
    q_high = jnp.where(low_q, 0.0, qr).astype(BF16)

    u = jax.nn.gelu(z_ref[:, COL_A:COL_A + A_WIDTH])
    v = jax.nn.gelu(z_ref[:, COL_A + A_WIDTH:COL_A + 2 * A_WIDTH])
    v_mean = _dot(v.astype(BF16), pm)
    yield
    vc = v - v_mean
    v_var = _dot((vc * vc).astype(BF16), pm)
    yield
    vn = (vc * lax.rsqrt(v_var + EPS) * lng_ref[...] + lnb_ref[...]).astype(BF16)
    lane = lax.broadcasted_iota(jnp.int32, (CHUNK, A_WIDTH), 1)
    in_head = [(lane >= h * HEAD_DIM) & (lane < (h + 1) * HEAD_DIM) for h in range(A_HEADS)]
    svs = []
    for c in range(rows // CHUNK):
        vn_c = vn[c * CHUNK:(c + 1) * CHUNK]
        stacked = jnp.concatenate(
            [jnp.where(in_head[h], vn_c, jnp.zeros((), BF16)) for h in range(A_HEADS)], axis=0)
        svs.append(_dot(ws_ref[...], stacked))
    yield
    for c in range(rows // CHUNK):
        chunk = slice(c * CHUNK, (c + 1) * CHUNK)
        ya_ref[chunk, :] = _rms(u[chunk] * (svs[c] + bs_ref[...]), mga_ref[...]).astype(BF16)

    qi = lax.broadcasted_iota(jnp.int32, (4 * WINDOW, WINDOW), 0) % WINDOW
    kc = lax.broadcasted_iota(jnp.int32, (4 * WINDOW, WINDOW), 1)
    from_cur = kc <= qi
    head_slot = lax.broadcasted_iota(jnp.int32, (4 * WINDOW, 1), 0) // WINDOW
    prev_ok = kc >= jnp.where(first_block, WINDOW, 0)
    low_o = lax.broadcasted_iota(jnp.int32, (WINDOW, LANES), 1) < HEAD_DIM
    zero = jnp.zeros((), F32)

    def scores(b, j):
        qrows = slice(b * WINDOW, (b + 1) * WINDOW)
        pair_a = slice((2 * j) * LANES, (2 * j + 1) * LANES)
        pair_b = slice((2 * j + 1) * LANES, (2 * j + 2) * LANES)
        qs = jnp.concatenate([q_low[qrows, pair_a], q_low[qrows, pair_b],
                              q_high[qrows, pair_a], q_high[qrows, pair_b]], axis=0)
        return _dot_nt(qs, k_dup[j][b * WINDOW:(b + 2) * WINDOW])

    units = [(b, j) for b in range(nq) for j in range(B_KV_HEADS)]
    s_next = scores(*units[0])
    yield
    pairs = []
    for u, (b, j) in enumerate(units):
        s = s_next
        if u + 1 < len(units):
            s_next = scores(*units[u + 1])
        s_prev = s[:, 0:WINDOW]
        if b == 0:
            s_prev = jnp.where(prev_ok, s_prev, NEG_BIG)
        s = jnp.where(from_cur, s[:, WINDOW:], s_prev)
        sink = jnp.where(head_slot == 0, sinks_ref[layer, 4 * j],
                         jnp.where(head_slot == 1, sinks_ref[layer, 4 * j + 2],
                                   jnp.where(head_slot == 2, sinks_ref[layer, 4 * j + 1],
                                             sinks_ref[layer, 4 * j + 3])))
        m = jnp.maximum(jnp.max(s, axis=-1, keepdims=True), sink)
        pr = jnp.exp(s - m)
        denom = jnp.sum(pr, axis=-1, keepdims=True) + jnp.exp(sink - m)
        pn = pr * (1.0 / denom)
        p2 = jnp.concatenate([jnp.where(from_cur, zero, pn),
                              jnp.where(from_cur, pn, zero)], axis=1).astype(BF16)
        yield
        o = _dot(p2, v_dup[j][b * WINDOW:(b + 2) * WINDOW])
        yield
        pairs += [jnp.where(low_o, o[0:WINDOW], o[2 * WINDOW:3 * WINDOW]),
                  jnp.where(low_o, o[WINDOW:2 * WINDOW], o[3 * WINDOW:4 * WINDOW])]
        if j == B_KV_HEADS - 1:
            yb = jnp.concatenate(pairs, axis=1)
            yb_ref[b * WINDOW:(b + 1) * WINDOW, :] = _rms(yb, mgb_ref[...]).astype(BF16)
            pairs = []


def _mixer_stages(rows):
    return 6 + 2 * B_KV_HEADS * (rows // WINDOW)


MIXER_STAGES = _mixer_stages(ROWS_MIX)
POST_STAGES = 5 + FF_SPLIT


def _run(*staged):
    total = max(n for _, n in staged)
    done = [0] * len(staged)
    for step in range(1, total + 1):
        for k, (gen, n) in enumerate(staged):
            while done[k] * total < step * n:
                next(gen, None)
                done[k] += 1
    for gen, _ in staged:
        for _ in gen:
            pass


def _cast_specs(items, n_steps):
    step = lambda i: jnp.minimum(i, n_steps - 1)
    in_specs, out_specs, out_shapes = [], [], []
    for arr, layer, col0 in items:
        _, rows, cols = arr.shape
        chunk = rows // n_steps
        in_specs.append(
            pl.BlockSpec((None, chunk, cols), lambda i, layer=layer: (layer, step(i), 0)))
        out_specs.append(pl.BlockSpec((chunk, cols - col0), lambda i: (step(i), 0)))
        out_shapes.append(jax.ShapeDtypeStruct((rows, cols - col0), BF16))
    return in_specs, out_specs, out_shapes


def _cast_chunks(src_refs, dst_refs):
    for src, dst in zip(src_refs, dst_refs):
        x = src[...]
        dst[...] = x[:, x.shape[1] - dst.shape[1]:].astype(BF16)


N_MIXER_IN = 9


def _mix_kernel(layer, blocks_per_seq, sinks_ref, h_ref, cos_ref, sin_ref, g_ref, wa_ref, wr_ref,
                *rest):
    w_ref = (wa_ref, wr_ref)
    mixer_in, rest = rest[:N_MIXER_IN], rest[N_MIXER_IN:]
    n_cast = (len(rest) - 6) // 2
    cast_src, outs, cast_dst = rest[:n_cast], rest[n_cast:n_cast + 3], rest[n_cast + 3:-3]
    mixer_refs = mixer_in + outs
    z_ref, kprev_ref, vprev_ref = rest[-3:]
    first_block = (pl.program_id(0) % blocks_per_seq) == 0

    @pl.when(first_block)
    def _():
        kprev_ref[...] = jnp.zeros_like(kprev_ref)
        vprev_ref[...] = jnp.zeros_like(vprev_ref)

    _cast_chunks(cast_src, cast_dst)
    _mix_project(h_ref, g_ref, w_ref, z_ref)
    rows = z_ref.shape[0]
    _run((_mix_mixers(layer, first_block, sinks_ref, cos_ref, sin_ref, *mixer_refs,
                      z_ref, kprev_ref, vprev_ref, rows=rows), _mixer_stages(rows)))


def _mix_param_specs(layer):
    return [
        _layer_spec((1, D_MODEL), layer),
        _layer_spec((D_MODEL, COL_Q), layer),
        _const_spec((D_MODEL, IN_COLS - COL_Q)),
        _layer_spec((CHUNK, A_HEADS * CHUNK), layer),
        _layer_spec((CHUNK, A_WIDTH), layer),
        _layer_spec((1, A_WIDTH), layer),
        _layer_spec((1, A_WIDTH), layer),
        _layer_spec((1, A_WIDTH), layer),
        _layer_spec((1, B_WIDTH), layer),
        _layer_spec((1, LANES), layer),
        _layer_spec((1, B_WIDTH), layer),
        _const_spec((256, 256)),
    ]


def _mix_out_shapes(n):
    return [jax.ShapeDtypeStruct((n, A_WIDTH), BF16),
            jax.ShapeDtypeStruct((n, B_WIDTH), BF16),
            jax.ShapeDtypeStruct((n, C_WIDTH), F32)]


def _mix_scratch(rows):
    return [
        pltpu.VMEM((rows, IN_COLS), F32),
        pltpu.VMEM((WINDOW, LANES), F32),
        pltpu.VMEM((WINDOW, LANES), F32),
    ]


def _mix(layer, h, cos2, sin2, sinks, mix_params, casts, seq_len):
    n = h.shape[0]
    n_blocks = n // ROWS_FIRST
    rb = lambda w_: pl.BlockSpec((ROWS_FIRST, w_), lambda i: (i, 0))
    cast_in, cast_out, cast_shapes = _cast_specs(casts, n_blocks)
    return pl.pallas_call(
        functools.partial(_mix_kernel, layer, seq_len // ROWS_FIRST),
        grid=(n_blocks,),
        in_specs=[pl.BlockSpec(memory_space=pltpu.SMEM), rb(D_MODEL), rb(LANES), rb(LANES)]
        + _mix_param_specs(layer) + cast_in,
        out_specs=[rb(A_WIDTH), rb(B_WIDTH), rb(C_WIDTH)] + cast_out,
        out_shape=_mix_out_shapes(n) + cast_shapes,
        scratch_shapes=_mix_scratch(ROWS_FIRST),
        compiler_params=pltpu.CompilerParams(
            dimension_semantics=("arbitrary",), vmem_limit_bytes=VMEM_LIMIT),
        name="mix",
    )(sinks, h, cos2, sin2, *mix_params, *[c[0] for c in casts])


def _tables_kernel(are_ref, aim_ref, ldt_ref, bre_ref, bim_ref, cre_ref, cim_ref, d_ref,
                   kt_ref, wb_ref, wct_ref, astep_ref, atab_ref):
    a_re, a_im = are_ref[...], aim_ref[...]
    dt = jnp.exp(ldt_ref[...])
    mag = jnp.exp(a_re * dt)
    lr, li = mag * jnp.cos(a_im * dt), mag * jnp.sin(a_im * dt)
    den = a_re * a_re + a_im * a_im
    fr, fi = _cmul(lr - 1.0, li, a_re / den, -a_im / den)
    bbr, bbi = _cmul(fr, fi, bre_ref[...], bim_ref[...])
    c_re, c_im = cre_ref[...], cim_ref[...]

    row_g = lax.broadcasted_iota(jnp.int32, (QUAD_WIDTH, QUAD_STATE // 2), 0) // C_GROUP
    col_g = lax.broadcasted_iota(jnp.int32, (QUAD_WIDTH, QUAD_STATE // 2), 1) // C_STATE
    same_group = row_g == col_g

    def quad_block(mr, mi, q):
        cols = slice(q * QUAD_STATE // 2, (q + 1) * QUAD_STATE // 2)
        tile = lambda m: jnp.where(same_group, jnp.concatenate([m[:, cols]] * QUAD_GROUPS, axis=0), 0.0)
        return jnp.concatenate([tile(mr), tile(mi)], axis=1)

    pows = [(jnp.ones_like(lr), jnp.zeros_like(li))]
    for _ in range(SSM_T):
        pows.append(_cmul(pows[-1][0], pows[-1][1], lr, li))

    rr = lax.broadcasted_iota(jnp.int32, (PACKED, PACKED), 0)
    cc = lax.broadcasted_iota(jnp.int32, (PACKED, PACKED), 1)
    for q in range(QUADS):
        into_state, from_state, lag_out = [], [], []
        for s in range(SSM_T):
            wr, wi = _cmul(pows[SSM_T - 1 - s][0], pows[SSM_T - 1 - s][1], bbr, bbi)
            into_state.append(quad_block(wr, wi, q))
            mr, mi = _cmul(pows[s + 1][0], pows[s + 1][1], c_re, c_im)
            from_state.append(quad_block(mr, -mi, q))
            mr, mi = _cmul(pows[s][0], pows[s][1], c_re, c_im)
            lag_out.append(quad_block(mr, -mi, q))
        wb_ref[q] = jnp.concatenate(into_state, axis=0).astype(BF16)
        wct_ref[q] = jnp.concatenate(from_state, axis=0).astype(BF16)
        a_hi, a_lo = _split(quad_block(bbr, bbi, q))
        b_hi, b_lo = _split(jnp.concatenate(lag_out, axis=0))
        lags = _dot_nt(a_hi, b_hi) + _dot_nt(a_hi, b_lo) + _dot_nt(a_lo, b_hi)
        lane = lax.broadcasted_iota(jnp.int32, lags.shape, 1)
        k = jnp.concatenate(
            [lags] + [jnp.where(lane >= s * QUAD_WIDTH, pltpu.roll(lags, s * QUAD_WIDTH, 1), 0.0)
                      for s in range(1, SSM_T)], axis=0)
        kt_ref[q] = (k + jnp.where(rr == cc, d_ref[q], 0.0)).astype(BF16)

    half = QUAD_STATE // 2
    for q in range(QUADS):
        src = slice(q * half, (q + 1) * half)
        re_cols, im_cols = pl.ds(q * QUAD_STATE, half), pl.ds(q * QUAD_STATE + half, half)
        sr, si = pows[SSM_T][0][:, src], pows[SSM_T][1][:, src]
        atab_ref[0:1, re_cols] = sr
        atab_ref[0:1, im_cols] = si
        for k in range(SCAN_STEPS):
            span = 1 << k
            astep_ref[pl.ds(k, 1), re_cols] = sr
            astep_ref[pl.ds(k, 1), im_cols] = si
            tr, ti = _cmul(atab_ref[0:span, re_cols], atab_ref[0:span, im_cols], sr, si)
            atab_ref[pl.ds(span, span), re_cols] = tr
            atab_ref[pl.ds(span, span), im_cols] = ti
            sr, si = _cmul(sr, si, sr, si)


def _tables(a_re, a_im, ldt, b_re, b_im, c_re, c_im, d_skip):
    depth = a_re.shape[0]
    vec = lambda w: pl.BlockSpec((None, 1, w), lambda l: (l, 0, 0))
    mat = lambda r, c: pl.BlockSpec((None, r, c), lambda l: (l, 0, 0))
    cube = lambda r, c: pl.BlockSpec((None, QUADS, r, c), lambda l: (l, 0, 0, 0))
    packed = jax.ShapeDtypeStruct((depth, QUADS, PACKED, PACKED), BF16)
    return pl.pallas_call(
        _tables_kernel,
        grid=(depth,),
        in_specs=[vec(N_STATE), vec(N_STATE), vec(N_STATE),
                  mat(C_GROUP, N_STATE), mat(C_GROUP, N_STATE),
                  mat(C_GROUP, N_STATE), mat(C_GROUP, N_STATE), cube(1, PACKED)],
        out_specs=[cube(PACKED, PACKED), cube(PACKED, PACKED), cube(PACKED, PACKED),
                   mat(SCAN_STEPS, 2 * N_STATE), mat(SUBLANES, 2 * N_STATE)],
        out_shape=[packed, packed, packed,
                   jax.ShapeDtypeStruct((depth, SCAN_STEPS, 2 * N_STATE), F32),
                   jax.ShapeDtypeStruct((depth, SUBLANES, 2 * N_STATE), F32)],
        compiler_params=pltpu.CompilerParams(
            dimension_semantics=("arbitrary",), vmem_limit_bytes=VMEM_LIMIT),
        name="ssm_tables",
    )(a_re, a_im, ldt, b_re, b_im, c_re, c_im, d_skip)


def _ssm_kernel(blocks_per_seq, ua_ref, ub_ref, kt_ref, wb_ref, wct_ref, astep_ref, atab_ref,
                ya_ref, yb_ref, inc_ref, sprev_ref, carry_ref):
    @pl.when((pl.program_id(0) % blocks_per_seq) == 0)
    def _():
        carry_ref[...] = jnp.zeros_like(carry_ref)

    low = lax.broadcasted_iota(jnp.int32, (SSM_N, LANES), 1) < QUAD_WIDTH

    halves = [[ref[pl.ds(j, SSM_N, stride=SSM_T), :] for j in range(SSM_T)]
              for ref in (ua_ref, ub_ref)]

    def pack(q):
        src, upper = halves[q // 2], q % 2 == 1
        cols = []
        for m in range(SSM_T // 2):
            even, odd = src[2 * m], src[2 * m + 1]
            if upper:
                even = pltpu.roll(even, QUAD_WIDTH, 1)
            else:
                odd = pltpu.roll(odd, QUAD_WIDTH, 1)
            cols.append(jnp.where(low, even, odd))
        return jnp.concatenate(cols, axis=1).astype(BF16)

    xs = [pack(q) for q in range(QUADS)]

    for q in range(QUADS):
        inc_ref[:, q * QUAD_STATE:(q + 1) * QUAD_STATE] = _dot(xs[q], wb_ref[q])

    n_groups = SSM_N // SUBLANES
    sub = lax.broadcasted_iota(jnp.int32, (n_groups, SUBLANES, LANES), 1)
    sub2 = lax.broadcasted_iota(jnp.int32, (SUBLANES, LANES), 0)
    tiles_per_quad = QUAD_STATE // 2 // LANES
    y_lag = []
    for lt in range(N_STATE // LANES):
        q, part = lt // tiles_per_quad, lt % tiles_per_quad
        re_cols = pl.ds(q * QUAD_STATE + part * LANES, LANES)
        im_cols = pl.ds(q * QUAD_STATE + QUAD_STATE // 2 + part * LANES, LANES)
        xr = inc_ref[:, re_cols].reshape(n_groups, SUBLANES, LANES)
        xi = inc_ref[:, im_cols].reshape(n_groups, SUBLANES, LANES)
        for k in range(SCAN_STEPS):
            sh = 1 << k
            ar = astep_ref[pl.ds(k, 1), re_cols]
            ai = astep_ref[pl.ds(k, 1), im_cols]
            keep = sub >= sh
            sr = jnp.where(keep, pltpu.roll(xr, sh, 1), 0.0)
            si = jnp.where(keep, pltpu.roll(xi, sh, 1), 0.0)
            xr, xi = xr + ar * sr - ai * si, xi + ar * si + ai * sr
        tr = atab_ref[:, re_cols]
        ti = atab_ref[:, im_cols]
        er = jnp.broadcast_to(carry_ref[:, re_cols], (SUBLANES, LANES))
        ei = jnp.broadcast_to(carry_ref[:, im_cols], (SUBLANES, LANES))
        before_r, before_i = [], []
        for g in range(n_groups):
            fr = xr[g] + tr * er - ti * ei
            fi = xi[g] + tr * ei + ti * er
            before_r.append(jnp.where(sub2 == 0, er, pltpu.roll(fr, 1, 0)))
            before_i.append(jnp.where(sub2 == 0, ei, pltpu.roll(fi, 1, 0)))
            er = jnp.broadcast_to(fr[SUBLANES - 1:SUBLANES, :], (SUBLANES, LANES))
            ei = jnp.broadcast_to(fi[SUBLANES - 1:SUBLANES, :], (SUBLANES, LANES))
        sprev_ref[:, re_cols] = jnp.concatenate(before_r, axis=0).astype(BF16)
        sprev_ref[:, im_cols] = jnp.concatenate(before_i, axis=0).astype(BF16)
        carry_ref[:, re_cols] = er[0:1, :]
        carry_ref[:, im_cols] = ei[0:1, :]

        if part == tiles_per_quad - 1:
            y_lag.append(_dot(xs[q], kt_ref[q]))

    ys = [y_lag[q] + _dot_nt(sprev_ref[:, q * QUAD_STATE:(q + 1) * QUAD_STATE], wct_ref[q])
          for q in range(QUADS)]
    for t in range(SSM_T):
        cols = slice((t // 2) * LANES, (t // 2 + 1) * LANES)
        for y_ref, (qa, qb) in ((ya_ref, (0, 1)), (yb_ref, (2, 3))):
            lower, upper = ys[qa][:, cols], ys[qb][:, cols]
            if t % 2 == 1:
                lower = pltpu.roll(lower, QUAD_WIDTH, 1)
            else:
                upper = pltpu.roll(upper, QUAD_WIDTH, 1)
            y_ref[pl.ds(t, SSM_N, stride=SSM_T), :] = jnp.where(low, lower, upper)


def _ssm(layer, zc, kt, wb, wct, astep, atab, seq_len):
    n = zc.shape[0]
    half = lambda c: pl.BlockSpec((ROWS_SSM, LANES), lambda i: (i, c))
    return pl.pallas_call(
        functools.partial(_ssm_kernel, seq_len // ROWS_SSM),
        grid=(n // ROWS_SSM,),
        in_specs=[
            half(0), half(1),
            _layer_spec((QUADS, PACKED, PACKED), layer),
            _layer_spec((QUADS, PACKED, QUAD_STATE), layer),
            _layer_spec((QUADS, PACKED, QUAD_STATE), layer),
            _layer_spec((SCAN_STEPS, 2 * N_STATE), layer),
            _layer_spec((SUBLANES, 2 * N_STATE), layer),
        ],
        out_specs=[pl.BlockSpec((ROWS_SSM, LANES), lambda i: (i, 0)) for _ in range(2)],
        out_shape=[jax.ShapeDtypeStruct((n, LANES), F32) for _ in range(2)],
        scratch_shapes=[
            pltpu.VMEM((SSM_N, 2 * N_STATE), F32),
            pltpu.VMEM((SSM_N, 2 * N_STATE), BF16),
            pltpu.VMEM((1, 2 * N_STATE), F32),
        ],
        compiler_params=pltpu.CompilerParams(
            dimension_semantics=("arbitrary",), vmem_limit_bytes=VMEM_LIMIT),
        name="ssm",
    )(zc, zc, kt, wb, wct, astep, atab)


def _post_stages(h_ref, ya_ref, yb_ref, sa_ref, sb_ref, p_ref, glu1_ref, glu2_ref, mgc_ref,
                 wo_ref, g1_ref, w1_ref, w2_ref, g2_ref, wg_ref, wp_ref, *emit):
    y = jax.nn.gelu(jnp.concatenate([sa_ref[...], sb_ref[...]], axis=1)).astype(BF16)
    glu_a, glu_b = _dot(y, glu1_ref[...]), _dot(y, glu2_ref[...])
    mixed = (_dot(ya_ref[...], wo_ref[0:A_WIDTH, :])
             + _dot(yb_ref[...], wo_ref[A_WIDTH:A_WIDTH + B_WIDTH, :]))
    yield
    yc = _rms(glu_a * jax.nn.sigmoid(glu_b), mgc_ref[...]).astype(BF16)
    mixed = mixed + _dot(yc, wo_ref[A_WIDTH + B_WIDTH:, :])
    yield
    h = h_ref[...] + mixed
    hn = _rms(h, g1_ref[...]).astype(BF16)
    ple = _dot(p_ref[...].astype(BF16), wp_ref[...])
    piece = D_FF // FF_SPLIT
    up = _dot(hn, w1_ref[:, 0:piece])
    yield
    ff = None
    for c in range(FF_SPLIT):
        a = jnp.maximum(up, 0.0)
        down = _dot((a * a).astype(BF16), w2_ref[c * piece:(c + 1) * piece, :])
        if c + 1 < FF_SPLIT:
            up = _dot(hn, w1_ref[:, (c + 1) * piece:(c + 2) * piece])
        yield
        ff = down if ff is None else ff + down
    h = h + ff
    gate = _dot(_rms(h, g2_ref[...]).astype(BF16), wg_ref[...])
    yield
    h = h + jax.nn.sigmoid(gate) * ple
    for fn in emit:
        fn(h)
    yield


N_POST_IN = 16


def _store_to(ref):
    def store(value):
        ref[...] = value
    return store


def _post_kernel(*refs):
    _run((_post_stages(*refs[:N_POST_IN], _store_to(refs[N_POST_IN])), POST_STAGES))


def _post_specs(layer, row_block, rows=ROWS_MIX):
    rb = lambda w: pl.BlockSpec((rows, w), lambda i: (row_block(i), 0))
    return [
        rb(D_MODEL), rb(A_WIDTH), rb(B_WIDTH), rb(LANES), rb(LANES),
        pl.BlockSpec((None, rows, PLE_DIM), lambda i: (layer, row_block(i), 0)),
        _layer_spec((C_WIDTH, C_WIDTH), layer),
        _layer_spec((C_WIDTH, C_WIDTH), layer),
        _layer_spec((1, C_WIDTH), layer),
        _const_spec((D_MODEL, D_MODEL)),
        _layer_spec((1, D_MODEL), layer),
        _const_spec((D_MODEL, D_FF)),
        _const_spec((D_FF, D_MODEL)),
        _layer_spec((1, D_MODEL), layer),
        _const_spec((D_MODEL, D_MODEL)),
        _layer_spec((PLE_DIM, D_MODEL), layer),
    ]


def _post(layer, h, ya, yb, yc, p, post_params):
    n = h.shape[0]
    return pl.pallas_call(
        _post_kernel,
        grid=(n // ROWS_PROJ,),
        in_specs=_post_specs(layer, lambda i: i, ROWS_PROJ),
        out_specs=pl.BlockSpec((ROWS_PROJ, D_MODEL), lambda i: (i, 0)),
        out_shape=jax.ShapeDtypeStruct((n, D_MODEL), F32),
        compiler_params=pltpu.CompilerParams(
            dimension_semantics=("arbitrary",), vmem_limit_bytes=VMEM_LIMIT),
        name="post",
    )(h, ya, yb, *yc, p, *post_params)


def _fused_kernel(layer, blocks_per_seq, n_blocks, sinks_ref, *refs):
    post_refs, refs = refs[:N_POST_IN], refs[N_POST_IN:]
    cos_ref, sin_ref, g_ref = refs[:3]
    w_ref = refs[3:5]
    mixer_refs, refs = refs[5:5 + N_MIXER_IN], refs[5 + N_MIXER_IN:]
    n_cast = (len(refs) - 8) // 2
    cast_src, cast_dst = refs[:n_cast], refs[n_cast + 4:-4]
    h_out_ref, ya_ref, yb_ref, zc_ref = refs[n_cast:n_cast + 4]
    hs_ref, z_ref, kprev_ref, vprev_ref = refs[-4:]
    i = pl.program_id(0)
    first_block = (i % blocks_per_seq) == 1 % blocks_per_seq

    @pl.when((i == 0) | first_block)
    def _():
        kprev_ref[...] = jnp.zeros_like(kprev_ref)
        vprev_ref[...] = jnp.zeros_like(vprev_ref)

    @pl.when(i == 0)
    def _():
        z_ref[...] = jnp.zeros_like(z_ref)

    def mixers():
        return (_mix_mixers(layer, first_block, sinks_ref, cos_ref, sin_ref, *mixer_refs,
                            ya_ref, yb_ref, zc_ref, z_ref, kprev_ref, vprev_ref), MIXER_STAGES)

    def project_for_next_step(h):
        hs_ref[...] = h
        _mix_project(hs_ref, g_ref, w_ref, z_ref)

    @pl.when(i < n_blocks)
    def _():
        _cast_chunks(cast_src, cast_dst)
        _run((_post_stages(*post_refs, _store_to(h_out_ref), project_for_next_step), POST_STAGES),
             mixers())

    @pl.when(i == n_blocks)
    def _():
        _run(mixers())


def _fused(layer, h, ya, yb, yc, p, post_params, cos2, sin2, sinks, mix_params, casts, seq_len):
    n = h.shape[0]
    n_blocks = n // ROWS_MIX
    merged = lambda i: jnp.minimum(i, n_blocks - 1)
    mixed = lambda i: jnp.maximum(i - 1, 0)
    rb = lambda w_: pl.BlockSpec((ROWS_MIX, w_), lambda i: (mixed(i), 0))
    cast_in, cast_out, cast_shapes = _cast_specs(casts, n_blocks)
    return pl.pallas_call(
        functools.partial(_fused_kernel, layer, seq_len // ROWS_MIX, n_blocks),
        grid=(n_blocks + 1,),
        in_specs=[pl.BlockSpec(memory_space=pltpu.SMEM)]
        + _post_specs(layer - 1, merged)
        + [rb(LANES), rb(LANES)] + _mix_param_specs(layer) + cast_in,
        out_specs=[pl.BlockSpec((ROWS_MIX, D_MODEL), lambda i: (merged(i), 0)),
                   rb(A_WIDTH), rb(B_WIDTH), rb(C_WIDTH)] + cast_out,
        out_shape=[jax.ShapeDtypeStruct((n, D_MODEL), F32)] + _mix_out_shapes(n) + cast_shapes,
        scratch_shapes=[pltpu.VMEM((ROWS_MIX, D_MODEL), F32)] + _mix_scratch(ROWS_MIX),
        compiler_params=pltpu.CompilerParams(
            dimension_semantics=("arbitrary",), vmem_limit_bytes=VMEM_LIMIT),
        name="fused",
    )(sinks, h, ya, yb, *yc, p, *post_params, cos2, sin2, *mix_params, *[c[0] for c in casts])


def kernel(x, p, positions, attn_norm_g, w_in, gmlp_ln_g, gmlp_ln_b, gmlp_ws, gmlp_bs, q_norm_g, k_norm_g, sinks, ssm_a_re, ssm_a_im, ssm_log_dt, ssm_b_re, ssm_b_im, ssm_c_re, ssm_c_im, ssm_d, glu_w1, glu_w2, mix_out_g, w_out, mlp_norm_g, w_ff1, w_ff2, ple_norm_g, w_ple_gate, w_ple_proj):
    bsz, seq_len, _ = x.shape
    depth = w_in.shape[0]
    n = bsz * seq_len

    inv = 1.0 / (ROPE_THETA ** (jnp.arange(0, HEAD_DIM, 2, dtype=F32) / HEAD_DIM))
    ang = positions.astype(F32).reshape(n, 1) * inv
    cos2 = jnp.tile(jnp.cos(ang), (1, 4))
    sin2 = jnp.tile(jnp.concatenate([-jnp.sin(ang), jnp.sin(ang)], axis=1), (1, 2))

    seg = np.arange(256) // HEAD_DIM
    pm = jnp.asarray((seg[:, None] == seg[None, :]) / HEAD_DIM, dtype=BF16)

    rows = lambda v: v.reshape(depth, 1, -1).astype(F32)
    g_attn, g_mlp, g_ple = rows(attn_norm_g), rows(mlp_norm_g), rows(ple_norm_g)
    ln_g, ln_b = rows(gmlp_ln_g), rows(gmlp_ln_b)
    mg_a = rows(mix_out_g[:, :A_WIDTH])
    mg_b = rows(mix_out_g[:, A_WIDTH:A_WIDTH + B_WIDTH])
    mg_c = rows(mix_out_g[:, A_WIDTH + B_WIDTH:])
    qg = rows(jnp.tile(q_norm_g, (1, B_Q_HEADS)))
    kg = rows(jnp.tile(k_norm_g, (1, B_KV_HEADS)))
    bs = jnp.repeat(jnp.swapaxes(gmlp_bs, 1, 2), HEAD_DIM, axis=2)
    w_a = w_in[:, :, :2 * A_WIDTH].reshape(depth, D_MODEL, A_HEADS, 2, HEAD_DIM)
    w_a = jnp.swapaxes(w_a, 2, 3).reshape(depth, D_MODEL, 2 * A_WIDTH)
    w_in_a, w_in_rest0 = w_a.astype(BF16), w_in[0, :, 2 * A_WIDTH:].astype(BF16)
    causal = np.tril(np.ones((CHUNK, CHUNK), dtype=bool))
    ws_b = jnp.swapaxes(jnp.where(causal, gmlp_ws, 0.0), 1, 2).reshape(
        depth, CHUNK, A_HEADS * CHUNK).astype(BF16)
    w_proj_b = w_ple_proj.astype(BF16)
    glu1_b, glu2_b = glu_w1.astype(BF16), glu_w2.astype(BF16)
    p2 = p.reshape(depth, n, PLE_DIM)

    kd, wb, wct, astep, atab = _tables(
        rows(ssm_a_re), rows(ssm_a_im), rows(jnp.repeat(ssm_log_dt, C_STATE, axis=1)),
        jnp.transpose(ssm_b_re, (0, 3, 1, 2)).reshape(depth, C_GROUP, N_STATE),
        jnp.transpose(ssm_b_im, (0, 3, 1, 2)).reshape(depth, C_GROUP, N_STATE),
        jnp.transpose(ssm_c_re, (0, 2, 1, 3)).reshape(depth, C_GROUP, N_STATE),
        jnp.transpose(ssm_c_im, (0, 2, 1, 3)).reshape(depth, C_GROUP, N_STATE),
        jnp.tile(ssm_d.reshape(depth, QUADS, 1, QUAD_WIDTH).astype(F32), (1, 1, 1, SSM_T)))

    def casts_for(layer):
        items = [(w_out, layer, 0), (w_ff1, layer, 0), (w_ff2, layer, 0), (w_ple_gate, layer, 0)]
        if layer + 1 < depth:
            items.append((w_in, layer + 1, 2 * A_WIDTH))
        return items

    def mix_params(w_in_rest):
        return (g_attn, w_in_a, w_in_rest, ws_b, bs, ln_g, ln_b, mg_a, qg, kg, mg_b, pm)

    def post_params(cast):
        w_out_b, w_ff1_b, w_ff2_b, w_gate_b = cast[:4]
        return (glu1_b, glu2_b, mg_c, w_out_b, g_mlp, w_ff1_b, w_ff2_b, g_ple, w_gate_b, w_proj_b)

    h = x.reshape(n, D_MODEL)
    ya, yb, zc, *cast = _mix(0, h, cos2, sin2, sinks, mix_params(w_in_rest0), casts_for(0),
                             seq_len)
    for i in range(depth):
        yc = _ssm(i, zc, kd, wb, wct, astep, atab, seq_len)
        if i + 1 < depth:
            h, ya, yb, zc, *cast = _fused(i + 1, h, ya, yb, yc, p2, post_params(cast), cos2, sin2,
                                          sinks, mix_params(cast[4]), casts_for(i + 1), seq_len)
        else:
            h = _post(i, h, ya, yb, yc, p2, post_params(cast))
    return h.reshape(bsz, seq_len, D_MODEL)
```

```python
import functools
import math

import numpy as np
import jax
import jax.numpy as jnp
from jax import lax
from jax.experimental import pallas as pl
from jax.experimental.pallas import tpu as pltpu

F32 = jnp.float32
BF16 = jnp.bfloat16

D_MODEL = 1024
HEAD_DIM = 64
A_WIDTH = 256
A_HEADS = 4
CHUNK = 128
B_WIDTH = 512
B_Q_HEADS = 8
B_KV_HEADS = 2
WINDOW = 128
ROPE_THETA = 10000.0
C_WIDTH = 256
C_GROUP = 16
C_GROUPS = 16
C_STATE = 64
N_STATE = C_GROUPS * C_STATE
IN_COLS = 1536
D_FF = 4096
PLE_DIM = 256
EPS = 1e-6
NEG_BIG = -1e30
LANES = 128

COL_A = 0
COL_Q = 512
COL_K = 1024
COL_V = 1152
COL_C = 1280

ROWS_PROJ = 512
ROWS_MIX = 512
ROWS_FIRST = 1024
ROWS_SSM = 2048
SSM_T = 8
SSM_N = ROWS_SSM // SSM_T
QUAD_GROUPS = 4
QUADS = C_GROUPS // QUAD_GROUPS
QUAD_WIDTH = QUAD_GROUPS * C_GROUP
QUAD_STATE = 2 * QUAD_GROUPS * C_STATE
PACKED = SSM_T * QUAD_WIDTH
SUBLANES = 8
SCAN_STEPS = int(math.log2(SUBLANES))
FF_SPLIT = 8
VMEM_LIMIT = 60 * 1024 * 1024


def _const_spec(shape):
    nd = len(shape)
    return pl.BlockSpec(shape, lambda *_: (0,) * nd, pipeline_mode=pl.Buffered(1))


def _layer_spec(shape, layer):
    nd = len(shape)
    return pl.BlockSpec((None,) + tuple(shape), lambda *_: (layer,) + (0,) * nd,
                        pipeline_mode=pl.Buffered(1))


def _rms(x, g):
    ms = jnp.mean(x * x, axis=-1, keepdims=True)
    return x * lax.rsqrt(ms + EPS) * g


def _dot(a, b):
    return jnp.dot(a, b, preferred_element_type=F32)


def _dot_nt(a, b):
    return lax.dot_general(a, b, (((1,), (1,)), ((), ())), preferred_element_type=F32)


def _split(x):
    hi = x.astype(BF16)
    return hi, (x - hi.astype(F32)).astype(BF16)


def _cmul(ar, ai, br, bi):
    return ar * br - ai * bi, ar * bi + ai * br


def _rope(x, cos, sin_signed):
    width = x.shape[-1]
    lane = lax.broadcasted_iota(jnp.int32, x.shape, 1)
    first_half = (lane % HEAD_DIM) < (HEAD_DIM // 2)
    partner = jnp.where(first_half,
                        pltpu.roll(x, width - HEAD_DIM // 2, 1),
                        pltpu.roll(x, HEAD_DIM // 2, 1))
    return x * cos + partner * sin_signed


def _dup_heads(x):
    low = lax.broadcasted_iota(jnp.int32, x.shape, 1) < HEAD_DIM
    sw = pltpu.roll(x, HEAD_DIM, 1)
    return [jnp.where(low, x, sw).astype(BF16), jnp.where(low, sw, x).astype(BF16)]


def _mix_project(h_ref, g_ref, w_refs, z_ref):
    wa_ref, wr_ref = w_refs
    xn = _rms(h_ref[...], g_ref[...]).astype(BF16)
    z_ref[:, COL_A:COL_Q] = _dot(xn, wa_ref[...])
    z_ref[:, COL_Q:] = _dot(xn, wr_ref[...])


def _mix_mixers(layer, first_block, sinks_ref, cos_ref, sin_ref, ws_ref, bs_ref, lng_ref, lnb_ref,
                mga_ref, qg_ref, kg_ref, mgb_ref, pm_ref, ya_ref, yb_ref, zc_ref,
                z_ref, kprev_ref, vprev_ref, rows=ROWS_MIX):
    nq = rows // WINDOW
    zc_ref[...] = z_ref[:, COL_C:COL_C + C_WIDTH]
    cos, sin = cos_ref[...], sin_ref[...]
    pm = pm_ref[...]
    pm_kv = pm[0:LANES, 0:LANES]

    k_raw = z_ref[:, COL_K:COL_K + LANES]
    k_ms = _dot((k_raw * k_raw).astype(BF16), pm_kv)
    yield
    k_cur = _rope(k_raw * lax.rsqrt(k_ms + EPS) * kg_ref[...], cos, sin)
    v_cur = z_ref[:, COL_V:COL_V + LANES]
    k_dup = _dup_heads(jnp.concatenate([kprev_ref[...], k_cur], axis=0))
    v_dup = _dup_heads(jnp.concatenate([vprev_ref[...], v_cur], axis=0))
    kprev_ref[...] = k_cur[rows - WINDOW:, :]
    vprev_ref[...] = v_cur[rows - WINDOW:, :]
    q = z_ref[:, COL_Q:COL_Q + B_WIDTH]
    q_ms = jnp.concatenate(
        [_dot((q[:, s:s + 256] * q[:, s:s + 256]).astype(BF16), pm) for s in (0, 256)], axis=1)
    yield
    cos_q = jnp.concatenate([cos] * 4, axis=1)
    sin_q = jnp.concatenate([sin] * 4, axis=1)
    qr = _rope(q * lax.rsqrt(q_ms + EPS) * qg_ref[...], cos_q, sin_q) * (HEAD_DIM ** -0.5)
    low_q = (lax.broadcasted_iota(jnp.int32, qr.shape, 1) % LANES) < HEAD_DIM
    q_low = jnp.where(low_q, qr, 0.0).astype(BF16)
    q_high = jnp.where(low_q, 0.0, qr).astype(BF16)

    u = jax.nn.gelu(z_ref[:, COL_A:COL_A + A_WIDTH])
    v = jax.nn.gelu(z_ref[:, COL_A + A_WIDTH:COL_A + 2 * A_WIDTH])
    v_mean = _dot(v.astype(BF16), pm)
    yield
    vc = v - v_mean
    v_var = _dot((vc * vc).astype(BF16), pm)
    yield
    vn = (vc * lax.rsqrt(v_var + EPS) * lng_ref[...] + lnb_ref[...]).astype(BF16)
    lane = lax.broadcasted_iota(jnp.int32, (CHUNK, A_WIDTH), 1)
    in_head = [(lane >= h * HEAD_DIM) & (lane < (h + 1) * HEAD_DIM) for h in range(A_HEADS)]
    svs = []
    for c in range(rows // CHUNK):
        vn_c = vn[c * CHUNK:(c + 1) * CHUNK]
        stacked = jnp.concatenate(
            [jnp.where(in_head[h], vn_c, jnp.zeros((), BF16)) for h in range(A_HEADS)], axis=0)
        svs.append(_dot(ws_ref[...], stacked))
    yield
    for c in range(rows // CHUNK):
        chunk = slice(c * CHUNK, (c + 1) * CHUNK)
        ya_ref[chunk, :] = _rms(u[chunk] * (svs[c] + bs_ref[...]), mga_ref[...]).astype(BF16)

    qi = lax.broadcasted_iota(jnp.int32, (4 * WINDOW, WINDOW), 0) % WINDOW
    kc = lax.broadcasted_iota(jnp.int32, (4 * WINDOW, WINDOW), 1)
    from_cur = kc <= qi
    head_slot = lax.broadcasted_iota(jnp.int32, (4 * WINDOW, 1), 0) // WINDOW
    prev_ok = kc >= jnp.where(first_block, WINDOW, 0)
    low_o = lax.broadcasted_iota(jnp.int32, (WINDOW, LANES), 1) < HEAD_DIM
    zero = jnp.zeros((), F32)

    def scores(b, j):
        qrows = slice(b * WINDOW, (b + 1) * WINDOW)
        pair_a = slice((2 * j) * LANES, (2 * j + 1) * LANES)
        pair_b = slice((2 * j + 1) * LANES, (2 * j + 2) * LANES)
        qs = jnp.concatenate([q_low[qrows, pair_a], q_low[qrows, pair_b],
                              q_high[qrows, pair_a], q_high[qrows, pair_b]], axis=0)
        return _dot_nt(qs, k_dup[j][b * WINDOW:(b + 2) * WINDOW])

    units = [(b, j) for b in range(nq) for j in range(B_KV_HEADS)]
    s_next = scores(*units[0])
    yield
    pairs = []
    for u, (b, j) in enumerate(units):
        s = s_next
        if u + 1 < len(units):
            s_next = scores(*units[u + 1])
        s_prev = s[:, 0:WINDOW]
        if b == 0:
            s_prev = jnp.where(prev_ok, s_prev, NEG_BIG)
        s = jnp.where(from_cur, s[:, WINDOW:], s_prev)
        sink = jnp.where(head_slot == 0, sinks_ref[layer, 4 * j],
                         jnp.where(head_slot == 1, sinks_ref[layer, 4 * j + 2],
                                   jnp.where(head_slot == 2, sinks_ref[layer, 4 * j + 1],
                                             sinks_ref[layer, 4 * j + 3])))
        m = jnp.maximum(jnp.max(s, axis=-1, keepdims=True), sink)
        pr = jnp.exp(s - m)
        denom = jnp.sum(pr, axis=-1, keepdims=True) + jnp.exp(sink - m)
        pn = pr * (1.0 / denom)
        p2 = jnp.concatenate([jnp.where(from_cur, zero, pn),
                              jnp.where(from_cur, pn, zero)], axis=1).astype(BF16)
        yield
        o = _dot(p2, v_dup[j][b * WINDOW:(b + 2) * WINDOW])
        yield
        pairs += [jnp.where(low_o, o[0:WINDOW], o[2 * WINDOW:3 * WINDOW]),
                  jnp.where(low_o, o[WINDOW:2 * WINDOW], o[3 * WINDOW:4 * WINDOW])]
        if j == B_KV_HEADS - 1:
            yb = jnp.concatenate(pairs, axis=1)
            yb_ref[b * WINDOW:(b + 1) * WINDOW, :] = _rms(yb, mgb_ref[...]).astype(BF16)
            pairs = []


def _mixer_stages(rows):
    return 6 + 2 * B_KV_HEADS * (rows // WINDOW)


MIXER_STAGES = _mixer_stages(ROWS_MIX)
POST_STAGES = 5 + FF_SPLIT


def _run(*staged):
    total = max(n for _, n in staged)
    done = [0] * len(staged)
    for step in range(1, total + 1):
        for k, (gen, n) in enumerate(staged):
            while done[k] * total < step * n:
                next(gen, None)
                done[k] += 1
    for gen, _ in staged:
        for _ in gen:
            pass


def _cast_specs(items, n_steps):
    step = lambda i: jnp.minimum(i, n_steps - 1)
    in_specs, out_specs, out_shapes = [], [], []
    for arr, layer, col0 in items:
        _, rows, cols = arr.shape
        chunk = rows // n_steps
        in_specs.append(
            pl.BlockSpec((None, chunk, cols), lambda i, layer=layer: (layer, step(i), 0)))
        out_specs.append(pl.BlockSpec((chunk, cols - col0), lambda i: (step(i), 0)))
        out_shapes.append(jax.ShapeDtypeStruct((rows, cols - col0), BF16))
    return in_specs, out_specs, out_shapes


def _cast_chunks(src_refs, dst_refs):
    for src, dst in zip(src_refs, dst_refs):
        x = src[...]
        dst[...] = x[:, x.shape[1] - dst.shape[1]:].astype(BF16)


N_MIXER_IN = 9


def _mix_kernel(layer, blocks_per_seq, sinks_ref, h_ref, cos_ref, sin_ref, g_ref, wa_ref, wr_ref,
                *rest):
    w_ref = (wa_ref, wr_ref)
    mixer_in, rest = rest[:N_MIXER_IN], rest[N_MIXER_IN:]
    n_cast = (len(rest) - 6) // 2
    cast_src, outs, cast_dst = rest[:n_cast], rest[n_cast:n_cast + 3], rest[n_cast + 3:-3]
    mixer_refs = mixer_in + outs
    z_ref, kprev_ref, vprev_ref = rest[-3:]
    first_block = (pl.program_id(0) % blocks_per_seq) == 0

    @pl.when(first_block)
    def _():
        kprev_ref[...] = jnp.zeros_like(kprev_ref)
        vprev_ref[...] = jnp.zeros_like(vprev_ref)

    _cast_chunks(cast_src, cast_dst)
    _mix_project(h_ref, g_ref, w_ref, z_ref)
    rows = z_ref.shape[0]
    _run((_mix_mixers(layer, first_block, sinks_ref, cos_ref, sin_ref, *mixer_refs,
                      z_ref, kprev_ref, vprev_ref, rows=rows), _mixer_stages(rows)))


def _mix_param_specs(layer):
    return [
        _layer_spec((1, D_MODEL), layer),
        _layer_spec((D_MODEL, COL_Q), layer),
        _const_spec((D_MODEL, IN_COLS - COL_Q)),
        _layer_spec((CHUNK, A_HEADS * CHUNK), layer),
        _layer_spec((CHUNK, A_WIDTH), layer),
        _layer_spec((1, A_WIDTH), layer),
        _layer_spec((1, A_WIDTH), layer),
        _layer_spec((1, A_WIDTH), layer),
        _layer_spec((1, B_WIDTH), layer),
        _layer_spec((1, LANES), layer),
        _layer_spec((1, B_WIDTH), layer),
        _const_spec((256, 256)),
    ]


def _mix_out_shapes(n):
    return [jax.ShapeDtypeStruct((n, A_WIDTH), BF16),
            jax.ShapeDtypeStruct((n, B_WIDTH), BF16),
            jax.ShapeDtypeStruct((n, C_WIDTH), F32)]


def _mix_scratch(rows):
    return [
        pltpu.VMEM((rows, IN_COLS), F32),
        pltpu.VMEM((WINDOW, LANES), F32),
        pltpu.VMEM((WINDOW, LANES), F32),
    ]


def _mix(layer, h, cos2, sin2, sinks, mix_params, casts, seq_len):
    n = h.shape[0]
    n_blocks = n // ROWS_FIRST
    rb = lambda w_: pl.BlockSpec((ROWS_FIRST, w_), lambda i: (i, 0))
    cast_in, cast_out, cast_shapes = _cast_specs(casts, n_blocks)
    return pl.pallas_call(
        functools.partial(_mix_kernel, layer, seq_len // ROWS_FIRST),
        grid=(n_blocks,),
        in_specs=[pl.BlockSpec(memory_space=pltpu.SMEM), rb(D_MODEL), rb(LANES), rb(LANES)]
        + _mix_param_specs(layer) + cast_in,
        out_specs=[rb(A_WIDTH), rb(B_WIDTH), rb(C_WIDTH)] + cast_out,
        out_shape=_mix_out_shapes(n) + cast_shapes,
        scratch_shapes=_mix_scratch(ROWS_FIRST),
        compiler_params=pltpu.CompilerParams(
            dimension_semantics=("arbitrary",), vmem_limit_bytes=VMEM_LIMIT),
        name="mix",
    )(sinks, h, cos2, sin2, *mix_params, *[c[0] for c in casts])


def _tables_kernel(are_ref, aim_ref, ldt_ref, bre_ref, bim_ref, cre_ref, cim_ref, d_ref,
                   kt_ref, wb_ref, wct_ref, astep_ref, atab_ref):
    a_re, a_im = are_ref[...], aim_ref[...]
    dt = jnp.exp(ldt_ref[...])
    mag = jnp.exp(a_re * dt)
    lr, li = mag * jnp.cos(a_im * dt), mag * jnp.sin(a_im * dt)
    den = a_re * a_re + a_im * a_im
    fr, fi = _cmul(lr - 1.0, li, a_re / den, -a_im / den)
    bbr, bbi = _cmul(fr, fi, bre_ref[...], bim_ref[...])
    c_re, c_im = cre_ref[...], cim_ref[...]

    row_g = lax.broadcasted_iota(jnp.int32, (QUAD_WIDTH, QUAD_STATE // 2), 0) // C_GROUP
    col_g = lax.broadcasted_iota(jnp.int32, (QUAD_WIDTH, QUAD_STATE // 2), 1) // C_STATE
    same_group = row_g == col_g

    def quad_block(mr, mi, q):
        cols = slice(q * QUAD_STATE // 2, (q + 1) * QUAD_STATE // 2)
        tile = lambda m: jnp.where(same_group, jnp.concatenate([m[:, cols]] * QUAD_GROUPS, axis=0), 0.0)
        return jnp.concatenate([tile(mr), tile(mi)], axis=1)

    pows = [(jnp.ones_like(lr), jnp.zeros_like(li))]
    for _ in range(SSM_T):
        pows.append(_cmul(pows[-1][0], pows[-1][1], lr, li))

    rr = lax.broadcasted_iota(jnp.int32, (PACKED, PACKED), 0)
    cc = lax.broadcasted_iota(jnp.int32, (PACKED, PACKED), 1)
    for q in range(QUADS):
        into_state, from_state, lag_out = [], [], []
        for s in range(SSM_T):
            wr, wi = _cmul(pows[SSM_T - 1 - s][0], pows[SSM_T - 1 - s][1], bbr, bbi)
            into_state.append(quad_block(wr, wi, q))
            mr, mi = _cmul(pows[s + 1][0], pows[s + 1][1], c_re, c_im)
            from_state.append(quad_block(mr, -mi, q))
            mr, mi = _cmul(pows[s][0], pows[s][1], c_re, c_im)
            lag_out.append(quad_block(mr, -mi, q))
        wb_ref[q] = jnp.concatenate(into_state, axis=0).astype(BF16)
        wct_ref[q] = jnp.concatenate(from_state, axis=0).astype(BF16)
        a_hi, a_lo = _split(quad_block(bbr, bbi, q))
        b_hi, b_lo = _split(jnp.concatenate(lag_out, axis=0))
        lags = _dot_nt(a_hi, b_hi) + _dot_nt(a_hi, b_lo) + _dot_nt(a_lo, b_hi)
        lane = lax.broadcasted_iota(jnp.int32, lags.shape, 1)
        k = jnp.concatenate(
            [lags] + [jnp.where(lane >= s * QUAD_WIDTH, pltpu.roll(lags, s * QUAD_WIDTH, 1), 0.0)
                      for s in range(1, SSM_T)], axis=0)
        kt_ref[q] = (k + jnp.where(rr == cc, d_ref[q], 0.0)).astype(BF16)

    half = QUAD_STATE // 2
    for q in range(QUADS):
        src = slice(q * half, (q + 1) * half)
        re_cols, im_cols = pl.ds(q * QUAD_STATE, half), pl.ds(q * QUAD_STATE + half, half)
        sr, si = pows[SSM_T][0][:, src], pows[SSM_T][1][:, src]
        atab_ref[0:1, re_cols] = sr
        atab_ref[0:1, im_cols] = si
        for k in range(SCAN_STEPS):
            span = 1 << k
            astep_ref[pl.ds(k, 1), re_cols] = sr
            astep_ref[pl.ds(k, 1), im_cols] = si
            tr, ti = _cmul(atab_ref[0:span, re_cols], atab_ref[0:span, im_cols], sr, si)
            atab_ref[pl.ds(span, span), re_cols] = tr
            atab_ref[pl.ds(span, span), im_cols] = ti
            sr, si = _cmul(sr, si, sr, si)


def _tables(a_re, a_im, ldt, b_re, b_im, c_re, c_im, d_skip):
    depth = a_re.shape[0]
    vec = lambda w: pl.BlockSpec((None, 1, w), lambda l: (l, 0, 0))
    mat = lambda r, c: pl.BlockSpec((None, r, c), lambda l: (l, 0, 0))
    cube = lambda r, c: pl.BlockSpec((None, QUADS, r, c), lambda l: (l, 0, 0, 0))
    packed = jax.ShapeDtypeStruct((depth, QUADS, PACKED, PACKED), BF16)
    return pl.pallas_call(
        _tables_kernel,
        grid=(depth,),
        in_specs=[vec(N_STATE), vec(N_STATE), vec(N_STATE),
                  mat(C_GROUP, N_STATE), mat(C_GROUP, N_STATE),
                  mat(C_GROUP, N_STATE), mat(C_GROUP, N_STATE), cube(1, PACKED)],
        out_specs=[cube(PACKED, PACKED), cube(PACKED, PACKED), cube(PACKED, PACKED),
                   mat(SCAN_STEPS, 2 * N_STATE), mat(SUBLANES, 2 * N_STATE)],
        out_shape=[packed, packed, packed,
                   jax.ShapeDtypeStruct((depth, SCAN_STEPS, 2 * N_STATE), F32),
                   jax.ShapeDtypeStruct((depth, SUBLANES, 2 * N_STATE), F32)],
        compiler_params=pltpu.CompilerParams(
            dimension_semantics=("arbitrary",), vmem_limit_bytes=VMEM_LIMIT),
        name="ssm_tables",
    )(a_re, a_im, ldt, b_re, b_im, c_re, c_im, d_skip)


def _ssm_kernel(blocks_per_seq, ua_ref, ub_ref, kt_ref, wb_ref, wct_ref, astep_ref, atab_ref,
                ya_ref, yb_ref, inc_ref, sprev_ref, carry_ref):
    @pl.when((pl.program_id(0) % blocks_per_seq) == 0)
    def _():
        carry_ref[...] = jnp.zeros_like(carry_ref)

    low = lax.broadcasted_iota(jnp.int32, (SSM_N, LANES), 1) < QUAD_WIDTH

    halves = [[ref[pl.ds(j, SSM_N, stride=SSM_T), :] for j in range(SSM_T)]
              for ref in (ua_ref, ub_ref)]

    def pack(q):
        src, upper = halves[q // 2], q % 2 == 1
        cols = []
        for m in range(SSM_T // 2):
            even, odd = src[2 * m], src[2 * m + 1]
            if upper:
                even = pltpu.roll(even, QUAD_WIDTH, 1)
            else:
                odd = pltpu.roll(odd, QUAD_WIDTH, 1)
            cols.append(jnp.where(low, even, odd))
        return jnp.concatenate(cols, axis=1).astype(BF16)

    xs = [pack(q) for q in range(QUADS)]

    for q in range(QUADS):
        inc_ref[:, q * QUAD_STATE:(q + 1) * QUAD_STATE] = _dot(xs[q], wb_ref[q])

    n_groups = SSM_N // SUBLANES
    sub = lax.broadcasted_iota(jnp.int32, (n_groups, SUBLANES, LANES), 1)
    sub2 = lax.broadcasted_iota(jnp.int32, (SUBLANES, LANES), 0)
    tiles_per_quad = QUAD_STATE // 2 // LANES
    y_lag = []
    for lt in range(N_STATE // LANES):
        q, part = lt // tiles_per_quad, lt % tiles_per_quad
        re_cols = pl.ds(q * QUAD_STATE + part * LANES, LANES)
        im_cols = pl.ds(q * QUAD_STATE + QUAD_STATE // 2 + part * LANES, LANES)
        xr = inc_ref[:, re_cols].reshape(n_groups, SUBLANES, LANES)
        xi = inc_ref[:, im_cols].reshape(n_groups, SUBLANES, LANES)
        for k in range(SCAN_STEPS):
            sh = 1 << k
            ar = astep_ref[pl.ds(k, 1), re_cols]
            ai = astep_ref[pl.ds(k, 1), im_cols]
            keep = sub >= sh
            sr = jnp.where(keep, pltpu.roll(xr, sh, 1), 0.0)
            si = jnp.where(keep, pltpu.roll(xi, sh, 1), 0.0)
            xr, xi = xr + ar * sr - ai * si, xi + ar * si + ai * sr
        tr = atab_ref[:, re_cols]
        ti = atab_ref[:, im_cols]
        er = jnp.broadcast_to(carry_ref[:, re_cols], (SUBLANES, LANES))
        ei = jnp.broadcast_to(carry_ref[:, im_cols], (SUBLANES, LANES))
        before_r, before_i = [], []
        for g in range(n_groups):
            fr = xr[g] + tr * er - ti * ei
            fi = xi[g] + tr * ei + ti * er
            before_r.append(jnp.where(sub2 == 0, er, pltpu.roll(fr, 1, 0)))
            before_i.append(jnp.where(sub2 == 0, ei, pltpu.roll(fi, 1, 0)))
            er = jnp.broadcast_to(fr[SUBLANES - 1:SUBLANES, :], (SUBLANES, LANES))
            ei = jnp.broadcast_to(fi[SUBLANES - 1:SUBLANES, :], (SUBLANES, LANES))
        sprev_ref[:, re_cols] = jnp.concatenate(before_r, axis=0).astype(BF16)
        sprev_ref[:, im_cols] = jnp.concatenate(before_i, axis=0).astype(BF16)
        carry_ref[:, re_cols] = er[0:1, :]
        carry_ref[:, im_cols] = ei[0:1, :]

        if part == tiles_per_quad - 1:
            y_lag.append(_dot(xs[q], kt_ref[q]))

    ys = [y_lag[q] + _dot_nt(sprev_ref[:, q * QUAD_STATE:(q + 1) * QUAD_STATE], wct_ref[q])
          for q in range(QUADS)]
    for t in range(SSM_T):
        cols = slice((t // 2) * LANES, (t // 2 + 1) * LANES)
        for y_ref, (qa, qb) in ((ya_ref, (0, 1)), (yb_ref, (2, 3))):
            lower, upper = ys[qa][:, cols], ys[qb][:, cols]
            if t % 2 == 1:
                lower = pltpu.roll(lower, QUAD_WIDTH, 1)
            else:
                upper = pltpu.roll(upper, QUAD_WIDTH, 1)
            y_ref[pl.ds(t, SSM_N, stride=SSM_T), :] = jnp.where(low, lower, upper)


def _ssm(layer, zc, kt, wb, wct, astep, atab, seq_len):
    n = zc.shape[0]
    half = lambda c: pl.BlockSpec((ROWS_SSM, LANES), lambda i: (i, c))
    return pl.pallas_call(
        functools.partial(_ssm_kernel, seq_len // ROWS_SSM),
        grid=(n // ROWS_SSM,),
        in_specs=[
            half(0), half(1),
            _layer_spec((QUADS, PACKED, PACKED), layer),
            _layer_spec((QUADS, PACKED, QUAD_STATE), layer),
            _layer_spec((QUADS, PACKED, QUAD_STATE), layer),
            _layer_spec((SCAN_STEPS, 2 * N_STATE), layer),
            _layer_spec((SUBLANES, 2 * N_STATE), layer),
        ],
        out_specs=[pl.BlockSpec((ROWS_SSM, LANES), lambda i: (i, 0)) for _ in range(2)],
        out_shape=[jax.ShapeDtypeStruct((n, LANES), F32) for _ in range(2)],
        scratch_shapes=[
            pltpu.VMEM((SSM_N, 2 * N_STATE), F32),
            pltpu.VMEM((SSM_N, 2 * N_STATE), BF16),
            pltpu.VMEM((1, 2 * N_STATE), F32),
        ],
        compiler_params=pltpu.CompilerParams(
            dimension_semantics=("arbitrary",), vmem_limit_bytes=VMEM_LIMIT),
        name="ssm",
    )(zc, zc, kt, wb, wct, astep, atab)


def _post_stages(h_ref, ya_ref, yb_ref, sa_ref, sb_ref, p_ref, glu1_ref, glu2_ref, mgc_ref,
                 wo_ref, g1_ref, w1_ref, w2_ref, g2_ref, wg_ref, wp_ref, *emit):
    y = jax.nn.gelu(jnp.concatenate([sa_ref[...], sb_ref[...]], axis=1)).astype(BF16)
    glu_a, glu_b = _dot(y, glu1_ref[...]), _dot(y, glu2_ref[...])
    mixed = (_dot(ya_ref[...], wo_ref[0:A_WIDTH, :])
             + _dot(yb_ref[...], wo_ref[A_WIDTH:A_WIDTH + B_WIDTH, :]))
    yield
    yc = _rms(glu_a * jax.nn.sigmoid(glu_b), mgc_ref[...]).astype(BF16)
    mixed = mixed + _dot(yc, wo_ref[A_WIDTH + B_WIDTH:, :])
    yield
    h = h_ref[...] + mixed
    hn = _rms(h, g1_ref[...]).astype(BF16)
    ple = _dot(p_ref[...].astype(BF16), wp_ref[...])
    piece = D_FF // FF_SPLIT
    up = _dot(hn, w1_ref[:, 0:piece])
    yield
    ff = None
    for c in range(FF_SPLIT):
        a = jnp.maximum(up, 0.0)
        down = _dot((a * a).astype(BF16), w2_ref[c * piece:(c + 1) * piece, :])
        if c + 1 < FF_SPLIT:
            up = _dot(hn, w1_ref[:, (c + 1) * piece:(c + 2) * piece])
        yield
        ff = down if ff is None else ff + down
    h = h + ff
    gate = _dot(_rms(h, g2_ref[...]).astype(BF16), wg_ref[...])
    yield
    h = h + jax.nn.sigmoid(gate) * ple
    for fn in emit:
        fn(h)
    yield


N_POST_IN = 16


def _store_to(ref):
    def store(value):
        ref[...] = value
    return store


def _post_kernel(*refs):
    _run((_post_stages(*refs[:N_POST_IN], _store_to(refs[N_POST_IN])), POST_STAGES))


def _post_specs(layer, row_block, rows=ROWS_MIX):
    rb = lambda w: pl.BlockSpec((rows, w), lambda i: (row_block(i), 0))
    return [
        rb(D_MODEL), rb(A_WIDTH), rb(B_WIDTH), rb(LANES), rb(LANES),
        pl.BlockSpec((None, rows, PLE_DIM), lambda i: (layer, row_block(i), 0)),
        _layer_spec((C_WIDTH, C_WIDTH), layer),
        _layer_spec((C_WIDTH, C_WIDTH), layer),
        _layer_spec((1, C_WIDTH), layer),
        _const_spec((D_MODEL, D_MODEL)),
        _layer_spec((1, D_MODEL), layer),
        _const_spec((D_MODEL, D_FF)),
        _const_spec((D_FF, D_MODEL)),
        _layer_spec((1, D_MODEL), layer),
        _const_spec((D_MODEL, D_MODEL)),
        _layer_spec((PLE_DIM, D_MODEL), layer),
    ]


def _post(layer, h, ya, yb, yc, p, post_params):
    n = h.shape[0]
    return pl.pallas_call(
        _post_kernel,
        grid=(n // ROWS_PROJ,),
        in_specs=_post_specs(layer, lambda i: i, ROWS_PROJ),
        out_specs=pl.BlockSpec((ROWS_PROJ, D_MODEL), lambda i: (i, 0)),
        out_shape=jax.ShapeDtypeStruct((n, D_MODEL), F32),
        compiler_params=pltpu.CompilerParams(
            dimension_semantics=("arbitrary",), vmem_limit_bytes=VMEM_LIMIT),
        name="post",
    )(h, ya, yb, *yc, p, *post_params)


def _fused_kernel(layer, blocks_per_seq, n_blocks, sinks_ref, *refs):
    post_refs, refs = refs[:N_POST_IN], refs[N_POST_IN:]
    cos_ref, sin_ref, g_ref = refs[:3]
    w_ref = refs[3:5]
    mixer_refs, refs = refs[5:5 + N_MIXER_IN], refs[5 + N_MIXER_IN:]
    n_cast = (len(refs) - 8) // 2
    cast_src, cast_dst = refs[:n_cast], refs[n_cast + 4:-4]
    h_out_ref, ya_ref, yb_ref, zc_ref = refs[n_cast:n_cast + 4]
    hs_ref, z_ref, kprev_ref, vprev_ref = refs[-4:]
    i = pl.program_id(0)
    first_block = (i % blocks_per_seq) == 1 % blocks_per_seq

    @pl.when((i == 0) | first_block)
    def _():
        kprev_ref[...] = jnp.zeros_like(kprev_ref)
        vprev_ref[...] = jnp.zeros_like(vprev_ref)

    @pl.when(i == 0)
    def _():
        hs_ref[...] = jnp.zeros_like(hs_ref)

    def mixers():
        return (_mix_mixers(layer, first_block, sinks_ref, cos_ref, sin_ref, *mixer_refs,
                            ya_ref, yb_ref, zc_ref, z_ref, kprev_ref, vprev_ref), MIXER_STAGES)

    @pl.when(i < n_blocks)
    def _():
        _cast_chunks(cast_src, cast_dst)
        _mix_project(hs_ref, g_ref, w_ref, z_ref)
        _run((_post_stages(*post_refs, _store_to(h_out_ref), _store_to(hs_ref)), POST_STAGES),
             mixers())

    @pl.when(i == n_blocks)
    def _():
        _mix_project(hs_ref, g_ref, w_ref, z_ref)
        _run(mixers())


def _fused(layer, h, ya, yb, yc, p, post_params, cos2, sin2, sinks, mix_params, casts, seq_len):
    n = h.shape[0]
    n_blocks = n // ROWS_MIX
    merged = lambda i: jnp.minimum(i, n_blocks - 1)
    mixed = lambda i: jnp.maximum(i - 1, 0)
    rb = lambda w_: pl.BlockSpec((ROWS_MIX, w_), lambda i: (mixed(i), 0))
    cast_in, cast_out, cast_shapes = _cast_specs(casts, n_blocks)
    return pl.pallas_call(
        functools.partial(_fused_kernel, layer, seq_len // ROWS_MIX, n_blocks),
        grid=(n_blocks + 1,),
        in_specs=[pl.BlockSpec(memory_space=pltpu.SMEM)]
        + _post_specs(layer - 1, merged)
        + [rb(LANES), rb(LANES)] + _mix_param_specs(layer) + cast_in,
        out_specs=[pl.BlockSpec((ROWS_MIX, D_MODEL), lambda i: (merged(i), 0)),
                   rb(A_WIDTH), rb(B_WIDTH), rb(C_WIDTH)] + cast_out,
        out_shape=[jax.ShapeDtypeStruct((n, D_MODEL), F32)] + _mix_out_shapes(n) + cast_shapes,
        scratch_shapes=[pltpu.VMEM((ROWS_MIX, D_MODEL), F32)] + _mix_scratch(ROWS_MIX),
        compiler_params=pltpu.CompilerParams(
            dimension_semantics=("arbitrary",), vmem_limit_bytes=VMEM_LIMIT),
        name="fused",
    )(sinks, h, ya, yb, *yc, p, *post_params, cos2, sin2, *mix_params, *[c[0] for c in casts])


def kernel(x, p, positions, attn_norm_g, w_in, gmlp_ln_g, gmlp_ln_b, gmlp_ws, gmlp_bs, q_norm_g, k_norm_g, sinks, ssm_a_re, ssm_a_im, ssm_log_dt, ssm_b_re, ssm_b_im, ssm_c_re, ssm_c_im, ssm_d, glu_w1, glu_w2, mix_out_g, w_out, mlp_norm_g, w_ff1, w_ff2, ple_norm_g, w_ple_gate, w_ple_proj):
    bsz, seq_len, d_model = x.shape
    depth = w_in.shape[0]
    n = bsz * seq_len
    assert d_model == D_MODEL and w_in.shape[1:] == (D_MODEL, IN_COLS)
    assert seq_len % ROWS_SSM == 0 and seq_len % ROWS_FIRST == 0 and seq_len % ROWS_MIX == 0
    assert n % ROWS_PROJ == 0 and p.shape == (depth, bsz, seq_len, PLE_DIM)

    inv = 1.0 / (ROPE_THETA ** (jnp.arange(0, HEAD_DIM, 2, dtype=F32) / HEAD_DIM))
    ang = positions.astype(F32).reshape(n, 1) * inv
    cos2 = jnp.tile(jnp.cos(ang), (1, 4))
    sin2 = jnp.tile(jnp.concatenate([-jnp.sin(ang), jnp.sin(ang)], axis=1), (1, 2))

    seg = np.arange(256) // HEAD_DIM
    pm = jnp.asarray((seg[:, None] == seg[None, :]) / HEAD_DIM, dtype=BF16)

    rows = lambda v: v.reshape(depth, 1, -1).astype(F32)
    g_attn, g_mlp, g_ple = rows(attn_norm_g), rows(mlp_norm_g), rows(ple_norm_g)
    ln_g, ln_b = rows(gmlp_ln_g), rows(gmlp_ln_b)
    mg_a = rows(mix_out_g[:, :A_WIDTH])
    mg_b = rows(mix_out_g[:, A_WIDTH:A_WIDTH + B_WIDTH])
    mg_c = rows(mix_out_g[:, A_WIDTH + B_WIDTH:])
    qg = rows(jnp.tile(q_norm_g, (1, B_Q_HEADS)))
    kg = rows(jnp.tile(k_norm_g, (1, B_KV_HEADS)))
    bs = jnp.repeat(jnp.swapaxes(gmlp_bs, 1, 2), HEAD_DIM, axis=2)
    w_a = w_in[:, :, :2 * A_WIDTH].reshape(depth, D_MODEL, A_HEADS, 2, HEAD_DIM)
    w_a = jnp.swapaxes(w_a, 2, 3).reshape(depth, D_MODEL, 2 * A_WIDTH)
    w_in_a, w_in_rest0 = w_a.astype(BF16), w_in[0, :, 2 * A_WIDTH:].astype(BF16)
    causal = np.tril(np.ones((CHUNK, CHUNK), dtype=bool))
    ws_b = jnp.swapaxes(jnp.where(causal, gmlp_ws, 0.0), 1, 2).reshape(
        depth, CHUNK, A_HEADS * CHUNK).astype(BF16)
    w_proj_b = w_ple_proj.astype(BF16)
    glu1_b, glu2_b = glu_w1.astype(BF16), glu_w2.astype(BF16)
    p2 = p.reshape(depth, n, PLE_DIM)

    kd, wb, wct, astep, atab = _tables(
        rows(ssm_a_re), rows(ssm_a_im), rows(jnp.repeat(ssm_log_dt, C_STATE, axis=1)),
        jnp.transpose(ssm_b_re, (0, 3, 1, 2)).reshape(depth, C_GROUP, N_STATE),
        jnp.transpose(ssm_b_im, (0, 3, 1, 2)).reshape(depth, C_GROUP, N_STATE),
        jnp.transpose(ssm_c_re, (0, 2, 1, 3)).reshape(depth, C_GROUP, N_STATE),
        jnp.transpose(ssm_c_im, (0, 2, 1, 3)).reshape(depth, C_GROUP, N_STATE),
        jnp.tile(ssm_d.reshape(depth, QUADS, 1, QUAD_WIDTH).astype(F32), (1, 1, 1, SSM_T)))

    def casts_for(layer):
        items = [(w_out, layer, 0), (w_ff1, layer, 0), (w_ff2, layer, 0), (w_ple_gate, layer, 0)]
        if layer + 1 < depth:
            items.append((w_in, layer + 1, 2 * A_WIDTH))
        return items

    def mix_params(w_in_rest):
        return (g_attn, w_in_a, w_in_rest, ws_b, bs, ln_g, ln_b, mg_a, qg, kg, mg_b, pm)

    def post_params(cast):
        w_out_b, w_ff1_b, w_ff2_b, w_gate_b = cast[:4]
        return (glu1_b, glu2_b, mg_c, w_out_b, g_mlp, w_ff1_b, w_ff2_b, g_ple, w_gate_b, w_proj_b)

    h = x.reshape(n, D_MODEL)
    ya, yb, zc, *cast = _mix(0, h, cos2, sin2, sinks, mix_params(w_in_rest0), casts_for(0),
                             seq_len)
    for i in range(depth):
        yc = _ssm(i, zc, kd, wb, wct, astep, atab, seq_len)
        if i + 1 < depth:
            h, ya, yb, zc, *cast = _fused(i + 1, h, ya, yb, yc, p2, post_params(cast), cos2, sin2,
                                          sinks, mix_params(cast[4]), casts_for(i + 1), seq_len)
        else:
            h = _post(i, h, ya, yb, yc, p2, post_params(cast))
    return h.reshape(bsz, seq_len, D_MODEL)
```

```python
import functools
import math

import numpy as np
import jax
import jax.numpy as jnp
from jax import lax
from jax.experimental import pallas as pl
from jax.experimental.pallas import tpu as pltpu

F32 = jnp.float32
BF16 = jnp.bfloat16

D_MODEL = 1024
HEAD_DIM = 64
A_WIDTH = 256
A_HEADS = 4
CHUNK = 128
B_WIDTH = 512
B_Q_HEADS = 8
B_KV_HEADS = 2
WINDOW = 128
ROPE_THETA = 10000.0
C_WIDTH = 256
C_GROUP = 16
C_GROUPS = 16
C_STATE = 64
N_STATE = C_GROUPS * C_STATE
IN_COLS = 1536
D_FF = 4096
PLE_DIM = 256
EPS = 1e-6
NEG_BIG = -1e30
LANES = 128

COL_A = 0
COL_Q = 512
COL_K = 1024
COL_V = 1152
COL_C = 1280

ROWS_PROJ = 512
ROWS_MIX = 512
ROWS_FIRST = 1024
ROWS_SSM = 2048
SSM_T = 8
SSM_N = ROWS_SSM // SSM_T
QUAD_GROUPS = 4
QUADS = C_GROUPS // QUAD_GROUPS
QUAD_WIDTH = QUAD_GROUPS * C_GROUP
QUAD_STATE = 2 * QUAD_GROUPS * C_STATE
PACKED = SSM_T * QUAD_WIDTH
SUBLANES = 8
SCAN_STEPS = int(math.log2(SUBLANES))
FF_SPLIT = 8
SCORES_AHEAD = 2
VMEM_LIMIT = 60 * 1024 * 1024


def _const_spec(shape):
    nd = len(shape)
    return pl.BlockSpec(shape, lambda *_: (0,) * nd, pipeline_mode=pl.Buffered(1))


def _layer_spec(shape, layer):
    nd = len(shape)
    return pl.BlockSpec((None,) + tuple(shape), lambda *_: (layer,) + (0,) * nd,
                        pipeline_mode=pl.Buffered(1))


def _rms(x, g):
    ms = jnp.mean(x * x, axis=-1, keepdims=True)
    return x * lax.rsqrt(ms + EPS) * g


def _dot(a, b):
    return jnp.dot(a, b, preferred_element_type=F32)


def _dot_nt(a, b):
    return lax.dot_general(a, b, (((1,), (1,)), ((), ())), preferred_element_type=F32)


def _split(x):
    hi = x.astype(BF16)
    return hi, (x - hi.astype(F32)).astype(BF16)


def _cmul(ar, ai, br, bi):
    return ar * br - ai * bi, ar * bi + ai * br


def _rope(x, cos, sin_signed):
    width = x.shape[-1]
    lane = lax.broadcasted_iota(jnp.int32, x.shape, 1)
    first_half = (lane % HEAD_DIM) < (HEAD_DIM // 2)
    partner = jnp.where(first_half,
                        pltpu.roll(x, width - HEAD_DIM // 2, 1),
                        pltpu.roll(x, HEAD_DIM // 2, 1))
    return x * cos + partner * sin_signed


def _dup_heads(x):
    low = lax.broadcasted_iota(jnp.int32, x.shape, 1) < HEAD_DIM
    sw = pltpu.roll(x, HEAD_DIM, 1)
    return [jnp.where(low, x, sw).astype(BF16), jnp.where(low, sw, x).astype(BF16)]


def _mix_project(h_ref, g_ref, w_refs, z_ref):
    wa_ref, wr_ref = w_refs
    xn = _rms(h_ref[...], g_ref[...]).astype(BF16)
    z_ref[:, COL_A:COL_Q] = _dot(xn, wa_ref[...])
    z_ref[:, COL_Q:] = _dot(xn, wr_ref[...])


def _mix_mixers(layer, first_block, sinks_ref, cos_ref, sin_ref, ws_ref, bs_ref, lng_ref, lnb_ref,
                mga_ref, qg_ref, kg_ref, mgb_ref, pm_ref, ya_ref, yb_ref, zc_ref,
                z_ref, kprev_ref, vprev_ref, rows=ROWS_MIX):
    nq = rows // WINDOW
    zc_ref[...] = z_ref[:, COL_C:COL_C + C_WIDTH]
    cos, sin = cos_ref[...], sin_ref[...]
    pm = pm_ref[...]
    pm_kv = pm[0:LANES, 0:LANES]

    k_raw = z_ref[:, COL_K:COL_K + LANES]
    k_ms = _dot((k_raw * k_raw).astype(BF16), pm_kv)
    yield
    k_cur = _rope(k_raw * lax.rsqrt(k_ms + EPS) * kg_ref[...], cos, sin)
    v_cur = z_ref[:, COL_V:COL_V + LANES]
    k_dup = _dup_heads(jnp.concatenate([kprev_ref[...], k_cur], axis=0))
    v_dup = _dup_heads(jnp.concatenate([vprev_ref[...], v_cur], axis=0))
    kprev_ref[...] = k_cur[rows - WINDOW:, :]
    vprev_ref[...] = v_cur[rows - WINDOW:, :]
    q = z_ref[:, COL_Q:COL_Q + B_WIDTH]
    q_ms = jnp.concatenate(
        [_dot((q[:, s:s + 256] * q[:, s:s + 256]).astype(BF16), pm) for s in (0, 256)], axis=1)
    yield
    cos_q = jnp.concatenate([cos] * 4, axis=1)
    sin_q = jnp.concatenate([sin] * 4, axis=1)
    qr = _rope(q * lax.rsqrt(q_ms + EPS) * qg_ref[...], cos_q, sin_q) * (HEAD_DIM ** -0.5)
    low_q = (lax.broadcasted_iota(jnp.int32, qr.shape, 1) % LANES) < HEAD_DIM
    q_low = jnp.where(low_q, qr, 0.0).astype(BF16)
    q_high = jnp.where(low_q, 0.0, qr).astype(BF16)

    u = jax.nn.gelu(z_ref[:, COL_A:COL_A + A_WIDTH])
    v = jax.nn.gelu(z_ref[:, COL_A + A_WIDTH:COL_A + 2 * A_WIDTH])
    v_mean = _dot(v.astype(BF16), pm)
    yield
    vc = v - v_mean
    v_var = _dot((vc * vc).astype(BF16), pm)
    yield
    vn = (vc * lax.rsqrt(v_var + EPS) * lng_ref[...] + lnb_ref[...]).astype(BF16)
    lane = lax.broadcasted_iota(jnp.int32, (CHUNK, A_WIDTH), 1)
    in_head = [(lane >= h * HEAD_DIM) & (lane < (h + 1) * HEAD_DIM) for h in range(A_HEADS)]
    svs = []
    for c in range(rows // CHUNK):
        vn_c = vn[c * CHUNK:(c + 1) * CHUNK]
        stacked = jnp.concatenate(
            [jnp.where(in_head[h], vn_c, jnp.zeros((), BF16)) for h in range(A_HEADS)], axis=0)
        svs.append(_dot(ws_ref[...], stacked))
    yield
    for c in range(rows // CHUNK):
        chunk = slice(c * CHUNK, (c + 1) * CHUNK)
        ya_ref[chunk, :] = _rms(u[chunk] * (svs[c] + bs_ref[...]), mga_ref[...]).astype(BF16)

    qi = lax.broadcasted_iota(jnp.int32, (4 * WINDOW, WINDOW), 0) % WINDOW
    kc = lax.broadcasted_iota(jnp.int32, (4 * WINDOW, WINDOW), 1)
    from_cur = kc <= qi
    head_slot = lax.broadcasted_iota(jnp.int32, (4 * WINDOW, 1), 0) // WINDOW
    prev_ok = kc >= jnp.where(first_block, WINDOW, 0)
    low_o = lax.broadcasted_iota(jnp.int32, (WINDOW, LANES), 1) < HEAD_DIM
    zero = jnp.zeros((), F32)

    def scores(b, j):
        qrows = slice(b * WINDOW, (b + 1) * WINDOW)
        pair_a = slice((2 * j) * LANES, (2 * j + 1) * LANES)
        pair_b = slice((2 * j + 1) * LANES, (2 * j + 2) * LANES)
        qs = jnp.concatenate([q_low[qrows, pair_a], q_low[qrows, pair_b],
                              q_high[qrows, pair_a], q_high[qrows, pair_b]], axis=0)
        return _dot_nt(qs, k_dup[j][b * WINDOW:(b + 2) * WINDOW])

    units = [(b, j) for b in range(nq) for j in range(B_KV_HEADS)]
    pending = [scores(*units[k]) for k in range(SCORES_AHEAD)]
    yield
    pairs = []
    for u, (b, j) in enumerate(units):
        s = pending.pop(0)
        if u + SCORES_AHEAD < len(units):
            pending.append(scores(*units[u + SCORES_AHEAD]))
        s_prev = s[:, 0:WINDOW]
        if b == 0:
            s_prev = jnp.where(prev_ok, s_prev, NEG_BIG)
        s = jnp.where(from_cur, s[:, WINDOW:], s_prev)
        sink = jnp.where(head_slot == 0, sinks_ref[layer, 4 * j],
                         jnp.where(head_slot == 1, sinks_ref[layer, 4 * j + 2],
                                   jnp.where(head_slot == 2, sinks_ref[layer, 4 * j + 1],
                                             sinks_ref[layer, 4 * j + 3])))
        m = jnp.maximum(jnp.max(s, axis=-1, keepdims=True), sink)
        pr = jnp.exp(s - m)
        denom = jnp.sum(pr, axis=-1, keepdims=True) + jnp.exp(sink - m)
        pn = pr * (1.0 / denom)
        p2 = jnp.concatenate([jnp.where(from_cur, zero, pn),
                              jnp.where(from_cur, pn, zero)], axis=1).astype(BF16)
        yield
        o = _dot(p2, v_dup[j][b * WINDOW:(b + 2) * WINDOW])
        yield
        pairs += [jnp.where(low_o, o[0:WINDOW], o[2 * WINDOW:3 * WINDOW]),
                  jnp.where(low_o, o[WINDOW:2 * WINDOW], o[3 * WINDOW:4 * WINDOW])]
        if j == B_KV_HEADS - 1:
            yb = jnp.concatenate(pairs, axis=1)
            yb_ref[b * WINDOW:(b + 1) * WINDOW, :] = _rms(yb, mgb_ref[...]).astype(BF16)
            pairs = []


def _mixer_stages(rows):
    return 6 + 2 * B_KV_HEADS * (rows // WINDOW)


MIXER_STAGES = _mixer_stages(ROWS_MIX)
POST_STAGES = 5 + FF_SPLIT


def _run(*staged):
    total = max(n for _, n in staged)
    done = [0] * len(staged)
    for step in range(1, total + 1):
        for k, (gen, n) in enumerate(staged):
            while done[k] * total < step * n:
                next(gen, None)
                done[k] += 1
    for gen, _ in staged:
        for _ in gen:
            pass


def _cast_specs(items, n_steps):
    step = lambda i: jnp.minimum(i, n_steps - 1)
    in_specs, out_specs, out_shapes = [], [], []
    for arr, layer, col0 in items:
        _, rows, cols = arr.shape
        chunk = rows // n_steps
        in_specs.append(
            pl.BlockSpec((None, chunk, cols), lambda i, layer=layer: (layer, step(i), 0)))
        out_specs.append(pl.BlockSpec((chunk, cols - col0), lambda i: (step(i), 0)))
        out_shapes.append(jax.ShapeDtypeStruct((rows, cols - col0), BF16))
    return in_specs, out_specs, out_shapes


def _cast_chunks(src_refs, dst_refs):
    for src, dst in zip(src_refs, dst_refs):
        x = src[...]
        dst[...] = x[:, x.shape[1] - dst.shape[1]:].astype(BF16)


N_MIXER_IN = 9


def _mix_kernel(layer, blocks_per_seq, sinks_ref, h_ref, cos_ref, sin_ref, g_ref, wa_ref, wr_ref,
                *rest):
    w_ref = (wa_ref, wr_ref)
    mixer_in, rest = rest[:N_MIXER_IN], rest[N_MIXER_IN:]
    n_cast = (len(rest) - 6) // 2
    cast_src, outs, cast_dst = rest[:n_cast], rest[n_cast:n_cast + 3], rest[n_cast + 3:-3]
    mixer_refs = mixer_in + outs
    z_ref, kprev_ref, vprev_ref = rest[-3:]
    first_block = (pl.program_id(0) % blocks_per_seq) == 0

    @pl.when(first_block)
    def _():
        kprev_ref[...] = jnp.zeros_like(kprev_ref)
        vprev_ref[...] = jnp.zeros_like(vprev_ref)

    _cast_chunks(cast_src, cast_dst)
    _mix_project(h_ref, g_ref, w_ref, z_ref)
    rows = z_ref.shape[0]
    _run((_mix_mixers(layer, first_block, sinks_ref, cos_ref, sin_ref, *mixer_refs,
                      z_ref, kprev_ref, vprev_ref, rows=rows), _mixer_stages(rows)))


def _mix_param_specs(layer):
    return [
        _layer_spec((1, D_MODEL), layer),
        _layer_spec((D_MODEL, COL_Q), layer),
        _const_spec((D_MODEL, IN_COLS - COL_Q)),
        _layer_spec((CHUNK, A_HEADS * CHUNK), layer),
        _layer_spec((CHUNK, A_WIDTH), layer),
        _layer_spec((1, A_WIDTH), layer),
        _layer_spec((1, A_WIDTH), layer),
        _layer_spec((1, A_WIDTH), layer),
        _layer_spec((1, B_WIDTH), layer),
        _layer_spec((1, LANES), layer),
        _layer_spec((1, B_WIDTH), layer),
        _const_spec((256, 256)),
    ]


def _mix_out_shapes(n):
    return [jax.ShapeDtypeStruct((n, A_WIDTH), BF16),
            jax.ShapeDtypeStruct((n, B_WIDTH), BF16),
            jax.ShapeDtypeStruct((n, C_WIDTH), F32)]


def _mix_scratch(rows):
    return [
        pltpu.VMEM((rows, IN_COLS), F32),
        pltpu.VMEM((WINDOW, LANES), F32),
        pltpu.VMEM((WINDOW, LANES), F32),
    ]


def _mix(layer, h, cos2, sin2, sinks, mix_params, casts, seq_len):
    n = h.shape[0]
    n_blocks = n // ROWS_FIRST
    rb = lambda w_: pl.BlockSpec((ROWS_FIRST, w_), lambda i: (i, 0))
    cast_in, cast_out, cast_shapes = _cast_specs(casts, n_blocks)
    return pl.pallas_call(
        functools.partial(_mix_kernel, layer, seq_len // ROWS_FIRST),
        grid=(n_blocks,),
        in_specs=[pl.BlockSpec(memory_space=pltpu.SMEM), rb(D_MODEL), rb(LANES), rb(LANES)]
        + _mix_param_specs(layer) + cast_in,
        out_specs=[rb(A_WIDTH), rb(B_WIDTH), rb(C_WIDTH)] + cast_out,
        out_shape=_mix_out_shapes(n) + cast_shapes,
        scratch_shapes=_mix_scratch(ROWS_FIRST),
        compiler_params=pltpu.CompilerParams(
            dimension_semantics=("arbitrary",), vmem_limit_bytes=VMEM_LIMIT),
        name="mix",
    )(sinks, h, cos2, sin2, *mix_params, *[c[0] for c in casts])


def _tables_kernel(are_ref, aim_ref, ldt_ref, bre_ref, bim_ref, cre_ref, cim_ref, d_ref,
                   kt_ref, wb_ref, wct_ref, astep_ref, atab_ref):
    a_re, a_im = are_ref[...], aim_ref[...]
    dt = jnp.exp(ldt_ref[...])
    mag = jnp.exp(a_re * dt)
    lr, li = mag * jnp.cos(a_im * dt), mag * jnp.sin(a_im * dt)
    den = a_re * a_re + a_im * a_im
    fr, fi = _cmul(lr - 1.0, li, a_re / den, -a_im / den)
    bbr, bbi = _cmul(fr, fi, bre_ref[...], bim_ref[...])
    c_re, c_im = cre_ref[...], cim_ref[...]

    row_g = lax.broadcasted_iota(jnp.int32, (QUAD_WIDTH, QUAD_STATE // 2), 0) // C_GROUP
    col_g = lax.broadcasted_iota(jnp.int32, (QUAD_WIDTH, QUAD_STATE // 2), 1) // C_STATE
    same_group = row_g == col_g

    def quad_block(mr, mi, q):
        cols = slice(q * QUAD_STATE // 2, (q + 1) * QUAD_STATE // 2)
        tile = lambda m: jnp.where(same_group, jnp.concatenate([m[:, cols]] * QUAD_GROUPS, axis=0), 0.0)
        return jnp.concatenate([tile(mr), tile(mi)], axis=1)

    pows = [(jnp.ones_like(lr), jnp.zeros_like(li))]
    for _ in range(SSM_T):
        pows.append(_cmul(pows[-1][0], pows[-1][1], lr, li))

    rr = lax.broadcasted_iota(jnp.int32, (PACKED, PACKED), 0)
    cc = lax.broadcasted_iota(jnp.int32, (PACKED, PACKED), 1)
    for q in range(QUADS):
        into_state, from_state, lag_out = [], [], []
        for s in range(SSM_T):
            wr, wi = _cmul(pows[SSM_T - 1 - s][0], pows[SSM_T - 1 - s][1], bbr, bbi)
            into_state.append(quad_block(wr, wi, q))
            mr, mi = _cmul(pows[s + 1][0], pows[s + 1][1], c_re, c_im)
            from_state.append(quad_block(mr, -mi, q))
            mr, mi = _cmul(pows[s][0], pows[s][1], c_re, c_im)
            lag_out.append(quad_block(mr, -mi, q))
        wb_ref[q] = jnp.concatenate(into_state, axis=0).astype(BF16)
        wct_ref[q] = jnp.concatenate(from_state, axis=0).astype(BF16)
        a_hi, a_lo = _split(quad_block(bbr, bbi, q))
        b_hi, b_lo = _split(jnp.concatenate(lag_out, axis=0))
        lags = _dot_nt(a_hi, b_hi) + _dot_nt(a_hi, b_lo) + _dot_nt(a_lo, b_hi)
        lane = lax.broadcasted_iota(jnp.int32, lags.shape, 1)
        k = jnp.concatenate(
            [lags] + [jnp.where(lane >= s * QUAD_WIDTH, pltpu.roll(lags, s * QUAD_WIDTH, 1), 0.0)
                      for s in range(1, SSM_T)], axis=0)
        kt_ref[q] = (k + jnp.where(rr == cc, d_ref[q], 0.0)).astype(BF16)

    half = QUAD_STATE // 2
    for q in range(QUADS):
        src = slice(q * half, (q + 1) * half)
        re_cols, im_cols = pl.ds(q * QUAD_STATE, half), pl.ds(q * QUAD_STATE + half, half)
        sr, si = pows[SSM_T][0][:, src], pows[SSM_T][1][:, src]
        atab_ref[0:1, re_cols] = sr
        atab_ref[0:1, im_cols] = si
        for k in range(SCAN_STEPS):
            span = 1 << k
            astep_ref[pl.ds(k, 1), re_cols] = sr
            astep_ref[pl.ds(k, 1), im_cols] = si
            tr, ti = _cmul(atab_ref[0:span, re_cols], atab_ref[0:span, im_cols], sr, si)
            atab_ref[pl.ds(span, span), re_cols] = tr
            atab_ref[pl.ds(span, span), im_cols] = ti
            sr, si = _cmul(sr, si, sr, si)


def _tables(a_re, a_im, ldt, b_re, b_im, c_re, c_im, d_skip):
    depth = a_re.shape[0]
    vec = lambda w: pl.BlockSpec((None, 1, w), lambda l: (l, 0, 0))
    mat = lambda r, c: pl.BlockSpec((None, r, c), lambda l: (l, 0, 0))
    cube = lambda r, c: pl.BlockSpec((None, QUADS, r, c), lambda l: (l, 0, 0, 0))
    packed = jax.ShapeDtypeStruct((depth, QUADS, PACKED, PACKED), BF16)
    return pl.pallas_call(
        _tables_kernel,
        grid=(depth,),
        in_specs=[vec(N_STATE), vec(N_STATE), vec(N_STATE),
                  mat(C_GROUP, N_STATE), mat(C_GROUP, N_STATE),
                  mat(C_GROUP, N_STATE), mat(C_GROUP, N_STATE), cube(1, PACKED)],
        out_specs=[cube(PACKED, PACKED), cube(PACKED, PACKED), cube(PACKED, PACKED),
                   mat(SCAN_STEPS, 2 * N_STATE), mat(SUBLANES, 2 * N_STATE)],
        out_shape=[packed, packed, packed,
                   jax.ShapeDtypeStruct((depth, SCAN_STEPS, 2 * N_STATE), F32),
                   jax.ShapeDtypeStruct((depth, SUBLANES, 2 * N_STATE), F32)],
        compiler_params=pltpu.CompilerParams(
            dimension_semantics=("arbitrary",), vmem_limit_bytes=VMEM_LIMIT),
        name="ssm_tables",
    )(a_re, a_im, ldt, b_re, b_im, c_re, c_im, d_skip)


def _ssm_kernel(blocks_per_seq, ua_ref, ub_ref, kt_ref, wb_ref, wct_ref, astep_ref, atab_ref,
                ya_ref, yb_ref, inc_ref, sprev_ref, carry_ref):
    @pl.when((pl.program_id(0) % blocks_per_seq) == 0)
    def _():
        carry_ref[...] = jnp.zeros_like(carry_ref)

    low = lax.broadcasted_iota(jnp.int32, (SSM_N, LANES), 1) < QUAD_WIDTH

    halves = [[ref[pl.ds(j, SSM_N, stride=SSM_T), :] for j in range(SSM_T)]
              for ref in (ua_ref, ub_ref)]

    def pack(q):
        src, upper = halves[q // 2], q % 2 == 1
        cols = []
        for m in range(SSM_T // 2):
            even, odd = src[2 * m], src[2 * m + 1]
            if upper:
                even = pltpu.roll(even, QUAD_WIDTH, 1)
            else:
                odd = pltpu.roll(odd, QUAD_WIDTH, 1)
            cols.append(jnp.where(low, even, odd))
        return jnp.concatenate(cols, axis=1).astype(BF16)

    xs = [pack(q) for q in range(QUADS)]

    for q in range(QUADS):
        inc_ref[:, q * QUAD_STATE:(q + 1) * QUAD_STATE] = _dot(xs[q], wb_ref[q])

    n_groups = SSM_N // SUBLANES
    sub = lax.broadcasted_iota(jnp.int32, (n_groups, SUBLANES, LANES), 1)
    sub2 = lax.broadcasted_iota(jnp.int32, (SUBLANES, LANES), 0)
    tiles_per_quad = QUAD_STATE // 2 // LANES
    y_lag = []
    for lt in range(N_STATE // LANES):
        q, part = lt // tiles_per_quad, lt % tiles_per_quad
        re_cols = pl.ds(q * QUAD_STATE + part * LANES, LANES)
        im_cols = pl.ds(q * QUAD_STATE + QUAD_STATE // 2 + part * LANES, LANES)
        xr = inc_ref[:, re_cols].reshape(n_groups, SUBLANES, LANES)
        xi = inc_ref[:, im_cols].reshape(n_groups, SUBLANES, LANES)
        for k in range(SCAN_STEPS):
            sh = 1 << k
            ar = astep_ref[pl.ds(k, 1), re_cols]
            ai = astep_ref[pl.ds(k, 1), im_cols]
            keep = sub >= sh
            sr = jnp.where(keep, pltpu.roll(xr, sh, 1), 0.0)
            si = jnp.where(keep, pltpu.roll(xi, sh, 1), 0.0)
            xr, xi = xr + ar * sr - ai * si, xi + ar * si + ai * sr
        tr = atab_ref[:, re_cols]
        ti = atab_ref[:, im_cols]
        er = jnp.broadcast_to(carry_ref[:, re_cols], (SUBLANES, LANES))
        ei = jnp.broadcast_to(carry_ref[:, im_cols], (SUBLANES, LANES))
        before_r, before_i = [], []
        for g in range(n_groups):
            fr = xr[g] + tr * er - ti * ei
            fi = xi[g] + tr * ei + ti * er
            before_r.append(jnp.where(sub2 == 0, er, pltpu.roll(fr, 1, 0)))
            before_i.append(jnp.where(sub2 == 0, ei, pltpu.roll(fi, 1, 0)))
            er = jnp.broadcast_to(fr[SUBLANES - 1:SUBLANES, :], (SUBLANES, LANES))
            ei = jnp.broadcast_to(fi[SUBLANES - 1:SUBLANES, :], (SUBLANES, LANES))
        sprev_ref[:, re_cols] = jnp.concatenate(before_r, axis=0).astype(BF16)
        sprev_ref[:, im_cols] = jnp.concatenate(before_i, axis=0).astype(BF16)
        carry_ref[:, re_cols] = er[0:1, :]
        carry_ref[:, im_cols] = ei[0:1, :]

        if part == tiles_per_quad - 1:
            y_lag.append(_dot(xs[q], kt_ref[q]))

    ys = [y_lag[q] + _dot_nt(sprev_ref[:, q * QUAD_STATE:(q + 1) * QUAD_STATE], wct_ref[q])
          for q in range(QUADS)]
    for t in range(SSM_T):
        cols = slice((t // 2) * LANES, (t // 2 + 1) * LANES)
        for y_ref, (qa, qb) in ((ya_ref, (0, 1)), (yb_ref, (2, 3))):
            lower, upper = ys[qa][:, cols], ys[qb][:, cols]
            if t % 2 == 1:
                lower = pltpu.roll(lower, QUAD_WIDTH, 1)
            else:
                upper = pltpu.roll(upper, QUAD_WIDTH, 1)
            y_ref[pl.ds(t, SSM_N, stride=SSM_T), :] = jnp.where(low, lower, upper)


def _ssm(layer, zc, kt, wb, wct, astep, atab, seq_len):
    n = zc.shape[0]
    half = lambda c: pl.BlockSpec((ROWS_SSM, LANES), lambda i: (i, c))
    return pl.pallas_call(
        functools.partial(_ssm_kernel, seq_len // ROWS_SSM),
        grid=(n // ROWS_SSM,),
        in_specs=[
            half(0), half(1),
            _layer_spec((QUADS, PACKED, PACKED), layer),
            _layer_spec((QUADS, PACKED, QUAD_STATE), layer),
            _layer_spec((QUADS, PACKED, QUAD_STATE), layer),
            _layer_spec((SCAN_STEPS, 2 * N_STATE), layer),
            _layer_spec((SUBLANES, 2 * N_STATE), layer),
        ],
        out_specs=[pl.BlockSpec((ROWS_SSM, LANES), lambda i: (i, 0)) for _ in range(2)],
        out_shape=[jax.ShapeDtypeStruct((n, LANES), F32) for _ in range(2)],
        scratch_shapes=[
            pltpu.VMEM((SSM_N, 2 * N_STATE), F32),
            pltpu.VMEM((SSM_N, 2 * N_STATE), BF16),
            pltpu.VMEM((1, 2 * N_STATE), F32),
        ],
        compiler_params=pltpu.CompilerParams(
            dimension_semantics=("arbitrary",), vmem_limit_bytes=VMEM_LIMIT),
        name="ssm",
    )(zc, zc, kt, wb, wct, astep, atab)


def _post_stages(h_ref, ya_ref, yb_ref, sa_ref, sb_ref, p_ref, glu1_ref, glu2_ref, mgc_ref,
                 wo_ref, g1_ref, w1_ref, w2_ref, g2_ref, wg_ref, wp_ref, *emit):
    y = jax.nn.gelu(jnp.concatenate([sa_ref[...], sb_ref[...]], axis=1)).astype(BF16)
    glu_a, glu_b = _dot(y, glu1_ref[...]), _dot(y, glu2_ref[...])
    mixed = (_dot(ya_ref[...], wo_ref[0:A_WIDTH, :])
             + _dot(yb_ref[...], wo_ref[A_WIDTH:A_WIDTH + B_WIDTH, :]))
    yield
    yc = _rms(glu_a * jax.nn.sigmoid(glu_b), mgc_ref[...]).astype(BF16)
    mixed = mixed + _dot(yc, wo_ref[A_WIDTH + B_WIDTH:, :])
    yield
    h = h_ref[...] + mixed
    hn = _rms(h, g1_ref[...]).astype(BF16)
    ple = _dot(p_ref[...].astype(BF16), wp_ref[...])
    piece = D_FF // FF_SPLIT
    up = _dot(hn, w1_ref[:, 0:piece])
    yield
    ff = None
    for c in range(FF_SPLIT):
        a = jnp.maximum(up, 0.0)
        down = _dot((a * a).astype(BF16), w2_ref[c * piece:(c + 1) * piece, :])
        if c + 1 < FF_SPLIT:
            up = _dot(hn, w1_ref[:, (c + 1) * piece:(c + 2) * piece])
        yield
        ff = down if ff is None else ff + down
    h = h + ff
    gate = _dot(_rms(h, g2_ref[...]).astype(BF16), wg_ref[...])
    yield
    h = h + jax.nn.sigmoid(gate) * ple
    for fn in emit:
        fn(h)
    yield


N_POST_IN = 16


def _store_to(ref):
    def store(value):
        ref[...] = value
    return store


def _post_kernel(*refs):
    _run((_post_stages(*refs[:N_POST_IN], _store_to(refs[N_POST_IN])), POST_STAGES))


def _post_specs(layer, row_block, rows=ROWS_MIX):
    rb = lambda w: pl.BlockSpec((rows, w), lambda i: (row_block(i), 0))
    return [
        rb(D_MODEL), rb(A_WIDTH), rb(B_WIDTH), rb(LANES), rb(LANES),
        pl.BlockSpec((None, rows, PLE_DIM), lambda i: (layer, row_block(i), 0)),
        _layer_spec((C_WIDTH, C_WIDTH), layer),
        _layer_spec((C_WIDTH, C_WIDTH), layer),
        _layer_spec((1, C_WIDTH), layer),
        _const_spec((D_MODEL, D_MODEL)),
        _layer_spec((1, D_MODEL), layer),
        _const_spec((D_MODEL, D_FF)),
        _const_spec((D_FF, D_MODEL)),
        _layer_spec((1, D_MODEL), layer),
        _const_spec((D_MODEL, D_MODEL)),
        _layer_spec((PLE_DIM, D_MODEL), layer),
    ]


def _post(layer, h, ya, yb, yc, p, post_params):
    n = h.shape[0]
    return pl.pallas_call(
        _post_kernel,
        grid=(n // ROWS_PROJ,),
        in_specs=_post_specs(layer, lambda i: i, ROWS_PROJ),
        out_specs=pl.BlockSpec((ROWS_PROJ, D_MODEL), lambda i: (i, 0)),
        out_shape=jax.ShapeDtypeStruct((n, D_MODEL), F32),
        compiler_params=pltpu.CompilerParams(
            dimension_semantics=("arbitrary",), vmem_limit_bytes=VMEM_LIMIT),
        name="post",
    )(h, ya, yb, *yc, p, *post_params)


def _fused_kernel(layer, blocks_per_seq, n_blocks, sinks_ref, *refs):
    post_refs, refs = refs[:N_POST_IN], refs[N_POST_IN:]
    cos_ref, sin_ref, g_ref = refs[:3]
    w_ref = refs[3:5]
    mixer_refs, refs = refs[5:5 + N_MIXER_IN], refs[5 + N_MIXER_IN:]
    n_cast = (len(refs) - 8) // 2
    cast_src, cast_dst = refs[:n_cast], refs[n_cast + 4:-4]
    h_out_ref, ya_ref, yb_ref, zc_ref = refs[n_cast:n_cast + 4]
    hs_ref, z_ref, kprev_ref, vprev_ref = refs[-4:]
    i = pl.program_id(0)
    first_block = (i % blocks_per_seq) == 1 % blocks_per_seq

    @pl.when((i == 0) | first_block)
    def _():
        kprev_ref[...] = jnp.zeros_like(kprev_ref)
        vprev_ref[...] = jnp.zeros_like(vprev_ref)

    @pl.when(i == 0)
    def _():
        hs_ref[...] = jnp.zeros_like(hs_ref)

    def mixers():
        return (_mix_mixers(layer, first_block, sinks_ref, cos_ref, sin_ref, *mixer_refs,
                            ya_ref, yb_ref, zc_ref, z_ref, kprev_ref, vprev_ref), MIXER_STAGES)

    @pl.when(i < n_blocks)
    def _():
        _cast_chunks(cast_src, cast_dst)
        _mix_project(hs_ref, g_ref, w_ref, z_ref)
        _run((_post_stages(*post_refs, _store_to(h_out_ref), _store_to(hs_ref)), POST_STAGES),
             mixers())

    @pl.when(i == n_blocks)
    def _():
        _mix_project(hs_ref, g_ref, w_ref, z_ref)
        _run(mixers())


def _fused(layer, h, ya, yb, yc, p, post_params, cos2, sin2, sinks, mix_params, casts, seq_len):
    n = h.shape[0]
    n_blocks = n // ROWS_MIX
    merged = lambda i: jnp.minimum(i, n_blocks - 1)
    mixed = lambda i: jnp.maximum(i - 1, 0)
    rb = lambda w_: pl.BlockSpec((ROWS_MIX, w_), lambda i: (mixed(i), 0))
    cast_in, cast_out, cast_shapes = _cast_specs(casts, n_blocks)
    return pl.pallas_call(
        functools.partial(_fused_kernel, layer, seq_len // ROWS_MIX, n_blocks),
        grid=(n_blocks + 1,),
        in_specs=[pl.BlockSpec(memory_space=pltpu.SMEM)]
        + _post_specs(layer - 1, merged)
        + [rb(LANES), rb(LANES)] + _mix_param_specs(layer) + cast_in,
        out_specs=[pl.BlockSpec((ROWS_MIX, D_MODEL), lambda i: (merged(i), 0)),
                   rb(A_WIDTH), rb(B_WIDTH), rb(C_WIDTH)] + cast_out,
        out_shape=[jax.ShapeDtypeStruct((n, D_MODEL), F32)] + _mix_out_shapes(n) + cast_shapes,
        scratch_shapes=[pltpu.VMEM((ROWS_MIX, D_MODEL), F32)] + _mix_scratch(ROWS_MIX),
        compiler_params=pltpu.CompilerParams(
            dimension_semantics=("arbitrary",), vmem_limit_bytes=VMEM_LIMIT),
        name="fused",
    )(sinks, h, ya, yb, *yc, p, *post_params, cos2, sin2, *mix_params, *[c[0] for c in casts])


def kernel(x, p, positions, attn_norm_g, w_in, gmlp_ln_g, gmlp_ln_b, gmlp_ws, gmlp_bs, q_norm_g, k_norm_g, sinks, ssm_a_re, ssm_a_im, ssm_log_dt, ssm_b_re, ssm_b_im, ssm_c_re, ssm_c_im, ssm_d, glu_w1, glu_w2, mix_out_g, w_out, mlp_norm_g, w_ff1, w_ff2, ple_norm_g, w_ple_gate, w_ple_proj):
    bsz, seq_len, d_model = x.shape
    depth = w_in.shape[0]
    n = bsz * seq_len
    assert d_model == D_MODEL and w_in.shape[1:] == (D_MODEL, IN_COLS)
    assert seq_len % ROWS_SSM == 0 and seq_len % ROWS_FIRST == 0 and seq_len % ROWS_MIX == 0
    assert n % ROWS_PROJ == 0 and p.shape == (depth, bsz, seq_len, PLE_DIM)

    inv = 1.0 / (ROPE_THETA ** (jnp.arange(0, HEAD_DIM, 2, dtype=F32) / HEAD_DIM))
    ang = positions.astype(F32).reshape(n, 1) * inv
    cos2 = jnp.tile(jnp.cos(ang), (1, 4))
    sin2 = jnp.tile(jnp.concatenate([-jnp.sin(ang), jnp.sin(ang)], axis=1), (1, 2))

    seg = np.arange(256) // HEAD_DIM
    pm = jnp.asarray((seg[:, None] == seg[None, :]) / HEAD_DIM, dtype=BF16)

    rows = lambda v: v.reshape(depth, 1, -1).astype(F32)
    g_attn, g_mlp, g_ple = rows(attn_norm_g), rows(mlp_norm_g), rows(ple_norm_g)
    ln_g, ln_b = rows(gmlp_ln_g), rows(gmlp_ln_b)
    mg_a = rows(mix_out_g[:, :A_WIDTH])
    mg_b = rows(mix_out_g[:, A_WIDTH:A_WIDTH + B_WIDTH])
    mg_c = rows(mix_out_g[:, A_WIDTH + B_WIDTH:])
    qg = rows(jnp.tile(q_norm_g, (1, B_Q_HEADS)))
    kg = rows(jnp.tile(k_norm_g, (1, B_KV_HEADS)))
    bs = jnp.repeat(jnp.swapaxes(gmlp_bs, 1, 2), HEAD_DIM, axis=2)
    w_a = w_in[:, :, :2 * A_WIDTH].reshape(depth, D_MODEL, A_HEADS, 2, HEAD_DIM)
    w_a = jnp.swapaxes(w_a, 2, 3).reshape(depth, D_MODEL, 2 * A_WIDTH)
    w_in_a, w_in_rest0 = w_a.astype(BF16), w_in[0, :, 2 * A_WIDTH:].astype(BF16)
    causal = np.tril(np.ones((CHUNK, CHUNK), dtype=bool))
    ws_b = jnp.swapaxes(jnp.where(causal, gmlp_ws, 0.0), 1, 2).reshape(
        depth, CHUNK, A_HEADS * CHUNK).astype(BF16)
    w_proj_b = w_ple_proj.astype(BF16)
    glu1_b, glu2_b = glu_w1.astype(BF16), glu_w2.astype(BF16)
    p2 = p.reshape(depth, n, PLE_DIM)

    kd, wb, wct, astep, atab = _tables(
        rows(ssm_a_re), rows(ssm_a_im), rows(jnp.repeat(ssm_log_dt, C_STATE, axis=1)),
        jnp.transpose(ssm_b_re, (0, 3, 1, 2)).reshape(depth, C_GROUP, N_STATE),
        jnp.transpose(ssm_b_im, (0, 3, 1, 2)).reshape(depth, C_GROUP, N_STATE),
        jnp.transpose(ssm_c_re, (0, 2, 1, 3)).reshape(depth, C_GROUP, N_STATE),
        jnp.transpose(ssm_c_im, (0, 2, 1, 3)).reshape(depth, C_GROUP, N_STATE),
        jnp.tile(ssm_d.reshape(depth, QUADS, 1, QUAD_WIDTH).astype(F32), (1, 1, 1, SSM_T)))

    def casts_for(layer):
        items = [(w_out, layer, 0), (w_ff1, layer, 0), (w_ff2, layer, 0), (w_ple_gate, layer, 0)]
        if layer + 1 < depth:
            items.append((w_in, layer + 1, 2 * A_WIDTH))
        return items

    def mix_params(w_in_rest):
        return (g_attn, w_in_a, w_in_rest, ws_b, bs, ln_g, ln_b, mg_a, qg, kg, mg_b, pm)

    def post_params(cast):
        w_out_b, w_ff1_b, w_ff2_b, w_gate_b = cast[:4]
        return (glu1_b, glu2_b, mg_c, w_out_b, g_mlp, w_ff1_b, w_ff2_b, g_ple, w_gate_b, w_proj_b)

    h = x.reshape(n, D_MODEL)
    ya, yb, zc, *cast = _mix(0, h, cos2, sin2, sinks, mix_params(w_in_rest0), casts_for(0),
                             seq_len)
    for i in range(depth):
        yc = _ssm(i, zc, kd, wb, wct, astep, atab, seq_len)
        if i + 1 < depth:
            h, ya, yb, zc, *cast = _fused(i + 1, h, ya, yb, yc, p2, post_params(cast), cos2, sin2,
                                          sinks, mix_params(cast[4]), casts_for(i + 1), seq_len)
        else:
            h = _post(i, h, ya, yb, yc, p2, post_params(cast))
    return h.reshape(bsz, seq_len, D_MODEL)
```

```python
import functools
import math

import numpy as np
import jax
import jax.numpy as jnp
from jax import lax
from jax.experimental import pallas as pl
from jax.experimental.pallas import tpu as pltpu

F32 = jnp.float32
BF16 = jnp.bfloat16

D_MODEL = 1024
HEAD_DIM = 64
A_WIDTH = 256
A_HEADS = 4
CHUNK = 128
B_WIDTH = 512
B_Q_HEADS = 8
B_KV_HEADS = 2
WINDOW = 128
ROPE_THETA = 10000.0
C_WIDTH = 256
C_GROUP = 16
C_GROUPS = 16
C_STATE = 64
N_STATE = C_GROUPS * C_STATE
IN_COLS = 1536
D_FF = 4096
PLE_DIM = 256
EPS = 1e-6
NEG_BIG = -1e30
LANES = 128

COL_A = 0
COL_Q = 512
COL_K = 1024
COL_V = 1152
COL_C = 1280

ROWS_PROJ = 512
ROWS_MIX = 512
ROWS_FIRST = 1024
ROWS_SSM = 2048
SSM_T = 8
SSM_N = ROWS_SSM // SSM_T
QUAD_GROUPS = 4
QUADS = C_GROUPS // QUAD_GROUPS
QUAD_WIDTH = QUAD_GROUPS * C_GROUP
QUAD_STATE = 2 * QUAD_GROUPS * C_STATE
PACKED = SSM_T * QUAD_WIDTH
SUBLANES = 8
SCAN_STEPS = int(math.log2(SUBLANES))
FF_SPLIT = 8
VMEM_LIMIT = 60 * 1024 * 1024


def _const_spec(shape):
    nd = len(shape)
    return pl.BlockSpec(shape, lambda *_: (0,) * nd, pipeline_mode=pl.Buffered(1))


def _layer_spec(shape, layer):
    nd = len(shape)
    return pl.BlockSpec((None,) + tuple(shape), lambda *_: (layer,) + (0,) * nd,
                        pipeline_mode=pl.Buffered(1))


def _rms(x, g):
    ms = jnp.mean(x * x, axis=-1, keepdims=True)
    return x * lax.rsqrt(ms + EPS) * g


def _dot(a, b):
    return jnp.dot(a, b, preferred_element_type=F32)


def _dot_nt(a, b):
    return lax.dot_general(a, b, (((1,), (1,)), ((), ())), preferred_element_type=F32)


def _split(x):
    hi = x.astype(BF16)
    return hi, (x - hi.astype(F32)).astype(BF16)


def _cmul(ar, ai, br, bi):
    return ar * br - ai * bi, ar * bi + ai * br


def _rope(x, cos, sin_signed):
    width = x.shape[-1]
    lane = lax.broadcasted_iota(jnp.int32, x.shape, 1)
    first_half = (lane % HEAD_DIM) < (HEAD_DIM // 2)
    partner = jnp.where(first_half,
                        pltpu.roll(x, width - HEAD_DIM // 2, 1),
                        pltpu.roll(x, HEAD_DIM // 2, 1))
    return x * cos + partner * sin_signed


def _dup_heads(x):
    low = lax.broadcasted_iota(jnp.int32, x.shape, 1) < HEAD_DIM
    sw = pltpu.roll(x, HEAD_DIM, 1)
    return [jnp.where(low, x, sw).astype(BF16), jnp.where(low, sw, x).astype(BF16)]


def _mix_project(h_ref, g_ref, w_refs, z_ref):
    wa_ref, wr_ref = w_refs
    xn = _rms(h_ref[...], g_ref[...]).astype(BF16)
    z_ref[:, COL_A:COL_Q] = _dot(xn, wa_ref[...])
    z_ref[:, COL_Q:] = _dot(xn, wr_ref[...])


def _mix_mixers(layer, first_block, sinks_ref, cos_ref, sin_ref, ws_ref, bs_ref, lng_ref, lnb_ref,
                mga_ref, qg_ref, kg_ref, mgb_ref, pm_ref, ya_ref, yb_ref, zc_ref,
                z_ref, kprev_ref, vprev_ref, rows=ROWS_MIX):
    nq = rows // WINDOW
    zc_ref[...] = z_ref[:, COL_C:COL_C + C_WIDTH]
    cos, sin = cos_ref[...], sin_ref[...]
    pm = pm_ref[...]
    pm_kv = pm[0:LANES, 0:LANES]

    k_raw = z_ref[:, COL_K:COL_K + LANES]
    k_ms = _dot((k_raw * k_raw).astype(BF16), pm_kv)
    yield
    k_cur = _rope(k_raw * lax.rsqrt(k_ms + EPS) * kg_ref[...], cos, sin)
    v_cur = z_ref[:, COL_V:COL_V + LANES]
    k_dup = _dup_heads(jnp.concatenate([kprev_ref[...], k_cur], axis=0))
    v_dup = _dup_heads(jnp.concatenate([vprev_ref[...], v_cur], axis=0))
    kprev_ref[...] = k_cur[rows - WINDOW:, :]
    vprev_ref[...] = v_cur[rows - WINDOW:, :]
    q = z_ref[:, COL_Q:COL_Q + B_WIDTH]
    q_ms = jnp.concatenate(
        [_dot((q[:, s:s + 256] * q[:, s:s + 256]).astype(BF16), pm) for s in (0, 256)], axis=1)
    yield
    cos_q = jnp.concatenate([cos] * 4, axis=1)
    sin_q = jnp.concatenate([sin] * 4, axis=1)
    qr = _rope(q * lax.rsqrt(q_ms + EPS) * qg_ref[...], cos_q, sin_q) * (HEAD_DIM ** -0.5)
    low_q = (lax.broadcasted_iota(jnp.int32, qr.shape, 1) % LANES) < HEAD_DIM
    q_low = jnp.where(low_q, qr, 0.0).astype(BF16)
    q_high = jnp.where(low_q, 0.0, qr).astype(BF16)

    u = jax.nn.gelu(z_ref[:, COL_A:COL_A + A_WIDTH])
    v = jax.nn.gelu(z_ref[:, COL_A + A_WIDTH:COL_A + 2 * A_WIDTH])
    v_mean = _dot(v.astype(BF16), pm)
    yield
    vc = v - v_mean
    v_var = _dot((vc * vc).astype(BF16), pm)
    yield
    vn = (vc * lax.rsqrt(v_var + EPS) * lng_ref[...] + lnb_ref[...]).astype(BF16)
    lane = lax.broadcasted_iota(jnp.int32, (CHUNK, A_WIDTH), 1)
    in_head = [(lane >= h * HEAD_DIM) & (lane < (h + 1) * HEAD_DIM) for h in range(A_HEADS)]
    svs = []
    for c in range(rows // CHUNK):
        vn_c = vn[c * CHUNK:(c + 1) * CHUNK]
        stacked = jnp.concatenate(
            [jnp.where(in_head[h], vn_c, jnp.zeros((), BF16)) for h in range(A_HEADS)], axis=0)
        svs.append(_dot(ws_ref[...], stacked))
    yield
    for c in range(rows // CHUNK):
        chunk = slice(c * CHUNK, (c + 1) * CHUNK)
        ya_ref[chunk, :] = _rms(u[chunk] * (svs[c] + bs_ref[...]), mga_ref[...]).astype(BF16)

    qi = lax.broadcasted_iota(jnp.int32, (4 * WINDOW, WINDOW), 0) % WINDOW
    kc = lax.broadcasted_iota(jnp.int32, (4 * WINDOW, WINDOW), 1)
    from_cur = kc <= qi
    head_slot = lax.broadcasted_iota(jnp.int32, (4 * WINDOW, 1), 0) // WINDOW
    prev_ok = kc >= jnp.where(first_block, WINDOW, 0)
    low_o = lax.broadcasted_iota(jnp.int32, (WINDOW, LANES), 1) < HEAD_DIM
    zero = jnp.zeros((), F32)

    def scores(b, j):
        qrows = slice(b * WINDOW, (b + 1) * WINDOW)
        pair_a = slice((2 * j) * LANES, (2 * j + 1) * LANES)
        pair_b = slice((2 * j + 1) * LANES, (2 * j + 2) * LANES)
        qs = jnp.concatenate([q_low[qrows, pair_a], q_low[qrows, pair_b],
                              q_high[qrows, pair_a], q_high[qrows, pair_b]], axis=0)
        return _dot_nt(qs, k_dup[j][b * WINDOW:(b + 2) * WINDOW])

    units = [(b, j) for b in range(nq) for j in range(B_KV_HEADS)]
    s_next = scores(*units[0])
    yield
    pairs = []
    for u, (b, j) in enumerate(units):
        s = s_next
        if u + 1 < len(units):
            s_next = scores(*units[u + 1])
        s_prev = s[:, 0:WINDOW]
        if b == 0:
            s_prev = jnp.where(prev_ok, s_prev, NEG_BIG)
        s = jnp.where(from_cur, s[:, WINDOW:], s_prev)
        sink = jnp.where(head_slot == 0, sinks_ref[layer, 4 * j],
                         jnp.where(head_slot == 1, sinks_ref[layer, 4 * j + 2],
                                   jnp.where(head_slot == 2, sinks_ref[layer, 4 * j + 1],
                                             sinks_ref[layer, 4 * j + 3])))
        m = jnp.maximum(jnp.max(s, axis=-1, keepdims=True), sink)
        pr = jnp.exp(s - m)
        denom = jnp.sum(pr, axis=-1, keepdims=True) + jnp.exp(sink - m)
        pn = pr * (1.0 / denom)
        p2 = jnp.concatenate([jnp.where(from_cur, zero, pn),
                              jnp.where(from_cur, pn, zero)], axis=1).astype(BF16)
        yield
        o = _dot(p2, v_dup[j][b * WINDOW:(b + 2) * WINDOW])
        yield
        pairs += [jnp.where(low_o, o[0:WINDOW], o[2 * WINDOW:3 * WINDOW]),
                  jnp.where(low_o, o[WINDOW:2 * WINDOW], o[3 * WINDOW:4 * WINDOW])]
        if j == B_KV_HEADS - 1:
            yb = jnp.concatenate(pairs, axis=1)
            yb_ref[b * WINDOW:(b + 1) * WINDOW, :] = _rms(yb, mgb_ref[...]).astype(BF16)
            pairs = []


def _mixer_stages(rows):
    return 6 + 2 * B_KV_HEADS * (rows // WINDOW)


MIXER_STAGES = _mixer_stages(ROWS_MIX)
POST_STAGES = 5 + FF_SPLIT


def _run(*staged):
    total = max(n for _, n in staged)
    done = [0] * len(staged)
    for step in range(1, total + 1):
        for k, (gen, n) in enumerate(staged):
            while done[k] * total < step * n:
                next(gen, None)
                done[k] += 1
    for gen, _ in staged:
        for _ in gen:
            pass


def _cast_specs(items, n_steps):
    step = lambda i: jnp.minimum(i, n_steps - 1)
    in_specs, out_specs, out_shapes = [], [], []
    for arr, layer, col0 in items:
        _, rows, cols = arr.shape
        chunk = rows // n_steps
        in_specs.append(
            pl.BlockSpec((None, chunk, cols), lambda i, layer=layer: (layer, step(i), 0)))
        out_specs.append(pl.BlockSpec((chunk, cols - col0), lambda i: (step(i), 0)))
        out_shapes.append(jax.ShapeDtypeStruct((rows, cols - col0), BF16))
    return in_specs, out_specs, out_shapes


def _cast_chunks(src_refs, dst_refs):
    for src, dst in zip(src_refs, dst_refs):
        x = src[...]
        dst[...] = x[:, x.shape[1] - dst.shape[1]:].astype(BF16)


N_MIXER_IN = 9


def _mix_kernel(layer, blocks_per_seq, sinks_ref, h_ref, cos_ref, sin_ref, g_ref, wa_ref, wr_ref,
                *rest):
    w_ref = (wa_ref, wr_ref)
    mixer_in, rest = rest[:N_MIXER_IN], rest[N_MIXER_IN:]
    n_cast = (len(rest) - 6) // 2
    cast_src, outs, cast_dst = rest[:n_cast], rest[n_cast:n_cast + 3], rest[n_cast + 3:-3]
    mixer_refs = mixer_in + outs
    z_ref, kprev_ref, vprev_ref = rest[-3:]
    first_block = (pl.program_id(0) % blocks_per_seq) == 0

    @pl.when(first_block)
    def _():
        kprev_ref[...] = jnp.zeros_like(kprev_ref)
        vprev_ref[...] = jnp.zeros_like(vprev_ref)

    _cast_chunks(cast_src, cast_dst)
    _mix_project(h_ref, g_ref, w_ref, z_ref)
    rows = z_ref.shape[0]
    _run((_mix_mixers(layer, first_block, sinks_ref, cos_ref, sin_ref, *mixer_refs,
                      z_ref, kprev_ref, vprev_ref, rows=rows), _mixer_stages(rows)))


def _mix_param_specs(layer):
    return [
        _layer_spec((1, D_MODEL), layer),
        _layer_spec((D_MODEL, COL_Q), layer),
        _const_spec((D_MODEL, IN_COLS - COL_Q)),
        _layer_spec((CHUNK, A_HEADS * CHUNK), layer),
        _layer_spec((CHUNK, A_WIDTH), layer),
        _layer_spec((1, A_WIDTH), layer),
        _layer_spec((1, A_WIDTH), layer),
        _layer_spec((1, A_WIDTH), layer),
        _layer_spec((1, B_WIDTH), layer),
        _layer_spec((1, LANES), layer),
        _layer_spec((1, B_WIDTH), layer),
        _const_spec((256, 256)),
    ]


def _mix_out_shapes(n):
    return [jax.ShapeDtypeStruct((n, A_WIDTH), BF16),
            jax.ShapeDtypeStruct((n, B_WIDTH), BF16),
            jax.ShapeDtypeStruct((n, C_WIDTH), F32)]


def _mix_scratch(rows):
    return [
        pltpu.VMEM((rows, IN_COLS), F32),
        pltpu.VMEM((WINDOW, LANES), F32),
        pltpu.VMEM((WINDOW, LANES), F32),
    ]


def _mix(layer, h, cos2, sin2, sinks, mix_params, casts, seq_len):
    n = h.shape[0]
    n_blocks = n // ROWS_FIRST
    rb = lambda w_: pl.BlockSpec((ROWS_FIRST, w_), lambda i: (i, 0))
    cast_in, cast_out, cast_shapes = _cast_specs(casts, n_blocks)
    return pl.pallas_call(
        functools.partial(_mix_kernel, layer, seq_len // ROWS_FIRST),
        grid=(n_blocks,),
        in_specs=[pl.BlockSpec(memory_space=pltpu.SMEM), rb(D_MODEL), rb(LANES), rb(LANES)]
        + _mix_param_specs(layer) + cast_in,
        out_specs=[rb(A_WIDTH), rb(B_WIDTH), rb(C_WIDTH)] + cast_out,
        out_shape=_mix_out_shapes(n) + cast_shapes,
        scratch_shapes=_mix_scratch(ROWS_FIRST),
        compiler_params=pltpu.CompilerParams(
            dimension_semantics=("arbitrary",), vmem_limit_bytes=VMEM_LIMIT),
        name="mix",
    )(sinks, h, cos2, sin2, *mix_params, *[c[0] for c in casts])


def _tables_kernel(are_ref, aim_ref, ldt_ref, bre_ref, bim_ref, cre_ref, cim_ref, d_ref,
                   kt_ref, wb_ref, wct_ref, astep_ref, atab_ref):
    a_re, a_im = are_ref[...], aim_ref[...]
    dt = jnp.exp(ldt_ref[...])
    mag = jnp.exp(a_re * dt)
    lr, li = mag * jnp.cos(a_im * dt), mag * jnp.sin(a_im * dt)
    den = a_re * a_re + a_im * a_im
    fr, fi = _cmul(lr - 1.0, li, a_re / den, -a_im / den)
    bbr, bbi = _cmul(fr, fi, bre_ref[...], bim_ref[...])
    c_re, c_im = cre_ref[...], cim_ref[...]

    row_g = lax.broadcasted_iota(jnp.int32, (QUAD_WIDTH, QUAD_STATE // 2), 0) // C_GROUP
    col_g = lax.broadcasted_iota(jnp.int32, (QUAD_WIDTH, QUAD_STATE // 2), 1) // C_STATE
    same_group = row_g == col_g

    def quad_block(mr, mi, q):
        cols = slice(q * QUAD_STATE // 2, (q + 1) * QUAD_STATE // 2)
        tile = lambda m: jnp.where(same_group, jnp.concatenate([m[:, cols]] * QUAD_GROUPS, axis=0), 0.0)
        return jnp.concatenate([tile(mr), tile(mi)], axis=1)

    pows = [(jnp.ones_like(lr), jnp.zeros_like(li))]
    for _ in range(SSM_T):
        pows.append(_cmul(pows[-1][0], pows[-1][1], lr, li))

    rr = lax.broadcasted_iota(jnp.int32, (PACKED, PACKED), 0)
    cc = lax.broadcasted_iota(jnp.int32, (PACKED, PACKED), 1)
    for q in range(QUADS):
        into_state, from_state, lag_out = [], [], []
        for s in range(SSM_T):
            wr, wi = _cmul(pows[SSM_T - 1 - s][0], pows[SSM_T - 1 - s][1], bbr, bbi)
            into_state.append(quad_block(wr, wi, q))
            mr, mi = _cmul(pows[s + 1][0], pows[s + 1][1], c_re, c_im)
            from_state.append(quad_block(mr, -mi, q))
            mr, mi = _cmul(pows[s][0], pows[s][1], c_re, c_im)
            lag_out.append(quad_block(mr, -mi, q))
        wb_ref[q] = jnp.concatenate(into_state, axis=0).astype(BF16)
        wct_ref[q] = jnp.concatenate(from_state, axis=0).astype(BF16)
        a_hi, a_lo = _split(quad_block(bbr, bbi, q))
        b_hi, b_lo = _split(jnp.concatenate(lag_out, axis=0))
        lags = _dot_nt(a_hi, b_hi) + _dot_nt(a_hi, b_lo) + _dot_nt(a_lo, b_hi)
        lane = lax.broadcasted_iota(jnp.int32, lags.shape, 1)
        k = jnp.concatenate(
            [lags] + [jnp.where(lane >= s * QUAD_WIDTH, pltpu.roll(lags, s * QUAD_WIDTH, 1), 0.0)
                      for s in range(1, SSM_T)], axis=0)
        kt_ref[q] = (k + jnp.where(rr == cc, d_ref[q], 0.0)).astype(BF16)

    half = QUAD_STATE // 2
    for q in range(QUADS):
        src = slice(q * half, (q + 1) * half)
        re_cols, im_cols = pl.ds(q * QUAD_STATE, half), pl.ds(q * QUAD_STATE + half, half)
        sr, si = pows[SSM_T][0][:, src], pows[SSM_T][1][:, src]
        atab_ref[0:1, re_cols] = sr
        atab_ref[0:1, im_cols] = si
        for k in range(SCAN_STEPS):
            span = 1 << k
            astep_ref[pl.ds(k, 1), re_cols] = sr
            astep_ref[pl.ds(k, 1), im_cols] = si
            tr, ti = _cmul(atab_ref[0:span, re_cols], atab_ref[0:span, im_cols], sr, si)
            atab_ref[pl.ds(span, span), re_cols] = tr
            atab_ref[pl.ds(span, span), im_cols] = ti
            sr, si = _cmul(sr, si, sr, si)


def _tables(a_re, a_im, ldt, b_re, b_im, c_re, c_im, d_skip):
    depth = a_re.shape[0]
    vec = lambda w: pl.BlockSpec((None, 1, w), lambda l: (l, 0, 0))
    mat = lambda r, c: pl.BlockSpec((None, r, c), lambda l: (l, 0, 0))
    cube = lambda r, c: pl.BlockSpec((None, QUADS, r, c), lambda l: (l, 0, 0, 0))
    packed = jax.ShapeDtypeStruct((depth, QUADS, PACKED, PACKED), BF16)
    return pl.pallas_call(
        _tables_kernel,
        grid=(depth,),
        in_specs=[vec(N_STATE), vec(N_STATE), vec(N_STATE),
                  mat(C_GROUP, N_STATE), mat(C_GROUP, N_STATE),
                  mat(C_GROUP, N_STATE), mat(C_GROUP, N_STATE), cube(1, PACKED)],
        out_specs=[cube(PACKED, PACKED), cube(PACKED, PACKED), cube(PACKED, PACKED),
                   mat(SCAN_STEPS, 2 * N_STATE), mat(SUBLANES, 2 * N_STATE)],
        out_shape=[packed, packed, packed,
                   jax.ShapeDtypeStruct((depth, SCAN_STEPS, 2 * N_STATE), F32),
                   jax.ShapeDtypeStruct((depth, SUBLANES, 2 * N_STATE), F32)],
        compiler_params=pltpu.CompilerParams(
            dimension_semantics=("arbitrary",), vmem_limit_bytes=VMEM_LIMIT),
        name="ssm_tables",
    )(a_re, a_im, ldt, b_re, b_im, c_re, c_im, d_skip)


def _ssm_kernel(blocks_per_seq, ua_ref, ub_ref, kt_ref, wb_ref, wct_ref, astep_ref, atab_ref,
                ya_ref, yb_ref, inc_ref, sprev_ref, carry_ref):
    @pl.when((pl.program_id(0) % blocks_per_seq) == 0)
    def _():
        carry_ref[...] = jnp.zeros_like(carry_ref)

    low = lax.broadcasted_iota(jnp.int32, (SSM_N, LANES), 1) < QUAD_WIDTH

    halves = [[ref[pl.ds(j, SSM_N, stride=SSM_T), :] for j in range(SSM_T)]
              for ref in (ua_ref, ub_ref)]

    def pack(q):
        src, upper = halves[q // 2], q % 2 == 1
        cols = []
        for m in range(SSM_T // 2):
            even, odd = src[2 * m], src[2 * m + 1]
            if upper:
                even = pltpu.roll(even, QUAD_WIDTH, 1)
            else:
                odd = pltpu.roll(odd, QUAD_WIDTH, 1)
            cols.append(jnp.where(low, even, odd))
        return jnp.concatenate(cols, axis=1).astype(BF16)

    xs = [pack(q) for q in range(QUADS)]

    for q in range(QUADS):
        inc_ref[:, q * QUAD_STATE:(q + 1) * QUAD_STATE] = _dot(xs[q], wb_ref[q])

    n_groups = SSM_N // SUBLANES
    sub = lax.broadcasted_iota(jnp.int32, (n_groups, SUBLANES, LANES), 1)
    sub2 = lax.broadcasted_iota(jnp.int32, (SUBLANES, LANES), 0)
    tiles_per_quad = QUAD_STATE // 2 // LANES
    y_lag = []
    for lt in range(N_STATE // LANES):
        q, part = lt // tiles_per_quad, lt % tiles_per_quad
        re_cols = pl.ds(q * QUAD_STATE + part * LANES, LANES)
        im_cols = pl.ds(q * QUAD_STATE + QUAD_STATE // 2 + part * LANES, LANES)
        xr = inc_ref[:, re_cols].reshape(n_groups, SUBLANES, LANES)
        xi = inc_ref[:, im_cols].reshape(n_groups, SUBLANES, LANES)
        for k in range(SCAN_STEPS):
            sh = 1 << k
            ar = astep_ref[pl.ds(k, 1), re_cols]
            ai = astep_ref[pl.ds(k, 1), im_cols]
            keep = sub >= sh
            sr = jnp.where(keep, pltpu.roll(xr, sh, 1), 0.0)
            si = jnp.where(keep, pltpu.roll(xi, sh, 1), 0.0)
            xr, xi = xr + ar * sr - ai * si, xi + ar * si + ai * sr
        tr = atab_ref[:, re_cols]
        ti = atab_ref[:, im_cols]
        er = jnp.broadcast_to(carry_ref[:, re_cols], (SUBLANES, LANES))
        ei = jnp.broadcast_to(carry_ref[:, im_cols], (SUBLANES, LANES))
        before_r, before_i = [], []
        for g in range(n_groups):
            fr = xr[g] + tr * er - ti * ei
            fi = xi[g] + tr * ei + ti * er
            before_r.append(jnp.where(sub2 == 0, er, pltpu.roll(fr, 1, 0)))
            before_i.append(jnp.where(sub2 == 0, ei, pltpu.roll(fi, 1, 0)))
            er = jnp.broadcast_to(fr[SUBLANES - 1:SUBLANES, :], (SUBLANES, LANES))
            ei = jnp.broadcast_to(fi[SUBLANES - 1:SUBLANES, :], (SUBLANES, LANES))
        sprev_ref[:, re_cols] = jnp.concatenate(before_r, axis=0).astype(BF16)
        sprev_ref[:, im_cols] = jnp.concatenate(before_i, axis=0).astype(BF16)
        carry_ref[:, re_cols] = er[0:1, :]
        carry_ref[:, im_cols] = ei[0:1, :]

        if part == tiles_per_quad - 1:
            y_lag.append(_dot(xs[q], kt_ref[q]))

    ys = [y_lag[q] + _dot_nt(sprev_ref[:, q * QUAD_STATE:(q + 1) * QUAD_STATE], wct_ref[q])
          for q in range(QUADS)]
    for t in range(SSM_T):
        cols = slice((t // 2) * LANES, (t // 2 + 1) * LANES)
        for y_ref, (qa, qb) in ((ya_ref, (0, 1)), (yb_ref, (2, 3))):
            lower, upper = ys[qa][:, cols], ys[qb][:, cols]
            if t % 2 == 1:
                lower = pltpu.roll(lower, QUAD_WIDTH, 1)
            else:
                upper = pltpu.roll(upper, QUAD_WIDTH, 1)
            y_ref[pl.ds(t, SSM_N, stride=SSM_T), :] = jnp.where(low, lower, upper)


def _ssm(layer, zc, kt, wb, wct, astep, atab, seq_len):
    n = zc.shape[0]
    half = lambda c: pl.BlockSpec((ROWS_SSM, LANES), lambda i: (i, c))
    return pl.pallas_call(
        functools.partial(_ssm_kernel, seq_len // ROWS_SSM),
        grid=(n // ROWS_SSM,),
        in_specs=[
            half(0), half(1),
            _layer_spec((QUADS, PACKED, PACKED), layer),
            _layer_spec((QUADS, PACKED, QUAD_STATE), layer),
            _layer_spec((QUADS, PACKED, QUAD_STATE), layer),
            _layer_spec((SCAN_STEPS, 2 * N_STATE), layer),
            _layer_spec((SUBLANES, 2 * N_STATE), layer),
        ],
        out_specs=[pl.BlockSpec((ROWS_SSM, LANES), lambda i: (i, 0)) for _ in range(2)],
        out_shape=[jax.ShapeDtypeStruct((n, LANES), F32) for _ in range(2)],
        scratch_shapes=[
            pltpu.VMEM((SSM_N, 2 * N_STATE), F32),
            pltpu.VMEM((SSM_N, 2 * N_STATE), BF16),
            pltpu.VMEM((1, 2 * N_STATE), F32),
        ],
        compiler_params=pltpu.CompilerParams(
            dimension_semantics=("arbitrary",), vmem_limit_bytes=VMEM_LIMIT),
        name="ssm",
    )(zc, zc, kt, wb, wct, astep, atab)


def _post_stages(h_ref, ya_ref, yb_ref, sa_ref, sb_ref, p_ref, glu1_ref, glu2_ref, mgc_ref,
                 wo_ref, g1_ref, w1_ref, w2_ref, g2_ref, wg_ref, wp_ref, *emit):
    y = jax.nn.gelu(jnp.concatenate([sa_ref[...], sb_ref[...]], axis=1)).astype(BF16)
    glu_a, glu_b = _dot(y, glu1_ref[...]), _dot(y, glu2_ref[...])
    mixed = (_dot(ya_ref[...], wo_ref[0:A_WIDTH, :])
             + _dot(yb_ref[...], wo_ref[A_WIDTH:A_WIDTH + B_WIDTH, :]))
    yield
    yc = _rms(glu_a * jax.nn.sigmoid(glu_b), mgc_ref[...]).astype(BF16)
    mixed = mixed + _dot(yc, wo_ref[A_WIDTH + B_WIDTH:, :])
    yield
    h = h_ref[...] + mixed
    hn = _rms(h, g1_ref[...]).astype(BF16)
    ple = _dot(p_ref[...].astype(BF16), wp_ref[...])
    piece = D_FF // FF_SPLIT
    up = _dot(hn, w1_ref[:, 0:piece])
    yield
    ff = None
    for c in range(FF_SPLIT):
        a = jnp.maximum(up, 0.0)
        down = _dot((a * a).astype(BF16), w2_ref[c * piece:(c + 1) * piece, :])
        if c + 1 < FF_SPLIT:
            up = _dot(hn, w1_ref[:, (c + 1) * piece:(c + 2) * piece])
        yield
        ff = down if ff is None else ff + down
    h = h + ff
    gate = _dot(_rms(h, g2_ref[...]).astype(BF16), wg_ref[...])
    yield
    h = h + jax.nn.sigmoid(gate) * ple
    for fn in emit:
        fn(h)
    yield


N_POST_IN = 16


def _store_to(ref):
    def store(value):
        ref[...] = value
    return store


def _post_kernel(*refs):
    _run((_post_stages(*refs[:N_POST_IN], _store_to(refs[N_POST_IN])), POST_STAGES))


def _post_specs(layer, row_block, rows=ROWS_MIX):
    rb = lambda w: pl.BlockSpec((rows, w), lambda i: (row_block(i), 0))
    return [
        rb(D_MODEL), rb(A_WIDTH), rb(B_WIDTH), rb(LANES), rb(LANES),
        pl.BlockSpec((None, rows, PLE_DIM), lambda i: (layer, row_block(i), 0)),
        _layer_spec((C_WIDTH, C_WIDTH), layer),
        _layer_spec((C_WIDTH, C_WIDTH), layer),
        _layer_spec((1, C_WIDTH), layer),
        _const_spec((D_MODEL, D_MODEL)),
        _layer_spec((1, D_MODEL), layer),
        _const_spec((D_MODEL, D_FF)),
        _const_spec((D_FF, D_MODEL)),
        _layer_spec((1, D_MODEL), layer),
        _const_spec((D_MODEL, D_MODEL)),
        _layer_spec((PLE_DIM, D_MODEL), layer),
    ]


def _post(layer, h, ya, yb, yc, p, post_params):
    n = h.shape[0]
    return pl.pallas_call(
        _post_kernel,
        grid=(n // ROWS_PROJ,),
        in_specs=_post_specs(layer, lambda i: i, ROWS_PROJ),
        out_specs=pl.BlockSpec((ROWS_PROJ, D_MODEL), lambda i: (i, 0)),
        out_shape=jax.ShapeDtypeStruct((n, D_MODEL), F32),
        compiler_params=pltpu.CompilerParams(
            dimension_semantics=("arbitrary",), vmem_limit_bytes=VMEM_LIMIT),
        name="post",
    )(h, ya, yb, *yc, p, *post_params)


def _fused_kernel(layer, blocks_per_seq, n_blocks, sinks_ref, *refs):
    post_refs, refs = refs[:N_POST_IN], refs[N_POST_IN:]
    cos_ref, sin_ref, g_ref = refs[:3]
    w_ref = refs[3:5]
    mixer_refs, refs = refs[5:5 + N_MIXER_IN], refs[5 + N_MIXER_IN:]
    n_cast = (len(refs) - 8) // 2
    cast_src, cast_dst = refs[:n_cast], refs[n_cast + 4:-4]
    h_out_ref, ya_ref, yb_ref, zc_ref = refs[n_cast:n_cast + 4]
    hs_ref, z_ref, kprev_ref, vprev_ref = refs[-4:]
    i = pl.program_id(0)
    first_block = (i % blocks_per_seq) == 1 % blocks_per_seq

    @pl.when((i == 0) | first_block)
    def _():
        kprev_ref[...] = jnp.zeros_like(kprev_ref)
        vprev_ref[...] = jnp.zeros_like(vprev_ref)

    @pl.when(i == 0)
    def _():
        hs_ref[...] = jnp.zeros_like(hs_ref)

    def mixers():
        return (_mix_mixers(layer, first_block, sinks_ref, cos_ref, sin_ref, *mixer_refs,
                            ya_ref, yb_ref, zc_ref, z_ref, kprev_ref, vprev_ref), MIXER_STAGES)

    @pl.when(i < n_blocks)
    def _():
        _cast_chunks(cast_src, cast_dst)
        _mix_project(hs_ref, g_ref, w_ref, z_ref)
        _run((_post_stages(*post_refs, _store_to(h_out_ref), _store_to(hs_ref)), POST_STAGES),
             mixers())

    @pl.when(i == n_blocks)
    def _():
        _mix_project(hs_ref, g_ref, w_ref, z_ref)
        _run(mixers())


def _fused(layer, h, ya, yb, yc, p, post_params, cos2, sin2, sinks, mix_params, casts, seq_len):
    n = h.shape[0]
    n_blocks = n // ROWS_MIX
    merged = lambda i: jnp.minimum(i, n_blocks - 1)
    mixed = lambda i: jnp.maximum(i - 1, 0)
    rb = lambda w_: pl.BlockSpec((ROWS_MIX, w_), lambda i: (mixed(i), 0))
    cast_in, cast_out, cast_shapes = _cast_specs(casts, n_blocks)
    return pl.pallas_call(
        functools.partial(_fused_kernel, layer, seq_len // ROWS_MIX, n_blocks),
        grid=(n_blocks + 1,),
        in_specs=[pl.BlockSpec(memory_space=pltpu.SMEM)]
        + _post_specs(layer - 1, merged)
        + [rb(LANES), rb(LANES)] + _mix_param_specs(layer) + cast_in,
        out_specs=[pl.BlockSpec((ROWS_MIX, D_MODEL), lambda i: (merged(i), 0)),
                   rb(A_WIDTH), rb(B_WIDTH), rb(C_WIDTH)] + cast_out,
        out_shape=[jax.ShapeDtypeStruct((n, D_MODEL), F32)] + _mix_out_shapes(n) + cast_shapes,
        scratch_shapes=[pltpu.VMEM((ROWS_MIX, D_MODEL), F32)] + _mix_scratch(ROWS_MIX),
        compiler_params=pltpu.CompilerParams(
            dimension_semantics=("arbitrary",), vmem_limit_bytes=VMEM_LIMIT),
        name="fused",
    )(sinks, h, ya, yb, *yc, p, *post_params, cos2, sin2, *mix_params, *[c[0] for c in casts])


def kernel(x, p, positions, attn_norm_g, w_in, gmlp_ln_g, gmlp_ln_b, gmlp_ws, gmlp_bs, q_norm_g, k_norm_g, sinks, ssm_a_re, ssm_a_im, ssm_log_dt, ssm_b_re, ssm_b_im, ssm_c_re, ssm_c_im, ssm_d, glu_w1, glu_w2, mix_out_g, w_out, mlp_norm_g, w_ff1, w_ff2, ple_norm_g, w_ple_gate, w_ple_proj):
    bsz, seq_len, d_model = x.shape
    depth = w_in.shape[0]
    n = bsz * seq_len
    assert d_model == D_MODEL and w_in.shape[1:] == (D_MODEL, IN_COLS)
    assert seq_len % ROWS_SSM == 0 and seq_len % ROWS_FIRST == 0 and seq_len % ROWS_MIX == 0
    assert n % ROWS_PROJ == 0 and p.shape == (depth, bsz, seq_len, PLE_DIM)

    inv = 1.0 / (ROPE_THETA ** (jnp.arange(0, HEAD_DIM, 2, dtype=F32) / HEAD_DIM))
    ang = positions.astype(F32).reshape(n, 1) * inv
    cos2 = jnp.tile(jnp.cos(ang), (1, 4))
    sin2 = jnp.tile(jnp.concatenate([-jnp.sin(ang), jnp.sin(ang)], axis=1), (1, 2))

    seg = np.arange(256) // HEAD_DIM
    pm = jnp.asarray((seg[:, None] == seg[None, :]) / HEAD_DIM, dtype=BF16)

    rows = lambda v: v.reshape(depth, 1, -1).astype(F32)
    g_attn, g_mlp, g_ple = rows(attn_norm_g), rows(mlp_norm_g), rows(ple_norm_g)
    ln_g, ln_b = rows(gmlp_ln_g), rows(gmlp_ln_b)
    mg_a = rows(mix_out_g[:, :A_WIDTH])
    mg_b = rows(mix_out_g[:, A_WIDTH:A_WIDTH + B_WIDTH])
    mg_c = rows(mix_out_g[:, A_WIDTH + B_WIDTH:])
    qg = rows(jnp.tile(q_norm_g, (1, B_Q_HEADS)))
    kg = rows(jnp.tile(k_norm_g, (1, B_KV_HEADS)))
    bs = jnp.repeat(jnp.swapaxes(gmlp_bs, 1, 2), HEAD_DIM, axis=2)
    w_a = w_in[:, :, :2 * A_WIDTH].reshape(depth, D_MODEL, A_HEADS, 2, HEAD_DIM)
    w_a = jnp.swapaxes(w_a, 2, 3).reshape(depth, D_MODEL, 2 * A_WIDTH)
    w_in_a, w_in_rest0 = w_a.astype(BF16), w_in[0, :, 2 * A_WIDTH:].astype(BF16)
    causal = np.tril(np.ones((CHUNK, CHUNK), dtype=bool))
    ws_b = jnp.swapaxes(jnp.where(causal, gmlp_ws, 0.0), 1, 2).reshape(
        depth, CHUNK, A_HEADS * CHUNK).astype(BF16)
    w_proj_b = w_ple_proj.astype(BF16)
    glu1_b, glu2_b = glu_w1.astype(BF16), glu_w2.astype(BF16)
    p2 = p.reshape(depth, n, PLE_DIM)

    kd, wb, wct, astep, atab = _tables(
        rows(ssm_a_re), rows(ssm_a_im), rows(jnp.repeat(ssm_log_dt, C_STATE, axis=1)),
        jnp.transpose(ssm_b_re, (0, 3, 1, 2)).reshape(depth, C_GROUP, N_STATE),
        jnp.transpose(ssm_b_im, (0, 3, 1, 2)).reshape(depth, C_GROUP, N_STATE),
        jnp.transpose(ssm_c_re, (0, 2, 1, 3)).reshape(depth, C_GROUP, N_STATE),
        jnp.transpose(ssm_c_im, (0, 2, 1, 3)).reshape(depth, C_GROUP, N_STATE),
        jnp.tile(ssm_d.reshape(depth, QUADS, 1, QUAD_WIDTH).astype(F32), (1, 1, 1, SSM_T)))

    def casts_for(layer):
        items = [(w_out, layer, 0), (w_ff1, layer, 0), (w_ff2, layer, 0), (w_ple_gate, layer, 0)]
        if layer + 1 < depth:
            items.append((w_in, layer + 1, 2 * A_WIDTH))
        return items

    def mix_params(w_in_rest):
        return (g_attn, w_in_a, w_in_rest, ws_b, bs, ln_g, ln_b, mg_a, qg, kg, mg_b, pm)

    def post_params(cast):
        w_out_b, w_ff1_b, w_ff2_b, w_gate_b = cast[:4]
        return (glu1_b, glu2_b, mg_c, w_out_b, g_mlp, w_ff1_b, w_ff2_b, g_ple, w_gate_b, w_proj_b)

    h = x.reshape(n, D_MODEL)
    ya, yb, zc, *cast = _mix(0, h, cos2, sin2, sinks, mix_params(w_in_rest0), casts_for(0),
                             seq_len)
    for i in range(depth):
        yc = _ssm(i, zc, kd, wb, wct, astep, atab, seq_len)
        if i + 1 < depth:
            h, ya, yb, zc, *cast = _fused(i + 1, h, ya, yb, yc, p2, post_params(cast), cos2, sin2,
                                          sinks, mix_params(cast[4]), casts_for(i + 1), seq_len)
        else:
            h = _post(i, h, ya, yb, yc, p2, post_params(cast))
    return h.reshape(bsz, seq_len, D_MODEL)
```
